```python
import math
import jax, jax.numpy as jnp
from jax import lax
import numpy as np

D_MODEL = 2048
BATCH = 1
SEQ = 8192
DEPTH = 1
DEC_BATCH = 128
DEC_SEQ = 8
PAST_LEN = 2048
PAGE_SIZE = 128

HEAD_DIM = 128
SWA_GROUPS = ((128, 1), (512, 4), (2048, 16))
SWA_HPG = 4
SWA_HEADS = SWA_HPG * len(SWA_GROUPS)
SWA_WIDTH = SWA_HEADS * HEAD_DIM
SWA_OUT = SWA_HPG * HEAD_DIM
DN_HEADS = 12
DN_WIDTH = DN_HEADS * HEAD_DIM
CONV_WIDTH = 4
DN_CHUNK = 64
MEM_TOKENS = 256
MEM_HEADS = 4
MEM_HEAD_DIM = 256
MEM_WIDTH = MEM_HEADS * MEM_HEAD_DIM
N_BUCKETS = 32
MAX_DISTANCE = 2048
D_FF = 4 * D_MODEL
N_BRANCHES = 3
EPS = 1e-6
IN_SIZES = (SWA_WIDTH, SWA_WIDTH, SWA_WIDTH, 3 * DN_WIDTH, DN_WIDTH, DN_HEADS, DN_HEADS, MEM_WIDTH, N_BRANCHES * D_MODEL)
D_IN = SWA_WIDTH * 3 + DN_WIDTH * 4 + DN_HEADS * 2 + MEM_WIDTH + N_BRANCHES * D_MODEL

kernel_name = "hybrid_dilated_swa_gated_delta_memory_step"


def rms_norm(x, g):
    xf = x.astype(jnp.float32)
    y = xf * lax.rsqrt(jnp.mean(xf * xf, axis=-1, keepdims=True) + EPS)
    return (y * g.astype(jnp.float32)).astype(x.dtype)


def l2_norm(x):
    xf = x.astype(jnp.float32)
    return xf * lax.rsqrt(jnp.sum(xf * xf, axis=-1, keepdims=True) + EPS)


def t5_bucket(dist):
    max_exact = N_BUCKETS // 2
    df = jnp.maximum(dist, 1).astype(jnp.float32)
    large = max_exact + (jnp.log(df / max_exact) / math.log(MAX_DISTANCE / max_exact)
                         * (N_BUCKETS - max_exact)).astype(jnp.int32)
    return jnp.where(dist < max_exact, dist, jnp.minimum(large, N_BUCKETS - 1))


def group_bias(rel_bias, g):
    window, dil = SWA_GROUPS[g]
    dist = jnp.arange(window // dil + 1, dtype=jnp.int32) * dil
    return rel_bias[t5_bucket(dist)][:, g * SWA_HPG:(g + 1) * SWA_HPG].astype(jnp.float32)


def swa_prompt_group(q, k, v, bias, dil, span):
    B, T, H, Dh = q.shape
    ts = T // dil
    nb = -(-ts // span)
    pad = nb * span - ts

    def sub(t):
        t = t.reshape(B, ts, dil, H, Dh).transpose(0, 2, 1, 3, 4)
        t = jnp.pad(t, ((0, 0), (0, 0), (0, pad), (0, 0), (0, 0)))
        return t.reshape(B, dil, nb, span, H, Dh)

    def band(t):
        prev = jnp.pad(t, ((0, 0), (0, 0), (1, 0), (0, 0), (0, 0), (0, 0)))[:, :, :nb]
        return jnp.concatenate([prev, t], axis=3)

    qs = sub(q)
    kb, vb = band(sub(k)), band(sub(v))
    s = jnp.einsum('brnqhd,brnkhd->brnhqk', qs, kb, preferred_element_type=jnp.float32) * (Dh ** -0.5)
    rel = jnp.arange(span)[:, None] + span - jnp.arange(2 * span)[None, :]
    valid = (rel >= 0) & (rel <= span)
    no_prev = (jnp.arange(nb)[:, None, None] == 0) & (jnp.arange(2 * span) < span)[None, None, :]
    valid = valid[None] & ~no_prev
    s = s + jnp.transpose(bias[jnp.clip(rel, 0, span)], (2, 0, 1))
    s = jnp.where(valid[None, None, :, None], s, -jnp.inf)
    m = jnp.max(s, axis=-1, keepdims=True)
    p = jnp.exp(s - m)
    den = jnp.sum(p, axis=-1)
    o = jnp.einsum('brnhqk,brnkhd->brnqhd', p, vb.astype(jnp.float32))
    o = o / jnp.transpose(den, (0, 1, 2, 4, 3))[..., None]
    lse = jnp.transpose(m[..., 0] + jnp.log(den), (0, 1, 2, 4, 3))

    def unsub(t):
        t = t.reshape((B, dil, nb * span) + t.shape[4:])[:, :, :ts]
        t = jnp.swapaxes(t, 1, 2)
        return t.reshape((B, T) + t.shape[3:])

    return unsub(o), unsub(lse)


def swa_sample_group(q, k, v, k_buf, v_buf, bias, dil, span):
    S, L, Dh = q.shape[1], k_buf.shape[1], q.shape[-1]
    kf = jnp.concatenate([k_buf.astype(k.dtype), k], axis=1)
    vf = jnp.concatenate([v_buf.astype(v.dtype), v], axis=1)
    idx = L + jnp.arange(S)[:, None] - jnp.arange(span + 1)[None, :] * dil
    valid = idx >= 0
    idx = jnp.maximum(idx, 0)
    kg = jnp.take(kf, idx, axis=1)
    vg = jnp.take(vf, idx, axis=1)
    s = jnp.einsum('bqhd,bqjhd->bhqj', q, kg, preferred_element_type=jnp.float32) * (Dh ** -0.5)
    s = s + jnp.transpose(bias)[None, :, None, :]
    s = jnp.where(valid[None, None], s, -jnp.inf)
    m = jnp.max(s, axis=-1, keepdims=True)
    p = jnp.exp(s - m)
    den = jnp.sum(p, axis=-1)
    o = jnp.einsum('bhqj,bqjhd->bqhd', p, vg.astype(jnp.float32))
    o = o / jnp.transpose(den, (0, 2, 1))[..., None]
    lse = jnp.transpose(m[..., 0] + jnp.log(den), (0, 2, 1))
    return o, lse


def gated_delta_chunked(q, k, v, beta, g, S0, chunk):
    B, T, H, Dk = q.shape
    Dv = v.shape[-1]
    n = T // chunk

    def blocks(t):
        return jnp.moveaxis(t.reshape((B, n, chunk) + t.shape[2:]), 2, 3)

    qc, kc, vc, bc, gc = (blocks(t) for t in (q, k, v, beta, g))
    gcum = jnp.cumsum(gc, axis=-1)
    tri = jnp.tril(jnp.ones((chunk, chunk), dtype=bool))
    strict = jnp.tril(jnp.ones((chunk, chunk), dtype=bool), -1)
    decay = jnp.exp(jnp.where(tri, gcum[..., :, None] - gcum[..., None, :], -jnp.inf))
    kbeta = kc * bc[..., None]
    a_mat = jnp.eye(chunk, dtype=jnp.float32) + jnp.where(
        strict, jnp.einsum('bnhid,bnhjd->bnhij', kbeta, kc) * decay, 0.0)
    rhs = jnp.concatenate([vc * bc[..., None], kbeta * jnp.exp(gcum)[..., None]], axis=-1)
    sol = lax.linalg.triangular_solve(a_mat, rhs, left_side=True, lower=True, unit_diagonal=True)
    u, w = sol[..., :Dv], sol[..., Dv:]
    attn = jnp.where(tri, jnp.einsum('bnhid,bnhjd->bnhij', qc, kc) * decay, 0.0)
    qdec = qc * jnp.exp(gcum)[..., None]
    kdec = kc * jnp.exp(gcum[..., -1:] - gcum)[..., None]
    glast = jnp.exp(gcum[..., -1])

    def step(S, xs):
        u_, w_, attn_, qdec_, kdec_, glast_ = xs
        v_new = u_ - jnp.einsum('bhcd,bhde->bhce', w_, S)
        o = jnp.einsum('bhcd,bhde->bhce', qdec_, S) + jnp.einsum('bhij,bhje->bhie', attn_, v_new)
        S = S * glast_[..., None, None] + jnp.einsum('bhcd,bhce->bhde', kdec_, v_new)
        return S, o

    xs = tuple(jnp.moveaxis(t, 1, 0) for t in (u, w, attn, qdec, kdec, glast))
    S_new, o = lax.scan(step, S0, xs)
    o = jnp.transpose(o, (1, 0, 3, 2, 4)).reshape(B, T, H, Dv)
    return o, S_new


def delta_branch(qkv_in, z, beta_in, a_in, conv_buf, S0, w_conv, A_log, dt_bias, norm_delta):
    B, T, _ = qkv_in.shape
    xp = jnp.concatenate([conv_buf.astype(qkv_in.dtype), qkv_in], axis=1)
    conv = sum(xp[:, i:i + T] * w_conv[i] for i in range(CONV_WIDTH))
    new_buf = xp[:, T:]
    q, k, v = jnp.split(jax.nn.silu(conv.astype(jnp.float32)), 3, axis=-1)
    q = l2_norm(q.reshape(B, T, DN_HEADS, HEAD_DIM)) * (HEAD_DIM ** -0.5)
    k = l2_norm(k.reshape(B, T, DN_HEADS, HEAD_DIM))
    v = v.reshape(B, T, DN_HEADS, HEAD_DIM)
    beta = jax.nn.sigmoid(beta_in.astype(jnp.float32))
    g = -jnp.exp(A_log.astype(jnp.float32)) * jax.nn.softplus(a_in.astype(jnp.float32) + dt_bias.astype(jnp.float32))
    chunk = DN_CHUNK if T % DN_CHUNK == 0 else T
    o, S_new = gated_delta_chunked(q, k, v, beta, g, S0.astype(jnp.float32), chunk)
    o = rms_norm(o, norm_delta) * jax.nn.silu(z.astype(jnp.float32).reshape(B, T, DN_HEADS, HEAD_DIM))
    return o.reshape(B, T, DN_WIDTH).astype(qkv_in.dtype), S_new.astype(S0.dtype), new_buf


def memory_kv(mem, norm_mem, w_mem_kv):
    B, M, _ = mem.shape
    k, v = jnp.split(rms_norm(mem, norm_mem) @ w_mem_kv, 2, axis=-1)
    return k.reshape(B, M, MEM_HEADS, MEM_HEAD_DIM), v.reshape(B, M, MEM_HEADS, MEM_HEAD_DIM)


def memory_attention(q, k, v):
    s = jnp.einsum('bqhd,bkhd->bhqk', q, k.astype(q.dtype), preferred_element_type=jnp.float32) * (MEM_HEAD_DIM ** -0.5)
    p = jax.nn.softmax(s, axis=-1)
    return jnp.einsum('bhqk,bkhd->bqhd', p, v.astype(jnp.float32))


def hybrid_layer(x, swa_caches, conv_buf, S0, mem_k, mem_v, rel_bias,
                 w_in, w_conv, A_log, dt_bias, norm_delta, w_o_swa, w_o_delta, w_o_mem, w_out,
                 norm_pre_mix, norm_post_mix, norm_pre_ffn, norm_post_ffn, w_ff1, w_ff2):
    B, T, _ = x.shape
    h = rms_norm(x, norm_pre_mix)
    proj = jnp.einsum('btd,de->bte', h, w_in)
    splits = np.cumsum(IN_SIZES)[:-1].tolist()
    a_q, a_k, a_v, b_qkv, b_z, b_beta, b_a, c_q, gate_in = jnp.split(proj, splits, axis=-1)
    a_q, a_k, a_v = (t.reshape(B, T, SWA_HEADS, HEAD_DIM) for t in (a_q, a_k, a_v))
    outs, lses, swa_new = [], [], []
    for g, (window, dil) in enumerate(SWA_GROUPS):
        hs = slice(g * SWA_HPG, (g + 1) * SWA_HPG)
        qg, kg, vg = a_q[:, :, hs], a_k[:, :, hs], a_v[:, :, hs]
        bias = group_bias(rel_bias, g)
        span = window // dil
        if swa_caches is None:
            o, lse = swa_prompt_group(qg, kg, vg, bias, dil, span)
            keep = min(window, T)
            swa_new += [kg[:, T - keep:], vg[:, T - keep:]]
        else:
            o, lse = swa_sample_group(qg, kg, vg, swa_caches[2 * g], swa_caches[2 * g + 1], bias, dil, span)
            swa_new += [kg, vg]
        outs.append(o)
        lses.append(lse)
    wts = jax.nn.softmax(jnp.stack(lses), axis=0)[..., None]
    a_out = jnp.sum(wts * jnp.stack(outs), axis=0).reshape(B, T, SWA_OUT).astype(x.dtype)
    b_out, S_new, conv_new = delta_branch(b_qkv, b_z, b_beta, b_a, conv_buf, S0, w_conv, A_log, dt_bias, norm_delta)
    c_out = memory_attention(c_q.reshape(B, T, MEM_HEADS, MEM_HEAD_DIM), mem_k, mem_v)
    c_out = c_out.reshape(B, T, MEM_WIDTH).astype(x.dtype)
    g_a, g_b, g_c = jnp.split(jax.nn.sigmoid(gate_in), N_BRANCHES, axis=-1)
    merged = g_a * (a_out @ w_o_swa) + g_b * (b_out @ w_o_delta) + g_c * (c_out @ w_o_mem)
    x = x + rms_norm(merged @ w_out, norm_post_mix)
    f = jnp.square(jax.nn.relu(rms_norm(x, norm_pre_ffn) @ w_ff1)) @ w_ff2
    x = x + rms_norm(f, norm_post_ffn)
    return x, swa_new, S_new, conv_new


def setup_inputs(seed: int = 0) -> dict:
    key = jax.random.key(seed)
    kit = iter(jax.random.split(key, 40))

    def nrm(shape, scale=1.0):
        return jax.random.normal(next(kit), shape, jnp.float32) * scale

    def gain(shape):
        return 1.0 + 0.02 * nrm(shape)

    L = [min(w, PAST_LEN) for w, _ in SWA_GROUPS]
    dt = jax.random.uniform(next(kit), (DEPTH, DN_HEADS), jnp.float32, minval=0.001, maxval=0.1)
    return {
        "x_prompt": nrm((BATCH, SEQ, D_MODEL)),
        "x_sample": nrm((DEC_BATCH, DEC_SEQ, D_MODEL)),
        "cache_swa0_k": nrm((DEPTH, DEC_BATCH, L[0], SWA_HPG, HEAD_DIM)),
        "cache_swa0_v": nrm((DEPTH, DEC_BATCH, L[0], SWA_HPG, HEAD_DIM)),
        "cache_swa1_k": nrm((DEPTH, DEC_BATCH, L[1], SWA_HPG, HEAD_DIM)),
        "cache_swa1_v": nrm((DEPTH, DEC_BATCH, L[1], SWA_HPG, HEAD_DIM)),
        "cache_swa2_k": nrm((DEPTH, DEC_BATCH, L[2], SWA_HPG, HEAD_DIM)),
        "cache_swa2_v": nrm((DEPTH, DEC_BATCH, L[2], SWA_HPG, HEAD_DIM)),
        "state_delta": nrm((DEPTH, DEC_BATCH, DN_HEADS, HEAD_DIM, HEAD_DIM), 0.1),
        "state_conv": nrm((DEPTH, DEC_BATCH, CONV_WIDTH - 1, 3 * DN_WIDTH)),
        "cache_mem_k": nrm((DEPTH, DEC_BATCH, MEM_TOKENS, MEM_HEADS, MEM_HEAD_DIM)),
        "cache_mem_v": nrm((DEPTH, DEC_BATCH, MEM_TOKENS, MEM_HEADS, MEM_HEAD_DIM)),
        "mem_prompt": nrm((BATCH, MEM_TOKENS, D_MODEL)),
        "rel_bias": nrm((N_BUCKETS, SWA_HEADS), 0.3),
        "w_in": nrm((DEPTH, D_MODEL, D_IN), D_MODEL ** -0.5),
        "w_conv": nrm((DEPTH, CONV_WIDTH, 3 * DN_WIDTH), CONV_WIDTH ** -0.5),
        "A_log": jnp.log(jax.random.uniform(next(kit), (DEPTH, DN_HEADS), jnp.float32, minval=1.0, maxval=16.0)),
        "dt_bias": jnp.log(jnp.expm1(dt)),
        "norm_delta": gain((DEPTH, HEAD_DIM)),
        "norm_mem": gain((DEPTH, D_MODEL)),
        "w_mem_kv": nrm((DEPTH, D_MODEL, 2 * MEM_WIDTH), D_MODEL ** -0.5),
        "w_o_swa": nrm((DEPTH, SWA_OUT, D_MODEL), SWA_OUT ** -0.5),
        "w_o_delta": nrm((DEPTH, DN_WIDTH, D_MODEL), DN_WIDTH ** -0.5),
        "w_o_mem": nrm((DEPTH, MEM_WIDTH, D_MODEL), MEM_WIDTH ** -0.5),
        "w_out": nrm((DEPTH, D_MODEL, D_MODEL), D_MODEL ** -0.5),
        "norm_pre_mix": gain((DEPTH, D_MODEL)),
        "norm_post_mix": gain((DEPTH, D_MODEL)),
        "norm_pre_ffn": gain((DEPTH, D_MODEL)),
        "norm_post_ffn": gain((DEPTH, D_MODEL)),
        "w_ff1": nrm((DEPTH, D_MODEL, D_FF), D_MODEL ** -0.5),
        "w_ff2": nrm((DEPTH, D_FF, D_MODEL), D_FF ** -0.5),
    }


def reference(x_prompt, x_sample, cache_swa0_k, cache_swa0_v, cache_swa1_k, cache_swa1_v,
              cache_swa2_k, cache_swa2_v, state_delta, state_conv, cache_mem_k, cache_mem_v,
              mem_prompt, rel_bias, w_in, w_conv, A_log, dt_bias, norm_delta, norm_mem, w_mem_kv,
              w_o_swa, w_o_delta, w_o_mem, w_out, norm_pre_mix, norm_post_mix, norm_pre_ffn,
              norm_post_ffn, w_ff1, w_ff2):
    layer_w = (w_in, w_conv, A_log, dt_bias, norm_delta, w_o_swa, w_o_delta, w_o_mem, w_out,
               norm_pre_mix, norm_post_mix, norm_pre_ffn, norm_post_ffn, w_ff1, w_ff2)
    sample_swa = (cache_swa0_k, cache_swa0_v, cache_swa1_k, cache_swa1_v, cache_swa2_k, cache_swa2_v)
    xp, xs = x_prompt, x_sample
    new_p = [[] for _ in range(10)]
    new_s = [[] for _ in range(8)]
    for l in range(DEPTH):
        lw = tuple(w[l] for w in layer_w)
        mk, mv = memory_kv(mem_prompt, norm_mem[l], w_mem_kv[l])
        zero_buf = jnp.zeros((xp.shape[0], CONV_WIDTH - 1, 3 * DN_WIDTH), xp.dtype)
        zero_S = jnp.zeros((xp.shape[0], DN_HEADS, HEAD_DIM, HEAD_DIM), jnp.float32)
        xp, swa_p, S_p, conv_p = hybrid_layer(xp, None, zero_buf, zero_S, mk, mv, rel_bias, *lw)
        xs, swa_s, S_s, conv_s = hybrid_layer(
            xs, tuple(c[l] for c in sample_swa), state_conv[l], state_delta[l],
            cache_mem_k[l], cache_mem_v[l], rel_bias, *lw)
        for lst, val in zip(new_p, (*swa_p, S_p, conv_p, mk, mv)):
            lst.append(val)
        for lst, val in zip(new_s, (*swa_s, S_s, conv_s)):
            lst.append(val)
    (p_swa0_k, p_swa0_v, p_swa1_k, p_swa1_v, p_swa2_k, p_swa2_v,
     p_delta, p_conv, p_mem_k, p_mem_v) = (jnp.stack(t) for t in new_p)
    (s_swa0_k, s_swa0_v, s_swa1_k, s_swa1_v, s_swa2_k, s_swa2_v,
     s_delta, s_conv) = (jnp.stack(t) for t in new_s)
    return (xp, xs, p_swa0_k, p_swa0_v, p_swa1_k, p_swa1_v, p_swa2_k, p_swa2_v, p_delta, p_conv,
            p_mem_k, p_mem_v, s_swa0_k, s_swa0_v, s_swa1_k, s_swa1_v, s_swa2_k, s_swa2_v, s_delta, s_conv)
```

```python
import functools
import math

import numpy as np
import jax
import jax.numpy as jnp
from jax import lax
from jax.experimental import pallas as pl
from jax.experimental.pallas import tpu as pltpu

EPS = 1e-6
HEAD_DIM = 128
SWA_GROUPS = ((128, 1), (512, 4), (2048, 16))
SWA_SPAN = 128
SWA_HPG = 4
SWA_WIDTH = SWA_HPG * len(SWA_GROUPS) * HEAD_DIM
DN_HEADS = 12
DN_WIDTH = DN_HEADS * HEAD_DIM
CONV_WIDTH = 4
DN_CHUNK = 64
MEM_HEADS = 4
MEM_HEAD_DIM = 256
MEM_WIDTH = MEM_HEADS * MEM_HEAD_DIM
N_BUCKETS = 32
MAX_DISTANCE = 2048
N_BRANCHES = 3

A_OFF = 0
B_OFF = 3 * SWA_WIDTH
Z_OFF = B_OFF + 3 * DN_WIDTH
BA_OFF = Z_OFF + DN_WIDTH
CQ_OFF = BA_OFF + 2 * DN_HEADS
GATE_OFF = CQ_OFF + MEM_WIDTH
LANES = 128
SWA_ROWS = 2048
NEG = -1e30
VMEM_LIMIT = 56 * 1024 * 1024

BF16 = jnp.bfloat16
F32 = jnp.float32


def _params(n_grid):
    return pltpu.CompilerParams(dimension_semantics=("arbitrary",) * n_grid, vmem_limit_bytes=VMEM_LIMIT)


def _dot(a, b):
    return jnp.dot(a, b, preferred_element_type=F32)


def _dot_nt(a, b):
    return lax.dot_general(a, b, (((1,), (1,)), ((), ())), preferred_element_type=F32)


def _dot_tn(a, b):
    return lax.dot_general(a, b, (((0,), (0,)), ((), ())), preferred_element_type=F32)


def _split2(x):
    hi = x.astype(BF16)
    lo = (x - hi.astype(F32)).astype(BF16)
    return hi, lo


def _dot3(a, b, dot=_dot):
    ah, al = _split2(a)
    bh, bl = _split2(b)
    return dot(ah, bh) + (dot(ah, bl) + dot(al, bh))


def _rmsnorm_cast_kernel(x_ref, g_ref, o_ref):
    x = x_ref[...]
    y = x * lax.rsqrt(jnp.mean(x * x, axis=-1, keepdims=True) + EPS)
    o_ref[...] = (y * g_ref[...]).astype(o_ref.dtype)


def _rmsnorm_cast(x, g, tm):
    m, d = x.shape
    return pl.pallas_call(
        _rmsnorm_cast_kernel, grid=(m // tm,),
        in_specs=[pl.BlockSpec((tm, d), lambda i: (i, 0)), pl.BlockSpec((1, d), lambda i: (0, 0))],
        out_specs=pl.BlockSpec((tm, d), lambda i: (i, 0)),
        out_shape=jax.ShapeDtypeStruct((m, d), BF16),
        compiler_params=_params(1), name="rmsnorm_cast")(x, g.reshape(1, d))


def _matmul_kernel(x_ref, w_ref, o_ref):
    o_ref[...] = _dot(x_ref[...], w_ref[...].astype(BF16)).astype(o_ref.dtype)


def _matmul(x, w, n_off, n, tm, tn, name):
    m, k = x.shape
    assert n_off % tn == 0 and n % tn == 0 and m % tm == 0
    jb = n_off // tn
    return pl.pallas_call(
        _matmul_kernel, grid=(m // tm, n // tn),
        in_specs=[pl.BlockSpec((tm, k), lambda i, j: (i, 0)), pl.BlockSpec((k, tn), lambda i, j: (0, jb + j))],
        out_specs=pl.BlockSpec((tm, tn), lambda i, j: (i, j)),
        out_shape=jax.ShapeDtypeStruct((m, n), F32),
        compiler_params=_params(2), name=name)(x, w)


def _t5_bucket(dist):
    max_exact = N_BUCKETS // 2
    df = jnp.maximum(dist, 1).astype(F32)
    large = max_exact + (jnp.log(df / max_exact) / math.log(MAX_DISTANCE / max_exact)
                         * (N_BUCKETS - max_exact)).astype(jnp.int32)
    return jnp.where(dist < max_exact, dist, jnp.minimum(large, N_BUCKETS - 1))


def _group_bias(rel_bias, g):
    _, dil = SWA_GROUPS[g]
    dist = jnp.arange(SWA_SPAN + 1, dtype=jnp.int32) * dil
    return rel_bias[_t5_bucket(dist)][:, g * SWA_HPG:(g + 1) * SWA_HPG].astype(F32)


def _prompt_bias_tables(rel_bias):
    span = SWA_SPAN
    rel = np.arange(span)[:, None] + span - np.arange(2 * span)[None, :]
    valid = (rel >= 0) & (rel <= span)
    first = valid & (np.arange(2 * span) >= span)[None, :]
    out = []
    for g in range(len(SWA_GROUPS)):
        tab = jnp.transpose(_group_bias(rel_bias, g)[np.clip(rel, 0, span)], (2, 0, 1))
        out.append(jnp.stack([jnp.where(first[None], tab, NEG), jnp.where(valid[None], tab, NEG)], axis=1))
    return jnp.stack(out)


def _sample_bias_tables(rel_bias, g, rows, n_new, cache_len):
    _, dil = SWA_GROUPS[g]
    s = np.arange(n_new)[:, None]

    def table(dist):
        j = dist // dil
        valid = (dist >= 0) & (dist % dil == 0) & (j <= SWA_SPAN)
        tab = jnp.transpose(_group_bias(rel_bias, g)[np.clip(j, 0, SWA_SPAN)], (2, 0, 1))
        return jnp.where(valid[None], tab, NEG)

    return table(cache_len + s - np.asarray(rows)[None, :]), table(s - np.arange(n_new)[None, :])


def _swa_prompt_kernel(*refs, n_groups):
    ins = refs[:5 * n_groups]
    bias_ref = refs[5 * n_groups]
    o_ref = refs[5 * n_groups + 1]
    scr = refs[5 * n_groups + 2:]
    kext, vext, og, lg = scr[:n_groups], scr[n_groups:2 * n_groups], scr[2 * n_groups:3 * n_groups], scr[3 * n_groups:]
    n = pl.program_id(0)
    scale = HEAD_DIM ** -0.5
    span = SWA_SPAN
    for g in range(n_groups):
        _, dil = SWA_GROUPS[g]
        q_ref, kc_ref, vc_ref, kp_ref, vp_ref = ins[5 * g:5 * g + 5]
        blk = span * dil
        kext[g][0:blk, :] = kp_ref[...]
        kext[g][blk:, :] = kc_ref[...]
        vext[g][0:blk, :] = vp_ref[...]
        vext[g][blk:, :] = vc_ref[...]

        def body(idx, carry, g=g, dil=dil, blk=blk, q_ref=q_ref):
            b = idx // dil
            r = idx - b * dil
            start = b * blk + r
            q = q_ref[pl.ds(start, span, stride=dil), :].astype(BF16)
            kk = kext[g][pl.ds(start, 2 * span, stride=dil), :].astype(BF16)
            vv = vext[g][pl.ds(start, 2 * span, stride=dil), :].astype(BF16)
            sel = jnp.where(jnp.logical_and(n == 0, b == 0), 0, 1)
            s = _dot_nt(q, kk) * scale + bias_ref[g, 0, sel]
            m = jnp.max(s, axis=-1, keepdims=True)
            p = jnp.exp(s - m)
            den = jnp.sum(p, axis=-1, keepdims=True)
            o = _dot(p.astype(BF16), vv) / den
            og[g][pl.ds(start, span, stride=dil), :] = o
            lg[g][pl.ds(start, span, stride=dil), :] = jnp.broadcast_to(m + jnp.log(den), (span, HEAD_DIM))
            return carry

        lax.fori_loop(0, SWA_ROWS // span, body, 0)
    lses = [lg[g][...] for g in range(n_groups)]
    mx = functools.reduce(jnp.maximum, lses)
    ws = [jnp.exp(l - mx) for l in lses]
    num = functools.reduce(lambda a, b: a + b, [w * og[g][...] for g, w in enumerate(ws)])
    o_ref[...] = (num / functools.reduce(lambda a, b: a + b, ws)).astype(o_ref.dtype)


def _swa_prompt(proj, bias_tables):
    t = proj.shape[0]
    n_groups = len(SWA_GROUPS)
    assert t % SWA_ROWS == 0
    in_specs, scratch_k, scratch_o = [], [], []
    for g, (_, dil) in enumerate(SWA_GROUPS):
        blk = SWA_SPAN * dil
        per = SWA_ROWS // blk
        assert SWA_ROWS % blk == 0
        qc, kc, vc = (sec * (SWA_WIDTH // HEAD_DIM) + g * SWA_HPG for sec in range(3))
        cur = lambda col: pl.BlockSpec((SWA_ROWS, HEAD_DIM), lambda n, h, col=col: (n, col + h))
        prev = lambda col, blk=blk, per=per: pl.BlockSpec(
            (blk, HEAD_DIM), lambda n, h, col=col, per=per: (jnp.maximum(n * per - 1, 0), col + h))
        in_specs += [cur(qc), cur(kc), cur(vc), prev(kc), prev(vc)]
        scratch_k.append(pltpu.VMEM((SWA_ROWS + blk, HEAD_DIM), F32))
        scratch_o.append(pltpu.VMEM((SWA_ROWS, HEAD_DIM), F32))
    in_specs.append(pl.BlockSpec((n_groups, 1, 2, SWA_SPAN, 2 * SWA_SPAN), lambda n, h: (0, h, 0, 0, 0)))
    return pl.pallas_call(
        functools.partial(_swa_prompt_kernel, n_groups=n_groups),
        grid=(t // SWA_ROWS, SWA_HPG),
        in_specs=in_specs,
        out_specs=pl.BlockSpec((SWA_ROWS, HEAD_DIM), lambda n, h: (n, h)),
        out_shape=jax.ShapeDtypeStruct((t, SWA_HPG * HEAD_DIM), BF16),
        scratch_shapes=scratch_k + scratch_k + scratch_o + scratch_o,
        compiler_params=_params(2), name="swa_prompt")(*([proj] * (5 * n_groups)), bias_tables)


def _swa_sample_kernel(*refs, n_groups):
    qkv_ref = refs[0]
    caches = refs[1:1 + 2 * n_groups]
    biases = refs[1 + 2 * n_groups:1 + 4 * n_groups]
    o_ref = refs[1 + 4 * n_groups]
    scale = HEAD_DIM ** -0.5
    hw = SWA_HPG * HEAD_DIM
    for h in range(SWA_HPG):
        parts = []
        for g in range(n_groups):
            col = (g * SWA_HPG + h) * HEAD_DIM
            q = qkv_ref[:, col:col + HEAD_DIM].astype(BF16)
            kn = qkv_ref[:, SWA_WIDTH + col:SWA_WIDTH + col + HEAD_DIM].astype(BF16)
            vn = qkv_ref[:, 2 * SWA_WIDTH + col:2 * SWA_WIDTH + col + HEAD_DIM].astype(BF16)
            k_ref, v_ref = caches[2 * g], caches[2 * g + 1]
            nk = math.prod(k_ref.shape[:-1])
            kc = k_ref[...].reshape(nk, hw)[:, h * HEAD_DIM:(h + 1) * HEAD_DIM].astype(BF16)
            vc = v_ref[...].reshape(nk, hw)[:, h * HEAD_DIM:(h + 1) * HEAD_DIM].astype(BF16)
            sc = _dot_nt(q, kc) * scale + biases[2 * g][h]
            sn = _dot_nt(q, kn) * scale + biases[2 * g + 1][h]
            m = jnp.maximum(jnp.max(sc, axis=-1, keepdims=True), jnp.max(sn, axis=-1, keepdims=True))
            pc = jnp.exp(sc - m)
            pn = jnp.exp(sn - m)
            den = jnp.sum(pc, axis=-1, keepdims=True) + jnp.sum(pn, axis=-1, keepdims=True)
            o = (_dot(pc.astype(BF16), vc) + _dot(pn.astype(BF16), vn)) / den
            parts.append((o, m + jnp.log(den)))
        mx = functools.reduce(jnp.maximum, [l for _, l in parts])
        ws = [jnp.exp(l - mx) for _, l in parts]
        num = functools.reduce(lambda a, b: a + b, [w * o for w, (o, _) in zip(ws, parts)])
        o_ref[:, h * HEAD_DIM:(h + 1) * HEAD_DIM] = num / functools.reduce(lambda a, b: a + b, ws)


def _swa_sample(proj, caches, rel_bias, n_new):
    rows_total = proj.shape[0]
    nb = rows_total // n_new
    n_groups = len(SWA_GROUPS)
    hw = SWA_HPG * HEAD_DIM
    cache_in, cache_specs, bias_in, bias_specs = [], [], [], []
    for g, (_, dil) in enumerate(SWA_GROUPS):
        cache_len = caches[2 * g].shape[1]
        if dil >= 2 * n_new and cache_len % dil == 0 and n_new % 8 == 0:
            rows = (np.arange(cache_len // dil)[:, None] * dil + np.arange(n_new)[None, :]).reshape(-1)
            shape = (nb, cache_len // dil, dil, hw)
            spec = pl.BlockSpec((1, cache_len // dil, n_new, hw), lambda b: (b, 0, 0, 0))
        else:
            rows = np.arange(cache_len)
            shape = (nb, cache_len, hw)
            spec = pl.BlockSpec((1, cache_len, hw), lambda b: (b, 0, 0))
        for c in caches[2 * g:2 * g + 2]:
            cache_in.append(c.reshape(shape))
            cache_specs.append(spec)
        for tab in _sample_bias_tables(rel_bias, g, rows, n_new, cache_len):
            bias_in.append(tab)
            bias_specs.append(pl.BlockSpec(tab.shape, lambda b: (0, 0, 0)))
    return pl.pallas_call(
        functools.partial(_swa_sample_kernel, n_groups=n_groups),
        grid=(nb,),
        in_specs=[pl.BlockSpec((n_new, 3 * SWA_WIDTH), lambda b: (b, 0))] + cache_specs + bias_specs,
        out_specs=pl.BlockSpec((n_new, hw), lambda b: (b, 0)),
        out_shape=jax.ShapeDtypeStruct((rows_total, hw), F32),
        compiler_params=_params(1), name="swa_sample")(proj, *cache_in, *bias_in)


def _silu(x):
    return x * (1.0 / (1.0 + jnp.exp(-x)))


def _delta_prep_kernel(x_ref, halo_ref, w_ref, o_ref, *, period):
    x = x_ref[...]
    tm = x.shape[0]
    acc = x * w_ref[CONV_WIDTH - 1:CONV_WIDTH, :]
    if period is None:
        halo = jnp.where(pl.program_id(0) > 0, halo_ref[...], 0.0)
        xp = jnp.concatenate([halo, x], axis=0)
        for k in range(1, CONV_WIDTH):
            acc = acc + pltpu.roll(xp, k, 0)[8:, :] * w_ref[CONV_WIDTH - 1 - k:CONV_WIDTH - k, :]
    else:
        hal = halo_ref[...]
        pos = lax.broadcasted_iota(jnp.int32, (tm, 1), 0) % period
        for k in range(1, CONV_WIDTH):
            prev_x = pltpu.roll(x, k, 0)
            prev_h = pltpu.roll(hal, (tm - (CONV_WIDTH - 1 - k)) % tm, 0)
            acc = acc + jnp.where(pos >= k, prev_x, prev_h) * w_ref[CONV_WIDTH - 1 - k:CONV_WIDTH - k, :]
    y = _silu(acc)
    for sec in range(3):
        for h in range(DN_HEADS):
            c = sec * DN_WIDTH + h * HEAD_DIM
            yh = y[:, c:c + HEAD_DIM]
            if sec < 2:
                yh = yh * lax.rsqrt(jnp.sum(yh * yh, axis=-1, keepdims=True) + EPS)
                if sec == 0:
                    yh = yh * (HEAD_DIM ** -0.5)
            o_ref[:, c:c + HEAD_DIM] = yh


def _delta_prep(proj, w_conv, halo, period, tm):
    m = proj.shape[0]
    width = 3 * DN_WIDTH
    cb = B_OFF // width
    assert B_OFF % width == 0 and m % tm == 0
    if period is None:
        halo_arr = proj
        halo_spec = pl.BlockSpec((8, width), lambda i: (jnp.maximum(i * (tm // 8) - 1, 0), cb))
    else:
        halo_arr = halo
        halo_spec = pl.BlockSpec((tm, width), lambda i: (i, 0))
    return pl.pallas_call(
        functools.partial(_delta_prep_kernel, period=period), grid=(m // tm,),
        in_specs=[pl.BlockSpec((tm, width), lambda i: (i, cb)), halo_spec,
                  pl.BlockSpec((CONV_WIDTH, width), lambda i: (0, 0))],
        out_specs=pl.BlockSpec((tm, width), lambda i: (i, 0)),
        out_shape=jax.ShapeDtypeStruct((m, width), F32),
        compiler_params=_params(1), name="delta_prep")(proj, halo_arr, w_conv)


def _softplus(x):
    return jnp.maximum(x, 0.0) + jnp.log(1.0 + jnp.exp(-jnp.abs(x)))


def _delta_scan_kernel(*refs, chunk, carry):
    if carry:
        qkv_ref, z_ref, ba_ref, alog_ref, dt_ref, gain_ref, o_ref, s_out_ref, s_scr = refs
        s_in_ref = None
    else:
        qkv_ref, z_ref, ba_ref, alog_ref, dt_ref, gain_ref, s_in_ref, o_ref, s_out_ref = refs
        s_scr = None
    c = chunk
    if carry:
        @pl.when(pl.program_id(0) == 0)
        def _():
            s_scr[...] = jnp.zeros_like(s_scr)

    ba = ba_ref[...]
    beta_all = 1.0 / (1.0 + jnp.exp(-ba))
    g_all = -jnp.exp(alog_ref[...]) * _softplus(ba + dt_ref[...])
    row = lax.broadcasted_iota(jnp.int32, (c, c), 0)
    col = lax.broadcasted_iota(jnp.int32, (c, c), 1)
    tri = row >= col
    strict = row > col
    tril = tri.astype(BF16)
    g1 = g_all.astype(BF16)
    r1 = g_all - g1.astype(F32)
    g2 = r1.astype(BF16)
    g3 = (r1 - g2.astype(F32)).astype(BF16)
    gcum_all = _dot(tril, g1) + (_dot(tril, g2) + _dot(tril, g3))
    gcum_t = gcum_all.T
    eye = (row == col).astype(F32)

    for h in range(DN_HEADS):
        q = qkv_ref[:, h * HEAD_DIM:(h + 1) * HEAD_DIM]
        k = qkv_ref[:, DN_WIDTH + h * HEAD_DIM:DN_WIDTH + (h + 1) * HEAD_DIM]
        v = qkv_ref[:, 2 * DN_WIDTH + h * HEAD_DIM:2 * DN_WIDTH + (h + 1) * HEAD_DIM]
        bcol = beta_all[:, h:h + 1]
        gc = gcum_all[:, DN_HEADS + h:DN_HEADS + h + 1]
        gr = gcum_t[DN_HEADS + h:DN_HEADS + h + 1, :]
        glast = gcum_all[c - 1:c, DN_HEADS + h:DN_HEADS + h + 1]
        decay = jnp.exp(jnp.where(tri, gc - gr, NEG))
        kb = k * bcol
        nmat = jnp.where(strict, _dot3(kb, k, _dot_nt) * decay, 0.0)
        inv = eye - nmat
        pw = nmat
        span = 2
        while span < c:
            pw = _dot3(pw, pw)
            inv = inv + _dot3(inv, pw)
            span *= 2
        eg = jnp.exp(gc)
        rhs = jnp.concatenate([v * bcol, kb * eg], axis=1)
        sol = _dot3(inv, rhs)
        u, w = sol[:, :HEAD_DIM], sol[:, HEAD_DIM:]
        attn = jnp.where(tri, _dot3(q, k, _dot_nt) * decay, 0.0)
        qdec = q * eg
        kdec = k * jnp.exp(glast - gc)
        s_prev = s_scr[h] if carry else s_in_ref[0, h]
        v_new = u - _dot3(w, s_prev)
        o = _dot3(qdec, s_prev) + _dot3(attn, v_new)
        s_new = s_prev * jnp.exp(glast) + _dot3(kdec, v_new, _dot_tn)
        if carry:
            s_scr[h] = s_new
            s_out_ref[h] = s_new
        else:
            s_out_ref[0, h] = s_new
        y = o * lax.rsqrt(jnp.mean(o * o, axis=-1, keepdims=True) + EPS) * gain_ref[...]
        zh = z_ref[:, h * HEAD_DIM:(h + 1) * HEAD_DIM]
        o_ref[:, h * HEAD_DIM:(h + 1) * HEAD_DIM] = (y * _silu(zh)).astype(o_ref.dtype)


def _delta_scan(qkv, proj, ba, a_log, dt_bias, norm_delta, s0, chunk):
    m = qkv.shape[0]
    carry = s0 is None
    n = m // chunk
    zb = Z_OFF // DN_WIDTH
    pad = lambda v_, off: jnp.zeros((1, LANES), F32).at[0, off:off + DN_HEADS].set(v_.astype(F32))
    consts = [pad(a_log, DN_HEADS), pad(dt_bias, DN_HEADS), norm_delta.reshape(1, HEAD_DIM).astype(F32)]
    const_specs = [pl.BlockSpec((1, LANES), lambda i: (0, 0))] * 3
    in_specs = [pl.BlockSpec((chunk, 3 * DN_WIDTH), lambda i: (i, 0)),
                pl.BlockSpec((chunk, DN_WIDTH), lambda i: (i, zb)),
                pl.BlockSpec((chunk, LANES), lambda i: (i, 0))] + const_specs
    args = [qkv, proj, ba] + consts
    state = (DN_HEADS, HEAD_DIM, HEAD_DIM)
    if carry:
        out_dtype = BF16
        s_shape, s_spec = state, pl.BlockSpec(state, lambda i: (0, 0, 0))
        scratch = [pltpu.VMEM(state, F32)]
    else:
        out_dtype = F32
        in_specs.append(pl.BlockSpec((1,) + state, lambda i: (i, 0, 0, 0)))
        args.append(s0)
        s_shape, s_spec = (n,) + state, pl.BlockSpec((1,) + state, lambda i: (i, 0, 0, 0))
        scratch = []
    return pl.pallas_call(
        functools.partial(_delta_scan_kernel, chunk=chunk, carry=carry), grid=(n,),
        in_specs=in_specs,
        out_specs=[pl.BlockSpec((chunk, DN_WIDTH), lambda i: (i, 0)), s_spec],
        out_shape=[jax.ShapeDtypeStruct((m, DN_WIDTH), out_dtype), jax.ShapeDtypeStruct(s_shape, F32)],
        scratch_shapes=scratch,
        compiler_params=_params(1), name="delta_scan")(*args)


def _mem_attn_kernel(q_ref, k_ref, v_ref, o_ref):
    scale = MEM_HEAD_DIM ** -0.5
    kk = k_ref[...].reshape(k_ref.shape[-2:])
    vv = v_ref[...].reshape(v_ref.shape[-2:])
    for h in range(MEM_HEADS):
        sl = slice(h * MEM_HEAD_DIM, (h + 1) * MEM_HEAD_DIM)
        s = _dot_nt(q_ref[:, sl].astype(BF16), kk[:, sl].astype(BF16)) * scale
        p = jnp.exp(s - jnp.max(s, axis=-1, keepdims=True))
        den = jnp.sum(p, axis=-1, keepdims=True)
        o_ref[:, sl] = (_dot(p.astype(BF16), vv[:, sl].astype(BF16)) / den).astype(o_ref.dtype)


def _mem_attn(tail, mem_k, mem_v, tm, out_dtype):
    m = tail.shape[0]
    nb, tokens, _ = mem_k.shape
    per = m // nb
    kv_spec = pl.BlockSpec((1, tokens, MEM_WIDTH), lambda i: (i * tm // per, 0, 0))
    return pl.pallas_call(
        _mem_attn_kernel, grid=(m // tm,),
        in_specs=[pl.BlockSpec((tm, MEM_WIDTH), lambda i: (i, 0)), kv_spec, kv_spec],
        out_specs=pl.BlockSpec((tm, MEM_WIDTH), lambda i: (i, 0)),
        out_shape=jax.ShapeDtypeStruct((m, MEM_WIDTH), out_dtype),
        compiler_params=_params(1), name="mem_attn")(tail, mem_k, mem_v)


def _merge_kernel(a_ref, b_ref, c_ref, ga_ref, gb_ref, gc_ref, wa_ref, wb_ref, wc_ref, o_ref):
    def branch(x_ref, g_ref, w_ref):
        y = _dot(x_ref[...].astype(BF16), w_ref[...].astype(BF16))
        return y * (1.0 / (1.0 + jnp.exp(-g_ref[...])))

    o_ref[...] = (branch(a_ref, ga_ref, wa_ref) + branch(b_ref, gb_ref, wb_ref)
                  + branch(c_ref, gc_ref, wc_ref)).astype(o_ref.dtype)


def _merge(a, b, c, tail, w_a, w_b, w_c, tm, tn):
    m = a.shape[0]
    d = w_a.shape[1]
    gate0 = (GATE_OFF - CQ_OFF) // tn
    assert (GATE_OFF - CQ_OFF) % tn == 0 and d % tn == 0
    row = lambda x: pl.BlockSpec((tm, x.shape[1]), lambda i, j: (i, 0))
    gate = lambda br: pl.BlockSpec((tm, tn), lambda i, j, br=br: (i, gate0 + br * (d // tn) + j))
    wsp = lambda w: pl.BlockSpec((w.shape[0], tn), lambda i, j: (0, j))
    return pl.pallas_call(
        _merge_kernel, grid=(m // tm, d // tn),
        in_specs=[row(a), row(b), row(c), gate(0), gate(1), gate(2), wsp(w_a), wsp(w_b), wsp(w_c)],
        out_specs=pl.BlockSpec((tm, tn), lambda i, j: (i, j)),
        out_shape=jax.ShapeDtypeStruct((m, d), BF16),
        compiler_params=_params(2), name="merge")(a, b, c, tail, tail, tail, w_a, w_b, w_c)


def _rms(y, g):
    return y * lax.rsqrt(jnp.mean(y * y, axis=-1, keepdims=True) + EPS) * g


def _proj_norm_residual_kernel(y_ref, w_ref, x_ref, g_post_ref, g_next_ref, o_ref, h_ref, acc_ref, *, tn):
    j = pl.program_id(1)
    acc_ref[:, pl.ds(pl.multiple_of(j * tn, tn), tn)] = _dot(y_ref[...], w_ref[...].astype(BF16))

    @pl.when(j == pl.num_programs(1) - 1)
    def _():
        x1 = x_ref[...] + _rms(acc_ref[...], g_post_ref[...])
        o_ref[...] = x1
        h_ref[...] = _rms(x1, g_next_ref[...]).astype(h_ref.dtype)


def _proj_norm_residual(y, w, x, g_post, g_next, tm, tn):
    m, d = x.shape
    k = y.shape[1]
    row = lambda width: pl.BlockSpec((tm, width), lambda i, j: (i, 0))
    gsp = pl.BlockSpec((1, d), lambda i, j: (0, 0))
    return pl.pallas_call(
        functools.partial(_proj_norm_residual_kernel, tn=tn), grid=(m // tm, d // tn),
        in_specs=[row(k), pl.BlockSpec((k, tn), lambda i, j: (0, j)), row(d), gsp, gsp],
        out_specs=[row(d), row(d)],
        out_shape=[jax.ShapeDtypeStruct((m, d), F32), jax.ShapeDtypeStruct((m, d), BF16)],
        scratch_shapes=[pltpu.VMEM((tm, d), F32)],
        compiler_params=_params(2), name="proj_norm_residual")(y, w, x, g_post.reshape(1, d), g_next.reshape(1, d))


def _ffn_kernel(h_ref, w1_ref, w2_ref, o_ref):
    j = pl.program_id(1)
    a = jnp.square(jnp.maximum(_dot(h_ref[...], w1_ref[...].astype(BF16)), 0.0))
    y = _dot(a.astype(BF16), w2_ref[...].astype(BF16))

    @pl.when(j == 0)
    def _():
        o_ref[...] = y

    @pl.when(j > 0)
    def _():
        o_ref[...] += y


def _ffn(h, w1, w2, tm, tf):
    m, d = h.shape
    f = w1.shape[1]
    return pl.pallas_call(
        _ffn_kernel, grid=(m // tm, f // tf),
        in_specs=[pl.BlockSpec((tm, d), lambda i, j: (i, 0)), pl.BlockSpec((d, tf), lambda i, j: (0, j)),
                  pl.BlockSpec((tf, d), lambda i, j: (j, 0))],
        out_specs=pl.BlockSpec((tm, d), lambda i, j: (i, 0)),
        out_shape=jax.ShapeDtypeStruct((m, d), F32),
        compiler_params=_params(2), name="ffn")(h, w1, w2)


def _norm_residual_kernel(x_ref, y_ref, g_ref, o_ref):
    o_ref[...] = x_ref[...] + _rms(y_ref[...], g_ref[...])


def _norm_residual(x, y, g, tm):
    m, d = x.shape
    row = pl.BlockSpec((tm, d), lambda i: (i, 0))
    return pl.pallas_call(
        _norm_residual_kernel, grid=(m // tm,),
        in_specs=[row, row, pl.BlockSpec((1, d), lambda i: (0, 0))], out_specs=row,
        out_shape=jax.ShapeDtypeStruct((m, d), F32),
        compiler_params=_params(1), name="norm_residual")(x, y, g.reshape(1, d))


def _row_tile(m, cap):
    t = min(m, cap)
    assert m % t == 0
    return t


def _layer(x, lw, rel_bias, bias_prompt, mem_k, mem_v, swa_caches, conv_buf, s0, n_seq):
    (w_in, w_tail, w_ba, w_conv, a_log, dt_bias, norm_delta, w_o_swa, w_o_delta, w_o_mem, w_out,
     norm_pre_mix, norm_post_mix, norm_pre_ffn, norm_post_ffn, w_ff1, w_ff2) = lw
    m, d = x.shape
    seq = m // n_seq
    prompt = swa_caches is None
    tm = _row_tile(m, 1024)
    h = _rmsnorm_cast(x, norm_pre_mix, _row_tile(m, 512))
    main = _matmul(h, w_in, 0, BA_OFF, tm, 512, "in_proj_main")
    tail = _matmul(h, w_tail, 0, w_tail.shape[1], tm, 512, "in_proj_tail")
    ba = _matmul(h, w_ba, 0, LANES, tm, LANES, "in_proj_ba")
    if prompt:
        assert n_seq == 1
        a_out = _swa_prompt(main, bias_prompt)
        qkv = _delta_prep(main, w_conv, None, None, _row_tile(m, 256))
        chunk = DN_CHUNK if seq % DN_CHUNK == 0 else seq
        b_out, s_new = _delta_scan(qkv, main, ba, a_log, dt_bias, norm_delta, None, chunk)
        c_out = _mem_attn(tail, mem_k, mem_v, _row_tile(m, 512), BF16)
    else:
        a_out = _swa_sample(main, swa_caches, rel_bias, seq)
        halo = jnp.pad(conv_buf, ((0, 0), (0, seq - (CONV_WIDTH - 1)), (0, 0))).reshape(m, 3 * DN_WIDTH)
        qkv = _delta_prep(main, w_conv, halo, seq, _row_tile(m, 256))
        b_out, s_new = _delta_scan(qkv, main, ba, a_log, dt_bias, norm_delta, s0, seq)
        c_out = _mem_attn(tail, mem_k, mem_v, seq, F32)
    merged = _merge(a_out, b_out, c_out, tail, w_o_swa, w_o_delta, w_o_mem, tm, 512)
    x1, h2 = _proj_norm_residual(merged, w_out, x, norm_post_mix, norm_pre_ffn, _row_tile(m, 512), 512)
    f = _ffn(h2, w_ff1, w_ff2, tm, 512)
    x2 = _norm_residual(x1, f, norm_post_ffn, _row_tile(m, 512))
    return x2, main, s_new


def kernel(x_prompt, x_sample, cache_swa0_k, cache_swa0_v, cache_swa1_k, cache_swa1_v, cache_swa2_k, cache_swa2_v, state_delta, state_conv, cache_mem_k, cache_mem_v, mem_prompt, rel_bias, w_in, w_conv, A_log, dt_bias, norm_delta, norm_mem, w_mem_kv, w_o_swa, w_o_delta, w_o_mem, w_out, norm_pre_mix, norm_post_mix, norm_pre_ffn, norm_post_ffn, w_ff1, w_ff2):
    depth = w_in.shape[0]
    bp, tp, d = x_prompt.shape
    bs, ts, _ = x_sample.shape
    assert bp == 1 and ts > CONV_WIDTH - 1 and ts % 8 == 0
    sample_swa = (cache_swa0_k, cache_swa0_v, cache_swa1_k, cache_swa1_v, cache_swa2_k, cache_swa2_v)
    bias_prompt = _prompt_bias_tables(rel_bias)
    xp = x_prompt.reshape(bp * tp, d)
    xs = x_sample.reshape(bs * ts, d)
    new_p = [[] for _ in range(10)]
    new_s = [[] for _ in range(8)]
    hw = SWA_HPG * HEAD_DIM
    for l in range(depth):
        w_tail = w_in[l][:, CQ_OFF:]
        w_ba = jnp.pad(w_in[l][:, BA_OFF:CQ_OFF], ((0, 0), (0, LANES - 2 * DN_HEADS)))
        lw = (w_in[l], w_tail, w_ba, w_conv[l], A_log[l], dt_bias[l], norm_delta[l], w_o_swa[l], w_o_delta[l],
              w_o_mem[l], w_out[l], norm_pre_mix[l], norm_post_mix[l], norm_pre_ffn[l], norm_post_ffn[l],
              w_ff1[l], w_ff2[l])
        mem = mem_prompt.reshape(-1, d)
        mkv = _matmul(_rmsnorm_cast(mem, norm_mem[l], _row_tile(mem.shape[0], 256)), w_mem_kv[l], 0,
                      2 * MEM_WIDTH, _row_tile(mem.shape[0], 256), 512, "mem_kv")
        mk = mkv[:, :MEM_WIDTH].reshape(bp, -1, MEM_WIDTH)
        mv = mkv[:, MEM_WIDTH:].reshape(bp, -1, MEM_WIDTH)
        xp, main_p, s_p = _layer(xp, lw, rel_bias, bias_prompt, mk, mv, None, None, None, bp)
        xs, main_s, s_s = _layer(
            xs, lw, rel_bias, bias_prompt,
            cache_mem_k[l].reshape(bs, -1, MEM_WIDTH), cache_mem_v[l].reshape(bs, -1, MEM_WIDTH),
            tuple(c[l] for c in sample_swa), state_conv[l], state_delta[l], bs)
        vals_p = []
        for g, (window, _) in enumerate(SWA_GROUPS):
            keep = min(window, tp)
            for sec in (1, 2):
                c0 = sec * SWA_WIDTH + g * hw
                vals_p.append(main_p[tp - keep:, c0:c0 + hw].reshape(bp, keep, SWA_HPG, HEAD_DIM))
        vals_p.append(s_p.reshape(bp, DN_HEADS, HEAD_DIM, HEAD_DIM))
        vals_p.append(main_p[tp - (CONV_WIDTH - 1):, B_OFF:B_OFF + 3 * DN_WIDTH].reshape(bp, CONV_WIDTH - 1, -1))
        vals_p += [mk.reshape(bp, -1, MEM_HEADS, MEM_HEAD_DIM), mv.reshape(bp, -1, MEM_HEADS, MEM_HEAD_DIM)]
        vals_s = []
        main_s3 = main_s.reshape(bs, ts, -1)
        for g in range(len(SWA_GROUPS)):
            for sec in (1, 2):
                c0 = sec * SWA_WIDTH + g * hw
                vals_s.append(main_s3[:, :, c0:c0 + hw].reshape(bs, ts, SWA_HPG, HEAD_DIM))
        vals_s.append(s_s)
        vals_s.append(main_s3[:, ts - (CONV_WIDTH - 1):, B_OFF:B_OFF + 3 * DN_WIDTH])
        for lst, val in zip(new_p, vals_p):
            lst.append(val)
        for lst, val in zip(new_s, vals_s):
            lst.append(val)
    outs_p = [jnp.stack(t) for t in new_p]
    outs_s = [jnp.stack(t) for t in new_s]
    return (xp.reshape(bp, tp, d), xs.reshape(bs, ts, d), *outs_p, *outs_s)
```

```python
import functools
import math

import numpy as np
import jax
import jax.numpy as jnp
from jax import lax
from jax.experimental import pallas as pl
from jax.experimental.pallas import tpu as pltpu

EPS = 1e-6
HEAD_DIM = 128
SWA_GROUPS = ((128, 1), (512, 4), (2048, 16))
SWA_SPAN = 128
SWA_HPG = 4
SWA_WIDTH = SWA_HPG * len(SWA_GROUPS) * HEAD_DIM
DN_HEADS = 12
DN_WIDTH = DN_HEADS * HEAD_DIM
CONV_WIDTH = 4
DN_CHUNK = 64
MEM_HEADS = 4
MEM_HEAD_DIM = 256
MEM_WIDTH = MEM_HEADS * MEM_HEAD_DIM
N_BUCKETS = 32
MAX_DISTANCE = 2048
N_BRANCHES = 3

A_OFF = 0
B_OFF = 3 * SWA_WIDTH
Z_OFF = B_OFF + 3 * DN_WIDTH
BA_OFF = Z_OFF + DN_WIDTH
CQ_OFF = BA_OFF + 2 * DN_HEADS
GATE_OFF = CQ_OFF + MEM_WIDTH
LANES = 128
SWA_ROWS = 2048
NEG = -1e30
VMEM_LIMIT = 56 * 1024 * 1024

BF16 = jnp.bfloat16
F32 = jnp.float32


def _params(n_grid):
    return pltpu.CompilerParams(dimension_semantics=("arbitrary",) * n_grid, vmem_limit_bytes=VMEM_LIMIT)


def _dot(a, b):
    return jnp.dot(a, b, preferred_element_type=F32)


def _dot_nt(a, b):
    return lax.dot_general(a, b, (((1,), (1,)), ((), ())), preferred_element_type=F32)


def _dot_tn(a, b):
    return lax.dot_general(a, b, (((0,), (0,)), ((), ())), preferred_element_type=F32)


def _split2(x):
    hi = x.astype(BF16)
    lo = (x - hi.astype(F32)).astype(BF16)
    return hi, lo


def _dot3(a, b, dot=_dot):
    ah, al = _split2(a)
    bh, bl = _split2(b)
    return dot(ah, bh) + (dot(ah, bl) + dot(al, bh))


def _rmsnorm_cast_kernel(x_ref, g_ref, o_ref):
    x = x_ref[...]
    y = x * lax.rsqrt(jnp.mean(x * x, axis=-1, keepdims=True) + EPS)
    o_ref[...] = (y * g_ref[...]).astype(o_ref.dtype)


def _rmsnorm_cast(x, g, tm):
    m, d = x.shape
    return pl.pallas_call(
        _rmsnorm_cast_kernel, grid=(m // tm,),
        in_specs=[pl.BlockSpec((tm, d), lambda i: (i, 0)), pl.BlockSpec((1, d), lambda i: (0, 0))],
        out_specs=pl.BlockSpec((tm, d), lambda i: (i, 0)),
        out_shape=jax.ShapeDtypeStruct((m, d), BF16),
        compiler_params=_params(1), name="rmsnorm_cast")(x, g.reshape(1, d))


def _matmul_kernel(x_ref, w_ref, o_ref):
    o_ref[...] = _dot(x_ref[...], w_ref[...].astype(BF16)).astype(o_ref.dtype)


def _matmul(x, w, n, tm, tn, name):
    m, k = x.shape
    assert n % tn == 0 and m % tm == 0
    return pl.pallas_call(
        _matmul_kernel, grid=(m // tm, n // tn),
        in_specs=[pl.BlockSpec((tm, k), lambda i, j: (i, 0)), pl.BlockSpec((k, tn), lambda i, j: (0, j))],
        out_specs=pl.BlockSpec((tm, tn), lambda i, j: (i, j)),
        out_shape=jax.ShapeDtypeStruct((m, n), F32),
        compiler_params=_params(2), name=name)(x, w)


SUBLANES = 8


def _matmul_nt_kernel(x_ref, w_ref, o_ref):
    o_ref[...] = _dot_nt(x_ref[...], w_ref[...].astype(BF16)).astype(o_ref.dtype)


def _matmul_nt(x, wt, n_off, n, tm, tn, name):
    m, k = x.shape
    assert n_off % SUBLANES == 0 and n % tn == 0 and m % tm == 0
    return pl.pallas_call(
        _matmul_nt_kernel, grid=(m // tm, n // tn),
        in_specs=[pl.BlockSpec((tm, k), lambda i, j: (i, 0)),
                  pl.BlockSpec((pl.Element(tn), pl.Element(k)),
                               lambda i, j: (pl.multiple_of(n_off + j * tn, SUBLANES), 0))],
        out_specs=pl.BlockSpec((tm, tn), lambda i, j: (i, j)),
        out_shape=jax.ShapeDtypeStruct((m, n), F32),
        compiler_params=_params(2), name=name)(x, wt)


def _t5_bucket(dist):
    max_exact = N_BUCKETS // 2
    df = jnp.maximum(dist, 1).astype(F32)
    large = max_exact + (jnp.log(df / max_exact) / math.log(MAX_DISTANCE / max_exact)
                         * (N_BUCKETS - max_exact)).astype(jnp.int32)
    return jnp.where(dist < max_exact, dist, jnp.minimum(large, N_BUCKETS - 1))


def _group_bias(rel_bias, g):
    _, dil = SWA_GROUPS[g]
    dist = jnp.arange(SWA_SPAN + 1, dtype=jnp.int32) * dil
    onehot = _t5_bucket(dist)[None, :, None] == jnp.arange(N_BUCKETS, dtype=jnp.int32)[None, None, :]
    heads = jnp.transpose(rel_bias[:, g * SWA_HPG:(g + 1) * SWA_HPG].astype(F32))
    return jnp.sum(jnp.where(onehot, heads[:, None, :], 0.0), axis=-1)


def _prompt_bias_tables(rel_bias):
    span = SWA_SPAN
    first = (np.arange(2 * span) >= span)[None, None, :]
    out = []
    for g in range(len(SWA_GROUPS)):
        w = _group_bias(rel_bias, g)
        h = w.shape[0]
        p = 3 * span
        e = jnp.concatenate([jnp.full((h, span - 1), NEG, F32), w[:, ::-1], jnp.full((h, p - 2 * span), NEG, F32)], 1)
        skew = jnp.broadcast_to(e[:, None, :], (h, span, p)).reshape(h, span * p)[:, :span * (p - 1)]
        tab = skew.reshape(h, span, p - 1)[:, :, span - 1:3 * span - 1]
        out.append(jnp.stack([jnp.where(first, tab, NEG), tab], axis=1))
    return jnp.stack(out)


def _sample_bias_tables(rel_bias, g, n_res, n_new, cache_len):
    _, dil = SWA_GROUPS[g]
    w = _group_bias(rel_bias, g)
    h = w.shape[0]
    lm = cache_len // dil
    assert cache_len % dil == 0
    wpad = jnp.concatenate([w, jnp.full((h, lm + n_new), NEG, F32)], axis=1)
    res = np.arange(n_res)[None, None, :]
    rows = []
    for s in range(n_new):
        vec = wpad[:, s // dil + 1:lm + s // dil + 1][:, ::-1]
        rows.append(jnp.where(res == s % dil, vec[:, :, None], NEG).reshape(h, lm * n_res))
    s = np.arange(n_new)
    dist = s[:, None] - s[None, :]
    new = jnp.full((h, n_new, n_new), NEG, F32)
    for j in range(min((n_new - 1) // dil, SWA_SPAN) + 1):
        new = jnp.where((dist == j * dil)[None], w[:, j][:, None, None], new)
    return jnp.stack(rows, axis=1), new


def _swa_prompt_kernel(*refs, n_groups):
    ins = refs[:5 * n_groups]
    bias_ref = refs[5 * n_groups]
    o_ref = refs[5 * n_groups + 1]
    scr = refs[5 * n_groups + 2:]
    kext, vext, og, lg = scr[:n_groups], scr[n_groups:2 * n_groups], scr[2 * n_groups:3 * n_groups], scr[3 * n_groups:]
    n = pl.program_id(0)
    scale = HEAD_DIM ** -0.5
    span = SWA_SPAN
    for g in range(n_groups):
        _, dil = SWA_GROUPS[g]
        q_ref, kc_ref, vc_ref, kp_ref, vp_ref = ins[5 * g:5 * g + 5]
        blk = span * dil
        kext[g][0:blk, :] = kp_ref[...]
        kext[g][blk:, :] = kc_ref[...]
        vext[g][0:blk, :] = vp_ref[...]
        vext[g][blk:, :] = vc_ref[...]

        def body(idx, carry, g=g, dil=dil, blk=blk, q_ref=q_ref):
            b = idx // dil
            r = idx - b * dil
            start = b * blk + r
            q = q_ref[pl.ds(start, span, stride=dil), :].astype(BF16)
            kk = kext[g][pl.ds(start, 2 * span, stride=dil), :].astype(BF16)
            vv = vext[g][pl.ds(start, 2 * span, stride=dil), :].astype(BF16)
            sel = jnp.where(jnp.logical_and(n == 0, b == 0), 0, 1)
            s = _dot_nt(q, kk) * scale + bias_ref[g, 0, sel]
            m = jnp.max(s, axis=-1, keepdims=True)
            p = jnp.exp(s - m)
            den = jnp.sum(p, axis=-1, keepdims=True)
            o = _dot(p.astype(BF16), vv) / den
            og[g][pl.ds(start, span, stride=dil), :] = o
            lg[g][pl.ds(start, span, stride=dil), :] = jnp.broadcast_to(m + jnp.log(den), (span, HEAD_DIM))
            return carry

        lax.fori_loop(0, SWA_ROWS // span, body, 0)
    lses = [lg[g][...] for g in range(n_groups)]
    mx = functools.reduce(jnp.maximum, lses)
    ws = [jnp.exp(l - mx) for l in lses]
    num = functools.reduce(lambda a, b: a + b, [w * og[g][...] for g, w in enumerate(ws)])
    o_ref[...] = (num / functools.reduce(lambda a, b: a + b, ws)).astype(o_ref.dtype)


def _swa_prompt(proj, bias_tables):
    t = proj.shape[0]
    n_groups = len(SWA_GROUPS)
    assert t % SWA_ROWS == 0
    in_specs, scratch_k, scratch_o = [], [], []
    for g, (_, dil) in enumerate(SWA_GROUPS):
        blk = SWA_SPAN * dil
        per = SWA_ROWS // blk
        assert SWA_ROWS % blk == 0
        qc, kc, vc = (sec * (SWA_WIDTH // HEAD_DIM) + g * SWA_HPG for sec in range(3))
        cur = lambda col: pl.BlockSpec((SWA_ROWS, HEAD_DIM), lambda n, h, col=col: (n, col + h))
        prev = lambda col, blk=blk, per=per: pl.BlockSpec(
            (blk, HEAD_DIM), lambda n, h, col=col, per=per: (jnp.maximum(n * per - 1, 0), col + h))
        in_specs += [cur(qc), cur(kc), cur(vc), prev(kc), prev(vc)]
        scratch_k.append(pltpu.VMEM((SWA_ROWS + blk, HEAD_DIM), F32))
        scratch_o.append(pltpu.VMEM((SWA_ROWS, HEAD_DIM), F32))
    in_specs.append(pl.BlockSpec((n_groups, 1, 2, SWA_SPAN, 2 * SWA_SPAN), lambda n, h: (0, h, 0, 0, 0)))
    return pl.pallas_call(
        functools.partial(_swa_prompt_kernel, n_groups=n_groups),
        grid=(t // SWA_ROWS, SWA_HPG),
        in_specs=in_specs,
        out_specs=pl.BlockSpec((SWA_ROWS, HEAD_DIM), lambda n, h: (n, h)),
        out_shape=jax.ShapeDtypeStruct((t, SWA_HPG * HEAD_DIM), BF16),
        scratch_shapes=scratch_k + scratch_k + scratch_o + scratch_o,
        compiler_params=_params(2), name="swa_prompt")(*([proj] * (5 * n_groups)), bias_tables)


def _swa_sample_kernel(*refs, n_groups):
    qkv_ref = refs[0]
    caches = refs[1:1 + 2 * n_groups]
    biases = refs[1 + 2 * n_groups:1 + 4 * n_groups]
    o_ref = refs[1 + 4 * n_groups]
    scale = HEAD_DIM ** -0.5
    hw = SWA_HPG * HEAD_DIM
    for h in range(SWA_HPG):
        parts = []
        for g in range(n_groups):
            col = (g * SWA_HPG + h) * HEAD_DIM
            q = qkv_ref[:, col:col + HEAD_DIM].astype(BF16)
            kn = qkv_ref[:, SWA_WIDTH + col:SWA_WIDTH + col + HEAD_DIM].astype(BF16)
            vn = qkv_ref[:, 2 * SWA_WIDTH + col:2 * SWA_WIDTH + col + HEAD_DIM].astype(BF16)
            k_ref, v_ref = caches[2 * g], caches[2 * g + 1]
            nk = math.prod(k_ref.shape[2:-2])
            if len(k_ref.shape) == 6:
                kc = k_ref[0, 0, :, :, h, :].reshape(nk, HEAD_DIM).astype(BF16)
                vc = v_ref[0, 0, :, :, h, :].reshape(nk, HEAD_DIM).astype(BF16)
            else:
                kc = k_ref[0, 0, :, h, :].astype(BF16)
                vc = v_ref[0, 0, :, h, :].astype(BF16)
            sc = _dot_nt(q, kc) * scale + biases[2 * g][h]
            sn = _dot_nt(q, kn) * scale + biases[2 * g + 1][h]
            m = jnp.maximum(jnp.max(sc, axis=-1, keepdims=True), jnp.max(sn, axis=-1, keepdims=True))
            pc = jnp.exp(sc - m)
            pn = jnp.exp(sn - m)
            den = jnp.sum(pc, axis=-1, keepdims=True) + jnp.sum(pn, axis=-1, keepdims=True)
            o = (_dot(pc.astype(BF16), vc) + _dot(pn.astype(BF16), vn)) / den
            parts.append((o, m + jnp.log(den)))
        mx = functools.reduce(jnp.maximum, [l for _, l in parts])
        ws = [jnp.exp(l - mx) for _, l in parts]
        num = functools.reduce(lambda a, b: a + b, [w * o for w, (o, _) in zip(ws, parts)])
        o_ref[:, h * HEAD_DIM:(h + 1) * HEAD_DIM] = num / functools.reduce(lambda a, b: a + b, ws)


def _swa_sample(proj, caches, layer, rel_bias, n_new):
    rows_total = proj.shape[0]
    nb = rows_total // n_new
    n_groups = len(SWA_GROUPS)
    hw = SWA_HPG * HEAD_DIM
    cache_in, cache_specs, bias_in, bias_specs = [], [], [], []
    for g, (_, dil) in enumerate(SWA_GROUPS):
        depth, _, cache_len, heads, hd = caches[2 * g].shape
        if dil >= 2 * n_new and n_new % 8 == 0:
            n_res = n_new
            shape = (depth, nb, cache_len // dil, dil, heads, hd)
            spec = pl.BlockSpec((1, 1, cache_len // dil, n_res, heads, hd), lambda b: (layer, b, 0, 0, 0, 0))
        else:
            n_res = dil
            shape = (depth, nb, cache_len, heads, hd)
            spec = pl.BlockSpec((1, 1, cache_len, heads, hd), lambda b: (layer, b, 0, 0, 0))
        for c in caches[2 * g:2 * g + 2]:
            cache_in.append(c.reshape(shape))
            cache_specs.append(spec)
        for tab in _sample_bias_tables(rel_bias, g, n_res, n_new, cache_len):
            bias_in.append(tab)
            bias_specs.append(pl.BlockSpec(tab.shape, lambda b: (0, 0, 0)))
    return pl.pallas_call(
        functools.partial(_swa_sample_kernel, n_groups=n_groups),
        grid=(nb,),
        in_specs=[pl.BlockSpec((n_new, 3 * SWA_WIDTH), lambda b: (b, 0))] + cache_specs + bias_specs,
        out_specs=pl.BlockSpec((n_new, hw), lambda b: (b, 0)),
        out_shape=jax.ShapeDtypeStruct((rows_total, hw), F32),
        compiler_params=_params(1), name="swa_sample")(proj, *cache_in, *bias_in)


def _silu(x):
    return x * (1.0 / (1.0 + jnp.exp(-x)))


def _delta_prep_kernel(x_ref, halo_ref, w_ref, o_ref, *, period):
    x = x_ref[...]
    tm = x.shape[0]
    acc = x * w_ref[CONV_WIDTH - 1:CONV_WIDTH, :]
    if period is None:
        halo = jnp.where(pl.program_id(0) > 0, halo_ref[...], 0.0)
        xp = jnp.concatenate([halo, x], axis=0)
        for k in range(1, CONV_WIDTH):
            acc = acc + pltpu.roll(xp, k, 0)[8:, :] * w_ref[CONV_WIDTH - 1 - k:CONV_WIDTH - k, :]
    else:
        hal = halo_ref[...]
        pos = lax.broadcasted_iota(jnp.int32, (tm, 1), 0) % period
        for k in range(1, CONV_WIDTH):
            prev_x = pltpu.roll(x, k, 0)
            prev_h = pltpu.roll(hal, (tm - (CONV_WIDTH - 1 - k)) % tm, 0)
            acc = acc + jnp.where(pos >= k, prev_x, prev_h) * w_ref[CONV_WIDTH - 1 - k:CONV_WIDTH - k, :]
    y = _silu(acc)
    for sec in range(3):
        for h in range(DN_HEADS):
            c = sec * DN_WIDTH + h * HEAD_DIM
            yh = y[:, c:c + HEAD_DIM]
            if sec < 2:
                yh = yh * lax.rsqrt(jnp.sum(yh * yh, axis=-1, keepdims=True) + EPS)
                if sec == 0:
                    yh = yh * (HEAD_DIM ** -0.5)
            o_ref[:, c:c + HEAD_DIM] = yh


def _delta_prep(proj, w_conv, halo, period, tm):
    m = proj.shape[0]
    width = 3 * DN_WIDTH
    cb = B_OFF // width
    assert B_OFF % width == 0 and m % tm == 0
    if period is None:
        halo_arr = proj
        halo_spec = pl.BlockSpec((8, width), lambda i: (jnp.maximum(i * (tm // 8) - 1, 0), cb))
    else:
        halo_arr = halo
        halo_spec = pl.BlockSpec((tm, width), lambda i: (i, 0))
    return pl.pallas_call(
        functools.partial(_delta_prep_kernel, period=period), grid=(m // tm,),
        in_specs=[pl.BlockSpec((tm, width), lambda i: (i, cb)), halo_spec,
                  pl.BlockSpec((CONV_WIDTH, width), lambda i: (0, 0))],
        out_specs=pl.BlockSpec((tm, width), lambda i: (i, 0)),
        out_shape=jax.ShapeDtypeStruct((m, width), F32),
        compiler_params=_params(1), name="delta_prep")(proj, halo_arr, w_conv)


_INV_DOT = _dot3


def _softplus(x):
    return jnp.maximum(x, 0.0) + jnp.log(1.0 + jnp.exp(-jnp.abs(x)))


def _delta_scan_kernel(*refs, chunk, carry):
    if carry:
        qkv_ref, z_ref, ba_ref, alog_ref, dt_ref, gain_ref, o_ref, s_out_ref, s_scr = refs
        s_in_ref = None
    else:
        qkv_ref, z_ref, ba_ref, alog_ref, dt_ref, gain_ref, s_in_ref, o_ref, s_out_ref = refs
        s_scr = None
    c = chunk
    if carry:
        @pl.when(pl.program_id(0) == 0)
        def _():
            s_scr[...] = jnp.zeros_like(s_scr)

    ba = ba_ref[...]
    beta_all = 1.0 / (1.0 + jnp.exp(-ba))
    g_all = -jnp.exp(alog_ref[...]) * _softplus(ba + dt_ref[...])
    row = lax.broadcasted_iota(jnp.int32, (c, c), 0)
    col = lax.broadcasted_iota(jnp.int32, (c, c), 1)
    tri = row >= col
    strict = row > col
    tril = tri.astype(BF16)
    g1 = g_all.astype(BF16)
    r1 = g_all - g1.astype(F32)
    g2 = r1.astype(BF16)
    g3 = (r1 - g2.astype(F32)).astype(BF16)
    gcum_all = _dot(tril, g1) + (_dot(tril, g2) + _dot(tril, g3))
    gcum_t = gcum_all.T
    eye = (row == col).astype(F32)

    heads = range(DN_HEADS)
    hs = lambda sec, h: slice(sec * DN_WIDTH + h * HEAD_DIM, sec * DN_WIDTH + (h + 1) * HEAD_DIM)
    lane = lambda h: slice(DN_HEADS + h, DN_HEADS + h + 1)
    q = [qkv_ref[:, hs(0, h)] for h in heads]
    k = [qkv_ref[:, hs(1, h)] for h in heads]
    v = [qkv_ref[:, hs(2, h)] for h in heads]
    bcol = [beta_all[:, h:h + 1] for h in heads]
    gc = [gcum_all[:, lane(h)] for h in heads]
    glast = [gcum_all[c - 1:c, lane(h)] for h in heads]
    decay = [jnp.exp(jnp.where(tri, gc[h] - gcum_t[lane(h), :], NEG)) for h in heads]
    kb = [k[h] * bcol[h] for h in heads]
    k16 = [k[h].astype(BF16) for h in heads]
    nmat = [jnp.where(strict, _dot_nt(kb[h].astype(BF16), k16[h]) * decay[h], 0.0) for h in heads]
    attn = [jnp.where(tri, _dot_nt(q[h].astype(BF16), k16[h]) * decay[h], 0.0) for h in heads]
    inv = [eye] * DN_HEADS
    blk = 1
    while blk < c:
        pair = (row // (2 * blk) == col // (2 * blk)) & (row % (2 * blk) >= blk) & (col % (2 * blk) < blk)
        low = [jnp.where(pair, nmat[h], 0.0) for h in heads]
        if blk == 1:
            inv = [eye - low[h] for h in heads]
        else:
            t = [_INV_DOT(low[h], inv[h]) for h in heads]
            inv = [inv[h] - _INV_DOT(inv[h], t[h]) for h in heads]
        blk *= 2
    eg = [jnp.exp(gc[h]) for h in heads]
    rhs = [jnp.concatenate([v[h] * bcol[h], kb[h] * eg[h]], axis=1) for h in heads]
    sol = [_INV_DOT(inv[h], rhs[h]) for h in heads]
    s_prev = [s_scr[h] if carry else s_in_ref[0, h] for h in heads]
    s16 = [s_prev[h].astype(BF16) for h in heads]
    wq = [jnp.concatenate([sol[h][:, HEAD_DIM:], q[h] * eg[h]], axis=0).astype(BF16) for h in heads]
    ws = [_dot(wq[h], s16[h]) for h in heads]
    v_new = [(sol[h][:, :HEAD_DIM] - ws[h][:c]).astype(BF16) for h in heads]
    o = [ws[h][c:] + _dot(attn[h].astype(BF16), v_new[h]) for h in heads]
    kdec = [(k[h] * jnp.exp(glast[h] - gc[h])).astype(BF16) for h in heads]
    s_new = [s_prev[h] * jnp.exp(glast[h]) + _dot_tn(kdec[h], v_new[h]) for h in heads]
    for h in heads:
        if carry:
            s_scr[h] = s_new[h]
            s_out_ref[h] = s_new[h]
        else:
            s_out_ref[0, h] = s_new[h]
        y = o[h] * lax.rsqrt(jnp.mean(o[h] * o[h], axis=-1, keepdims=True) + EPS) * gain_ref[...]
        o_ref[:, hs(0, h)] = (y * _silu(z_ref[:, hs(0, h)])).astype(o_ref.dtype)


def _delta_scan(qkv, proj, ba, a_log, dt_bias, norm_delta, s0, chunk):
    m = qkv.shape[0]
    carry = s0 is None
    n = m // chunk
    zb = Z_OFF // DN_WIDTH
    pad = lambda v_, off: jnp.zeros((1, LANES), F32).at[0, off:off + DN_HEADS].set(v_.astype(F32))
    consts = [pad(a_log, DN_HEADS), pad(dt_bias, DN_HEADS), norm_delta.reshape(1, HEAD_DIM).astype(F32)]
    const_specs = [pl.BlockSpec((1, LANES), lambda i: (0, 0))] * 3
    in_specs = [pl.BlockSpec((chunk, 3 * DN_WIDTH), lambda i: (i, 0)),
                pl.BlockSpec((chunk, DN_WIDTH), lambda i: (i, zb)),
                pl.BlockSpec((chunk, LANES), lambda i: (i, 0))] + const_specs
    args = [qkv, proj, ba] + consts
    state = (DN_HEADS, HEAD_DIM, HEAD_DIM)
    if carry:
        out_dtype = BF16
        s_shape, s_spec = state, pl.BlockSpec(state, lambda i: (0, 0, 0))
        scratch = [pltpu.VMEM(state, F32)]
    else:
        out_dtype = F32
        in_specs.append(pl.BlockSpec((1,) + state, lambda i: (i, 0, 0, 0)))
        args.append(s0)
        s_shape, s_spec = (n,) + state, pl.BlockSpec((1,) + state, lambda i: (i, 0, 0, 0))
        scratch = []
    return pl.pallas_call(
        functools.partial(_delta_scan_kernel, chunk=chunk, carry=carry), grid=(n,),
        in_specs=in_specs,
        out_specs=[pl.BlockSpec((chunk, DN_WIDTH), lambda i: (i, 0)), s_spec],
        out_shape=[jax.ShapeDtypeStruct((m, DN_WIDTH), out_dtype), jax.ShapeDtypeStruct(s_shape, F32)],
        scratch_shapes=scratch,
        compiler_params=_params(1), name="delta_scan")(*args)


def _mem_attn_kernel(q_ref, k_ref, v_ref, o_ref):
    scale = MEM_HEAD_DIM ** -0.5
    for h in range(MEM_HEADS):
        sl = slice(h * MEM_HEAD_DIM, (h + 1) * MEM_HEAD_DIM)
        s = _dot_nt(q_ref[:, sl].astype(BF16), k_ref[0, 0, :, h, :].astype(BF16)) * scale
        p = jnp.exp(s - jnp.max(s, axis=-1, keepdims=True))
        den = jnp.sum(p, axis=-1, keepdims=True)
        o_ref[:, sl] = (_dot(p.astype(BF16), v_ref[0, 0, :, h, :].astype(BF16)) / den).astype(o_ref.dtype)


def _mem_attn(tail, mem_k, mem_v, layer, tm, out_dtype):
    m = tail.shape[0]
    _, nb, tokens, heads, hd = mem_k.shape
    per = m // nb
    kv_spec = pl.BlockSpec((1, 1, tokens, heads, hd), lambda i: (layer, i * tm // per, 0, 0, 0))
    return pl.pallas_call(
        _mem_attn_kernel, grid=(m // tm,),
        in_specs=[pl.BlockSpec((tm, MEM_WIDTH), lambda i: (i, 0)), kv_spec, kv_spec],
        out_specs=pl.BlockSpec((tm, MEM_WIDTH), lambda i: (i, 0)),
        out_shape=jax.ShapeDtypeStruct((m, MEM_WIDTH), out_dtype),
        compiler_params=_params(1), name="mem_attn")(tail, mem_k, mem_v)


def _merge_kernel(a_ref, b_ref, c_ref, ga_ref, gb_ref, gc_ref, wa_ref, wb_ref, wc_ref, o_ref):
    def branch(x_ref, g_ref, w_ref):
        y = _dot(x_ref[...].astype(BF16), w_ref[...].astype(BF16))
        return y * (1.0 / (1.0 + jnp.exp(-g_ref[...])))

    o_ref[...] = (branch(a_ref, ga_ref, wa_ref) + branch(b_ref, gb_ref, wb_ref)
                  + branch(c_ref, gc_ref, wc_ref)).astype(o_ref.dtype)


def _merge(a, b, c, tail, w_a, w_b, w_c, tm, tn):
    m = a.shape[0]
    d = w_a.shape[1]
    gate0 = (GATE_OFF - CQ_OFF) // tn
    assert (GATE_OFF - CQ_OFF) % tn == 0 and d % tn == 0
    row = lambda x: pl.BlockSpec((tm, x.shape[1]), lambda i, j: (i, 0))
    gate = lambda br: pl.BlockSpec((tm, tn), lambda i, j, br=br: (i, gate0 + br * (d // tn) + j))
    wsp = lambda w: pl.BlockSpec((w.shape[0], tn), lambda i, j: (0, j))
    return pl.pallas_call(
        _merge_kernel, grid=(m // tm, d // tn),
        in_specs=[row(a), row(b), row(c), gate(0), gate(1), gate(2), wsp(w_a), wsp(w_b), wsp(w_c)],
        out_specs=pl.BlockSpec((tm, tn), lambda i, j: (i, j)),
        out_shape=jax.ShapeDtypeStruct((m, d), BF16),
        compiler_params=_params(2), name="merge")(a, b, c, tail, tail, tail, w_a, w_b, w_c)


def _rms(y, g):
    return y * lax.rsqrt(jnp.mean(y * y, axis=-1, keepdims=True) + EPS) * g


def _proj_norm_residual_kernel(y_ref, w_ref, x_ref, g_post_ref, g_next_ref, o_ref, h_ref, acc_ref, *, tn):
    j = pl.program_id(1)
    acc_ref[:, pl.ds(pl.multiple_of(j * tn, tn), tn)] = _dot(y_ref[...], w_ref[...].astype(BF16))

    @pl.when(j == pl.num_programs(1) - 1)
    def _():
        x1 = x_ref[...] + _rms(acc_ref[...], g_post_ref[...])
        o_ref[...] = x1
        h_ref[...] = _rms(x1, g_next_ref[...]).astype(h_ref.dtype)


def _proj_norm_residual(y, w, x, g_post, g_next, tm, tn):
    m, d = x.shape
    k = y.shape[1]
    row = lambda width: pl.BlockSpec((tm, width), lambda i, j: (i, 0))
    gsp = pl.BlockSpec((1, d), lambda i, j: (0, 0))
    return pl.pallas_call(
        functools.partial(_proj_norm_residual_kernel, tn=tn), grid=(m // tm, d // tn),
        in_specs=[row(k), pl.BlockSpec((k, tn), lambda i, j: (0, j)), row(d), gsp, gsp],
        out_specs=[row(d), row(d)],
        out_shape=[jax.ShapeDtypeStruct((m, d), F32), jax.ShapeDtypeStruct((m, d), BF16)],
        scratch_shapes=[pltpu.VMEM((tm, d), F32)],
        compiler_params=_params(2), name="proj_norm_residual")(y, w, x, g_post.reshape(1, d), g_next.reshape(1, d))


def _ffn_kernel(h_ref, w1_ref, w2_ref, o_ref):
    j = pl.program_id(1)
    a = jnp.square(jnp.maximum(_dot(h_ref[...], w1_ref[...].astype(BF16)), 0.0))
    y = _dot(a.astype(BF16), w2_ref[...].astype(BF16))

    @pl.when(j == 0)
    def _():
        o_ref[...] = y

    @pl.when(j > 0)
    def _():
        o_ref[...] += y


def _ffn(h, w1, w2, tm, tf):
    m, d = h.shape
    f = w1.shape[1]
    return pl.pallas_call(
        _ffn_kernel, grid=(m // tm, f // tf),
        in_specs=[pl.BlockSpec((tm, d), lambda i, j: (i, 0)), pl.BlockSpec((d, tf), lambda i, j: (0, j)),
                  pl.BlockSpec((tf, d), lambda i, j: (j, 0))],
        out_specs=pl.BlockSpec((tm, d), lambda i, j: (i, 0)),
        out_shape=jax.ShapeDtypeStruct((m, d), F32),
        compiler_params=_params(2), name="ffn")(h, w1, w2)


def _norm_residual_kernel(x_ref, y_ref, g_ref, o_ref):
    o_ref[...] = x_ref[...] + _rms(y_ref[...], g_ref[...])


def _norm_residual(x, y, g, tm):
    m, d = x.shape
    row = pl.BlockSpec((tm, d), lambda i: (i, 0))
    return pl.pallas_call(
        _norm_residual_kernel, grid=(m // tm,),
        in_specs=[row, row, pl.BlockSpec((1, d), lambda i: (0, 0))], out_specs=row,
        out_shape=jax.ShapeDtypeStruct((m, d), F32),
        compiler_params=_params(1), name="norm_residual")(x, y, g.reshape(1, d))


def _row_tile(m, cap):
    t = min(m, cap)
    assert m % t == 0
    return t


def _layer(x, lw, layer, rel_bias, bias_prompt, mem_k, mem_v, mem_layer, swa_caches, conv_buf, s0, n_seq):
    (w_in_t, w_conv, a_log, dt_bias, norm_delta, w_o_swa, w_o_delta, w_o_mem, w_out,
     norm_pre_mix, norm_post_mix, norm_pre_ffn, norm_post_ffn, w_ff1, w_ff2) = lw
    m, d = x.shape
    seq = m // n_seq
    prompt = swa_caches is None
    tm = _row_tile(m, 1024)
    h = _rmsnorm_cast(x, norm_pre_mix, _row_tile(m, 512))
    main = _matmul_nt(h, w_in_t, 0, BA_OFF, tm, 512, "in_proj_main")
    tail = _matmul_nt(h, w_in_t, CQ_OFF, w_in_t.shape[0] - CQ_OFF, tm, 512, "in_proj_tail")
    ba = _matmul_nt(h, w_in_t, BA_OFF, LANES, tm, LANES, "in_proj_ba")
    if prompt:
        assert n_seq == 1
        a_out = _swa_prompt(main, bias_prompt)
        qkv = _delta_prep(main, w_conv, None, None, _row_tile(m, 256))
        chunk = DN_CHUNK if seq % DN_CHUNK == 0 else seq
        b_out, s_new = _delta_scan(qkv, main, ba, a_log, dt_bias, norm_delta, None, chunk)
        c_out = _mem_attn(tail, mem_k, mem_v, mem_layer, _row_tile(m, 512), BF16)
    else:
        a_out = _swa_sample(main, swa_caches, layer, rel_bias, seq)
        halo = jnp.pad(conv_buf, ((0, 0), (0, seq - (CONV_WIDTH - 1)), (0, 0))).reshape(m, 3 * DN_WIDTH)
        qkv = _delta_prep(main, w_conv, halo, seq, _row_tile(m, 256))
        b_out, s_new = _delta_scan(qkv, main, ba, a_log, dt_bias, norm_delta, s0, seq)
        c_out = _mem_attn(tail, mem_k, mem_v, mem_layer, seq, F32)
    merged = _merge(a_out, b_out, c_out, tail, w_o_swa, w_o_delta, w_o_mem, tm, 512)
    x1, h2 = _proj_norm_residual(merged, w_out, x, norm_post_mix, norm_pre_ffn, _row_tile(m, 512), 512)
    f = _ffn(h2, w_ff1, w_ff2, tm, 512)
    x2 = _norm_residual(x1, f, norm_post_ffn, _row_tile(m, 512))
    return x2, main, s_new


def kernel(x_prompt, x_sample, cache_swa0_k, cache_swa0_v, cache_swa1_k, cache_swa1_v, cache_swa2_k, cache_swa2_v, state_delta, state_conv, cache_mem_k, cache_mem_v, mem_prompt, rel_bias, w_in, w_conv, A_log, dt_bias, norm_delta, norm_mem, w_mem_kv, w_o_swa, w_o_delta, w_o_mem, w_out, norm_pre_mix, norm_post_mix, norm_pre_ffn, norm_post_ffn, w_ff1, w_ff2):
    depth = w_in.shape[0]
    bp, tp, d = x_prompt.shape
    bs, ts, _ = x_sample.shape
    assert bp == 1 and ts > CONV_WIDTH - 1 and ts % 8 == 0
    sample_swa = (cache_swa0_k, cache_swa0_v, cache_swa1_k, cache_swa1_v, cache_swa2_k, cache_swa2_v)
    bias_prompt = _prompt_bias_tables(rel_bias)
    xp = x_prompt.reshape(bp * tp, d)
    xs = x_sample.reshape(bs * ts, d)
    new_p = [[] for _ in range(10)]
    new_s = [[] for _ in range(8)]
    hw = SWA_HPG * HEAD_DIM
    for l in range(depth):
        lw = (jnp.swapaxes(w_in, 1, 2)[l], w_conv[l], A_log[l], dt_bias[l], norm_delta[l], w_o_swa[l], w_o_delta[l],
              w_o_mem[l], w_out[l], norm_pre_mix[l], norm_post_mix[l], norm_pre_ffn[l], norm_post_ffn[l],
              w_ff1[l], w_ff2[l])
        mem = mem_prompt.reshape(-1, d)
        mkv = _matmul(_rmsnorm_cast(mem, norm_mem[l], _row_tile(mem.shape[0], 256)), w_mem_kv[l],
                      2 * MEM_WIDTH, _row_tile(mem.shape[0], 256), 512, "mem_kv")
        mk = mkv[:, :MEM_WIDTH].reshape(1, bp, -1, MEM_HEADS, MEM_HEAD_DIM)
        mv = mkv[:, MEM_WIDTH:].reshape(1, bp, -1, MEM_HEADS, MEM_HEAD_DIM)
        xp, main_p, s_p = _layer(xp, lw, l, rel_bias, bias_prompt, mk, mv, 0, None, None, None, bp)
        xs, main_s, s_s = _layer(xs, lw, l, rel_bias, bias_prompt, cache_mem_k, cache_mem_v, l,
                                 sample_swa, state_conv[l], state_delta[l], bs)
        vals_p = []
        for g, (window, _) in enumerate(SWA_GROUPS):
            keep = min(window, tp)
            for sec in (1, 2):
                c0 = sec * SWA_WIDTH + g * hw
                vals_p.append(main_p[tp - keep:, c0:c0 + hw].reshape(bp, keep, SWA_HPG, HEAD_DIM))
        vals_p.append(s_p.reshape(bp, DN_HEADS, HEAD_DIM, HEAD_DIM))
        vals_p.append(main_p[tp - (CONV_WIDTH - 1):, B_OFF:B_OFF + 3 * DN_WIDTH].reshape(bp, CONV_WIDTH - 1, -1))
        vals_p += [mk[0], mv[0]]
        vals_s = []
        main_s3 = main_s.reshape(bs, ts, -1)
        for g in range(len(SWA_GROUPS)):
            for sec in (1, 2):
                c0 = sec * SWA_WIDTH + g * hw
                vals_s.append(main_s3[:, :, c0:c0 + hw].reshape(bs, ts, SWA_HPG, HEAD_DIM))
        vals_s.append(s_s)
        vals_s.append(main_s3[:, ts - (CONV_WIDTH - 1):, B_OFF:B_OFF + 3 * DN_WIDTH])
        for lst, val in zip(new_p, vals_p):
            lst.append(val)
        for lst, val in zip(new_s, vals_s):
            lst.append(val)
    outs_p = [jnp.stack(t) for t in new_p]
    outs_s = [jnp.stack(t) for t in new_s]
    return (xp.reshape(bp, tp, d), xs.reshape(bs, ts, d), *outs_p, *outs_s)
```

```python
import functools
import math

import numpy as np
import jax
import jax.numpy as jnp
from jax import lax
from jax.experimental import pallas as pl
from jax.experimental.pallas import tpu as pltpu

EPS = 1e-6
HEAD_DIM = 128
SWA_GROUPS = ((128, 1), (512, 4), (2048, 16))
SWA_SPAN = 128
SWA_HPG = 4
SWA_WIDTH = SWA_HPG * len(SWA_GROUPS) * HEAD_DIM
DN_HEADS = 12
DN_WIDTH = DN_HEADS * HEAD_DIM
CONV_WIDTH = 4
DN_CHUNK = 64
MEM_HEADS = 4
MEM_HEAD_DIM = 256
MEM_WIDTH = MEM_HEADS * MEM_HEAD_DIM
N_BUCKETS = 32
MAX_DISTANCE = 2048
N_BRANCHES = 3

A_OFF = 0
B_OFF = 3 * SWA_WIDTH
Z_OFF = B_OFF + 3 * DN_WIDTH
BA_OFF = Z_OFF + DN_WIDTH
CQ_OFF = BA_OFF + 2 * DN_HEADS
GATE_OFF = CQ_OFF + MEM_WIDTH
LANES = 128
SWA_ROWS = 2048
SWA_UNROLL = 4
NEG = -1e30
VMEM_LIMIT = 56 * 1024 * 1024

BF16 = jnp.bfloat16
F32 = jnp.float32


def _params(n_grid):
    return pltpu.CompilerParams(dimension_semantics=("arbitrary",) * n_grid, vmem_limit_bytes=VMEM_LIMIT)


def _dot(a, b):
    return jnp.dot(a, b, preferred_element_type=F32)


def _dot_nt(a, b):
    return lax.dot_general(a, b, (((1,), (1,)), ((), ())), preferred_element_type=F32)


def _dot_tn(a, b):
    return lax.dot_general(a, b, (((0,), (0,)), ((), ())), preferred_element_type=F32)


def _split2(x):
    hi = x.astype(BF16)
    lo = (x - hi.astype(F32)).astype(BF16)
    return hi, lo


def _dot3(a, b, dot=_dot):
    ah, al = _split2(a)
    bh, bl = _split2(b)
    return dot(ah, bh) + (dot(ah, bl) + dot(al, bh))


def _rmsnorm_cast_kernel(x_ref, g_ref, o_ref):
    x = x_ref[...]
    y = x * lax.rsqrt(jnp.mean(x * x, axis=-1, keepdims=True) + EPS)
    o_ref[...] = (y * g_ref[...]).astype(o_ref.dtype)


def _rmsnorm_cast(x, g, tm):
    m, d = x.shape
    return pl.pallas_call(
        _rmsnorm_cast_kernel, grid=(m // tm,),
        in_specs=[pl.BlockSpec((tm, d), lambda i: (i, 0)), pl.BlockSpec((1, d), lambda i: (0, 0))],
        out_specs=pl.BlockSpec((tm, d), lambda i: (i, 0)),
        out_shape=jax.ShapeDtypeStruct((m, d), BF16),
        compiler_params=_params(1), name="rmsnorm_cast")(x, g.reshape(1, d))


def _matmul_kernel(x_ref, w_ref, o_ref):
    o_ref[...] = _dot(x_ref[...], w_ref[...].astype(BF16)).astype(o_ref.dtype)


def _matmul(x, w, n, tm, tn, name):
    m, k = x.shape
    assert n % tn == 0 and m % tm == 0
    return pl.pallas_call(
        _matmul_kernel, grid=(m // tm, n // tn),
        in_specs=[pl.BlockSpec((tm, k), lambda i, j: (i, 0)), pl.BlockSpec((k, tn), lambda i, j: (0, j))],
        out_specs=pl.BlockSpec((tm, tn), lambda i, j: (i, j)),
        out_shape=jax.ShapeDtypeStruct((m, n), F32),
        compiler_params=_params(2), name=name)(x, w)


SUBLANES = 8


def _matmul_nt_kernel(x_ref, w_ref, o_ref, w16_ref):
    @pl.when(pl.program_id(1) == 0)
    def _():
        w16_ref[...] = w_ref[...].astype(BF16)

    o_ref[...] = _dot_nt(x_ref[...], w16_ref[...]).astype(o_ref.dtype)


def _matmul_nt(x, wt, n_off, n, tm, tn, name):
    m, k = x.shape
    assert n_off % SUBLANES == 0 and n % tn == 0 and m % tm == 0
    return pl.pallas_call(
        _matmul_nt_kernel, grid=(n // tn, m // tm),
        in_specs=[pl.BlockSpec((tm, k), lambda j, i: (i, 0)),
                  pl.BlockSpec((pl.Element(tn), pl.Element(k)),
                               lambda j, i: (pl.multiple_of(n_off + j * tn, SUBLANES), 0))],
        out_specs=pl.BlockSpec((tm, tn), lambda j, i: (i, j)),
        out_shape=jax.ShapeDtypeStruct((m, n), F32),
        scratch_shapes=[pltpu.VMEM((tn, k), BF16)],
        compiler_params=_params(2), name=name)(x, wt)


def _t5_bucket(dist):
    max_exact = N_BUCKETS // 2
    df = jnp.maximum(dist, 1).astype(F32)
    large = max_exact + (jnp.log(df / max_exact) / math.log(MAX_DISTANCE / max_exact)
                         * (N_BUCKETS - max_exact)).astype(jnp.int32)
    return jnp.where(dist < max_exact, dist, jnp.minimum(large, N_BUCKETS - 1))


def _group_bias(rel_bias, g):
    _, dil = SWA_GROUPS[g]
    dist = jnp.arange(SWA_SPAN + 1, dtype=jnp.int32) * dil
    onehot = _t5_bucket(dist)[None, :, None] == jnp.arange(N_BUCKETS, dtype=jnp.int32)[None, None, :]
    heads = jnp.transpose(rel_bias[:, g * SWA_HPG:(g + 1) * SWA_HPG].astype(F32))
    return jnp.sum(jnp.where(onehot, heads[:, None, :], 0.0), axis=-1)


def _prompt_bias_tables(rel_bias):
    span = SWA_SPAN
    first = (np.arange(2 * span) >= span)[None, None, :]
    out = []
    for g in range(len(SWA_GROUPS)):
        w = _group_bias(rel_bias, g)
        h = w.shape[0]
        p = 3 * span
        e = jnp.concatenate([jnp.full((h, span - 1), NEG, F32), w[:, ::-1], jnp.full((h, p - 2 * span), NEG, F32)], 1)
        skew = jnp.broadcast_to(e[:, None, :], (h, span, p)).reshape(h, span * p)[:, :span * (p - 1)]
        tab = skew.reshape(h, span, p - 1)[:, :, span - 1:3 * span - 1]
        out.append(jnp.stack([jnp.where(first, tab, NEG), tab], axis=1))
    return jnp.stack(out)


def _sample_bias_tables(rel_bias, g, n_res, n_new, cache_len):
    _, dil = SWA_GROUPS[g]
    w = _group_bias(rel_bias, g)
    h = w.shape[0]
    lm = cache_len // dil
    assert cache_len % dil == 0
    wpad = jnp.concatenate([w, jnp.full((h, lm + n_new), NEG, F32)], axis=1)
    res = np.arange(n_res)[None, None, :]
    rows = []
    for s in range(n_new):
        vec = wpad[:, s // dil + 1:lm + s // dil + 1][:, ::-1]
        rows.append(jnp.where(res == s % dil, vec[:, :, None], NEG).reshape(h, lm * n_res))
    s = np.arange(n_new)
    dist = s[:, None] - s[None, :]
    new = jnp.full((h, n_new, n_new), NEG, F32)
    for j in range(min((n_new - 1) // dil, SWA_SPAN) + 1):
        new = jnp.where((dist == j * dil)[None], w[:, j][:, None, None], new)
    return jnp.stack(rows, axis=1), new


def _swa_prompt_kernel(*refs, n_groups):
    ins = refs[:5 * n_groups]
    bias_ref = refs[5 * n_groups]
    o_ref = refs[5 * n_groups + 1]
    scr = refs[5 * n_groups + 2:]
    kext, vext, og, lg = scr[:n_groups], scr[n_groups:2 * n_groups], scr[2 * n_groups:3 * n_groups], scr[3 * n_groups:]
    n = pl.program_id(0)
    scale = HEAD_DIM ** -0.5
    span = SWA_SPAN
    for g in range(n_groups):
        _, dil = SWA_GROUPS[g]
        q_ref, kc_ref, vc_ref, kp_ref, vp_ref = ins[5 * g:5 * g + 5]
        blk = span * dil
        kext[g][0:blk, :] = kp_ref[...]
        kext[g][blk:, :] = kc_ref[...]
        vext[g][0:blk, :] = vp_ref[...]
        vext[g][blk:, :] = vc_ref[...]

        def body(it, carry, g=g, dil=dil, blk=blk, q_ref=q_ref):
            items = range(SWA_UNROLL)
            starts, sels = [], []
            for u in items:
                idx = it * SWA_UNROLL + u
                b = idx // dil
                starts.append(b * blk + (idx - b * dil))
                sels.append(jnp.where(jnp.logical_and(n == 0, b == 0), 0, 1))
            q = [q_ref[pl.ds(starts[u], span, stride=dil), :].astype(BF16) for u in items]
            kk = [kext[g][pl.ds(starts[u], 2 * span, stride=dil), :].astype(BF16) for u in items]
            s = [_dot_nt(q[u], kk[u]) * scale + bias_ref[g, 0, sels[u]] for u in items]
            m = [jnp.max(s[u], axis=-1, keepdims=True) for u in items]
            p = [jnp.exp(s[u] - m[u]) for u in items]
            den = [jnp.sum(p[u], axis=-1, keepdims=True) for u in items]
            vv = [vext[g][pl.ds(starts[u], 2 * span, stride=dil), :].astype(BF16) for u in items]
            o = [_dot(p[u].astype(BF16), vv[u]) / den[u] for u in items]
            for u in items:
                og[g][pl.ds(starts[u], span, stride=dil), :] = o[u]
                lg[g][pl.ds(starts[u], span, stride=dil), :] = jnp.broadcast_to(
                    m[u] + jnp.log(den[u]), (span, HEAD_DIM))
            return carry

        lax.fori_loop(0, SWA_ROWS // span // SWA_UNROLL, body, 0)
    lses = [lg[g][...] for g in range(n_groups)]
    mx = functools.reduce(jnp.maximum, lses)
    ws = [jnp.exp(l - mx) for l in lses]
    num = functools.reduce(lambda a, b: a + b, [w * og[g][...] for g, w in enumerate(ws)])
    o_ref[...] = (num / functools.reduce(lambda a, b: a + b, ws)).astype(o_ref.dtype)


def _swa_prompt(proj, bias_tables):
    t = proj.shape[0]
    n_groups = len(SWA_GROUPS)
    assert t % SWA_ROWS == 0
    in_specs, scratch_k, scratch_o = [], [], []
    for g, (_, dil) in enumerate(SWA_GROUPS):
        blk = SWA_SPAN * dil
        per = SWA_ROWS // blk
        assert SWA_ROWS % blk == 0
        qc, kc, vc = (sec * (SWA_WIDTH // HEAD_DIM) + g * SWA_HPG for sec in range(3))
        cur = lambda col: pl.BlockSpec((SWA_ROWS, HEAD_DIM), lambda n, h, col=col: (n, col + h))
        prev = lambda col, blk=blk, per=per: pl.BlockSpec(
            (blk, HEAD_DIM), lambda n, h, col=col, per=per: (jnp.maximum(n * per - 1, 0), col + h))
        in_specs += [cur(qc), cur(kc), cur(vc), prev(kc), prev(vc)]
        scratch_k.append(pltpu.VMEM((SWA_ROWS + blk, HEAD_DIM), F32))
        scratch_o.append(pltpu.VMEM((SWA_ROWS, HEAD_DIM), F32))
    in_specs.append(pl.BlockSpec((n_groups, 1, 2, SWA_SPAN, 2 * SWA_SPAN), lambda n, h: (0, h, 0, 0, 0)))
    return pl.pallas_call(
        functools.partial(_swa_prompt_kernel, n_groups=n_groups),
        grid=(t // SWA_ROWS, SWA_HPG),
        in_specs=in_specs,
        out_specs=pl.BlockSpec((SWA_ROWS, HEAD_DIM), lambda n, h: (n, h)),
        out_shape=jax.ShapeDtypeStruct((t, SWA_HPG * HEAD_DIM), BF16),
        scratch_shapes=scratch_k + scratch_k + scratch_o + scratch_o,
        compiler_params=_params(2), name="swa_prompt")(*([proj] * (5 * n_groups)), bias_tables)


def _swa_sample_kernel(*refs, n_groups):
    qkv_ref = refs[0]
    caches = refs[1:1 + 2 * n_groups]
    biases = refs[1 + 2 * n_groups:1 + 4 * n_groups]
    o_ref = refs[1 + 4 * n_groups]
    scale = HEAD_DIM ** -0.5
    hw = SWA_HPG * HEAD_DIM
    items = [(g, h) for h in range(SWA_HPG) for g in range(n_groups)]
    idx = range(len(items))

    def new(sec, g, h):
        col = sec * SWA_WIDTH + (g * SWA_HPG + h) * HEAD_DIM
        return qkv_ref[:, col:col + HEAD_DIM].astype(BF16)

    def cached(ref, h):
        nk = math.prod(ref.shape[2:-1]) // SWA_HPG
        if len(ref.shape) == 5:
            pick = (0, 0, slice(None), pl.ds(h, ref.shape[3] // SWA_HPG, stride=SWA_HPG), slice(None))
        else:
            pick = (0, 0, pl.ds(h, nk, stride=SWA_HPG), slice(None))
        return ref[pick].reshape(nk, HEAD_DIM).astype(BF16)

    q = [new(0, g, h) for g, h in items]
    sc = [_dot_nt(q[i], cached(caches[2 * g], h)) * scale + biases[2 * g][h] for i, (g, h) in enumerate(items)]
    sn = [_dot_nt(q[i], new(1, g, h)) * scale + biases[2 * g + 1][h] for i, (g, h) in enumerate(items)]
    m = [jnp.maximum(jnp.max(sc[i], axis=-1, keepdims=True), jnp.max(sn[i], axis=-1, keepdims=True)) for i in idx]
    pc = [jnp.exp(sc[i] - m[i]) for i in idx]
    pn = [jnp.exp(sn[i] - m[i]) for i in idx]
    den = [jnp.sum(pc[i], axis=-1, keepdims=True) + jnp.sum(pn[i], axis=-1, keepdims=True) for i in idx]
    o = [(_dot(pc[i].astype(BF16), cached(caches[2 * g + 1], h)) + _dot(pn[i].astype(BF16), new(2, g, h))) / den[i]
         for i, (g, h) in enumerate(items)]
    lse = [m[i] + jnp.log(den[i]) for i in idx]
    for h in range(SWA_HPG):
        mine = [i for i in idx if items[i][1] == h]
        mx = functools.reduce(jnp.maximum, [lse[i] for i in mine])
        ws = [jnp.exp(lse[i] - mx) for i in mine]
        num = functools.reduce(lambda a, b: a + b, [w * o[i] for w, i in zip(ws, mine)])
        o_ref[:, h * HEAD_DIM:(h + 1) * HEAD_DIM] = num / functools.reduce(lambda a, b: a + b, ws)


def _swa_sample(proj, caches, layer, rel_bias, n_new):
    rows_total = proj.shape[0]
    nb = rows_total // n_new
    n_groups = len(SWA_GROUPS)
    hw = SWA_HPG * HEAD_DIM
    cache_in, cache_specs, bias_in, bias_specs = [], [], [], []
    for g, (_, dil) in enumerate(SWA_GROUPS):
        depth, _, cache_len, heads, hd = caches[2 * g].shape
        if dil >= 2 * n_new and n_new % 8 == 0:
            n_res = n_new
            shape = (depth, nb, cache_len // dil, dil * heads, hd)
            spec = pl.BlockSpec((1, 1, cache_len // dil, n_res * heads, hd), lambda b: (layer, b, 0, 0, 0))
        else:
            n_res = dil
            shape = (depth, nb, cache_len * heads, hd)
            spec = pl.BlockSpec((1, 1, cache_len * heads, hd), lambda b: (layer, b, 0, 0))
        for c in caches[2 * g:2 * g + 2]:
            cache_in.append(c.reshape(shape))
            cache_specs.append(spec)
        for tab in _sample_bias_tables(rel_bias, g, n_res, n_new, cache_len):
            bias_in.append(tab)
            bias_specs.append(pl.BlockSpec(tab.shape, lambda b: (0, 0, 0)))
    return pl.pallas_call(
        functools.partial(_swa_sample_kernel, n_groups=n_groups),
        grid=(nb,),
        in_specs=[pl.BlockSpec((n_new, 3 * SWA_WIDTH), lambda b: (b, 0))] + cache_specs + bias_specs,
        out_specs=pl.BlockSpec((n_new, hw), lambda b: (b, 0)),
        out_shape=jax.ShapeDtypeStruct((rows_total, hw), F32),
        compiler_params=_params(1), name="swa_sample")(proj, *cache_in, *bias_in)


def _silu(x):
    return x * (1.0 / (1.0 + jnp.exp(-x)))


def _delta_prep_kernel(x_ref, halo_ref, w_ref, o_ref, *, period):
    x = x_ref[...]
    tm = x.shape[0]
    acc = x * w_ref[CONV_WIDTH - 1:CONV_WIDTH, :]
    if period is None:
        halo = jnp.where(pl.program_id(0) > 0, halo_ref[...], 0.0)
        xp = jnp.concatenate([halo, x], axis=0)
        for k in range(1, CONV_WIDTH):
            acc = acc + pltpu.roll(xp, k, 0)[8:, :] * w_ref[CONV_WIDTH - 1 - k:CONV_WIDTH - k, :]
    else:
        hal = halo_ref[...]
        pos = lax.broadcasted_iota(jnp.int32, (tm, 1), 0) % period
        for k in range(1, CONV_WIDTH):
            prev_x = pltpu.roll(x, k, 0)
            prev_h = pltpu.roll(hal, (tm - (CONV_WIDTH - 1 - k)) % tm, 0)
            acc = acc + jnp.where(pos >= k, prev_x, prev_h) * w_ref[CONV_WIDTH - 1 - k:CONV_WIDTH - k, :]
    y = _silu(acc)
    for sec in range(3):
        for h in range(DN_HEADS):
            c = sec * DN_WIDTH + h * HEAD_DIM
            yh = y[:, c:c + HEAD_DIM]
            if sec < 2:
                yh = yh * lax.rsqrt(jnp.sum(yh * yh, axis=-1, keepdims=True) + EPS)
                if sec == 0:
                    yh = yh * (HEAD_DIM ** -0.5)
            o_ref[:, c:c + HEAD_DIM] = yh


def _delta_prep(proj, w_conv, halo, period, tm):
    m = proj.shape[0]
    width = 3 * DN_WIDTH
    cb = B_OFF // width
    assert B_OFF % width == 0 and m % tm == 0
    if period is None:
        halo_arr = proj
        halo_spec = pl.BlockSpec((8, width), lambda i: (jnp.maximum(i * (tm // 8) - 1, 0), cb))
    else:
        halo_arr = halo
        halo_spec = pl.BlockSpec((tm, width), lambda i: (i, 0))
    return pl.pallas_call(
        functools.partial(_delta_prep_kernel, period=period), grid=(m // tm,),
        in_specs=[pl.BlockSpec((tm, width), lambda i: (i, cb)), halo_spec,
                  pl.BlockSpec((CONV_WIDTH, width), lambda i: (0, 0))],
        out_specs=pl.BlockSpec((tm, width), lambda i: (i, 0)),
        out_shape=jax.ShapeDtypeStruct((m, width), F32),
        compiler_params=_params(1), name="delta_prep")(proj, halo_arr, w_conv)


_INV_DOT = _dot3


def _softplus(x):
    return jnp.maximum(x, 0.0) + jnp.log(1.0 + jnp.exp(-jnp.abs(x)))


def _delta_scan_kernel(*refs, chunk, carry):
    if carry:
        qkv_ref, z_ref, ba_ref, alog_ref, dt_ref, gain_ref, o_ref, s_out_ref, s_scr = refs
        s_in_ref = None
    else:
        qkv_ref, z_ref, ba_ref, alog_ref, dt_ref, gain_ref, s_in_ref, o_ref, s_out_ref = refs
        s_scr = None
    c = chunk
    if carry:
        @pl.when(pl.program_id(0) == 0)
        def _():
            s_scr[...] = jnp.zeros_like(s_scr)

    ba = ba_ref[...]
    beta_all = 1.0 / (1.0 + jnp.exp(-ba))
    g_all = -jnp.exp(alog_ref[...]) * _softplus(ba + dt_ref[...])
    row = lax.broadcasted_iota(jnp.int32, (c, c), 0)
    col = lax.broadcasted_iota(jnp.int32, (c, c), 1)
    tri = row >= col
    strict = row > col
    tril = tri.astype(BF16)
    g1 = g_all.astype(BF16)
    r1 = g_all - g1.astype(F32)
    g2 = r1.astype(BF16)
    g3 = (r1 - g2.astype(F32)).astype(BF16)
    gcum_all = _dot(tril, g1) + (_dot(tril, g2) + _dot(tril, g3))
    gcum_t = gcum_all.T
    eye = (row == col).astype(F32)

    heads = range(DN_HEADS)
    hs = lambda sec, h: slice(sec * DN_WIDTH + h * HEAD_DIM, sec * DN_WIDTH + (h + 1) * HEAD_DIM)
    lane = lambda h: slice(DN_HEADS + h, DN_HEADS + h + 1)
    q = [qkv_ref[:, hs(0, h)] for h in heads]
    k = [qkv_ref[:, hs(1, h)] for h in heads]
    v = [qkv_ref[:, hs(2, h)] for h in heads]
    bcol = [beta_all[:, h:h + 1] for h in heads]
    gc = [gcum_all[:, lane(h)] for h in heads]
    glast = [gcum_all[c - 1:c, lane(h)] for h in heads]
    decay = [jnp.exp(jnp.where(tri, gc[h] - gcum_t[lane(h), :], NEG)) for h in heads]
    kb = [k[h] * bcol[h] for h in heads]
    k16 = [k[h].astype(BF16) for h in heads]
    nmat = [jnp.where(strict, _dot_nt(kb[h].astype(BF16), k16[h]) * decay[h], 0.0) for h in heads]
    attn = [jnp.where(tri, _dot_nt(q[h].astype(BF16), k16[h]) * decay[h], 0.0) for h in heads]
    inv = [eye] * DN_HEADS
    blk = 1
    while blk < c:
        pair = (row // (2 * blk) == col // (2 * blk)) & (row % (2 * blk) >= blk) & (col % (2 * blk) < blk)
        low = [jnp.where(pair, nmat[h], 0.0) for h in heads]
        if blk == 1:
            inv = [eye - low[h] for h in heads]
        else:
            t = [_INV_DOT(low[h], inv[h]) for h in heads]
            inv = [inv[h] - _INV_DOT(inv[h], t[h]) for h in heads]
        blk *= 2
    eg = [jnp.exp(gc[h]) for h in heads]
    rhs = [jnp.concatenate([v[h] * bcol[h], kb[h] * eg[h]], axis=1) for h in heads]
    sol = [_INV_DOT(inv[h], rhs[h]) for h in heads]
    s_prev = [s_scr[h] if carry else s_in_ref[0, h] for h in heads]
    s16 = [s_prev[h].astype(BF16) for h in heads]
    wq = [jnp.concatenate([sol[h][:, HEAD_DIM:], q[h] * eg[h]], axis=0).astype(BF16) for h in heads]
    ws = [_dot(wq[h], s16[h]) for h in heads]
    v_new = [(sol[h][:, :HEAD_DIM] - ws[h][:c]).astype(BF16) for h in heads]
    o = [ws[h][c:] + _dot(attn[h].astype(BF16), v_new[h]) for h in heads]
    kdec = [(k[h] * jnp.exp(glast[h] - gc[h])).astype(BF16) for h in heads]
    s_new = [s_prev[h] * jnp.exp(glast[h]) + _dot_tn(kdec[h], v_new[h]) for h in heads]
    for h in heads:
        if carry:
            s_scr[h] = s_new[h]
            s_out_ref[h] = s_new[h]
        else:
            s_out_ref[0, h] = s_new[h]
        y = o[h] * lax.rsqrt(jnp.mean(o[h] * o[h], axis=-1, keepdims=True) + EPS) * gain_ref[...]
        o_ref[:, hs(0, h)] = (y * _silu(z_ref[:, hs(0, h)])).astype(o_ref.dtype)


def _delta_scan(qkv, proj, ba, a_log, dt_bias, norm_delta, s0, chunk):
    m = qkv.shape[0]
    carry = s0 is None
    n = m // chunk
    zb = Z_OFF // DN_WIDTH
    pad = lambda v_, off: jnp.zeros((1, LANES), F32).at[0, off:off + DN_HEADS].set(v_.astype(F32))
    consts = [pad(a_log, DN_HEADS), pad(dt_bias, DN_HEADS), norm_delta.reshape(1, HEAD_DIM).astype(F32)]
    const_specs = [pl.BlockSpec((1, LANES), lambda i: (0, 0))] * 3
    in_specs = [pl.BlockSpec((chunk, 3 * DN_WIDTH), lambda i: (i, 0)),
                pl.BlockSpec((chunk, DN_WIDTH), lambda i: (i, zb)),
                pl.BlockSpec((chunk, LANES), lambda i: (i, 0))] + const_specs
    args = [qkv, proj, ba] + consts
    state = (DN_HEADS, HEAD_DIM, HEAD_DIM)
    if carry:
        out_dtype = BF16
        s_shape, s_spec = state, pl.BlockSpec(state, lambda i: (0, 0, 0))
        scratch = [pltpu.VMEM(state, F32)]
    else:
        out_dtype = F32
        in_specs.append(pl.BlockSpec((1,) + state, lambda i: (i, 0, 0, 0)))
        args.append(s0)
        s_shape, s_spec = (n,) + state, pl.BlockSpec((1,) + state, lambda i: (i, 0, 0, 0))
        scratch = []
    return pl.pallas_call(
        functools.partial(_delta_scan_kernel, chunk=chunk, carry=carry), grid=(n,),
        in_specs=in_specs,
        out_specs=[pl.BlockSpec((chunk, DN_WIDTH), lambda i: (i, 0)), s_spec],
        out_shape=[jax.ShapeDtypeStruct((m, DN_WIDTH), out_dtype), jax.ShapeDtypeStruct(s_shape, F32)],
        scratch_shapes=scratch,
        compiler_params=_params(1), name="delta_scan")(*args)


def _mem_attn_kernel(q_ref, k_ref, v_ref, o_ref):
    scale = MEM_HEAD_DIM ** -0.5
    parts = MEM_HEAD_DIM // LANES
    period = MEM_HEADS * parts
    tokens = k_ref.shape[2] // period
    pick = lambda c, h: (0, 0, pl.ds(c * MEM_HEADS + h, tokens, stride=period), slice(None))
    heads = range(MEM_HEADS)
    q = [[q_ref[:, h * MEM_HEAD_DIM + c * LANES:h * MEM_HEAD_DIM + (c + 1) * LANES].astype(BF16)
          for c in range(parts)] for h in heads]
    s = [functools.reduce(lambda a, b: a + b, [_dot_nt(q[h][c], k_ref[pick(c, h)].astype(BF16))
                                               for c in range(parts)]) * scale for h in heads]
    p = [jnp.exp(s[h] - jnp.max(s[h], axis=-1, keepdims=True)) for h in heads]
    den = [jnp.sum(p[h], axis=-1, keepdims=True) for h in heads]
    for h in heads:
        for c in range(parts):
            col = h * MEM_HEAD_DIM + c * LANES
            o_ref[:, col:col + LANES] = (_dot(p[h].astype(BF16), v_ref[pick(c, h)].astype(BF16))
                                         / den[h]).astype(o_ref.dtype)


def _mem_attn(tail, mem_k, mem_v, layer, tm, out_dtype):
    m = tail.shape[0]
    depth, nb, tokens, heads, hd = mem_k.shape
    per = m // nb
    parts = hd // LANES
    mem_k, mem_v = (t.reshape(depth, nb, tokens, heads, parts, LANES).transpose(0, 1, 2, 4, 3, 5)
                    .reshape(depth, nb, tokens * heads * parts, LANES) for t in (mem_k, mem_v))
    kv_spec = pl.BlockSpec((1, 1, tokens * heads * parts, LANES), lambda i: (layer, i * tm // per, 0, 0))
    return pl.pallas_call(
        _mem_attn_kernel, grid=(m // tm,),
        in_specs=[pl.BlockSpec((tm, MEM_WIDTH), lambda i: (i, 0)), kv_spec, kv_spec],
        out_specs=pl.BlockSpec((tm, MEM_WIDTH), lambda i: (i, 0)),
        out_shape=jax.ShapeDtypeStruct((m, MEM_WIDTH), out_dtype),
        compiler_params=_params(1), name="mem_attn")(tail, mem_k, mem_v)


def _merge_kernel(a_ref, b_ref, c_ref, ga_ref, gb_ref, gc_ref, wa_ref, wb_ref, wc_ref, o_ref):
    def branch(x_ref, g_ref, w_ref):
        y = _dot(x_ref[...].astype(BF16), w_ref[...].astype(BF16))
        return y * (1.0 / (1.0 + jnp.exp(-g_ref[...])))

    o_ref[...] = (branch(a_ref, ga_ref, wa_ref) + branch(b_ref, gb_ref, wb_ref)
                  + branch(c_ref, gc_ref, wc_ref)).astype(o_ref.dtype)


def _merge(a, b, c, tail, w_a, w_b, w_c, tm, tn):
    m = a.shape[0]
    d = w_a.shape[1]
    gate0 = (GATE_OFF - CQ_OFF) // tn
    assert (GATE_OFF - CQ_OFF) % tn == 0 and d % tn == 0
    row = lambda x: pl.BlockSpec((tm, x.shape[1]), lambda i, j: (i, 0))
    gate = lambda br: pl.BlockSpec((tm, tn), lambda i, j, br=br: (i, gate0 + br * (d // tn) + j))
    wsp = lambda w: pl.BlockSpec((w.shape[0], tn), lambda i, j: (0, j))
    return pl.pallas_call(
        _merge_kernel, grid=(m // tm, d // tn),
        in_specs=[row(a), row(b), row(c), gate(0), gate(1), gate(2), wsp(w_a), wsp(w_b), wsp(w_c)],
        out_specs=pl.BlockSpec((tm, tn), lambda i, j: (i, j)),
        out_shape=jax.ShapeDtypeStruct((m, d), BF16),
        compiler_params=_params(2), name="merge")(a, b, c, tail, tail, tail, w_a, w_b, w_c)


def _rms(y, g):
    return y * lax.rsqrt(jnp.mean(y * y, axis=-1, keepdims=True) + EPS) * g


def _proj_norm_residual_kernel(y_ref, w_ref, x_ref, g_post_ref, g_next_ref, o_ref, h_ref, acc_ref, *, tn):
    j = pl.program_id(1)
    acc_ref[:, pl.ds(pl.multiple_of(j * tn, tn), tn)] = _dot(y_ref[...], w_ref[...].astype(BF16))

    @pl.when(j == pl.num_programs(1) - 1)
    def _():
        x1 = x_ref[...] + _rms(acc_ref[...], g_post_ref[...])
        o_ref[...] = x1
        h_ref[...] = _rms(x1, g_next_ref[...]).astype(h_ref.dtype)


def _proj_norm_residual(y, w, x, g_post, g_next, tm, tn):
    m, d = x.shape
    k = y.shape[1]
    row = lambda width: pl.BlockSpec((tm, width), lambda i, j: (i, 0))
    gsp = pl.BlockSpec((1, d), lambda i, j: (0, 0))
    return pl.pallas_call(
        functools.partial(_proj_norm_residual_kernel, tn=tn), grid=(m // tm, d // tn),
        in_specs=[row(k), pl.BlockSpec((k, tn), lambda i, j: (0, j)), row(d), gsp, gsp],
        out_specs=[row(d), row(d)],
        out_shape=[jax.ShapeDtypeStruct((m, d), F32), jax.ShapeDtypeStruct((m, d), BF16)],
        scratch_shapes=[pltpu.VMEM((tm, d), F32)],
        compiler_params=_params(2), name="proj_norm_residual")(y, w, x, g_post.reshape(1, d), g_next.reshape(1, d))


def _ffn_kernel(h_ref, w1_ref, w2_ref, o_ref):
    j = pl.program_id(1)
    a = jnp.square(jnp.maximum(_dot(h_ref[...], w1_ref[...]), 0.0))
    y = _dot(a.astype(BF16), w2_ref[...])

    @pl.when(j == 0)
    def _():
        o_ref[...] = y

    @pl.when(j > 0)
    def _():
        o_ref[...] += y


def _ffn(h, w1, w2, tm, tf):
    m, d = h.shape
    f = w1.shape[1]
    return pl.pallas_call(
        _ffn_kernel, grid=(m // tm, f // tf),
        in_specs=[pl.BlockSpec((tm, d), lambda i, j: (i, 0)), pl.BlockSpec((d, tf), lambda i, j: (0, j)),
                  pl.BlockSpec((tf, d), lambda i, j: (j, 0))],
        out_specs=pl.BlockSpec((tm, d), lambda i, j: (i, 0)),
        out_shape=jax.ShapeDtypeStruct((m, d), F32),
        compiler_params=_params(2), name="ffn")(h, w1, w2)


def _norm_residual_kernel(x_ref, y_ref, g_ref, o_ref):
    o_ref[...] = x_ref[...] + _rms(y_ref[...], g_ref[...])


def _norm_residual(x, y, g, tm):
    m, d = x.shape
    row = pl.BlockSpec((tm, d), lambda i: (i, 0))
    return pl.pallas_call(
        _norm_residual_kernel, grid=(m // tm,),
        in_specs=[row, row, pl.BlockSpec((1, d), lambda i: (0, 0))], out_specs=row,
        out_shape=jax.ShapeDtypeStruct((m, d), F32),
        compiler_params=_params(1), name="norm_residual")(x, y, g.reshape(1, d))


def _row_tile(m, cap):
    t = min(m, cap)
    assert m % t == 0
    return t


def _layer(x, lw, layer, rel_bias, bias_prompt, mem_k, mem_v, mem_layer, swa_caches, conv_buf, s0, n_seq):
    (w_in_t, w_conv, a_log, dt_bias, norm_delta, w_o_swa, w_o_delta, w_o_mem, w_out,
     norm_pre_mix, norm_post_mix, norm_pre_ffn, norm_post_ffn, w_ff1, w_ff2) = lw
    m, d = x.shape
    seq = m // n_seq
    prompt = swa_caches is None
    tm = _row_tile(m, 1024)
    h = _rmsnorm_cast(x, norm_pre_mix, _row_tile(m, 512))
    main = _matmul_nt(h, w_in_t, 0, BA_OFF, tm, 512, "in_proj_main")
    tail = _matmul_nt(h, w_in_t, CQ_OFF, w_in_t.shape[0] - CQ_OFF, tm, 512, "in_proj_tail")
    ba = _matmul_nt(h, w_in_t, BA_OFF, LANES, tm, LANES, "in_proj_ba")
    if prompt:
        assert n_seq == 1
        a_out = _swa_prompt(main, bias_prompt)
        qkv = _delta_prep(main, w_conv, None, None, _row_tile(m, 256))
        chunk = DN_CHUNK if seq % DN_CHUNK == 0 else seq
        b_out, s_new = _delta_scan(qkv, main, ba, a_log, dt_bias, norm_delta, None, chunk)
        c_out = _mem_attn(tail, mem_k, mem_v, mem_layer, _row_tile(m, 512), BF16)
    else:
        a_out = _swa_sample(main, swa_caches, layer, rel_bias, seq)
        halo = jnp.pad(conv_buf, ((0, 0), (0, seq - (CONV_WIDTH - 1)), (0, 0))).reshape(m, 3 * DN_WIDTH)
        qkv = _delta_prep(main, w_conv, halo, seq, _row_tile(m, 256))
        b_out, s_new = _delta_scan(qkv, main, ba, a_log, dt_bias, norm_delta, s0, seq)
        c_out = _mem_attn(tail, mem_k, mem_v, mem_layer, seq, F32)
    merged = _merge(a_out, b_out, c_out, tail, w_o_swa, w_o_delta, w_o_mem, tm, 512)
    x1, h2 = _proj_norm_residual(merged, w_out, x, norm_post_mix, norm_pre_ffn, _row_tile(m, 512), 512)
    f = _ffn(h2, w_ff1, w_ff2, tm, 1024)
    x2 = _norm_residual(x1, f, norm_post_ffn, _row_tile(m, 512))
    return x2, main, s_new


def kernel(x_prompt, x_sample, cache_swa0_k, cache_swa0_v, cache_swa1_k, cache_swa1_v, cache_swa2_k, cache_swa2_v, state_delta, state_conv, cache_mem_k, cache_mem_v, mem_prompt, rel_bias, w_in, w_conv, A_log, dt_bias, norm_delta, norm_mem, w_mem_kv, w_o_swa, w_o_delta, w_o_mem, w_out, norm_pre_mix, norm_post_mix, norm_pre_ffn, norm_post_ffn, w_ff1, w_ff2):
    depth = w_in.shape[0]
    bp, tp, d = x_prompt.shape
    bs, ts, _ = x_sample.shape
    assert bp == 1 and ts > CONV_WIDTH - 1 and ts % 8 == 0
    sample_swa = (cache_swa0_k, cache_swa0_v, cache_swa1_k, cache_swa1_v, cache_swa2_k, cache_swa2_v)
    bias_prompt = _prompt_bias_tables(rel_bias)
    xp = x_prompt.reshape(bp * tp, d)
    xs = x_sample.reshape(bs * ts, d)
    new_p = [[] for _ in range(10)]
    new_s = [[] for _ in range(8)]
    hw = SWA_HPG * HEAD_DIM
    for l in range(depth):
        lw = (jnp.swapaxes(w_in, 1, 2)[l], w_conv[l], A_log[l], dt_bias[l], norm_delta[l], w_o_swa[l], w_o_delta[l],
              w_o_mem[l], w_out[l], norm_pre_mix[l], norm_post_mix[l], norm_pre_ffn[l], norm_post_ffn[l],
              w_ff1[l].astype(BF16), w_ff2[l].astype(BF16))
        mem = mem_prompt.reshape(-1, d)
        mkv = _matmul(_rmsnorm_cast(mem, norm_mem[l], _row_tile(mem.shape[0], 256)), w_mem_kv[l],
                      2 * MEM_WIDTH, _row_tile(mem.shape[0], 256), 512, "mem_kv")
        mk = mkv[:, :MEM_WIDTH].reshape(1, bp, -1, MEM_HEADS, MEM_HEAD_DIM)
        mv = mkv[:, MEM_WIDTH:].reshape(1, bp, -1, MEM_HEADS, MEM_HEAD_DIM)
        xp, main_p, s_p = _layer(xp, lw, l, rel_bias, bias_prompt, mk, mv, 0, None, None, None, bp)
        xs, main_s, s_s = _layer(xs, lw, l, rel_bias, bias_prompt, cache_mem_k, cache_mem_v, l,
                                 sample_swa, state_conv[l], state_delta[l], bs)
        vals_p = []
        for g, (window, _) in enumerate(SWA_GROUPS):
            keep = min(window, tp)
            for sec in (1, 2):
                c0 = sec * SWA_WIDTH + g * hw
                vals_p.append(main_p[tp - keep:, c0:c0 + hw].reshape(bp, keep, SWA_HPG, HEAD_DIM))
        vals_p.append(s_p.reshape(bp, DN_HEADS, HEAD_DIM, HEAD_DIM))
        vals_p.append(main_p[tp - (CONV_WIDTH - 1):, B_OFF:B_OFF + 3 * DN_WIDTH].reshape(bp, CONV_WIDTH - 1, -1))
        vals_p += [mk[0], mv[0]]
        vals_s = []
        main_s3 = main_s.reshape(bs, ts, -1)
        for g in range(len(SWA_GROUPS)):
            for sec in (1, 2):
                c0 = sec * SWA_WIDTH + g * hw
                vals_s.append(main_s3[:, :, c0:c0 + hw].reshape(bs, ts, SWA_HPG, HEAD_DIM))
        vals_s.append(s_s)
        vals_s.append(main_s3[:, ts - (CONV_WIDTH - 1):, B_OFF:B_OFF + 3 * DN_WIDTH])
        for lst, val in zip(new_p, vals_p):
            lst.append(val)
        for lst, val in zip(new_s, vals_s):
            lst.append(val)
    outs_p = [jnp.stack(t) for t in new_p]
    outs_s = [jnp.stack(t) for t in new_s]
    return (xp.reshape(bp, tp, d), xs.reshape(bs, ts, d), *outs_p, *outs_s)
```

```python
import functools
import math

import numpy as np
import jax
import jax.numpy as jnp
from jax import lax
from jax.experimental import pallas as pl
from jax.experimental.pallas import tpu as pltpu

EPS = 1e-6
HEAD_DIM = 128
SWA_GROUPS = ((128, 1), (512, 4), (2048, 16))
SWA_SPAN = 128
SWA_HPG = 4
SWA_WIDTH = SWA_HPG * len(SWA_GROUPS) * HEAD_DIM
DN_HEADS = 12
DN_WIDTH = DN_HEADS * HEAD_DIM
CONV_WIDTH = 4
DN_CHUNK = 64
MEM_HEADS = 4
MEM_HEAD_DIM = 256
MEM_WIDTH = MEM_HEADS * MEM_HEAD_DIM
N_BUCKETS = 32
MAX_DISTANCE = 2048
N_BRANCHES = 3

A_OFF = 0
B_OFF = 3 * SWA_WIDTH
Z_OFF = B_OFF + 3 * DN_WIDTH
BA_OFF = Z_OFF + DN_WIDTH
CQ_OFF = BA_OFF + 2 * DN_HEADS
GATE_OFF = CQ_OFF + MEM_WIDTH
LANES = 128
PROJ_TILE = 512
P_BA = BA_OFF
P_CQ = P_BA + PROJ_TILE
P_GATE = P_CQ + MEM_WIDTH
SWA_ROWS = 2048
SWA_UNROLL = 4
NEG = -1e30
VMEM_LIMIT = 56 * 1024 * 1024

BF16 = jnp.bfloat16
F32 = jnp.float32


def _params(n_grid):
    return pltpu.CompilerParams(dimension_semantics=("arbitrary",) * n_grid, vmem_limit_bytes=VMEM_LIMIT)


def _dot(a, b):
    return jnp.dot(a, b, preferred_element_type=F32)


def _dot_nt(a, b):
    return lax.dot_general(a, b, (((1,), (1,)), ((), ())), preferred_element_type=F32)


def _dot_tn(a, b):
    return lax.dot_general(a, b, (((0,), (0,)), ((), ())), preferred_element_type=F32)


def _split2(x):
    hi = x.astype(BF16)
    lo = (x - hi.astype(F32)).astype(BF16)
    return hi, lo


def _dot3(a, b, dot=_dot):
    ah, al = _split2(a)
    bh, bl = _split2(b)
    return dot(ah, bh) + (dot(ah, bl) + dot(al, bh))


def _rmsnorm_cast_kernel(x_ref, g_ref, o_ref):
    x = x_ref[...]
    y = x * lax.rsqrt(jnp.mean(x * x, axis=-1, keepdims=True) + EPS)
    o_ref[...] = (y * g_ref[...]).astype(o_ref.dtype)


def _rmsnorm_cast(x, g, tm):
    m, d = x.shape
    return pl.pallas_call(
        _rmsnorm_cast_kernel, grid=(m // tm,),
        in_specs=[pl.BlockSpec((tm, d), lambda i: (i, 0)), pl.BlockSpec((1, d), lambda i: (0, 0))],
        out_specs=pl.BlockSpec((tm, d), lambda i: (i, 0)),
        out_shape=jax.ShapeDtypeStruct((m, d), BF16),
        compiler_params=_params(1), name="rmsnorm_cast")(x, g.reshape(1, d))


def _matmul_kernel(x_ref, w_ref, o_ref):
    o_ref[...] = _dot(x_ref[...], w_ref[...].astype(BF16)).astype(o_ref.dtype)


def _matmul(x, w, n, tm, tn, name):
    m, k = x.shape
    assert n % tn == 0 and m % tm == 0
    return pl.pallas_call(
        _matmul_kernel, grid=(m // tm, n // tn),
        in_specs=[pl.BlockSpec((tm, k), lambda i, j: (i, 0)), pl.BlockSpec((k, tn), lambda i, j: (0, j))],
        out_specs=pl.BlockSpec((tm, tn), lambda i, j: (i, j)),
        out_shape=jax.ShapeDtypeStruct((m, n), F32),
        compiler_params=_params(2), name=name)(x, w)


def _in_proj_kernel(x_ref, g_ref, w_ref, o_ref, h_ref):
    @pl.when(pl.program_id(1) == 0)
    def _():
        h_ref[...] = _rms(x_ref[...], g_ref[...]).astype(BF16)

    o_ref[...] = _dot_nt(h_ref[...], w_ref[...])


def _in_proj(x, g, wt, tm, tn):
    m, k = x.shape
    n = wt.shape[0]
    assert n % tn == 0 and m % tm == 0
    return pl.pallas_call(
        _in_proj_kernel, grid=(m // tm, n // tn),
        in_specs=[pl.BlockSpec((tm, k), lambda i, j: (i, 0)), pl.BlockSpec((1, k), lambda i, j: (0, 0)),
                  pl.BlockSpec((tn, k), lambda i, j: (j, 0))],
        out_specs=pl.BlockSpec((tm, tn), lambda i, j: (i, j)),
        out_shape=jax.ShapeDtypeStruct((m, n), F32),
        scratch_shapes=[pltpu.VMEM((tm, k), BF16)],
        compiler_params=_params(2), name="in_proj")(x, g.reshape(1, k), wt)


def _t5_bucket(dist):
    max_exact = N_BUCKETS // 2
    df = jnp.maximum(dist, 1).astype(F32)
    large = max_exact + (jnp.log(df / max_exact) / math.log(MAX_DISTANCE / max_exact)
                         * (N_BUCKETS - max_exact)).astype(jnp.int32)
    return jnp.where(dist < max_exact, dist, jnp.minimum(large, N_BUCKETS - 1))


def _group_bias(rel_bias, g):
    _, dil = SWA_GROUPS[g]
    dist = jnp.arange(SWA_SPAN + 1, dtype=jnp.int32) * dil
    onehot = _t5_bucket(dist)[None, :, None] == jnp.arange(N_BUCKETS, dtype=jnp.int32)[None, None, :]
    heads = jnp.transpose(rel_bias[:, g * SWA_HPG:(g + 1) * SWA_HPG].astype(F32))
    return jnp.sum(jnp.where(onehot, heads[:, None, :], 0.0), axis=-1)


def _prompt_bias_tables(rel_bias):
    span = SWA_SPAN
    first = (np.arange(2 * span) >= span)[None, None, :]
    out = []
    for g in range(len(SWA_GROUPS)):
        w = _group_bias(rel_bias, g)
        h = w.shape[0]
        p = 3 * span
        e = jnp.concatenate([jnp.full((h, span - 1), NEG, F32), w[:, ::-1], jnp.full((h, p - 2 * span), NEG, F32)], 1)
        skew = jnp.broadcast_to(e[:, None, :], (h, span, p)).reshape(h, span * p)[:, :span * (p - 1)]
        tab = skew.reshape(h, span, p - 1)[:, :, span - 1:3 * span - 1]
        out.append(jnp.stack([jnp.where(first, tab, NEG), tab], axis=1))
    return jnp.stack(out)


def _sample_bias_tables(rel_bias, g, n_res, n_new, cache_len):
    _, dil = SWA_GROUPS[g]
    w = _group_bias(rel_bias, g)
    h = w.shape[0]
    lm = cache_len // dil
    assert cache_len % dil == 0
    wpad = jnp.concatenate([w, jnp.full((h, lm + n_new), NEG, F32)], axis=1)
    res = np.arange(n_res)[None, None, :]
    rows = []
    for s in range(n_new):
        vec = wpad[:, s // dil + 1:lm + s // dil + 1][:, ::-1]
        rows.append(jnp.where(res == s % dil, vec[:, :, None], NEG).reshape(h, lm * n_res))
    s = np.arange(n_new)
    dist = s[:, None] - s[None, :]
    new = jnp.full((h, n_new, n_new), NEG, F32)
    for j in range(min((n_new - 1) // dil, SWA_SPAN) + 1):
        new = jnp.where((dist == j * dil)[None], w[:, j][:, None, None], new)
    return jnp.stack(rows, axis=1), new


def _swa_prompt_kernel(*refs, n_groups):
    ins = refs[:5 * n_groups]
    bias_ref = refs[5 * n_groups]
    o_ref = refs[5 * n_groups + 1]
    scr = refs[5 * n_groups + 2:]
    kext, vext, og, lg = scr[:n_groups], scr[n_groups:2 * n_groups], scr[2 * n_groups:3 * n_groups], scr[3 * n_groups:]
    n = pl.program_id(0)
    scale = HEAD_DIM ** -0.5
    span = SWA_SPAN
    for g in range(n_groups):
        _, dil = SWA_GROUPS[g]
        q_ref, kc_ref, vc_ref, kp_ref, vp_ref = ins[5 * g:5 * g + 5]
        blk = span * dil
        kext[g][0:blk, :] = kp_ref[...]
        kext[g][blk:, :] = kc_ref[...]
        vext[g][0:blk, :] = vp_ref[...]
        vext[g][blk:, :] = vc_ref[...]

        def body(it, carry, g=g, dil=dil, blk=blk, q_ref=q_ref):
            items = range(SWA_UNROLL)
            starts, sels = [], []
            for u in items:
                idx = it * SWA_UNROLL + u
                b = idx // dil
                starts.append(b * blk + (idx - b * dil))
                sels.append(jnp.where(jnp.logical_and(n == 0, b == 0), 0, 1))
            q = [q_ref[pl.ds(starts[u], span, stride=dil), :].astype(BF16) for u in items]
            kk = [kext[g][pl.ds(starts[u], 2 * span, stride=dil), :].astype(BF16) for u in items]
            s = [_dot_nt(q[u], kk[u]) * scale + bias_ref[g, 0, sels[u]] for u in items]
            m = [jnp.max(s[u], axis=-1, keepdims=True) for u in items]
            p = [jnp.exp(s[u] - m[u]) for u in items]
            den = [jnp.sum(p[u], axis=-1, keepdims=True) for u in items]
            vv = [vext[g][pl.ds(starts[u], 2 * span, stride=dil), :].astype(BF16) for u in items]
            o = [_dot(p[u].astype(BF16), vv[u]) / den[u] for u in items]
            for u in items:
                og[g][pl.ds(starts[u], span, stride=dil), :] = o[u]
                lg[g][pl.ds(starts[u], span, stride=dil), :] = jnp.broadcast_to(
                    m[u] + jnp.log(den[u]), (span, HEAD_DIM))
            return carry

        lax.fori_loop(0, SWA_ROWS // span // SWA_UNROLL, body, 0)
    lses = [lg[g][...] for g in range(n_groups)]
    mx = functools.reduce(jnp.maximum, lses)
    ws = [jnp.exp(l - mx) for l in lses]
    num = functools.reduce(lambda a, b: a + b, [w * og[g][...] for g, w in enumerate(ws)])
    o_ref[...] = (num / functools.reduce(lambda a, b: a + b, ws)).astype(o_ref.dtype)


def _swa_prompt(proj, bias_tables):
    t = proj.shape[0]
    n_groups = len(SWA_GROUPS)
    assert t % SWA_ROWS == 0
    in_specs, scratch_k, scratch_o = [], [], []
    for g, (_, dil) in enumerate(SWA_GROUPS):
        blk = SWA_SPAN * dil
        per = SWA_ROWS // blk
        assert SWA_ROWS % blk == 0
        qc, kc, vc = (sec * (SWA_WIDTH // HEAD_DIM) + g * SWA_HPG for sec in range(3))
        cur = lambda col: pl.BlockSpec((SWA_ROWS, HEAD_DIM), lambda n, h, col=col: (n, col + h))
        prev = lambda col, blk=blk, per=per: pl.BlockSpec(
            (blk, HEAD_DIM), lambda n, h, col=col, per=per: (jnp.maximum(n * per - 1, 0), col + h))
        in_specs += [cur(qc), cur(kc), cur(vc), prev(kc), prev(vc)]
        scratch_k.append(pltpu.VMEM((SWA_ROWS + blk, HEAD_DIM), F32))
        scratch_o.append(pltpu.VMEM((SWA_ROWS, HEAD_DIM), F32))
    in_specs.append(pl.BlockSpec((n_groups, 1, 2, SWA_SPAN, 2 * SWA_SPAN), lambda n, h: (0, h, 0, 0, 0)))
    return pl.pallas_call(
        functools.partial(_swa_prompt_kernel, n_groups=n_groups),
        grid=(t // SWA_ROWS, SWA_HPG),
        in_specs=in_specs,
        out_specs=pl.BlockSpec((SWA_ROWS, HEAD_DIM), lambda n, h: (n, h)),
        out_shape=jax.ShapeDtypeStruct((t, SWA_HPG * HEAD_DIM), BF16),
        scratch_shapes=scratch_k + scratch_k + scratch_o + scratch_o,
        compiler_params=_params(2), name="swa_prompt")(*([proj] * (5 * n_groups)), bias_tables)


def _swa_sample_kernel(*refs, n_groups):
    qkv_ref = refs[0]
    caches = refs[1:1 + 2 * n_groups]
    biases = refs[1 + 2 * n_groups:1 + 4 * n_groups]
    o_ref = refs[1 + 4 * n_groups]
    scale = HEAD_DIM ** -0.5
    hw = SWA_HPG * HEAD_DIM
    items = [(g, h) for h in range(SWA_HPG) for g in range(n_groups)]
    idx = range(len(items))

    def new(sec, g, h):
        col = sec * SWA_WIDTH + (g * SWA_HPG + h) * HEAD_DIM
        return qkv_ref[:, col:col + HEAD_DIM].astype(BF16)

    def cached(ref, h):
        nk = math.prod(ref.shape[2:-1]) // SWA_HPG
        if len(ref.shape) == 5:
            pick = (0, 0, slice(None), pl.ds(h, ref.shape[3] // SWA_HPG, stride=SWA_HPG), slice(None))
        else:
            pick = (0, 0, pl.ds(h, nk, stride=SWA_HPG), slice(None))
        return ref[pick].reshape(nk, HEAD_DIM).astype(BF16)

    q = [new(0, g, h) for g, h in items]
    sc = [_dot_nt(q[i], cached(caches[2 * g], h)) * scale + biases[2 * g][h] for i, (g, h) in enumerate(items)]
    sn = [_dot_nt(q[i], new(1, g, h)) * scale + biases[2 * g + 1][h] for i, (g, h) in enumerate(items)]
    m = [jnp.maximum(jnp.max(sc[i], axis=-1, keepdims=True), jnp.max(sn[i], axis=-1, keepdims=True)) for i in idx]
    pc = [jnp.exp(sc[i] - m[i]) for i in idx]
    pn = [jnp.exp(sn[i] - m[i]) for i in idx]
    den = [jnp.sum(pc[i], axis=-1, keepdims=True) + jnp.sum(pn[i], axis=-1, keepdims=True) for i in idx]
    o = [(_dot(pc[i].astype(BF16), cached(caches[2 * g + 1], h)) + _dot(pn[i].astype(BF16), new(2, g, h))) / den[i]
         for i, (g, h) in enumerate(items)]
    lse = [m[i] + jnp.log(den[i]) for i in idx]
    for h in range(SWA_HPG):
        mine = [i for i in idx if items[i][1] == h]
        mx = functools.reduce(jnp.maximum, [lse[i] for i in mine])
        ws = [jnp.exp(lse[i] - mx) for i in mine]
        num = functools.reduce(lambda a, b: a + b, [w * o[i] for w, i in zip(ws, mine)])
        o_ref[:, h * HEAD_DIM:(h + 1) * HEAD_DIM] = num / functools.reduce(lambda a, b: a + b, ws)


def _swa_sample(proj, caches, layer, rel_bias, n_new):
    rows_total = proj.shape[0]
    nb = rows_total // n_new
    n_groups = len(SWA_GROUPS)
    hw = SWA_HPG * HEAD_DIM
    cache_in, cache_specs, bias_in, bias_specs = [], [], [], []
    for g, (_, dil) in enumerate(SWA_GROUPS):
        depth, _, cache_len, heads, hd = caches[2 * g].shape
        if dil >= 2 * n_new and n_new % 8 == 0:
            n_res = n_new
            shape = (depth, nb, cache_len // dil, dil * heads, hd)
            spec = pl.BlockSpec((1, 1, cache_len // dil, n_res * heads, hd), lambda b: (layer, b, 0, 0, 0))
        else:
            n_res = dil
            shape = (depth, nb, cache_len * heads, hd)
            spec = pl.BlockSpec((1, 1, cache_len * heads, hd), lambda b: (layer, b, 0, 0))
        for c in caches[2 * g:2 * g + 2]:
            cache_in.append(c.reshape(shape))
            cache_specs.append(spec)
        for tab in _sample_bias_tables(rel_bias, g, n_res, n_new, cache_len):
            bias_in.append(tab)
            bias_specs.append(pl.BlockSpec(tab.shape, lambda b: (0, 0, 0)))
    return pl.pallas_call(
        functools.partial(_swa_sample_kernel, n_groups=n_groups),
        grid=(nb,),
        in_specs=[pl.BlockSpec((n_new, 3 * SWA_WIDTH), lambda b: (b, 0))] + cache_specs + bias_specs,
        out_specs=pl.BlockSpec((n_new, hw), lambda b: (b, 0)),
        out_shape=jax.ShapeDtypeStruct((rows_total, hw), F32),
        compiler_params=_params(1), name="swa_sample")(proj, *cache_in, *bias_in)


def _silu(x):
    return x * (1.0 / (1.0 + jnp.exp(-x)))


def _softplus(x):
    return jnp.maximum(x, 0.0) + jnp.log(1.0 + jnp.exp(-jnp.abs(x)))


def _inv_dot(a, b):
    return _dot(a.astype(BF16), b.astype(BF16))


HALO = 8


def _delta_scan_kernel(*refs, chunk, carry):
    if carry:
        x_ref, z_ref, ba_ref, wc_ref, alog_ref, dt_ref, gain_ref, o_ref, s_out_ref, s_scr, halo_scr = refs
        s_in_ref = halo_ref = None
    else:
        x_ref, z_ref, ba_ref, wc_ref, alog_ref, dt_ref, gain_ref, s_in_ref, halo_ref, o_ref, s_out_ref = refs
        s_scr = halo_scr = None
    c = chunk
    if carry:
        @pl.when(pl.program_id(0) == 0)
        def _():
            s_scr[...] = jnp.zeros_like(s_scr)
            halo_scr[...] = jnp.zeros_like(halo_scr)

    x = x_ref[...]
    xp = jnp.concatenate([halo_scr[...] if carry else halo_ref[...], x], axis=0)
    acc = x * wc_ref[CONV_WIDTH - 1:CONV_WIDTH, :]
    for t in range(1, CONV_WIDTH):
        acc = acc + pltpu.roll(xp, t, 0)[HALO:, :] * wc_ref[CONV_WIDTH - 1 - t:CONV_WIDTH - t, :]
    if carry:
        halo_scr[...] = x[c - HALO:, :]
    qkv = _silu(acc)

    ba = ba_ref[...]
    beta_all = 1.0 / (1.0 + jnp.exp(-ba))
    g_all = -jnp.exp(alog_ref[...]) * _softplus(ba + dt_ref[...])
    row = lax.broadcasted_iota(jnp.int32, (c, c), 0)
    col = lax.broadcasted_iota(jnp.int32, (c, c), 1)
    tri = row >= col
    strict = row > col
    tril = tri.astype(BF16)
    g1 = g_all.astype(BF16)
    r1 = g_all - g1.astype(F32)
    g2 = r1.astype(BF16)
    g3 = (r1 - g2.astype(F32)).astype(BF16)
    gcum_all = _dot(tril, g1) + (_dot(tril, g2) + _dot(tril, g3))
    gcum_t = gcum_all.T
    eye = (row == col).astype(F32)

    heads = range(DN_HEADS)
    hs = lambda sec, h: slice(sec * DN_WIDTH + h * HEAD_DIM, sec * DN_WIDTH + (h + 1) * HEAD_DIM)
    lane = lambda h: slice(DN_HEADS + h, DN_HEADS + h + 1)
    l2 = lambda t: t * lax.rsqrt(jnp.sum(t * t, axis=-1, keepdims=True) + EPS)
    q = [l2(qkv[:, hs(0, h)]) * (HEAD_DIM ** -0.5) for h in heads]
    k = [l2(qkv[:, hs(1, h)]) for h in heads]
    v = [qkv[:, hs(2, h)] for h in heads]
    bcol = [beta_all[:, h:h + 1] for h in heads]
    gc = [gcum_all[:, lane(h)] for h in heads]
    glast = [gcum_all[c - 1:c, lane(h)] for h in heads]
    decay = [jnp.exp(jnp.where(tri, gc[h] - gcum_t[lane(h), :], NEG)) for h in heads]
    kb = [k[h] * bcol[h] for h in heads]
    k16 = [k[h].astype(BF16) for h in heads]
    nmat = [jnp.where(strict, _dot_nt(kb[h].astype(BF16), k16[h]) * decay[h], 0.0) for h in heads]
    attn = [jnp.where(tri, _dot_nt(q[h].astype(BF16), k16[h]) * decay[h], 0.0) for h in heads]
    inv = [eye] * DN_HEADS
    blk = 1
    while blk < c:
        pair = (row // (2 * blk) == col // (2 * blk)) & (row % (2 * blk) >= blk) & (col % (2 * blk) < blk)
        low = [jnp.where(pair, nmat[h], 0.0) for h in heads]
        if blk == 1:
            inv = [eye - low[h] for h in heads]
        else:
            t = [_inv_dot(low[h], inv[h]) for h in heads]
            inv = [inv[h] - _inv_dot(inv[h], t[h]) for h in heads]
        blk *= 2
    eg = [jnp.exp(gc[h]) for h in heads]
    rhs = [jnp.concatenate([v[h] * bcol[h], kb[h] * eg[h]], axis=1) for h in heads]
    sol = [_inv_dot(inv[h], rhs[h]) for h in heads]
    s_prev = [s_scr[h] if carry else s_in_ref[0, h] for h in heads]
    s16 = [s_prev[h].astype(BF16) for h in heads]
    wq = [jnp.concatenate([sol[h][:, HEAD_DIM:], q[h] * eg[h]], axis=0).astype(BF16) for h in heads]
    ws = [_dot(wq[h], s16[h]) for h in heads]
    v_new = [(sol[h][:, :HEAD_DIM] - ws[h][:c]).astype(BF16) for h in heads]
    o = [ws[h][c:] + _dot(attn[h].astype(BF16), v_new[h]) for h in heads]
    kdec = [(k[h] * jnp.exp(glast[h] - gc[h])).astype(BF16) for h in heads]
    s_new = [s_prev[h] * jnp.exp(glast[h]) + _dot_tn(kdec[h], v_new[h]) for h in heads]
    for h in heads:
        if carry:
            s_scr[h] = s_new[h]
            s_out_ref[h] = s_new[h]
        else:
            s_out_ref[0, h] = s_new[h]
        y = o[h] * lax.rsqrt(jnp.mean(o[h] * o[h], axis=-1, keepdims=True) + EPS) * gain_ref[...]
        o_ref[:, hs(0, h)] = (y * _silu(z_ref[:, hs(0, h)])).astype(o_ref.dtype)


def _delta_scan(proj, w_conv, a_log, dt_bias, norm_delta, s0, conv_buf, chunk):
    m = proj.shape[0]
    carry = s0 is None
    n = m // chunk
    width = 3 * DN_WIDTH
    assert chunk >= HALO and B_OFF % width == 0
    pad = lambda v_, off: jnp.zeros((1, LANES), F32).at[0, off:off + DN_HEADS].set(v_.astype(F32))
    consts = [w_conv, pad(a_log, DN_HEADS), pad(dt_bias, DN_HEADS), norm_delta.reshape(1, HEAD_DIM).astype(F32)]
    const_specs = [pl.BlockSpec((CONV_WIDTH, width), lambda i: (0, 0))] + [pl.BlockSpec((1, LANES), lambda i: (0, 0))] * 3
    in_specs = [pl.BlockSpec((chunk, width), lambda i: (i, B_OFF // width)),
                pl.BlockSpec((chunk, DN_WIDTH), lambda i: (i, Z_OFF // DN_WIDTH)),
                pl.BlockSpec((chunk, LANES), lambda i: (i, P_BA // LANES))] + const_specs
    args = [proj, proj, proj] + consts
    state = (DN_HEADS, HEAD_DIM, HEAD_DIM)
    if carry:
        out_dtype = BF16
        s_shape, s_spec = state, pl.BlockSpec(state, lambda i: (0, 0, 0))
        scratch = [pltpu.VMEM(state, F32), pltpu.VMEM((HALO, width), F32)]
    else:
        out_dtype = F32
        halo = jnp.pad(conv_buf, ((0, 0), (HALO - (CONV_WIDTH - 1), 0), (0, 0))).reshape(n * HALO, width)
        in_specs += [pl.BlockSpec((1,) + state, lambda i: (i, 0, 0, 0)), pl.BlockSpec((HALO, width), lambda i: (i, 0))]
        args += [s0, halo]
        s_shape, s_spec = (n,) + state, pl.BlockSpec((1,) + state, lambda i: (i, 0, 0, 0))
        scratch = []
    return pl.pallas_call(
        functools.partial(_delta_scan_kernel, chunk=chunk, carry=carry), grid=(n,),
        in_specs=in_specs,
        out_specs=[pl.BlockSpec((chunk, DN_WIDTH), lambda i: (i, 0)), s_spec],
        out_shape=[jax.ShapeDtypeStruct((m, DN_WIDTH), out_dtype), jax.ShapeDtypeStruct(s_shape, F32)],
        scratch_shapes=scratch,
        compiler_params=_params(1), name="delta_scan")(*args)


def _mem_attn_kernel(q_ref, k_ref, v_ref, o_ref):
    scale = MEM_HEAD_DIM ** -0.5
    parts = MEM_HEAD_DIM // LANES
    period = MEM_HEADS * parts
    tokens = k_ref.shape[2] // period
    pick = lambda c, h: (0, 0, pl.ds(c * MEM_HEADS + h, tokens, stride=period), slice(None))
    heads = range(MEM_HEADS)
    q = [[q_ref[:, h * MEM_HEAD_DIM + c * LANES:h * MEM_HEAD_DIM + (c + 1) * LANES].astype(BF16)
          for c in range(parts)] for h in heads]
    s = [functools.reduce(lambda a, b: a + b, [_dot_nt(q[h][c], k_ref[pick(c, h)].astype(BF16))
                                               for c in range(parts)]) * scale for h in heads]
    p = [jnp.exp(s[h] - jnp.max(s[h], axis=-1, keepdims=True)) for h in heads]
    den = [jnp.sum(p[h], axis=-1, keepdims=True) for h in heads]
    for h in heads:
        for c in range(parts):
            col = h * MEM_HEAD_DIM + c * LANES
            o_ref[:, col:col + LANES] = (_dot(p[h].astype(BF16), v_ref[pick(c, h)].astype(BF16))
                                         / den[h]).astype(o_ref.dtype)


def _mem_attn(proj, mem_k, mem_v, layer, tm, out_dtype):
    m = proj.shape[0]
    assert P_CQ % MEM_WIDTH == 0
    depth, nb, tokens, heads, hd = mem_k.shape
    per = m // nb
    parts = hd // LANES
    mem_k, mem_v = (t.reshape(depth, nb, tokens, heads, parts, LANES).transpose(0, 1, 2, 4, 3, 5)
                    .reshape(depth, nb, tokens * heads * parts, LANES) for t in (mem_k, mem_v))
    kv_spec = pl.BlockSpec((1, 1, tokens * heads * parts, LANES), lambda i: (layer, i * tm // per, 0, 0))
    return pl.pallas_call(
        _mem_attn_kernel, grid=(m // tm,),
        in_specs=[pl.BlockSpec((tm, MEM_WIDTH), lambda i: (i, P_CQ // MEM_WIDTH)), kv_spec, kv_spec],
        out_specs=pl.BlockSpec((tm, MEM_WIDTH), lambda i: (i, 0)),
        out_shape=jax.ShapeDtypeStruct((m, MEM_WIDTH), out_dtype),
        compiler_params=_params(1), name="mem_attn")(proj, mem_k, mem_v)


def _merge_kernel(a_ref, b_ref, c_ref, ga_ref, gb_ref, gc_ref, wa_ref, wb_ref, wc_ref, o_ref):
    def branch(x_ref, g_ref, w_ref):
        y = _dot(x_ref[...].astype(BF16), w_ref[...])
        return y * (1.0 / (1.0 + jnp.exp(-g_ref[...])))

    o_ref[...] = (branch(a_ref, ga_ref, wa_ref) + branch(b_ref, gb_ref, wb_ref)
                  + branch(c_ref, gc_ref, wc_ref)).astype(o_ref.dtype)


def _merge(a, b, c, tail, w_a, w_b, w_c, tm, tn):
    m = a.shape[0]
    d = w_a.shape[1]
    gate0 = P_GATE // tn
    assert P_GATE % tn == 0 and d % tn == 0
    row = lambda x: pl.BlockSpec((tm, x.shape[1]), lambda i, j: (i, 0))
    gate = lambda br: pl.BlockSpec((tm, tn), lambda i, j, br=br: (i, gate0 + br * (d // tn) + j))
    wsp = lambda w: pl.BlockSpec((w.shape[0], tn), lambda i, j: (0, j))
    return pl.pallas_call(
        _merge_kernel, grid=(m // tm, d // tn),
        in_specs=[row(a), row(b), row(c), gate(0), gate(1), gate(2), wsp(w_a), wsp(w_b), wsp(w_c)],
        out_specs=pl.BlockSpec((tm, tn), lambda i, j: (i, j)),
        out_shape=jax.ShapeDtypeStruct((m, d), BF16),
        compiler_params=_params(2), name="merge")(a, b, c, tail, tail, tail, w_a, w_b, w_c)


def _rms(y, g):
    return y * lax.rsqrt(jnp.mean(y * y, axis=-1, keepdims=True) + EPS) * g


def _proj_norm_residual_kernel(y_ref, w_ref, x_ref, g_post_ref, g_next_ref, o_ref, h_ref):
    x1 = x_ref[...] + _rms(_dot(y_ref[...], w_ref[...]), g_post_ref[...])
    o_ref[...] = x1
    h_ref[...] = _rms(x1, g_next_ref[...]).astype(h_ref.dtype)


def _proj_norm_residual(y, w, x, g_post, g_next, tm):
    m, d = x.shape
    k = y.shape[1]
    row = lambda width: pl.BlockSpec((tm, width), lambda i: (i, 0))
    gsp = pl.BlockSpec((1, d), lambda i: (0, 0))
    return pl.pallas_call(
        _proj_norm_residual_kernel, grid=(m // tm,),
        in_specs=[row(k), pl.BlockSpec((k, d), lambda i: (0, 0)), row(d), gsp, gsp],
        out_specs=[row(d), row(d)],
        out_shape=[jax.ShapeDtypeStruct((m, d), F32), jax.ShapeDtypeStruct((m, d), BF16)],
        compiler_params=_params(1), name="proj_norm_residual")(y, w, x, g_post.reshape(1, d), g_next.reshape(1, d))


def _ffn_kernel(h_ref, w1_ref, w2_ref, o_ref):
    j = pl.program_id(1)
    a = jnp.square(jnp.maximum(_dot(h_ref[...], w1_ref[...]), 0.0))
    y = _dot(a.astype(BF16), w2_ref[...])

    @pl.when(j == 0)
    def _():
        o_ref[...] = y

    @pl.when(j > 0)
    def _():
        o_ref[...] += y


def _ffn(h, w1, w2, tm, tf):
    m, d = h.shape
    f = w1.shape[1]
    return pl.pallas_call(
        _ffn_kernel, grid=(m // tm, f // tf),
        in_specs=[pl.BlockSpec((tm, d), lambda i, j: (i, 0)), pl.BlockSpec((d, tf), lambda i, j: (0, j)),
                  pl.BlockSpec((tf, d), lambda i, j: (j, 0))],
        out_specs=pl.BlockSpec((tm, d), lambda i, j: (i, 0)),
        out_shape=jax.ShapeDtypeStruct((m, d), F32),
        compiler_params=_params(2), name="ffn")(h, w1, w2)


def _norm_residual_kernel(x_ref, y_ref, g_ref, o_ref):
    o_ref[...] = x_ref[...] + _rms(y_ref[...], g_ref[...])


def _norm_residual(x, y, g, tm):
    m, d = x.shape
    row = pl.BlockSpec((tm, d), lambda i: (i, 0))
    return pl.pallas_call(
        _norm_residual_kernel, grid=(m // tm,),
        in_specs=[row, row, pl.BlockSpec((1, d), lambda i: (0, 0))], out_specs=row,
        out_shape=jax.ShapeDtypeStruct((m, d), F32),
        compiler_params=_params(1), name="norm_residual")(x, y, g.reshape(1, d))


def _row_tile(m, cap):
    t = min(m, cap)
    assert m % t == 0
    return t


def _layer(x, lw, layer, rel_bias, bias_prompt, mem_k, mem_v, mem_layer, swa_caches, conv_buf, s0, n_seq):
    (w_in_t, w_conv, a_log, dt_bias, norm_delta, w_o_swa, w_o_delta, w_o_mem, w_out,
     norm_pre_mix, norm_post_mix, norm_pre_ffn, norm_post_ffn, w_ff1, w_ff2) = lw
    m, d = x.shape
    seq = m // n_seq
    prompt = swa_caches is None
    tm = _row_tile(m, 1024)
    proj = _in_proj(x, norm_pre_mix, w_in_t, tm, PROJ_TILE)
    if prompt:
        assert n_seq == 1
        a_out = _swa_prompt(proj, bias_prompt)
        chunk = DN_CHUNK if seq % DN_CHUNK == 0 else seq
        b_out, s_new = _delta_scan(proj, w_conv, a_log, dt_bias, norm_delta, None, None, chunk)
        c_out = _mem_attn(proj, mem_k, mem_v, mem_layer, _row_tile(m, 512), BF16)
    else:
        a_out = _swa_sample(proj, swa_caches, layer, rel_bias, seq)
        b_out, s_new = _delta_scan(proj, w_conv, a_log, dt_bias, norm_delta, s0, conv_buf, seq)
        c_out = _mem_attn(proj, mem_k, mem_v, mem_layer, seq, F32)
    merged = _merge(a_out, b_out, c_out, proj, w_o_swa, w_o_delta, w_o_mem, tm, 512)
    x1, h2 = _proj_norm_residual(merged, w_out, x, norm_post_mix, norm_pre_ffn, _row_tile(m, 512))
    f = _ffn(h2, w_ff1, w_ff2, tm, 1024)
    x2 = _norm_residual(x1, f, norm_post_ffn, _row_tile(m, 512))
    return x2, proj, s_new


def kernel(x_prompt, x_sample, cache_swa0_k, cache_swa0_v, cache_swa1_k, cache_swa1_v, cache_swa2_k, cache_swa2_v, state_delta, state_conv, cache_mem_k, cache_mem_v, mem_prompt, rel_bias, w_in, w_conv, A_log, dt_bias, norm_delta, norm_mem, w_mem_kv, w_o_swa, w_o_delta, w_o_mem, w_out, norm_pre_mix, norm_post_mix, norm_pre_ffn, norm_post_ffn, w_ff1, w_ff2):
    depth = w_in.shape[0]
    bp, tp, d = x_prompt.shape
    bs, ts, _ = x_sample.shape
    assert bp == 1 and ts > CONV_WIDTH - 1 and ts % 8 == 0
    sample_swa = (cache_swa0_k, cache_swa0_v, cache_swa1_k, cache_swa1_v, cache_swa2_k, cache_swa2_v)
    bias_prompt = _prompt_bias_tables(rel_bias)
    xp = x_prompt.reshape(bp * tp, d)
    xs = x_sample.reshape(bs * ts, d)
    new_p = [[] for _ in range(10)]
    new_s = [[] for _ in range(8)]
    hw = SWA_HPG * HEAD_DIM
    for l in range(depth):
        wt = jnp.swapaxes(w_in, 1, 2)[l]
        w_in_t = jnp.concatenate([wt[:BA_OFF], wt[BA_OFF:BA_OFF + PROJ_TILE], wt[CQ_OFF:]], axis=0).astype(BF16)
        lw = (w_in_t, w_conv[l], A_log[l], dt_bias[l], norm_delta[l], w_o_swa[l].astype(BF16),
              w_o_delta[l].astype(BF16), w_o_mem[l].astype(BF16), w_out[l].astype(BF16),
              norm_pre_mix[l], norm_post_mix[l], norm_pre_ffn[l], norm_post_ffn[l],
              w_ff1[l].astype(BF16), w_ff2[l].astype(BF16))
        mem = mem_prompt.reshape(-1, d)
        mkv = _matmul(_rmsnorm_cast(mem, norm_mem[l], _row_tile(mem.shape[0], 256)), w_mem_kv[l],
                      2 * MEM_WIDTH, _row_tile(mem.shape[0], 256), 512, "mem_kv")
        mk = mkv[:, :MEM_WIDTH].reshape(1, bp, -1, MEM_HEADS, MEM_HEAD_DIM)
        mv = mkv[:, MEM_WIDTH:].reshape(1, bp, -1, MEM_HEADS, MEM_HEAD_DIM)
        xp, main_p, s_p = _layer(xp, lw, l, rel_bias, bias_prompt, mk, mv, 0, None, None, None, bp)
        xs, main_s, s_s = _layer(xs, lw, l, rel_bias, bias_prompt, cache_mem_k, cache_mem_v, l,
                                 sample_swa, state_conv[l], state_delta[l], bs)
        vals_p = []
        for g, (window, _) in enumerate(SWA_GROUPS):
            keep = min(window, tp)
            for sec in (1, 2):
                c0 = sec * SWA_WIDTH + g * hw
                vals_p.append(main_p[tp - keep:, c0:c0 + hw].reshape(bp, keep, SWA_HPG, HEAD_DIM))
        vals_p.append(s_p.reshape(bp, DN_HEADS, HEAD_DIM, HEAD_DIM))
        vals_p.append(main_p[tp - (CONV_WIDTH - 1):, B_OFF:B_OFF + 3 * DN_WIDTH].reshape(bp, CONV_WIDTH - 1, -1))
        vals_p += [mk[0], mv[0]]
        vals_s = []
        main_s3 = main_s.reshape(bs, ts, -1)
        for g in range(len(SWA_GROUPS)):
            for sec in (1, 2):
                c0 = sec * SWA_WIDTH + g * hw
                vals_s.append(main_s3[:, :, c0:c0 + hw].reshape(bs, ts, SWA_HPG, HEAD_DIM))
        vals_s.append(s_s)
        vals_s.append(main_s3[:, ts - (CONV_WIDTH - 1):, B_OFF:B_OFF + 3 * DN_WIDTH])
        for lst, val in zip(new_p, vals_p):
            lst.append(val)
        for lst, val in zip(new_s, vals_s):
            lst.append(val)
    outs_p = [jnp.stack(t) for t in new_p]
    outs_s = [jnp.stack(t) for t in new_s]
    return (xp.reshape(bp, tp, d), xs.reshape(bs, ts, d), *outs_p, *outs_s)
```

```python
import functools
import math

import numpy as np
import jax
import jax.numpy as jnp
from jax import lax
from jax.experimental import pallas as pl
from jax.experimental.pallas import tpu as pltpu

EPS = 1e-6
HEAD_DIM = 128
SWA_GROUPS = ((128, 1), (512, 4), (2048, 16))
SWA_SPAN = 128
SWA_HPG = 4
SWA_WIDTH = SWA_HPG * len(SWA_GROUPS) * HEAD_DIM
DN_HEADS = 12
DN_WIDTH = DN_HEADS * HEAD_DIM
CONV_WIDTH = 4
DN_CHUNK = 64
MEM_HEADS = 4
MEM_HEAD_DIM = 256
MEM_WIDTH = MEM_HEADS * MEM_HEAD_DIM
N_BUCKETS = 32
MAX_DISTANCE = 2048
N_BRANCHES = 3

A_OFF = 0
B_OFF = 3 * SWA_WIDTH
Z_OFF = B_OFF + 3 * DN_WIDTH
BA_OFF = Z_OFF + DN_WIDTH
CQ_OFF = BA_OFF + 2 * DN_HEADS
GATE_OFF = CQ_OFF + MEM_WIDTH
LANES = 128
SUBLANES = 8
PROJ_TILE = 512
P_BA = BA_OFF
P_CQ = P_BA + PROJ_TILE
P_GATE = P_CQ + MEM_WIDTH
SWA_ROWS = 2048
SWA_UNROLL = 4
NEG = -1e30
VMEM_LIMIT = 56 * 1024 * 1024

BF16 = jnp.bfloat16
F32 = jnp.float32


def _params(n_grid):
    return pltpu.CompilerParams(dimension_semantics=("arbitrary",) * n_grid, vmem_limit_bytes=VMEM_LIMIT)


def _dot(a, b):
    return jnp.dot(a, b, preferred_element_type=F32)


def _dot_nt(a, b):
    return lax.dot_general(a, b, (((1,), (1,)), ((), ())), preferred_element_type=F32)


def _dot_tn(a, b):
    return lax.dot_general(a, b, (((0,), (0,)), ((), ())), preferred_element_type=F32)


def _split2(x):
    hi = x.astype(BF16)
    lo = (x - hi.astype(F32)).astype(BF16)
    return hi, lo


def _dot3(a, b, dot=_dot):
    ah, al = _split2(a)
    bh, bl = _split2(b)
    return dot(ah, bh) + (dot(ah, bl) + dot(al, bh))


def _rmsnorm_cast_kernel(x_ref, g_ref, o_ref):
    x = x_ref[...]
    y = x * lax.rsqrt(jnp.mean(x * x, axis=-1, keepdims=True) + EPS)
    o_ref[...] = (y * g_ref[...]).astype(o_ref.dtype)


def _rmsnorm_cast(x, g, tm):
    m, d = x.shape
    return pl.pallas_call(
        _rmsnorm_cast_kernel, grid=(m // tm,),
        in_specs=[pl.BlockSpec((tm, d), lambda i: (i, 0)), pl.BlockSpec((1, d), lambda i: (0, 0))],
        out_specs=pl.BlockSpec((tm, d), lambda i: (i, 0)),
        out_shape=jax.ShapeDtypeStruct((m, d), BF16),
        compiler_params=_params(1), name="rmsnorm_cast")(x, g.reshape(1, d))


def _matmul_kernel(x_ref, w_ref, o_ref):
    o_ref[...] = _dot(x_ref[...], w_ref[...].astype(BF16)).astype(o_ref.dtype)


def _matmul(x, w, n, tm, tn, name):
    m, k = x.shape
    assert n % tn == 0 and m % tm == 0
    return pl.pallas_call(
        _matmul_kernel, grid=(m // tm, n // tn),
        in_specs=[pl.BlockSpec((tm, k), lambda i, j: (i, 0)), pl.BlockSpec((k, tn), lambda i, j: (0, j))],
        out_specs=pl.BlockSpec((tm, tn), lambda i, j: (i, j)),
        out_shape=jax.ShapeDtypeStruct((m, n), F32),
        compiler_params=_params(2), name=name)(x, w)


def _in_proj_kernel(x_ref, g_ref, w_ref, o_ref, h_ref):
    @pl.when(pl.program_id(1) == 0)
    def _():
        h_ref[...] = _rms(x_ref[...], g_ref[...]).astype(BF16)

    o_ref[...] = _dot_nt(h_ref[...], w_ref[...].astype(BF16))


def _in_proj(x, g, wt, tm):
    m, k = x.shape
    tn = PROJ_TILE
    n = P_CQ + wt.shape[0] - CQ_OFF
    assert n % tn == 0 and m % tm == 0 and P_CQ % tn == 0 and CQ_OFF % SUBLANES == 0

    def w_rows(i, j):
        return (pl.multiple_of(jnp.where(j * tn < P_CQ, j * tn, j * tn - P_CQ + CQ_OFF), SUBLANES), 0)

    return pl.pallas_call(
        _in_proj_kernel, grid=(m // tm, n // tn),
        in_specs=[pl.BlockSpec((tm, k), lambda i, j: (i, 0), pipeline_mode=pl.Buffered(1)),
                  pl.BlockSpec((1, k), lambda i, j: (0, 0)),
                  pl.BlockSpec((pl.Element(tn), pl.Element(k)), w_rows)],
        out_specs=pl.BlockSpec((tm, tn), lambda i, j: (i, j)),
        out_shape=jax.ShapeDtypeStruct((m, n), F32),
        scratch_shapes=[pltpu.VMEM((tm, k), BF16)],
        compiler_params=_params(2), name="in_proj")(x, g.reshape(1, k), wt)


def _t5_bucket(dist):
    max_exact = N_BUCKETS // 2
    df = jnp.maximum(dist, 1).astype(F32)
    large = max_exact + (jnp.log(df / max_exact) / math.log(MAX_DISTANCE / max_exact)
                         * (N_BUCKETS - max_exact)).astype(jnp.int32)
    return jnp.where(dist < max_exact, dist, jnp.minimum(large, N_BUCKETS - 1))


def _group_bias(rel_bias, g):
    _, dil = SWA_GROUPS[g]
    dist = jnp.arange(SWA_SPAN + 1, dtype=jnp.int32) * dil
    onehot = _t5_bucket(dist)[None, :, None] == jnp.arange(N_BUCKETS, dtype=jnp.int32)[None, None, :]
    heads = jnp.transpose(rel_bias[:, g * SWA_HPG:(g + 1) * SWA_HPG].astype(F32))
    return jnp.sum(jnp.where(onehot, heads[:, None, :], 0.0), axis=-1)


def _prompt_bias_tables(rel_bias):
    span = SWA_SPAN
    first = (np.arange(2 * span) >= span)[None, None, :]
    out = []
    for g in range(len(SWA_GROUPS)):
        w = _group_bias(rel_bias, g)
        h = w.shape[0]
        p = 3 * span
        e = jnp.concatenate([jnp.full((h, span - 1), NEG, F32), w[:, ::-1], jnp.full((h, p - 2 * span), NEG, F32)], 1)
        skew = jnp.broadcast_to(e[:, None, :], (h, span, p)).reshape(h, span * p)[:, :span * (p - 1)]
        tab = skew.reshape(h, span, p - 1)[:, :, span - 1:3 * span - 1]
        out.append(jnp.stack([jnp.where(first, tab, NEG), tab], axis=1))
    return jnp.stack(out)


def _sample_bias_tables(rel_bias, g, n_res, n_new, cache_len):
    _, dil = SWA_GROUPS[g]
    w = _group_bias(rel_bias, g)
    h = w.shape[0]
    lm = cache_len // dil
    assert cache_len % dil == 0
    wpad = jnp.concatenate([w, jnp.full((h, lm + n_new), NEG, F32)], axis=1)
    res = np.arange(n_res)[None, None, :]
    rows = []
    for s in range(n_new):
        vec = wpad[:, s // dil + 1:lm + s // dil + 1][:, ::-1]
        rows.append(jnp.where(res == s % dil, vec[:, :, None], NEG).reshape(h, lm * n_res))
    s = np.arange(n_new)
    dist = s[:, None] - s[None, :]
    new = jnp.full((h, n_new, n_new), NEG, F32)
    for j in range(min((n_new - 1) // dil, SWA_SPAN) + 1):
        new = jnp.where((dist == j * dil)[None], w[:, j][:, None, None], new)
    return jnp.stack(rows, axis=1), new


def _swa_prompt_kernel(*refs, n_groups):
    ins = refs[:5 * n_groups]
    bias_ref = refs[5 * n_groups]
    o_ref = refs[5 * n_groups + 1]
    scr = refs[5 * n_groups + 2:]
    kext, vext, og, lg = scr[:n_groups], scr[n_groups:2 * n_groups], scr[2 * n_groups:3 * n_groups], scr[3 * n_groups:]
    n = pl.program_id(0)
    scale = HEAD_DIM ** -0.5
    span = SWA_SPAN
    for g in range(n_groups):
        _, dil = SWA_GROUPS[g]
        q_ref, kc_ref, vc_ref, kp_ref, vp_ref = ins[5 * g:5 * g + 5]
        blk = span * dil
        kext[g][0:blk, :] = kp_ref[...]
        kext[g][blk:, :] = kc_ref[...]
        vext[g][0:blk, :] = vp_ref[...]
        vext[g][blk:, :] = vc_ref[...]

        def body(it, carry, g=g, dil=dil, blk=blk, q_ref=q_ref):
            items = range(SWA_UNROLL)
            starts, sels = [], []
            for u in items:
                idx = it * SWA_UNROLL + u
                b = idx // dil
                starts.append(b * blk + (idx - b * dil))
                sels.append(jnp.where(jnp.logical_and(n == 0, b == 0), 0, 1))
            q = [q_ref[pl.ds(starts[u], span, stride=dil), :].astype(BF16) for u in items]
            kk = [kext[g][pl.ds(starts[u], 2 * span, stride=dil), :].astype(BF16) for u in items]
            s = [_dot_nt(q[u], kk[u]) * scale + bias_ref[g, 0, sels[u]] for u in items]
            m = [jnp.max(s[u], axis=-1, keepdims=True) for u in items]
            p = [jnp.exp(s[u] - m[u]) for u in items]
            den = [jnp.sum(p[u], axis=-1, keepdims=True) for u in items]
            vv = [vext[g][pl.ds(starts[u], 2 * span, stride=dil), :].astype(BF16) for u in items]
            o = [_dot(p[u].astype(BF16), vv[u]) / den[u] for u in items]
            for u in items:
                og[g][pl.ds(starts[u], span, stride=dil), :] = o[u]
                lg[g][pl.ds(starts[u], span, stride=dil), :] = jnp.broadcast_to(
                    m[u] + jnp.log(den[u]), (span, HEAD_DIM))
            return carry

        lax.fori_loop(0, SWA_ROWS // span // SWA_UNROLL, body, 0)
    lses = [lg[g][...] for g in range(n_groups)]
    mx = functools.reduce(jnp.maximum, lses)
    ws = [jnp.exp(l - mx) for l in lses]
    num = functools.reduce(lambda a, b: a + b, [w * og[g][...] for g, w in enumerate(ws)])
    o_ref[...] = (num / functools.reduce(lambda a, b: a + b, ws)).astype(o_ref.dtype)


def _swa_prompt(proj, bias_tables):
    t = proj.shape[0]
    n_groups = len(SWA_GROUPS)
    assert t % SWA_ROWS == 0
    in_specs, scratch_k, scratch_o = [], [], []
    for g, (_, dil) in enumerate(SWA_GROUPS):
        blk = SWA_SPAN * dil
        per = SWA_ROWS // blk
        assert SWA_ROWS % blk == 0
        qc, kc, vc = (sec * (SWA_WIDTH // HEAD_DIM) + g * SWA_HPG for sec in range(3))
        cur = lambda col: pl.BlockSpec((SWA_ROWS, HEAD_DIM), lambda n, h, col=col: (n, col + h))
        prev = lambda col, blk=blk, per=per: pl.BlockSpec(
            (blk, HEAD_DIM), lambda n, h, col=col, per=per: (jnp.maximum(n * per - 1, 0), col + h))
        in_specs += [cur(qc), cur(kc), cur(vc), prev(kc), prev(vc)]
        scratch_k.append(pltpu.VMEM((SWA_ROWS + blk, HEAD_DIM), F32))
        scratch_o.append(pltpu.VMEM((SWA_ROWS, HEAD_DIM), F32))
    in_specs.append(pl.BlockSpec((n_groups, 1, 2, SWA_SPAN, 2 * SWA_SPAN), lambda n, h: (0, h, 0, 0, 0)))
    return pl.pallas_call(
        functools.partial(_swa_prompt_kernel, n_groups=n_groups),
        grid=(t // SWA_ROWS, SWA_HPG),
        in_specs=in_specs,
        out_specs=pl.BlockSpec((SWA_ROWS, HEAD_DIM), lambda n, h: (n, h)),
        out_shape=jax.ShapeDtypeStruct((t, SWA_HPG * HEAD_DIM), BF16),
        scratch_shapes=scratch_k + scratch_k + scratch_o + scratch_o,
        compiler_params=_params(2), name="swa_prompt")(*([proj] * (5 * n_groups)), bias_tables)


def _swa_sample_kernel(*refs, n_groups):
    qkv_ref = refs[0]
    caches = refs[1:1 + 2 * n_groups]
    biases = refs[1 + 2 * n_groups:1 + 4 * n_groups]
    o_ref = refs[1 + 4 * n_groups]
    scale = HEAD_DIM ** -0.5
    hw = SWA_HPG * HEAD_DIM
    items = [(g, h) for h in range(SWA_HPG) for g in range(n_groups)]
    idx = range(len(items))

    def new(sec, g, h):
        col = sec * SWA_WIDTH + (g * SWA_HPG + h) * HEAD_DIM
        return qkv_ref[:, col:col + HEAD_DIM].astype(BF16)

    def cached(ref, h):
        nk = math.prod(ref.shape[2:-1]) // SWA_HPG
        if len(ref.shape) == 5:
            pick = (0, 0, slice(None), pl.ds(h, ref.shape[3] // SWA_HPG, stride=SWA_HPG), slice(None))
        else:
            pick = (0, 0, pl.ds(h, nk, stride=SWA_HPG), slice(None))
        return ref[pick].reshape(nk, HEAD_DIM).astype(BF16)

    q = [new(0, g, h) for g, h in items]
    sc = [_dot_nt(q[i], cached(caches[2 * g], h)) * scale + biases[2 * g][h] for i, (g, h) in enumerate(items)]
    sn = [_dot_nt(q[i], new(1, g, h)) * scale + biases[2 * g + 1][h] for i, (g, h) in enumerate(items)]
    m = [jnp.maximum(jnp.max(sc[i], axis=-1, keepdims=True), jnp.max(sn[i], axis=-1, keepdims=True)) for i in idx]
    pc = [jnp.exp(sc[i] - m[i]) for i in idx]
    pn = [jnp.exp(sn[i] - m[i]) for i in idx]
    den = [jnp.sum(pc[i], axis=-1, keepdims=True) + jnp.sum(pn[i], axis=-1, keepdims=True) for i in idx]
    o = [(_dot(pc[i].astype(BF16), cached(caches[2 * g + 1], h)) + _dot(pn[i].astype(BF16), new(2, g, h))) / den[i]
         for i, (g, h) in enumerate(items)]
    lse = [m[i] + jnp.log(den[i]) for i in idx]
    for h in range(SWA_HPG):
        mine = [i for i in idx if items[i][1] == h]
        mx = functools.reduce(jnp.maximum, [lse[i] for i in mine])
        ws = [jnp.exp(lse[i] - mx) for i in mine]
        num = functools.reduce(lambda a, b: a + b, [w * o[i] for w, i in zip(ws, mine)])
        o_ref[:, h * HEAD_DIM:(h + 1) * HEAD_DIM] = num / functools.reduce(lambda a, b: a + b, ws)


def _swa_sample(proj, caches, layer, rel_bias, n_new):
    rows_total = proj.shape[0]
    nb = rows_total // n_new
    n_groups = len(SWA_GROUPS)
    hw = SWA_HPG * HEAD_DIM
    cache_in, cache_specs, bias_in, bias_specs = [], [], [], []
    for g, (_, dil) in enumerate(SWA_GROUPS):
        depth, _, cache_len, heads, hd = caches[2 * g].shape
        if dil >= 2 * n_new and n_new % 8 == 0:
            n_res = n_new
            shape = (depth, nb, cache_len // dil, dil * heads, hd)
            spec = pl.BlockSpec((1, 1, cache_len // dil, n_res * heads, hd), lambda b: (layer, b, 0, 0, 0))
        else:
            n_res = dil
            shape = (depth, nb, cache_len * heads, hd)
            spec = pl.BlockSpec((1, 1, cache_len * heads, hd), lambda b: (layer, b, 0, 0))
        for c in caches[2 * g:2 * g + 2]:
            cache_in.append(c.reshape(shape))
            cache_specs.append(spec)
        for tab in _sample_bias_tables(rel_bias, g, n_res, n_new, cache_len):
            bias_in.append(tab)
            bias_specs.append(pl.BlockSpec(tab.shape, lambda b: (0, 0, 0)))
    return pl.pallas_call(
        functools.partial(_swa_sample_kernel, n_groups=n_groups),
        grid=(nb,),
        in_specs=[pl.BlockSpec((n_new, 3 * SWA_WIDTH), lambda b: (b, 0))] + cache_specs + bias_specs,
        out_specs=pl.BlockSpec((n_new, hw), lambda b: (b, 0)),
        out_shape=jax.ShapeDtypeStruct((rows_total, hw), F32),
        compiler_params=_params(1), name="swa_sample")(proj, *cache_in, *bias_in)


def _silu(x):
    return x * (1.0 / (1.0 + jnp.exp(-x)))


def _softplus(x):
    return jnp.maximum(x, 0.0) + jnp.log(1.0 + jnp.exp(-jnp.abs(x)))


def _inv_dot(a, b):
    return _dot(a.astype(BF16), b.astype(BF16))


HALO = 8


def _delta_scan_kernel(*refs, chunk, carry):
    if carry:
        x_ref, z_ref, ba_ref, wc_ref, alog_ref, dt_ref, gain_ref, o_ref, s_out_ref, s_scr, halo_scr = refs
        s_in_ref = halo_ref = None
    else:
        x_ref, z_ref, ba_ref, wc_ref, alog_ref, dt_ref, gain_ref, s_in_ref, halo_ref, o_ref, s_out_ref = refs
        s_scr = halo_scr = None
    c = chunk
    if carry:
        @pl.when(pl.program_id(0) == 0)
        def _():
            s_scr[...] = jnp.zeros_like(s_scr)
            halo_scr[...] = jnp.zeros_like(halo_scr)

    x = x_ref[...]
    xp = jnp.concatenate([halo_scr[...] if carry else halo_ref[...], x], axis=0)
    acc = x * wc_ref[CONV_WIDTH - 1:CONV_WIDTH, :]
    for t in range(1, CONV_WIDTH):
        acc = acc + pltpu.roll(xp, t, 0)[HALO:, :] * wc_ref[CONV_WIDTH - 1 - t:CONV_WIDTH - t, :]
    if carry:
        halo_scr[...] = x[c - HALO:, :]
    qkv = _silu(acc)

    ba = ba_ref[...]
    beta_all = 1.0 / (1.0 + jnp.exp(-ba))
    g_all = -jnp.exp(alog_ref[...]) * _softplus(ba + dt_ref[...])
    row = lax.broadcasted_iota(jnp.int32, (c, c), 0)
    col = lax.broadcasted_iota(jnp.int32, (c, c), 1)
    tri = row >= col
    strict = row > col
    tril = tri.astype(BF16)
    g1 = g_all.astype(BF16)
    r1 = g_all - g1.astype(F32)
    g2 = r1.astype(BF16)
    g3 = (r1 - g2.astype(F32)).astype(BF16)
    gcum_all = _dot(tril, g1) + (_dot(tril, g2) + _dot(tril, g3))
    gcum_t = gcum_all.T
    eye = (row == col).astype(F32)

    heads = range(DN_HEADS)
    hs = lambda sec, h: slice(sec * DN_WIDTH + h * HEAD_DIM, sec * DN_WIDTH + (h + 1) * HEAD_DIM)
    lane = lambda h: slice(DN_HEADS + h, DN_HEADS + h + 1)
    l2 = lambda t: t * lax.rsqrt(jnp.sum(t * t, axis=-1, keepdims=True) + EPS)
    q = [l2(qkv[:, hs(0, h)]) * (HEAD_DIM ** -0.5) for h in heads]
    k = [l2(qkv[:, hs(1, h)]) for h in heads]
    v = [qkv[:, hs(2, h)] for h in heads]
    bcol = [beta_all[:, h:h + 1] for h in heads]
    gc = [gcum_all[:, lane(h)] for h in heads]
    glast = [gcum_all[c - 1:c, lane(h)] for h in heads]
    decay = [jnp.exp(jnp.where(tri, gc[h] - gcum_t[lane(h), :], NEG)) for h in heads]
    kb = [k[h] * bcol[h] for h in heads]
    k16 = [k[h].astype(BF16) for h in heads]
    nmat = [jnp.where(strict, _dot_nt(kb[h].astype(BF16), k16[h]) * decay[h], 0.0) for h in heads]
    attn = [jnp.where(tri, _dot_nt(q[h].astype(BF16), k16[h]) * decay[h], 0.0) for h in heads]
    inv = [eye] * DN_HEADS
    blk = 1
    while blk < c:
        pair = (row // (2 * blk) == col // (2 * blk)) & (row % (2 * blk) >= blk) & (col % (2 * blk) < blk)
        low = [jnp.where(pair, nmat[h], 0.0) for h in heads]
        if blk == 1:
            inv = [eye - low[h] for h in heads]
        else:
            t = [_inv_dot(low[h], inv[h]) for h in heads]
            inv = [inv[h] - _inv_dot(inv[h], t[h]) for h in heads]
        blk *= 2
    eg = [jnp.exp(gc[h]) for h in heads]
    rhs = [jnp.concatenate([v[h] * bcol[h], kb[h] * eg[h]], axis=1) for h in heads]
    sol = [_inv_dot(inv[h], rhs[h]) for h in heads]
    s_prev = [s_scr[h] if carry else s_in_ref[0, h] for h in heads]
    s16 = [s_prev[h].astype(BF16) for h in heads]
    wq = [jnp.concatenate([sol[h][:, HEAD_DIM:], q[h] * eg[h]], axis=0).astype(BF16) for h in heads]
    ws = [_dot(wq[h], s16[h]) for h in heads]
    v_new = [(sol[h][:, :HEAD_DIM] - ws[h][:c]).astype(BF16) for h in heads]
    o = [ws[h][c:] + _dot(attn[h].astype(BF16), v_new[h]) for h in heads]
    kdec = [(k[h] * jnp.exp(glast[h] - gc[h])).astype(BF16) for h in heads]
    s_new = [s_prev[h] * jnp.exp(glast[h]) + _dot_tn(kdec[h], v_new[h]) for h in heads]
    for h in heads:
        if carry:
            s_scr[h] = s_new[h]
            s_out_ref[h] = s_new[h]
        else:
            s_out_ref[0, h] = s_new[h]
        y = o[h] * lax.rsqrt(jnp.mean(o[h] * o[h], axis=-1, keepdims=True) + EPS) * gain_ref[...]
        o_ref[:, hs(0, h)] = (y * _silu(z_ref[:, hs(0, h)])).astype(o_ref.dtype)


def _delta_scan(proj, w_conv, a_log, dt_bias, norm_delta, s0, conv_buf, chunk):
    m = proj.shape[0]
    carry = s0 is None
    n = m // chunk
    width = 3 * DN_WIDTH
    assert chunk >= HALO and B_OFF % width == 0
    pad = lambda v_, off: jnp.zeros((1, LANES), F32).at[0, off:off + DN_HEADS].set(v_.astype(F32))
    consts = [w_conv, pad(a_log, DN_HEADS), pad(dt_bias, DN_HEADS), norm_delta.reshape(1, HEAD_DIM).astype(F32)]
    const_specs = [pl.BlockSpec((CONV_WIDTH, width), lambda i: (0, 0))] + [pl.BlockSpec((1, LANES), lambda i: (0, 0))] * 3
    in_specs = [pl.BlockSpec((chunk, width), lambda i: (i, B_OFF // width)),
                pl.BlockSpec((chunk, DN_WIDTH), lambda i: (i, Z_OFF // DN_WIDTH)),
                pl.BlockSpec((chunk, LANES), lambda i: (i, P_BA // LANES))] + const_specs
    args = [proj, proj, proj] + consts
    state = (DN_HEADS, HEAD_DIM, HEAD_DIM)
    if carry:
        out_dtype = BF16
        s_shape, s_spec = state, pl.BlockSpec(state, lambda i: (0, 0, 0))
        scratch = [pltpu.VMEM(state, F32), pltpu.VMEM((HALO, width), F32)]
    else:
        out_dtype = F32
        halo = jnp.pad(conv_buf, ((0, 0), (HALO - (CONV_WIDTH - 1), 0), (0, 0))).reshape(n * HALO, width)
        in_specs += [pl.BlockSpec((1,) + state, lambda i: (i, 0, 0, 0)), pl.BlockSpec((HALO, width), lambda i: (i, 0))]
        args += [s0, halo]
        s_shape, s_spec = (n,) + state, pl.BlockSpec((1,) + state, lambda i: (i, 0, 0, 0))
        scratch = []
    return pl.pallas_call(
        functools.partial(_delta_scan_kernel, chunk=chunk, carry=carry), grid=(n,),
        in_specs=in_specs,
        out_specs=[pl.BlockSpec((chunk, DN_WIDTH), lambda i: (i, 0)), s_spec],
        out_shape=[jax.ShapeDtypeStruct((m, DN_WIDTH), out_dtype), jax.ShapeDtypeStruct(s_shape, F32)],
        scratch_shapes=scratch,
        compiler_params=_params(1), name="delta_scan")(*args)


def _mem_attn_kernel(q_ref, k_ref, v_ref, o_ref):
    scale = MEM_HEAD_DIM ** -0.5
    parts = MEM_HEAD_DIM // LANES
    period = MEM_HEADS * parts
    tokens = k_ref.shape[2] // period
    pick = lambda b, c, h: (0, b, pl.ds(c * MEM_HEADS + h, tokens, stride=period), slice(None))
    nb = k_ref.shape[1]
    rows = q_ref.shape[0] // nb
    items = [(b, h) for b in range(nb) for h in range(MEM_HEADS)]
    idx = range(len(items))
    q = [[q_ref[b * rows:(b + 1) * rows, h * MEM_HEAD_DIM + c * LANES:h * MEM_HEAD_DIM + (c + 1) * LANES]
          .astype(BF16) for c in range(parts)] for b, h in items]
    s = [functools.reduce(lambda x, y: x + y, [_dot_nt(q[i][c], k_ref[pick(b, c, h)].astype(BF16))
                                               for c in range(parts)]) * scale for i, (b, h) in enumerate(items)]
    p = [jnp.exp(s[i] - jnp.max(s[i], axis=-1, keepdims=True)) for i in idx]
    den = [jnp.sum(p[i], axis=-1, keepdims=True) for i in idx]
    for i, (b, h) in enumerate(items):
        for c in range(parts):
            col = h * MEM_HEAD_DIM + c * LANES
            o_ref[b * rows:(b + 1) * rows, col:col + LANES] = (
                _dot(p[i].astype(BF16), v_ref[pick(b, c, h)].astype(BF16)) / den[i]).astype(o_ref.dtype)


MEM_PER_STEP = 4


def _mem_attn(proj, mem_k, mem_v, layer, tm, out_dtype):
    m = proj.shape[0]
    assert P_CQ % MEM_WIDTH == 0
    depth, nb, tokens, heads, hd = mem_k.shape
    per = m // nb
    parts = hd // LANES
    mem_k, mem_v = (t.reshape(depth, nb, tokens, heads, parts, LANES).transpose(0, 1, 2, 4, 3, 5)
                    .reshape(depth, nb, tokens * heads * parts, LANES) for t in (mem_k, mem_v))
    mems = max(tm // per, 1)
    kv_spec = pl.BlockSpec((1, mems, tokens * heads * parts, LANES), lambda i: (layer, i * tm // (per * mems), 0, 0))
    return pl.pallas_call(
        _mem_attn_kernel, grid=(m // tm,),
        in_specs=[pl.BlockSpec((tm, MEM_WIDTH), lambda i: (i, P_CQ // MEM_WIDTH)), kv_spec, kv_spec],
        out_specs=pl.BlockSpec((tm, MEM_WIDTH), lambda i: (i, 0)),
        out_shape=jax.ShapeDtypeStruct((m, MEM_WIDTH), out_dtype),
        compiler_params=_params(1), name="mem_attn")(proj, mem_k, mem_v)


def _merge_kernel(a_ref, b_ref, c_ref, ga_ref, gb_ref, gc_ref, wa_ref, wb_ref, wc_ref, o_ref):
    def branch(x_ref, g_ref, w_ref):
        y = _dot(x_ref[...].astype(BF16), w_ref[...])
        return y * (1.0 / (1.0 + jnp.exp(-g_ref[...])))

    o_ref[...] = (branch(a_ref, ga_ref, wa_ref) + branch(b_ref, gb_ref, wb_ref)
                  + branch(c_ref, gc_ref, wc_ref)).astype(o_ref.dtype)


def _merge(a, b, c, tail, w_a, w_b, w_c, tm, tn):
    m = a.shape[0]
    d = w_a.shape[1]
    gate0 = P_GATE // tn
    assert P_GATE % tn == 0 and d % tn == 0
    row = lambda x: pl.BlockSpec((tm, x.shape[1]), lambda i, j: (i, 0))
    gate = lambda br: pl.BlockSpec((tm, tn), lambda i, j, br=br: (i, gate0 + br * (d // tn) + j))
    wsp = lambda w: pl.BlockSpec((w.shape[0], tn), lambda i, j: (0, j))
    return pl.pallas_call(
        _merge_kernel, grid=(m // tm, d // tn),
        in_specs=[row(a), row(b), row(c), gate(0), gate(1), gate(2), wsp(w_a), wsp(w_b), wsp(w_c)],
        out_specs=pl.BlockSpec((tm, tn), lambda i, j: (i, j)),
        out_shape=jax.ShapeDtypeStruct((m, d), BF16),
        compiler_params=_params(2), name="merge")(a, b, c, tail, tail, tail, w_a, w_b, w_c)


def _rms(y, g):
    return y * lax.rsqrt(jnp.mean(y * y, axis=-1, keepdims=True) + EPS) * g


def _proj_norm_residual_kernel(y_ref, w_ref, x_ref, g_post_ref, g_next_ref, o_ref, h_ref):
    x1 = x_ref[...] + _rms(_dot(y_ref[...], w_ref[...]), g_post_ref[...])
    o_ref[...] = x1
    h_ref[...] = _rms(x1, g_next_ref[...]).astype(h_ref.dtype)


def _proj_norm_residual(y, w, x, g_post, g_next, tm):
    m, d = x.shape
    k = y.shape[1]
    row = lambda width: pl.BlockSpec((tm, width), lambda i: (i, 0))
    gsp = pl.BlockSpec((1, d), lambda i: (0, 0))
    return pl.pallas_call(
        _proj_norm_residual_kernel, grid=(m // tm,),
        in_specs=[row(k), pl.BlockSpec((k, d), lambda i: (0, 0)), row(d), gsp, gsp],
        out_specs=[row(d), row(d)],
        out_shape=[jax.ShapeDtypeStruct((m, d), F32), jax.ShapeDtypeStruct((m, d), BF16)],
        compiler_params=_params(1), name="proj_norm_residual")(y, w, x, g_post.reshape(1, d), g_next.reshape(1, d))


FFN_SPLIT = 4


def _ffn_kernel(h_ref, w1_ref, w2_ref, o_ref, a_ref):
    j = pl.program_id(1)
    h = h_ref[...]
    piece = w1_ref.shape[1] // FFN_SPLIT
    for s in range(FFN_SPLIT):
        cols = slice(s * piece, (s + 1) * piece)
        a_ref[:, cols] = jnp.square(jnp.maximum(_dot(h, w1_ref[:, cols]), 0.0)).astype(BF16)
    y = _dot(a_ref[...], w2_ref[...])

    @pl.when(j == 0)
    def _():
        o_ref[...] = y

    @pl.when(j > 0)
    def _():
        o_ref[...] += y


def _ffn(h, w1, w2, tm, tf):
    m, d = h.shape
    f = w1.shape[1]
    return pl.pallas_call(
        _ffn_kernel, grid=(m // tm, f // tf),
        in_specs=[pl.BlockSpec((tm, d), lambda i, j: (i, 0)), pl.BlockSpec((d, tf), lambda i, j: (0, j)),
                  pl.BlockSpec((tf, d), lambda i, j: (j, 0))],
        out_specs=pl.BlockSpec((tm, d), lambda i, j: (i, 0)),
        out_shape=jax.ShapeDtypeStruct((m, d), F32),
        scratch_shapes=[pltpu.VMEM((tm, tf), BF16)],
        compiler_params=_params(2), name="ffn")(h, w1, w2)


def _norm_residual_kernel(x_ref, y_ref, g_ref, o_ref):
    o_ref[...] = x_ref[...] + _rms(y_ref[...], g_ref[...])


def _norm_residual(x, y, g, tm):
    m, d = x.shape
    row = pl.BlockSpec((tm, d), lambda i: (i, 0))
    return pl.pallas_call(
        _norm_residual_kernel, grid=(m // tm,),
        in_specs=[row, row, pl.BlockSpec((1, d), lambda i: (0, 0))], out_specs=row,
        out_shape=jax.ShapeDtypeStruct((m, d), F32),
        compiler_params=_params(1), name="norm_residual")(x, y, g.reshape(1, d))


def _row_tile(m, cap):
    t = min(m, cap)
    assert m % t == 0
    return t


def _layer(x, lw, layer, rel_bias, bias_prompt, mem_k, mem_v, mem_layer, swa_caches, conv_buf, s0, n_seq):
    (w_in_t, w_conv, a_log, dt_bias, norm_delta, w_o_swa, w_o_delta, w_o_mem, w_out,
     norm_pre_mix, norm_post_mix, norm_pre_ffn, norm_post_ffn, w_ff1, w_ff2) = lw
    m, d = x.shape
    seq = m // n_seq
    prompt = swa_caches is None
    tm = _row_tile(m, 1024)
    proj = _in_proj(x, norm_pre_mix, w_in_t, _row_tile(m, 2048))
    if prompt:
        assert n_seq == 1
        a_out = _swa_prompt(proj, bias_prompt)
        chunk = DN_CHUNK if seq % DN_CHUNK == 0 else seq
        b_out, s_new = _delta_scan(proj, w_conv, a_log, dt_bias, norm_delta, None, None, chunk)
        c_out = _mem_attn(proj, mem_k, mem_v, mem_layer, _row_tile(m, 512), BF16)
    else:
        a_out = _swa_sample(proj, swa_caches, layer, rel_bias, seq)
        b_out, s_new = _delta_scan(proj, w_conv, a_log, dt_bias, norm_delta, s0, conv_buf, seq)
        c_out = _mem_attn(proj, mem_k, mem_v, mem_layer, seq * math.gcd(n_seq, MEM_PER_STEP), F32)
    merged = _merge(a_out, b_out, c_out, proj, w_o_swa, w_o_delta, w_o_mem, tm, 512)
    x1, h2 = _proj_norm_residual(merged, w_out, x, norm_post_mix, norm_pre_ffn, _row_tile(m, 512))
    f = _ffn(h2, w_ff1, w_ff2, tm, 1024)
    x2 = _norm_residual(x1, f, norm_post_ffn, _row_tile(m, 512))
    return x2, proj, s_new


def kernel(x_prompt, x_sample, cache_swa0_k, cache_swa0_v, cache_swa1_k, cache_swa1_v, cache_swa2_k, cache_swa2_v, state_delta, state_conv, cache_mem_k, cache_mem_v, mem_prompt, rel_bias, w_in, w_conv, A_log, dt_bias, norm_delta, norm_mem, w_mem_kv, w_o_swa, w_o_delta, w_o_mem, w_out, norm_pre_mix, norm_post_mix, norm_pre_ffn, norm_post_ffn, w_ff1, w_ff2):
    depth = w_in.shape[0]
    bp, tp, d = x_prompt.shape
    bs, ts, _ = x_sample.shape
    assert bp == 1 and ts > CONV_WIDTH - 1 and ts % 8 == 0
    sample_swa = (cache_swa0_k, cache_swa0_v, cache_swa1_k, cache_swa1_v, cache_swa2_k, cache_swa2_v)
    bias_prompt = _prompt_bias_tables(rel_bias)
    xp = x_prompt.reshape(bp * tp, d)
    xs = x_sample.reshape(bs * ts, d)
    new_p = [[] for _ in range(10)]
    new_s = [[] for _ in range(8)]
    hw = SWA_HPG * HEAD_DIM
    for l in range(depth):
        lw = (jnp.swapaxes(w_in, 1, 2)[l], w_conv[l], A_log[l], dt_bias[l], norm_delta[l], w_o_swa[l].astype(BF16),
              w_o_delta[l].astype(BF16), w_o_mem[l].astype(BF16), w_out[l].astype(BF16),
              norm_pre_mix[l], norm_post_mix[l], norm_pre_ffn[l], norm_post_ffn[l],
              w_ff1[l].astype(BF16), w_ff2[l].astype(BF16))
        mem = mem_prompt.reshape(-1, d)
        mkv = _matmul(_rmsnorm_cast(mem, norm_mem[l], _row_tile(mem.shape[0], 256)), w_mem_kv[l],
                      2 * MEM_WIDTH, _row_tile(mem.shape[0], 256), 512, "mem_kv")
        mk = mkv[:, :MEM_WIDTH].reshape(1, bp, -1, MEM_HEADS, MEM_HEAD_DIM)
        mv = mkv[:, MEM_WIDTH:].reshape(1, bp, -1, MEM_HEADS, MEM_HEAD_DIM)
        xp, main_p, s_p = _layer(xp, lw, l, rel_bias, bias_prompt, mk, mv, 0, None, None, None, bp)
        xs, main_s, s_s = _layer(xs, lw, l, rel_bias, bias_prompt, cache_mem_k, cache_mem_v, l,
                                 sample_swa, state_conv[l], state_delta[l], bs)
        vals_p = []
        for g, (window, _) in enumerate(SWA_GROUPS):
            keep = min(window, tp)
            for sec in (1, 2):
                c0 = sec * SWA_WIDTH + g * hw
                vals_p.append(main_p[tp - keep:, c0:c0 + hw].reshape(bp, keep, SWA_HPG, HEAD_DIM))
        vals_p.append(s_p.reshape(bp, DN_HEADS, HEAD_DIM, HEAD_DIM))
        vals_p.append(main_p[tp - (CONV_WIDTH - 1):, B_OFF:B_OFF + 3 * DN_WIDTH].reshape(bp, CONV_WIDTH - 1, -1))
        vals_p += [mk[0], mv[0]]
        vals_s = []
        main_s3 = main_s.reshape(bs, ts, -1)
        for g in range(len(SWA_GROUPS)):
            for sec in (1, 2):
                c0 = sec * SWA_WIDTH + g * hw
                vals_s.append(main_s3[:, :, c0:c0 + hw].reshape(bs, ts, SWA_HPG, HEAD_DIM))
        vals_s.append(s_s)
        vals_s.append(main_s3[:, ts - (CONV_WIDTH - 1):, B_OFF:B_OFF + 3 * DN_WIDTH])
        for lst, val in zip(new_p, vals_p):
            lst.append(val)
        for lst, val in zip(new_s, vals_s):
            lst.append(val)
    outs_p = [jnp.stack(t) for t in new_p]
    outs_s = [jnp.stack(t) for t in new_s]
    return (xp.reshape(bp, tp, d), xs.reshape(bs, ts, d), *outs_p, *outs_s)
```

```python
import functools
import math

import numpy as np
import jax
import jax.numpy as jnp
from jax import lax
from jax.experimental import pallas as pl
from jax.experimental.pallas import tpu as pltpu

EPS = 1e-6
HEAD_DIM = 128
SWA_GROUPS = ((128, 1), (512, 4), (2048, 16))
SWA_SPAN = 128
SWA_HPG = 4
SWA_WIDTH = SWA_HPG * len(SWA_GROUPS) * HEAD_DIM
DN_HEADS = 12
DN_WIDTH = DN_HEADS * HEAD_DIM
CONV_WIDTH = 4
DN_CHUNK = 64
MEM_HEADS = 4
MEM_HEAD_DIM = 256
MEM_WIDTH = MEM_HEADS * MEM_HEAD_DIM
N_BUCKETS = 32
MAX_DISTANCE = 2048
N_BRANCHES = 3

A_OFF = 0
B_OFF = 3 * SWA_WIDTH
Z_OFF = B_OFF + 3 * DN_WIDTH
BA_OFF = Z_OFF + DN_WIDTH
CQ_OFF = BA_OFF + 2 * DN_HEADS
GATE_OFF = CQ_OFF + MEM_WIDTH
LANES = 128
SUBLANES = 8
PROJ_TILE = 512
P_BA = BA_OFF
P_CQ = P_BA + PROJ_TILE
P_GATE = P_CQ + MEM_WIDTH
SWA_ROWS = 2048
SWA_UNROLL = 8
NEG = -1e30
VMEM_LIMIT = 58 * 1024 * 1024

BF16 = jnp.bfloat16
F32 = jnp.float32


def _params(n_grid):
    return pltpu.CompilerParams(dimension_semantics=("arbitrary",) * n_grid, vmem_limit_bytes=VMEM_LIMIT)


def _dot(a, b):
    return jnp.dot(a, b, preferred_element_type=F32)


def _dot_nt(a, b):
    return lax.dot_general(a, b, (((1,), (1,)), ((), ())), preferred_element_type=F32)


def _dot_tn(a, b):
    return lax.dot_general(a, b, (((0,), (0,)), ((), ())), preferred_element_type=F32)


def _split2(x):
    hi = x.astype(BF16)
    lo = (x - hi.astype(F32)).astype(BF16)
    return hi, lo


def _dot3(a, b, dot=_dot):
    ah, al = _split2(a)
    bh, bl = _split2(b)
    return dot(ah, bh) + (dot(ah, bl) + dot(al, bh))


def _rmsnorm_cast_kernel(x_ref, g_ref, o_ref):
    x = x_ref[...]
    y = x * lax.rsqrt(jnp.mean(x * x, axis=-1, keepdims=True) + EPS)
    o_ref[...] = (y * g_ref[...]).astype(o_ref.dtype)


def _rmsnorm_cast(x, g, tm):
    m, d = x.shape
    return pl.pallas_call(
        _rmsnorm_cast_kernel, grid=(m // tm,),
        in_specs=[pl.BlockSpec((tm, d), lambda i: (i, 0)), pl.BlockSpec((1, d), lambda i: (0, 0))],
        out_specs=pl.BlockSpec((tm, d), lambda i: (i, 0)),
        out_shape=jax.ShapeDtypeStruct((m, d), BF16),
        compiler_params=_params(1), name="rmsnorm_cast")(x, g.reshape(1, d))


def _matmul_kernel(x_ref, w_ref, o_ref):
    o_ref[...] = _dot(x_ref[...], w_ref[...].astype(BF16)).astype(o_ref.dtype)


def _matmul(x, w, n, tm, tn, name):
    m, k = x.shape
    assert n % tn == 0 and m % tm == 0
    return pl.pallas_call(
        _matmul_kernel, grid=(m // tm, n // tn),
        in_specs=[pl.BlockSpec((tm, k), lambda i, j: (i, 0)), pl.BlockSpec((k, tn), lambda i, j: (0, j))],
        out_specs=pl.BlockSpec((tm, tn), lambda i, j: (i, j)),
        out_shape=jax.ShapeDtypeStruct((m, n), F32),
        compiler_params=_params(2), name=name)(x, w)


def _in_proj_kernel(x_ref, g_ref, w_ref, o_ref, h_ref):
    @pl.when(pl.program_id(1) == 0)
    def _():
        h_ref[...] = _rms(x_ref[...], g_ref[...]).astype(BF16)

    o_ref[...] = _dot_nt(h_ref[...], w_ref[...].astype(BF16))


def _in_proj(x, g, wt, tm):
    m, k = x.shape
    tn = PROJ_TILE
    n = P_CQ + wt.shape[0] - CQ_OFF
    assert n % tn == 0 and m % tm == 0 and P_CQ % tn == 0 and CQ_OFF % SUBLANES == 0

    def w_rows(i, j):
        return (pl.multiple_of(jnp.where(j * tn < P_CQ, j * tn, j * tn - P_CQ + CQ_OFF), SUBLANES), 0)

    return pl.pallas_call(
        _in_proj_kernel, grid=(m // tm, n // tn),
        in_specs=[pl.BlockSpec((tm, k), lambda i, j: (i, 0), pipeline_mode=pl.Buffered(1)),
                  pl.BlockSpec((1, k), lambda i, j: (0, 0)),
                  pl.BlockSpec((pl.Element(tn), pl.Element(k)), w_rows)],
        out_specs=pl.BlockSpec((tm, tn), lambda i, j: (i, j)),
        out_shape=jax.ShapeDtypeStruct((m, n), F32),
        scratch_shapes=[pltpu.VMEM((tm, k), BF16)],
        compiler_params=_params(2), name="in_proj")(x, g.reshape(1, k), wt)


def _t5_bucket(dist):
    max_exact = N_BUCKETS // 2
    df = jnp.maximum(dist, 1).astype(F32)
    large = max_exact + (jnp.log(df / max_exact) / math.log(MAX_DISTANCE / max_exact)
                         * (N_BUCKETS - max_exact)).astype(jnp.int32)
    return jnp.where(dist < max_exact, dist, jnp.minimum(large, N_BUCKETS - 1))


def _group_bias(rel_bias, g):
    _, dil = SWA_GROUPS[g]
    dist = jnp.arange(SWA_SPAN + 1, dtype=jnp.int32) * dil
    onehot = _t5_bucket(dist)[None, :, None] == jnp.arange(N_BUCKETS, dtype=jnp.int32)[None, None, :]
    heads = jnp.transpose(rel_bias[:, g * SWA_HPG:(g + 1) * SWA_HPG].astype(F32))
    return jnp.sum(jnp.where(onehot, heads[:, None, :], 0.0), axis=-1)


def _prompt_bias_tables(rel_bias):
    span = SWA_SPAN
    first = (np.arange(2 * span) >= span)[None, None, :]
    out = []
    for g in range(len(SWA_GROUPS)):
        w = _group_bias(rel_bias, g)
        h = w.shape[0]
        p = 3 * span
        e = jnp.concatenate([jnp.full((h, span - 1), NEG, F32), w[:, ::-1], jnp.full((h, p - 2 * span), NEG, F32)], 1)
        skew = jnp.broadcast_to(e[:, None, :], (h, span, p)).reshape(h, span * p)[:, :span * (p - 1)]
        tab = skew.reshape(h, span, p - 1)[:, :, span - 1:3 * span - 1]
        out.append(jnp.stack([jnp.where(first, tab, NEG), tab], axis=1))
    return jnp.stack(out)


def _sample_bias_tables(rel_bias, g, n_res, n_new, cache_len):
    _, dil = SWA_GROUPS[g]
    w = _group_bias(rel_bias, g)
    h = w.shape[0]
    lm = cache_len // dil
    assert cache_len % dil == 0
    wpad = jnp.concatenate([w, jnp.full((h, lm + n_new), NEG, F32)], axis=1)
    res = np.arange(n_res)[None, None, :]
    rows = []
    for s in range(n_new):
        vec = wpad[:, s // dil + 1:lm + s // dil + 1][:, ::-1]
        rows.append(jnp.where(res == s % dil, vec[:, :, None], NEG).reshape(h, lm * n_res))
    s = np.arange(n_new)
    dist = s[:, None] - s[None, :]
    new = jnp.full((h, n_new, n_new), NEG, F32)
    for j in range(min((n_new - 1) // dil, SWA_SPAN) + 1):
        new = jnp.where((dist == j * dil)[None], w[:, j][:, None, None], new)
    return jnp.stack(rows, axis=1), new


def _swa_prompt_kernel(*refs, n_groups):
    ins = refs[:5 * n_groups]
    bias_ref = refs[5 * n_groups]
    o_ref = refs[5 * n_groups + 1]
    scr = refs[5 * n_groups + 2:]
    kext, vext, og, lg = scr[:n_groups], scr[n_groups:2 * n_groups], scr[2 * n_groups:3 * n_groups], scr[3 * n_groups:]
    n = pl.program_id(0)
    scale = HEAD_DIM ** -0.5
    span = SWA_SPAN
    for g in range(n_groups):
        _, dil = SWA_GROUPS[g]
        q_ref, kc_ref, vc_ref, kp_ref, vp_ref = ins[5 * g:5 * g + 5]
        blk = span * dil
        kext[g][0:blk, :] = kp_ref[...]
        kext[g][blk:, :] = kc_ref[...]
        vext[g][0:blk, :] = vp_ref[...]
        vext[g][blk:, :] = vc_ref[...]

        def body(it, carry, g=g, dil=dil, blk=blk, q_ref=q_ref):
            items = range(SWA_UNROLL)
            starts, sels = [], []
            for u in items:
                idx = it * SWA_UNROLL + u
                b = idx // dil
                starts.append(b * blk + (idx - b * dil))
                sels.append(jnp.where(jnp.logical_and(n == 0, b == 0), 0, 1))
            q = [q_ref[pl.ds(starts[u], span, stride=dil), :].astype(BF16) for u in items]
            kk = [kext[g][pl.ds(starts[u], 2 * span, stride=dil), :].astype(BF16) for u in items]
            s = [_dot_nt(q[u], kk[u]) * scale + bias_ref[g, 0, sels[u]] for u in items]
            m = [jnp.max(s[u], axis=-1, keepdims=True) for u in items]
            p = [jnp.exp(s[u] - m[u]) for u in items]
            den = [jnp.sum(p[u], axis=-1, keepdims=True) for u in items]
            vv = [vext[g][pl.ds(starts[u], 2 * span, stride=dil), :].astype(BF16) for u in items]
            o = [_dot(p[u].astype(BF16), vv[u]) / den[u] for u in items]
            for u in items:
                og[g][pl.ds(starts[u], span, stride=dil), :] = o[u]
                lg[g][pl.ds(starts[u], span, stride=dil), :] = jnp.broadcast_to(
                    m[u] + jnp.log(den[u]), (span, HEAD_DIM))
            return carry

        lax.fori_loop(0, SWA_ROWS // span // SWA_UNROLL, body, 0)
    lses = [lg[g][...] for g in range(n_groups)]
    mx = functools.reduce(jnp.maximum, lses)
    ws = [jnp.exp(l - mx) for l in lses]
    num = functools.reduce(lambda a, b: a + b, [w * og[g][...] for g, w in enumerate(ws)])
    o_ref[...] = (num / functools.reduce(lambda a, b: a + b, ws)).astype(o_ref.dtype)


def _swa_prompt(proj, bias_tables):
    t = proj.shape[0]
    n_groups = len(SWA_GROUPS)
    assert t % SWA_ROWS == 0
    in_specs, scratch_k, scratch_o = [], [], []
    for g, (_, dil) in enumerate(SWA_GROUPS):
        blk = SWA_SPAN * dil
        per = SWA_ROWS // blk
        assert SWA_ROWS % blk == 0
        qc, kc, vc = (sec * (SWA_WIDTH // HEAD_DIM) + g * SWA_HPG for sec in range(3))
        cur = lambda col: pl.BlockSpec((SWA_ROWS, HEAD_DIM), lambda n, h, col=col: (n, col + h))
        prev = lambda col, blk=blk, per=per: pl.BlockSpec(
            (blk, HEAD_DIM), lambda n, h, col=col, per=per: (jnp.maximum(n * per - 1, 0), col + h))
        in_specs += [cur(qc), cur(kc), cur(vc), prev(kc), prev(vc)]
        scratch_k.append(pltpu.VMEM((SWA_ROWS + blk, HEAD_DIM), F32))
        scratch_o.append(pltpu.VMEM((SWA_ROWS, HEAD_DIM), F32))
    in_specs.append(pl.BlockSpec((n_groups, 1, 2, SWA_SPAN, 2 * SWA_SPAN), lambda n, h: (0, h, 0, 0, 0)))
    return pl.pallas_call(
        functools.partial(_swa_prompt_kernel, n_groups=n_groups),
        grid=(t // SWA_ROWS, SWA_HPG),
        in_specs=in_specs,
        out_specs=pl.BlockSpec((SWA_ROWS, HEAD_DIM), lambda n, h: (n, h)),
        out_shape=jax.ShapeDtypeStruct((t, SWA_HPG * HEAD_DIM), BF16),
        scratch_shapes=scratch_k + scratch_k + scratch_o + scratch_o,
        compiler_params=_params(2), name="swa_prompt")(*([proj] * (5 * n_groups)), bias_tables)


def _swa_sample_kernel(*refs, n_groups):
    qkv_ref = refs[0]
    caches = refs[1:1 + 2 * n_groups]
    biases = refs[1 + 2 * n_groups:1 + 4 * n_groups]
    o_ref = refs[1 + 4 * n_groups]
    scale = HEAD_DIM ** -0.5
    hw = SWA_HPG * HEAD_DIM
    items = [(g, h) for h in range(SWA_HPG) for g in range(n_groups)]
    idx = range(len(items))

    def new(sec, g, h):
        col = sec * SWA_WIDTH + (g * SWA_HPG + h) * HEAD_DIM
        return qkv_ref[:, col:col + HEAD_DIM].astype(BF16)

    def cached(ref, h):
        nk = math.prod(ref.shape[2:-1]) // SWA_HPG
        if len(ref.shape) == 5:
            pick = (0, 0, slice(None), pl.ds(h, ref.shape[3] // SWA_HPG, stride=SWA_HPG), slice(None))
        else:
            pick = (0, 0, pl.ds(h, nk, stride=SWA_HPG), slice(None))
        return ref[pick].reshape(nk, HEAD_DIM).astype(BF16)

    q = [new(0, g, h) for g, h in items]
    sc = [_dot_nt(q[i], cached(caches[2 * g], h)) * scale + biases[2 * g][h] for i, (g, h) in enumerate(items)]
    sn = [_dot_nt(q[i], new(1, g, h)) * scale + biases[2 * g + 1][h] for i, (g, h) in enumerate(items)]
    m = [jnp.maximum(jnp.max(sc[i], axis=-1, keepdims=True), jnp.max(sn[i], axis=-1, keepdims=True)) for i in idx]
    pc = [jnp.exp(sc[i] - m[i]) for i in idx]
    pn = [jnp.exp(sn[i] - m[i]) for i in idx]
    den = [jnp.sum(pc[i], axis=-1, keepdims=True) + jnp.sum(pn[i], axis=-1, keepdims=True) for i in idx]
    o = [(_dot(pc[i].astype(BF16), cached(caches[2 * g + 1], h)) + _dot(pn[i].astype(BF16), new(2, g, h))) / den[i]
         for i, (g, h) in enumerate(items)]
    lse = [m[i] + jnp.log(den[i]) for i in idx]
    for h in range(SWA_HPG):
        mine = [i for i in idx if items[i][1] == h]
        mx = functools.reduce(jnp.maximum, [lse[i] for i in mine])
        ws = [jnp.exp(lse[i] - mx) for i in mine]
        num = functools.reduce(lambda a, b: a + b, [w * o[i] for w, i in zip(ws, mine)])
        o_ref[:, h * HEAD_DIM:(h + 1) * HEAD_DIM] = num / functools.reduce(lambda a, b: a + b, ws)


def _swa_sample(proj, caches, layer, rel_bias, n_new):
    rows_total = proj.shape[0]
    nb = rows_total // n_new
    n_groups = len(SWA_GROUPS)
    hw = SWA_HPG * HEAD_DIM
    cache_in, cache_specs, bias_in, bias_specs = [], [], [], []
    for g, (_, dil) in enumerate(SWA_GROUPS):
        depth, _, cache_len, heads, hd = caches[2 * g].shape
        if dil >= 2 * n_new and n_new % 8 == 0:
            n_res = n_new
            shape = (depth, nb, cache_len // dil, dil * heads, hd)
            spec = pl.BlockSpec((1, 1, cache_len // dil, n_res * heads, hd), lambda b: (layer, b, 0, 0, 0))
        else:
            n_res = dil
            shape = (depth, nb, cache_len * heads, hd)
            spec = pl.BlockSpec((1, 1, cache_len * heads, hd), lambda b: (layer, b, 0, 0))
        for c in caches[2 * g:2 * g + 2]:
            cache_in.append(c.reshape(shape))
            cache_specs.append(spec)
        for tab in _sample_bias_tables(rel_bias, g, n_res, n_new, cache_len):
            bias_in.append(tab)
            bias_specs.append(pl.BlockSpec(tab.shape, lambda b: (0, 0, 0)))
    return pl.pallas_call(
        functools.partial(_swa_sample_kernel, n_groups=n_groups),
        grid=(nb,),
        in_specs=[pl.BlockSpec((n_new, 3 * SWA_WIDTH), lambda b: (b, 0))] + cache_specs + bias_specs,
        out_specs=pl.BlockSpec((n_new, hw), lambda b: (b, 0)),
        out_shape=jax.ShapeDtypeStruct((rows_total, hw), F32),
        compiler_params=_params(1), name="swa_sample")(proj, *cache_in, *bias_in)


def _silu(x):
    return x * (1.0 / (1.0 + jnp.exp(-x)))


def _softplus(x):
    return jnp.maximum(x, 0.0) + jnp.log(1.0 + jnp.exp(-jnp.abs(x)))


def _inv_dot(a, b):
    return _dot(a.astype(BF16), b.astype(BF16))


HALO = 8
DELTA_SEQS_PER_STEP = 8


def _delta_scan_kernel(*refs, chunk, nseq, carry):
    if carry:
        x_ref, z_ref, ba_ref, wc_ref, alog_ref, dt_ref, gain_ref, o_ref, s_out_ref, s_scr, halo_scr = refs
        s_in_ref = halo_ref = None
    else:
        x_ref, z_ref, ba_ref, wc_ref, alog_ref, dt_ref, gain_ref, s_in_ref, halo_ref, o_ref, s_out_ref = refs
        s_scr = halo_scr = None
    c = chunk
    rows = nseq * c
    seqs = range(nseq)
    part = lambda t, b: t[b * c:(b + 1) * c]
    if carry:
        @pl.when(pl.program_id(0) == 0)
        def _():
            s_scr[...] = jnp.zeros_like(s_scr)
            halo_scr[...] = jnp.zeros_like(halo_scr)

    x = x_ref[...]
    hist = halo_scr[...] if carry else halo_ref[...]
    xp = jnp.concatenate([t for b in seqs for t in (hist[b * HALO:(b + 1) * HALO], part(x, b))], axis=0)
    acc = x * wc_ref[CONV_WIDTH - 1:CONV_WIDTH, :]
    for t in range(1, CONV_WIDTH):
        rolled = pltpu.roll(xp, t, 0)
        shifted = jnp.concatenate([rolled[b * (HALO + c) + HALO:(b + 1) * (HALO + c)] for b in seqs], axis=0)
        acc = acc + shifted * wc_ref[CONV_WIDTH - 1 - t:CONV_WIDTH - t, :]
    if carry:
        halo_scr[...] = x[c - HALO:, :]
    qkv = _silu(acc)

    ba = ba_ref[...]
    beta_all = 1.0 / (1.0 + jnp.exp(-ba))
    g_all = -jnp.exp(alog_ref[...]) * _softplus(ba + dt_ref[...])
    row = lax.broadcasted_iota(jnp.int32, (rows, rows), 0)
    col = lax.broadcasted_iota(jnp.int32, (rows, rows), 1)
    same = (row // c) == (col // c)
    tri = (row >= col) & same
    strict = (row > col) & same
    tril = tri.astype(BF16)
    g1 = g_all.astype(BF16)
    r1 = g_all - g1.astype(F32)
    g2 = r1.astype(BF16)
    g3 = (r1 - g2.astype(F32)).astype(BF16)
    gcum_all = _dot(tril, g1) + (_dot(tril, g2) + _dot(tril, g3))
    gcum_t = gcum_all.T
    eye = (row == col).astype(F32)

    heads = range(DN_HEADS)
    hs = lambda sec, h: slice(sec * DN_WIDTH + h * HEAD_DIM, sec * DN_WIDTH + (h + 1) * HEAD_DIM)
    lane = lambda h: slice(DN_HEADS + h, DN_HEADS + h + 1)
    l2 = lambda t: t * lax.rsqrt(jnp.sum(t * t, axis=-1, keepdims=True) + EPS)
    q = [l2(qkv[:, hs(0, h)]) * (HEAD_DIM ** -0.5) for h in heads]
    k = [l2(qkv[:, hs(1, h)]) for h in heads]
    v = [qkv[:, hs(2, h)] for h in heads]
    bcol = [beta_all[:, h:h + 1] for h in heads]
    gc = [gcum_all[:, lane(h)] for h in heads]
    glast = [[gcum_all[(b + 1) * c - 1:(b + 1) * c, lane(h)] for b in seqs] for h in heads]
    decay = [jnp.exp(jnp.where(tri, gc[h] - gcum_t[lane(h), :], NEG)) for h in heads]
    kb = [k[h] * bcol[h] for h in heads]
    k16 = [k[h].astype(BF16) for h in heads]
    nmat = [jnp.where(strict, _dot_nt(kb[h].astype(BF16), k16[h]) * decay[h], 0.0) for h in heads]
    attn = [jnp.where(tri, _dot_nt(q[h].astype(BF16), k16[h]) * decay[h], 0.0) for h in heads]
    inv = [eye] * DN_HEADS
    blk = 1
    while blk < c:
        pair = (row // (2 * blk) == col // (2 * blk)) & (row % (2 * blk) >= blk) & (col % (2 * blk) < blk)
        low = [jnp.where(pair, nmat[h], 0.0) for h in heads]
        if blk == 1:
            inv = [eye - low[h] for h in heads]
        else:
            t = [_inv_dot(low[h], inv[h]) for h in heads]
            inv = [inv[h] - _inv_dot(inv[h], t[h]) for h in heads]
        blk *= 2
    eg = [jnp.exp(gc[h]) for h in heads]
    rhs = [jnp.concatenate([v[h] * bcol[h], kb[h] * eg[h]], axis=1) for h in heads]
    sol = [_inv_dot(inv[h], rhs[h]) for h in heads]
    qdec = [q[h] * eg[h] for h in heads]
    hb = [(h, b) for h in heads for b in seqs]
    s_prev = {(h, b): (s_scr[h] if carry else s_in_ref[b, h]) for h, b in hb}
    wq = {(h, b): jnp.concatenate([part(sol[h][:, HEAD_DIM:], b), part(qdec[h], b)], axis=0).astype(BF16)
          for h, b in hb}
    ws = {i: _dot(wq[i], s_prev[i].astype(BF16)) for i in hb}
    vn = {(h, b): part(sol[h][:, :HEAD_DIM], b) - ws[h, b][:c] for h, b in hb}
    v_new = [jnp.concatenate([vn[h, b] for b in seqs], axis=0).astype(BF16) for h in heads]
    o = [jnp.concatenate([ws[h, b][c:] for b in seqs], axis=0) + _dot(attn[h].astype(BF16), v_new[h])
         for h in heads]
    kdec = {(h, b): (part(k[h], b) * jnp.exp(glast[h][b] - part(gc[h], b))).astype(BF16) for h, b in hb}
    s_new = {(h, b): s_prev[h, b] * jnp.exp(glast[h][b]) + _dot_tn(kdec[h, b], vn[h, b].astype(BF16))
             for h, b in hb}
    for h in heads:
        if carry:
            s_scr[h] = s_new[h, 0]
            s_out_ref[h] = s_new[h, 0]
        else:
            for b in seqs:
                s_out_ref[b, h] = s_new[h, b]
        y = o[h] * lax.rsqrt(jnp.mean(o[h] * o[h], axis=-1, keepdims=True) + EPS) * gain_ref[...]
        o_ref[:, hs(0, h)] = (y * _silu(z_ref[:, hs(0, h)])).astype(o_ref.dtype)


def _delta_scan(proj, w_conv, a_log, dt_bias, norm_delta, s0, conv_buf, chunk):
    m = proj.shape[0]
    carry = s0 is None
    nseq = 1 if carry else math.gcd(m // chunk, DELTA_SEQS_PER_STEP)
    rows = nseq * chunk
    n = m // rows
    width = 3 * DN_WIDTH
    assert chunk >= HALO and B_OFF % width == 0
    pad = lambda v_, off: jnp.zeros((1, LANES), F32).at[0, off:off + DN_HEADS].set(v_.astype(F32))
    consts = [w_conv, pad(a_log, DN_HEADS), pad(dt_bias, DN_HEADS), norm_delta.reshape(1, HEAD_DIM).astype(F32)]
    const_specs = [pl.BlockSpec((CONV_WIDTH, width), lambda i: (0, 0))] + [pl.BlockSpec((1, LANES), lambda i: (0, 0))] * 3
    in_specs = [pl.BlockSpec((rows, width), lambda i: (i, B_OFF // width)),
                pl.BlockSpec((rows, DN_WIDTH), lambda i: (i, Z_OFF // DN_WIDTH)),
                pl.BlockSpec((rows, LANES), lambda i: (i, P_BA // LANES))] + const_specs
    args = [proj, proj, proj] + consts
    state = (DN_HEADS, HEAD_DIM, HEAD_DIM)
    if carry:
        out_dtype = BF16
        s_shape, s_spec = state, pl.BlockSpec(state, lambda i: (0, 0, 0))
        scratch = [pltpu.VMEM(state, F32), pltpu.VMEM((HALO, width), F32)]
    else:
        out_dtype = F32
        halo = jnp.pad(conv_buf, ((0, 0), (HALO - (CONV_WIDTH - 1), 0), (0, 0))).reshape(-1, width)
        in_specs += [pl.BlockSpec((nseq,) + state, lambda i: (i, 0, 0, 0)),
                     pl.BlockSpec((nseq * HALO, width), lambda i: (i, 0))]
        args += [s0, halo]
        s_shape, s_spec = (m // chunk,) + state, pl.BlockSpec((nseq,) + state, lambda i: (i, 0, 0, 0))
        scratch = []
    return pl.pallas_call(
        functools.partial(_delta_scan_kernel, chunk=chunk, nseq=nseq, carry=carry), grid=(n,),
        in_specs=in_specs,
        out_specs=[pl.BlockSpec((rows, DN_WIDTH), lambda i: (i, 0)), s_spec],
        out_shape=[jax.ShapeDtypeStruct((m, DN_WIDTH), out_dtype), jax.ShapeDtypeStruct(s_shape, F32)],
        scratch_shapes=scratch,
        compiler_params=_params(1), name="delta_scan")(*args)


def _mem_attn_kernel(q_ref, k_ref, v_ref, o_ref):
    scale = MEM_HEAD_DIM ** -0.5
    parts = MEM_HEAD_DIM // LANES
    period = MEM_HEADS * parts
    tokens = k_ref.shape[2] // period
    pick = lambda b, c, h: (0, b, pl.ds(c * MEM_HEADS + h, tokens, stride=period), slice(None))
    nb = k_ref.shape[1]
    rows = q_ref.shape[0] // nb
    items = [(b, h) for b in range(nb) for h in range(MEM_HEADS)]
    idx = range(len(items))
    q = [[q_ref[b * rows:(b + 1) * rows, h * MEM_HEAD_DIM + c * LANES:h * MEM_HEAD_DIM + (c + 1) * LANES]
          .astype(BF16) for c in range(parts)] for b, h in items]
    s = [functools.reduce(lambda x, y: x + y, [_dot_nt(q[i][c], k_ref[pick(b, c, h)].astype(BF16))
                                               for c in range(parts)]) * scale for i, (b, h) in enumerate(items)]
    p = [jnp.exp(s[i] - jnp.max(s[i], axis=-1, keepdims=True)) for i in idx]
    den = [jnp.sum(p[i], axis=-1, keepdims=True) for i in idx]
    for i, (b, h) in enumerate(items):
        for c in range(parts):
            col = h * MEM_HEAD_DIM + c * LANES
            o_ref[b * rows:(b + 1) * rows, col:col + LANES] = (
                _dot(p[i].astype(BF16), v_ref[pick(b, c, h)].astype(BF16)) / den[i]).astype(o_ref.dtype)


MEM_PER_STEP = 4


def _mem_attn(proj, mem_k, mem_v, layer, tm, out_dtype):
    m = proj.shape[0]
    assert P_CQ % MEM_WIDTH == 0
    depth, nb, tokens, heads, hd = mem_k.shape
    per = m // nb
    parts = hd // LANES
    mem_k, mem_v = (t.reshape(depth, nb, tokens, heads, parts, LANES).transpose(0, 1, 2, 4, 3, 5)
                    .reshape(depth, nb, tokens * heads * parts, LANES) for t in (mem_k, mem_v))
    mems = max(tm // per, 1)
    kv_spec = pl.BlockSpec((1, mems, tokens * heads * parts, LANES), lambda i: (layer, i * tm // (per * mems), 0, 0))
    return pl.pallas_call(
        _mem_attn_kernel, grid=(m // tm,),
        in_specs=[pl.BlockSpec((tm, MEM_WIDTH), lambda i: (i, P_CQ // MEM_WIDTH)), kv_spec, kv_spec],
        out_specs=pl.BlockSpec((tm, MEM_WIDTH), lambda i: (i, 0)),
        out_shape=jax.ShapeDtypeStruct((m, MEM_WIDTH), out_dtype),
        compiler_params=_params(1), name="mem_attn")(proj, mem_k, mem_v)


def _merge_kernel(a_ref, b_ref, c_ref, ga_ref, gb_ref, gc_ref, wa_ref, wb_ref, wc_ref, o_ref):
    def branch(x_ref, g_ref, w_ref):
        y = _dot(x_ref[...].astype(BF16), w_ref[...])
        return y * (1.0 / (1.0 + jnp.exp(-g_ref[...])))

    o_ref[...] = (branch(a_ref, ga_ref, wa_ref) + branch(b_ref, gb_ref, wb_ref)
                  + branch(c_ref, gc_ref, wc_ref)).astype(o_ref.dtype)


def _merge(a, b, c, tail, w_a, w_b, w_c, tm, tn):
    m = a.shape[0]
    d = w_a.shape[1]
    gate0 = P_GATE // tn
    assert P_GATE % tn == 0 and d % tn == 0
    row = lambda x: pl.BlockSpec((tm, x.shape[1]), lambda i, j: (i, 0))
    gate = lambda br: pl.BlockSpec((tm, tn), lambda i, j, br=br: (i, gate0 + br * (d // tn) + j))
    wsp = lambda w: pl.BlockSpec((w.shape[0], tn), lambda i, j: (0, j))
    return pl.pallas_call(
        _merge_kernel, grid=(m // tm, d // tn),
        in_specs=[row(a), row(b), row(c), gate(0), gate(1), gate(2), wsp(w_a), wsp(w_b), wsp(w_c)],
        out_specs=pl.BlockSpec((tm, tn), lambda i, j: (i, j)),
        out_shape=jax.ShapeDtypeStruct((m, d), BF16),
        compiler_params=_params(2), name="merge")(a, b, c, tail, tail, tail, w_a, w_b, w_c)


def _rms(y, g):
    return y * lax.rsqrt(jnp.mean(y * y, axis=-1, keepdims=True) + EPS) * g


def _proj_norm_residual_kernel(y_ref, w_ref, x_ref, g_post_ref, g_next_ref, o_ref, h_ref):
    x1 = x_ref[...] + _rms(_dot(y_ref[...], w_ref[...]), g_post_ref[...])
    o_ref[...] = x1
    h_ref[...] = _rms(x1, g_next_ref[...]).astype(h_ref.dtype)


def _proj_norm_residual(y, w, x, g_post, g_next, tm):
    m, d = x.shape
    k = y.shape[1]
    row = lambda width: pl.BlockSpec((tm, width), lambda i: (i, 0))
    gsp = pl.BlockSpec((1, d), lambda i: (0, 0))
    return pl.pallas_call(
        _proj_norm_residual_kernel, grid=(m // tm,),
        in_specs=[row(k), pl.BlockSpec((k, d), lambda i: (0, 0)), row(d), gsp, gsp],
        out_specs=[row(d), row(d)],
        out_shape=[jax.ShapeDtypeStruct((m, d), F32), jax.ShapeDtypeStruct((m, d), BF16)],
        compiler_params=_params(1), name="proj_norm_residual")(y, w, x, g_post.reshape(1, d), g_next.reshape(1, d))


FFN_SPLIT = 4


def _ffn_kernel(h_ref, w1_ref, w2_ref, x_ref, g_ref, o_ref, a_ref):
    j = pl.program_id(1)
    last = pl.num_programs(1) - 1
    h = h_ref[...]
    piece = w1_ref.shape[1] // FFN_SPLIT
    for s in range(FFN_SPLIT):
        cols = slice(s * piece, (s + 1) * piece)
        a_ref[:, cols] = jnp.square(jnp.maximum(_dot(h, w1_ref[:, cols]), 0.0)).astype(BF16)
    y = _dot(a_ref[...], w2_ref[...])

    @pl.when(j == 0)
    def _():
        o_ref[...] = y

    @pl.when(jnp.logical_and(j > 0, j < last))
    def _():
        o_ref[...] += y

    @pl.when(j == last)
    def _():
        o_ref[...] = x_ref[...] + _rms(o_ref[...] + y, g_ref[...])


def _ffn(h, w1, w2, x, g, tm, tf):
    m, d = h.shape
    f = w1.shape[1]
    assert f // tf >= 2
    once = lambda shape: pl.BlockSpec(shape, lambda i, j: (i, 0), pipeline_mode=pl.Buffered(1))
    return pl.pallas_call(
        _ffn_kernel, grid=(m // tm, f // tf),
        in_specs=[pl.BlockSpec((tm, d), lambda i, j: (i, 0)), pl.BlockSpec((d, tf), lambda i, j: (0, j)),
                  pl.BlockSpec((tf, d), lambda i, j: (j, 0)), once((tm, d)), pl.BlockSpec((1, d), lambda i, j: (0, 0))],
        out_specs=once((tm, d)),
        out_shape=jax.ShapeDtypeStruct((m, d), F32),
        scratch_shapes=[pltpu.VMEM((tm, tf), BF16)],
        compiler_params=_params(2), name="ffn")(h, w1, w2, x, g.reshape(1, d))


def _row_tile(m, cap):
    t = min(m, cap)
    assert m % t == 0
    return t


def _layer(x, lw, layer, rel_bias, bias_prompt, mem_k, mem_v, mem_layer, swa_caches, conv_buf, s0, n_seq):
    (w_in_t, w_conv, a_log, dt_bias, norm_delta, w_o_swa, w_o_delta, w_o_mem, w_out,
     norm_pre_mix, norm_post_mix, norm_pre_ffn, norm_post_ffn, w_ff1, w_ff2) = lw
    m, d = x.shape
    seq = m // n_seq
    prompt = swa_caches is None
    tm = _row_tile(m, 1024)
    proj = _in_proj(x, norm_pre_mix, w_in_t, _row_tile(m, 2048))
    if prompt:
        assert n_seq == 1
        a_out = _swa_prompt(proj, bias_prompt)
        chunk = DN_CHUNK if seq % DN_CHUNK == 0 else seq
        b_out, s_new = _delta_scan(proj, w_conv, a_log, dt_bias, norm_delta, None, None, chunk)
        c_out = _mem_attn(proj, mem_k, mem_v, mem_layer, _row_tile(m, 512), BF16)
    else:
        a_out = _swa_sample(proj, swa_caches, layer, rel_bias, seq)
        b_out, s_new = _delta_scan(proj, w_conv, a_log, dt_bias, norm_delta, s0, conv_buf, seq)
        c_out = _mem_attn(proj, mem_k, mem_v, mem_layer, seq * math.gcd(n_seq, MEM_PER_STEP), F32)
    merged = _merge(a_out, b_out, c_out, proj, w_o_swa, w_o_delta, w_o_mem, tm, 512)
    x1, h2 = _proj_norm_residual(merged, w_out, x, norm_post_mix, norm_pre_ffn, _row_tile(m, 512))
    x2 = _ffn(h2, w_ff1, w_ff2, x1, norm_post_ffn, tm, 1024)
    return x2, proj, s_new


def kernel(x_prompt, x_sample, cache_swa0_k, cache_swa0_v, cache_swa1_k, cache_swa1_v, cache_swa2_k, cache_swa2_v, state_delta, state_conv, cache_mem_k, cache_mem_v, mem_prompt, rel_bias, w_in, w_conv, A_log, dt_bias, norm_delta, norm_mem, w_mem_kv, w_o_swa, w_o_delta, w_o_mem, w_out, norm_pre_mix, norm_post_mix, norm_pre_ffn, norm_post_ffn, w_ff1, w_ff2):
    depth = w_in.shape[0]
    bp, tp, d = x_prompt.shape
    bs, ts, _ = x_sample.shape
    assert bp == 1 and ts > CONV_WIDTH - 1 and ts % 8 == 0
    sample_swa = (cache_swa0_k, cache_swa0_v, cache_swa1_k, cache_swa1_v, cache_swa2_k, cache_swa2_v)
    bias_prompt = _prompt_bias_tables(rel_bias)
    xp = x_prompt.reshape(bp * tp, d)
    xs = x_sample.reshape(bs * ts, d)
    new_p = [[] for _ in range(10)]
    new_s = [[] for _ in range(8)]
    hw = SWA_HPG * HEAD_DIM
    for l in range(depth):
        lw = (jnp.swapaxes(w_in, 1, 2)[l], w_conv[l], A_log[l], dt_bias[l], norm_delta[l], w_o_swa[l].astype(BF16),
              w_o_delta[l].astype(BF16), w_o_mem[l].astype(BF16), w_out[l].astype(BF16),
              norm_pre_mix[l], norm_post_mix[l], norm_pre_ffn[l], norm_post_ffn[l],
              w_ff1[l].astype(BF16), w_ff2[l].astype(BF16))
        mem = mem_prompt.reshape(-1, d)
        mkv = _matmul(_rmsnorm_cast(mem, norm_mem[l], _row_tile(mem.shape[0], 256)), w_mem_kv[l],
                      2 * MEM_WIDTH, _row_tile(mem.shape[0], 256), 512, "mem_kv")
        mk = mkv[:, :MEM_WIDTH].reshape(1, bp, -1, MEM_HEADS, MEM_HEAD_DIM)
        mv = mkv[:, MEM_WIDTH:].reshape(1, bp, -1, MEM_HEADS, MEM_HEAD_DIM)
        xp, main_p, s_p = _layer(xp, lw, l, rel_bias, bias_prompt, mk, mv, 0, None, None, None, bp)
        xs, main_s, s_s = _layer(xs, lw, l, rel_bias, bias_prompt, cache_mem_k, cache_mem_v, l,
                                 sample_swa, state_conv[l], state_delta[l], bs)
        vals_p = []
        for g, (window, _) in enumerate(SWA_GROUPS):
            keep = min(window, tp)
            for sec in (1, 2):
                c0 = sec * SWA_WIDTH + g * hw
                vals_p.append(main_p[tp - keep:, c0:c0 + hw].reshape(bp, keep, SWA_HPG, HEAD_DIM))
        vals_p.append(s_p.reshape(bp, DN_HEADS, HEAD_DIM, HEAD_DIM))
        vals_p.append(main_p[tp - (CONV_WIDTH - 1):, B_OFF:B_OFF + 3 * DN_WIDTH].reshape(bp, CONV_WIDTH - 1, -1))
        vals_p += [mk[0], mv[0]]
        vals_s = []
        main_s3 = main_s.reshape(bs, ts, -1)
        for g in range(len(SWA_GROUPS)):
            for sec in (1, 2):
                c0 = sec * SWA_WIDTH + g * hw
                vals_s.append(main_s3[:, :, c0:c0 + hw].reshape(bs, ts, SWA_HPG, HEAD_DIM))
        vals_s.append(s_s)
        vals_s.append(main_s3[:, ts - (CONV_WIDTH - 1):, B_OFF:B_OFF + 3 * DN_WIDTH])
        for lst, val in zip(new_p, vals_p):
            lst.append(val)
        for lst, val in zip(new_s, vals_s):
            lst.append(val)
    outs_p = [jnp.stack(t) for t in new_p]
    outs_s = [jnp.stack(t) for t in new_s]
    return (xp.reshape(bp, tp, d), xs.reshape(bs, ts, d), *outs_p, *outs_s)
```

```python
import functools
import math

import numpy as np
import jax
import jax.numpy as jnp
from jax import lax
from jax.experimental import pallas as pl
from jax.experimental.pallas import tpu as pltpu

EPS = 1e-6
HEAD_DIM = 128
SWA_GROUPS = ((128, 1), (512, 4), (2048, 16))
SWA_SPAN = 128
SWA_HPG = 4
SWA_WIDTH = SWA_HPG * len(SWA_GROUPS) * HEAD_DIM
DN_HEADS = 12
DN_WIDTH = DN_HEADS * HEAD_DIM
CONV_WIDTH = 4
DN_CHUNK = 64
MEM_HEADS = 4
MEM_HEAD_DIM = 256
MEM_WIDTH = MEM_HEADS * MEM_HEAD_DIM
N_BUCKETS = 32
MAX_DISTANCE = 2048
N_BRANCHES = 3

A_OFF = 0
B_OFF = 3 * SWA_WIDTH
Z_OFF = B_OFF + 3 * DN_WIDTH
BA_OFF = Z_OFF + DN_WIDTH
CQ_OFF = BA_OFF + 2 * DN_HEADS
GATE_OFF = CQ_OFF + MEM_WIDTH
LANES = 128
SUBLANES = 8
PROJ_TILE = 512
P_BA = BA_OFF
P_CQ = P_BA + PROJ_TILE
P_GATE = P_CQ + MEM_WIDTH
SWA_ROWS = 2048
SWA_UNROLL = 8
NEG = -1e30
VMEM_LIMIT = 58 * 1024 * 1024

BF16 = jnp.bfloat16
F32 = jnp.float32


def _params(n_grid):
    return pltpu.CompilerParams(dimension_semantics=("arbitrary",) * n_grid, vmem_limit_bytes=VMEM_LIMIT)


def _dot(a, b):
    return jnp.dot(a, b, preferred_element_type=F32)


def _dot_nt(a, b):
    return lax.dot_general(a, b, (((1,), (1,)), ((), ())), preferred_element_type=F32)


def _dot_tn(a, b):
    return lax.dot_general(a, b, (((0,), (0,)), ((), ())), preferred_element_type=F32)


def _split2(x):
    hi = x.astype(BF16)
    lo = (x - hi.astype(F32)).astype(BF16)
    return hi, lo


def _dot3(a, b, dot=_dot):
    ah, al = _split2(a)
    bh, bl = _split2(b)
    return dot(ah, bh) + (dot(ah, bl) + dot(al, bh))


def _rmsnorm_cast_kernel(x_ref, g_ref, o_ref):
    x = x_ref[...]
    y = x * lax.rsqrt(jnp.mean(x * x, axis=-1, keepdims=True) + EPS)
    o_ref[...] = (y * g_ref[...]).astype(o_ref.dtype)


def _rmsnorm_cast(x, g, tm):
    m, d = x.shape
    return pl.pallas_call(
        _rmsnorm_cast_kernel, grid=(m // tm,),
        in_specs=[pl.BlockSpec((tm, d), lambda i: (i, 0)), pl.BlockSpec((1, d), lambda i: (0, 0))],
        out_specs=pl.BlockSpec((tm, d), lambda i: (i, 0)),
        out_shape=jax.ShapeDtypeStruct((m, d), BF16),
        compiler_params=_params(1), name="rmsnorm_cast")(x, g.reshape(1, d))


def _matmul_kernel(x_ref, w_ref, o_ref):
    o_ref[...] = _dot(x_ref[...], w_ref[...].astype(BF16)).astype(o_ref.dtype)


def _matmul(x, w, n, tm, tn, name):
    m, k = x.shape
    assert n % tn == 0 and m % tm == 0
    return pl.pallas_call(
        _matmul_kernel, grid=(m // tm, n // tn),
        in_specs=[pl.BlockSpec((tm, k), lambda i, j: (i, 0)), pl.BlockSpec((k, tn), lambda i, j: (0, j))],
        out_specs=pl.BlockSpec((tm, tn), lambda i, j: (i, j)),
        out_shape=jax.ShapeDtypeStruct((m, n), F32),
        compiler_params=_params(2), name=name)(x, w)


def _in_proj_kernel(x_ref, g_ref, w_ref, o_ref, h_ref):
    @pl.when(pl.program_id(1) == 0)
    def _():
        h_ref[...] = _rms(x_ref[...], g_ref[...]).astype(BF16)

    o_ref[...] = _dot_nt(h_ref[...], w_ref[...].astype(BF16))


def _in_proj(x, g, wt, tm):
    m, k = x.shape
    tn = PROJ_TILE
    n = P_CQ + wt.shape[0] - CQ_OFF
    assert n % tn == 0 and m % tm == 0 and P_CQ % tn == 0 and CQ_OFF % SUBLANES == 0

    def w_rows(i, j):
        return (pl.multiple_of(jnp.where(j * tn < P_CQ, j * tn, j * tn - P_CQ + CQ_OFF), SUBLANES), 0)

    return pl.pallas_call(
        _in_proj_kernel, grid=(m // tm, n // tn),
        in_specs=[pl.BlockSpec((tm, k), lambda i, j: (i, 0), pipeline_mode=pl.Buffered(1)),
                  pl.BlockSpec((1, k), lambda i, j: (0, 0)),
                  pl.BlockSpec((pl.Element(tn), pl.Element(k)), w_rows)],
        out_specs=pl.BlockSpec((tm, tn), lambda i, j: (i, j)),
        out_shape=jax.ShapeDtypeStruct((m, n), F32),
        scratch_shapes=[pltpu.VMEM((tm, k), BF16)],
        compiler_params=_params(2), name="in_proj")(x, g.reshape(1, k), wt)


def _t5_bucket(dist):
    max_exact = N_BUCKETS // 2
    df = jnp.maximum(dist, 1).astype(F32)
    large = max_exact + (jnp.log(df / max_exact) / math.log(MAX_DISTANCE / max_exact)
                         * (N_BUCKETS - max_exact)).astype(jnp.int32)
    return jnp.where(dist < max_exact, dist, jnp.minimum(large, N_BUCKETS - 1))


def _group_bias(rel_bias, g):
    _, dil = SWA_GROUPS[g]
    dist = jnp.arange(SWA_SPAN + 1, dtype=jnp.int32) * dil
    onehot = _t5_bucket(dist)[None, :, None] == jnp.arange(N_BUCKETS, dtype=jnp.int32)[None, None, :]
    heads = jnp.transpose(rel_bias[:, g * SWA_HPG:(g + 1) * SWA_HPG].astype(F32))
    return jnp.sum(jnp.where(onehot, heads[:, None, :], 0.0), axis=-1)


def _prompt_bias_tables(rel_bias):
    span = SWA_SPAN
    first = (np.arange(2 * span) >= span)[None, None, :]
    out = []
    for g in range(len(SWA_GROUPS)):
        w = _group_bias(rel_bias, g)
        h = w.shape[0]
        p = 3 * span
        e = jnp.concatenate([jnp.full((h, span - 1), NEG, F32), w[:, ::-1], jnp.full((h, p - 2 * span), NEG, F32)], 1)
        skew = jnp.broadcast_to(e[:, None, :], (h, span, p)).reshape(h, span * p)[:, :span * (p - 1)]
        tab = skew.reshape(h, span, p - 1)[:, :, span - 1:3 * span - 1]
        out.append(jnp.stack([jnp.where(first, tab, NEG), tab], axis=1))
    return jnp.stack(out)


def _sample_bias_tables(rel_bias, g, n_res, n_new, cache_len):
    _, dil = SWA_GROUPS[g]
    w = _group_bias(rel_bias, g)
    h = w.shape[0]
    lm = cache_len // dil
    assert cache_len % dil == 0
    wpad = jnp.concatenate([w, jnp.full((h, lm + n_new), NEG, F32)], axis=1)
    res = np.arange(n_res)[None, None, :]
    rows = []
    for s in range(n_new):
        vec = wpad[:, s // dil + 1:lm + s // dil + 1][:, ::-1]
        rows.append(jnp.where(res == s % dil, vec[:, :, None], NEG).reshape(h, lm * n_res))
    s = np.arange(n_new)
    dist = s[:, None] - s[None, :]
    new = jnp.full((h, n_new, n_new), NEG, F32)
    for j in range(min((n_new - 1) // dil, SWA_SPAN) + 1):
        new = jnp.where((dist == j * dil)[None], w[:, j][:, None, None], new)
    return jnp.stack(rows, axis=1), new


def _swa_prompt_kernel(*refs, n_groups):
    ins = refs[:5 * n_groups]
    bias_ref = refs[5 * n_groups]
    o_ref = refs[5 * n_groups + 1]
    scr = refs[5 * n_groups + 2:]
    kext, vext, og, lg = scr[:n_groups], scr[n_groups:2 * n_groups], scr[2 * n_groups:3 * n_groups], scr[3 * n_groups:]
    n = pl.program_id(0)
    scale = HEAD_DIM ** -0.5
    span = SWA_SPAN
    for g in range(n_groups):
        _, dil = SWA_GROUPS[g]
        q_ref, kc_ref, vc_ref, kp_ref, vp_ref = ins[5 * g:5 * g + 5]
        blk = span * dil
        kext[g][0:blk, :] = kp_ref[...]
        kext[g][blk:, :] = kc_ref[...]
        vext[g][0:blk, :] = vp_ref[...]
        vext[g][blk:, :] = vc_ref[...]

        def body(it, carry, g=g, dil=dil, blk=blk, q_ref=q_ref):
            items = range(SWA_UNROLL)
            starts, sels = [], []
            for u in items:
                idx = it * SWA_UNROLL + u
                b = idx // dil
                starts.append(b * blk + (idx - b * dil))
                sels.append(jnp.where(jnp.logical_and(n == 0, b == 0), 0, 1))
            q = [q_ref[pl.ds(starts[u], span, stride=dil), :].astype(BF16) for u in items]
            kk = [kext[g][pl.ds(starts[u], 2 * span, stride=dil), :].astype(BF16) for u in items]
            s = [_dot_nt(q[u], kk[u]) * scale + bias_ref[g, 0, sels[u]] for u in items]
            m = [jnp.max(s[u], axis=-1, keepdims=True) for u in items]
            p = [jnp.exp(s[u] - m[u]) for u in items]
            den = [jnp.sum(p[u], axis=-1, keepdims=True) for u in items]
            vv = [vext[g][pl.ds(starts[u], 2 * span, stride=dil), :].astype(BF16) for u in items]
            o = [_dot(p[u].astype(BF16), vv[u]) / den[u] for u in items]
            for u in items:
                og[g][pl.ds(starts[u], span, stride=dil), :] = o[u]
                lg[g][pl.ds(starts[u], span, stride=dil), :] = jnp.broadcast_to(
                    m[u] + jnp.log(den[u]), (span, HEAD_DIM))
            return carry

        lax.fori_loop(0, SWA_ROWS // span // SWA_UNROLL, body, 0)
    lses = [lg[g][...] for g in range(n_groups)]
    mx = functools.reduce(jnp.maximum, lses)
    ws = [jnp.exp(l - mx) for l in lses]
    num = functools.reduce(lambda a, b: a + b, [w * og[g][...] for g, w in enumerate(ws)])
    o_ref[...] = (num / functools.reduce(lambda a, b: a + b, ws)).astype(o_ref.dtype)


def _swa_prompt(proj, bias_tables):
    t = proj.shape[0]
    n_groups = len(SWA_GROUPS)
    assert t % SWA_ROWS == 0
    in_specs, scratch_k, scratch_o = [], [], []
    for g, (_, dil) in enumerate(SWA_GROUPS):
        blk = SWA_SPAN * dil
        per = SWA_ROWS // blk
        assert SWA_ROWS % blk == 0
        qc, kc, vc = (sec * (SWA_WIDTH // HEAD_DIM) + g * SWA_HPG for sec in range(3))
        cur = lambda col: pl.BlockSpec((SWA_ROWS, HEAD_DIM), lambda n, h, col=col: (n, col + h))
        prev = lambda col, blk=blk, per=per: pl.BlockSpec(
            (blk, HEAD_DIM), lambda n, h, col=col, per=per: (jnp.maximum(n * per - 1, 0), col + h))
        in_specs += [cur(qc), cur(kc), cur(vc), prev(kc), prev(vc)]
        scratch_k.append(pltpu.VMEM((SWA_ROWS + blk, HEAD_DIM), F32))
        scratch_o.append(pltpu.VMEM((SWA_ROWS, HEAD_DIM), F32))
    in_specs.append(pl.BlockSpec((n_groups, 1, 2, SWA_SPAN, 2 * SWA_SPAN), lambda n, h: (0, h, 0, 0, 0)))
    return pl.pallas_call(
        functools.partial(_swa_prompt_kernel, n_groups=n_groups),
        grid=(t // SWA_ROWS, SWA_HPG),
        in_specs=in_specs,
        out_specs=pl.BlockSpec((SWA_ROWS, HEAD_DIM), lambda n, h: (n, h)),
        out_shape=jax.ShapeDtypeStruct((t, SWA_HPG * HEAD_DIM), BF16),
        scratch_shapes=scratch_k + scratch_k + scratch_o + scratch_o,
        compiler_params=_params(2), name="swa_prompt")(*([proj] * (5 * n_groups)), bias_tables)


def _swa_sample_kernel(*refs, n_groups):
    qkv_ref = refs[0]
    caches = refs[1:1 + 2 * n_groups]
    biases = refs[1 + 2 * n_groups:1 + 4 * n_groups]
    o_ref = refs[1 + 4 * n_groups]
    scale = HEAD_DIM ** -0.5
    hw = SWA_HPG * HEAD_DIM
    items = [(g, h) for h in range(SWA_HPG) for g in range(n_groups)]
    idx = range(len(items))

    def new(sec, g, h):
        col = sec * SWA_WIDTH + (g * SWA_HPG + h) * HEAD_DIM
        return qkv_ref[:, col:col + HEAD_DIM].astype(BF16)

    def cached(ref, h):
        nk = math.prod(ref.shape[2:-1]) // SWA_HPG
        if len(ref.shape) == 5:
            pick = (0, 0, slice(None), pl.ds(h, ref.shape[3] // SWA_HPG, stride=SWA_HPG), slice(None))
        else:
            pick = (0, 0, pl.ds(h, nk, stride=SWA_HPG), slice(None))
        return ref[pick].reshape(nk, HEAD_DIM).astype(BF16)

    q = [new(0, g, h) for g, h in items]
    sc = [_dot_nt(q[i], cached(caches[2 * g], h)) * scale + biases[2 * g][h] for i, (g, h) in enumerate(items)]
    sn = [_dot_nt(q[i], new(1, g, h)) * scale + biases[2 * g + 1][h] for i, (g, h) in enumerate(items)]
    m = [jnp.maximum(jnp.max(sc[i], axis=-1, keepdims=True), jnp.max(sn[i], axis=-1, keepdims=True)) for i in idx]
    pc = [jnp.exp(sc[i] - m[i]) for i in idx]
    pn = [jnp.exp(sn[i] - m[i]) for i in idx]
    den = [jnp.sum(pc[i], axis=-1, keepdims=True) + jnp.sum(pn[i], axis=-1, keepdims=True) for i in idx]
    o = [(_dot(pc[i].astype(BF16), cached(caches[2 * g + 1], h)) + _dot(pn[i].astype(BF16), new(2, g, h))) / den[i]
         for i, (g, h) in enumerate(items)]
    lse = [m[i] + jnp.log(den[i]) for i in idx]
    for h in range(SWA_HPG):
        mine = [i for i in idx if items[i][1] == h]
        mx = functools.reduce(jnp.maximum, [lse[i] for i in mine])
        ws = [jnp.exp(lse[i] - mx) for i in mine]
        num = functools.reduce(lambda a, b: a + b, [w * o[i] for w, i in zip(ws, mine)])
        o_ref[:, h * HEAD_DIM:(h + 1) * HEAD_DIM] = num / functools.reduce(lambda a, b: a + b, ws)


class _Task:
    def __init__(self, name, body, steps, args, in_specs, out_shapes, out_specs):
        self.name, self.body, self.steps = name, body, steps
        self.args, self.in_specs, self.out_shapes, self.out_specs = args, in_specs, out_shapes, out_specs


def _run_task(task):
    return pl.pallas_call(
        task.body, grid=(task.steps,), in_specs=task.in_specs, out_specs=task.out_specs,
        out_shape=task.out_shapes, compiler_params=_params(1), name=task.name)(*task.args)


def _swa_sample_task(proj, caches, layer, rel_bias, n_new):
    rows_total = proj.shape[0]
    nb = rows_total // n_new
    n_groups = len(SWA_GROUPS)
    hw = SWA_HPG * HEAD_DIM
    cache_in, cache_specs, bias_in, bias_specs = [], [], [], []
    for g, (_, dil) in enumerate(SWA_GROUPS):
        depth, _, cache_len, heads, hd = caches[2 * g].shape
        if dil >= 2 * n_new and n_new % 8 == 0:
            n_res = n_new
            shape = (depth, nb, cache_len // dil, dil * heads, hd)
            spec = pl.BlockSpec((1, 1, cache_len // dil, n_res * heads, hd), lambda b: (layer, b, 0, 0, 0))
        else:
            n_res = dil
            shape = (depth, nb, cache_len * heads, hd)
            spec = pl.BlockSpec((1, 1, cache_len * heads, hd), lambda b: (layer, b, 0, 0))
        for c in caches[2 * g:2 * g + 2]:
            cache_in.append(c.reshape(shape))
            cache_specs.append(spec)
        for tab in _sample_bias_tables(rel_bias, g, n_res, n_new, cache_len):
            bias_in.append(tab)
            bias_specs.append(pl.BlockSpec(tab.shape, lambda b: (0, 0, 0)))
    return _Task(
        "swa_sample", functools.partial(_swa_sample_kernel, n_groups=n_groups), nb,
        [proj, *cache_in, *bias_in],
        [pl.BlockSpec((n_new, 3 * SWA_WIDTH), lambda b: (b, 0))] + cache_specs + bias_specs,
        [jax.ShapeDtypeStruct((rows_total, hw), F32)], [pl.BlockSpec((n_new, hw), lambda b: (b, 0))])


def _silu(x):
    return x * (1.0 / (1.0 + jnp.exp(-x)))


def _softplus(x):
    return jnp.maximum(x, 0.0) + jnp.log(1.0 + jnp.exp(-jnp.abs(x)))


def _inv_dot(a, b):
    return _dot(a.astype(BF16), b.astype(BF16))


HALO = 8
DELTA_SEQS_PER_STEP = 8


def _delta_scan_kernel(*refs, chunk, nseq, carry):
    if carry:
        x_ref, z_ref, ba_ref, wc_ref, alog_ref, dt_ref, gain_ref, o_ref, s_out_ref, s_scr, halo_scr = refs
        s_in_ref = halo_ref = None
    else:
        x_ref, z_ref, ba_ref, wc_ref, alog_ref, dt_ref, gain_ref, s_in_ref, halo_ref, o_ref, s_out_ref = refs
        s_scr = halo_scr = None
    c = chunk
    rows = nseq * c
    seqs = range(nseq)
    part = lambda t, b: t[b * c:(b + 1) * c]
    if carry:
        @pl.when(pl.program_id(0) == 0)
        def _():
            s_scr[...] = jnp.zeros_like(s_scr)
            halo_scr[...] = jnp.zeros_like(halo_scr)

    x = x_ref[...]
    hist = halo_scr[...] if carry else halo_ref[...]
    xp = jnp.concatenate([t for b in seqs for t in (hist[b * HALO:(b + 1) * HALO], part(x, b))], axis=0)
    acc = x * wc_ref[CONV_WIDTH - 1:CONV_WIDTH, :]
    for t in range(1, CONV_WIDTH):
        rolled = pltpu.roll(xp, t, 0)
        shifted = jnp.concatenate([rolled[b * (HALO + c) + HALO:(b + 1) * (HALO + c)] for b in seqs], axis=0)
        acc = acc + shifted * wc_ref[CONV_WIDTH - 1 - t:CONV_WIDTH - t, :]
    if carry:
        halo_scr[...] = x[c - HALO:, :]
    qkv = _silu(acc)

    ba = ba_ref[...]
    beta_all = 1.0 / (1.0 + jnp.exp(-ba))
    g_all = -jnp.exp(alog_ref[...]) * _softplus(ba + dt_ref[...])
    row = lax.broadcasted_iota(jnp.int32, (rows, rows), 0)
    col = lax.broadcasted_iota(jnp.int32, (rows, rows), 1)
    same = (row // c) == (col // c)
    tri = (row >= col) & same
    strict = (row > col) & same
    tril = tri.astype(BF16)
    g1 = g_all.astype(BF16)
    r1 = g_all - g1.astype(F32)
    g2 = r1.astype(BF16)
    g3 = (r1 - g2.astype(F32)).astype(BF16)
    gcum_all = _dot(tril, g1) + (_dot(tril, g2) + _dot(tril, g3))
    gcum_t = gcum_all.T
    eye = (row == col).astype(F32)

    heads = range(DN_HEADS)
    hs = lambda sec, h: slice(sec * DN_WIDTH + h * HEAD_DIM, sec * DN_WIDTH + (h + 1) * HEAD_DIM)
    lane = lambda h: slice(DN_HEADS + h, DN_HEADS + h + 1)
    l2 = lambda t: t * lax.rsqrt(jnp.sum(t * t, axis=-1, keepdims=True) + EPS)
    q = [l2(qkv[:, hs(0, h)]) * (HEAD_DIM ** -0.5) for h in heads]
    k = [l2(qkv[:, hs(1, h)]) for h in heads]
    v = [qkv[:, hs(2, h)] for h in heads]
    bcol = [beta_all[:, h:h + 1] for h in heads]
    gc = [gcum_all[:, lane(h)] for h in heads]
    glast = [[gcum_all[(b + 1) * c - 1:(b + 1) * c, lane(h)] for b in seqs] for h in heads]
    decay = [jnp.exp(jnp.where(tri, gc[h] - gcum_t[lane(h), :], NEG)) for h in heads]
    kb = [k[h] * bcol[h] for h in heads]
    k16 = [k[h].astype(BF16) for h in heads]
    nmat = [jnp.where(strict, _dot_nt(kb[h].astype(BF16), k16[h]) * decay[h], 0.0) for h in heads]
    attn = [jnp.where(tri, _dot_nt(q[h].astype(BF16), k16[h]) * decay[h], 0.0) for h in heads]
    inv = [eye] * DN_HEADS
    blk = 1
    while blk < c:
        pair = (row // (2 * blk) == col // (2 * blk)) & (row % (2 * blk) >= blk) & (col % (2 * blk) < blk)
        low = [jnp.where(pair, nmat[h], 0.0) for h in heads]
        if blk == 1:
            inv = [eye - low[h] for h in heads]
        else:
            t = [_inv_dot(low[h], inv[h]) for h in heads]
            inv = [inv[h] - _inv_dot(inv[h], t[h]) for h in heads]
        blk *= 2
    eg = [jnp.exp(gc[h]) for h in heads]
    rhs = [jnp.concatenate([v[h] * bcol[h], kb[h] * eg[h]], axis=1) for h in heads]
    sol = [_inv_dot(inv[h], rhs[h]) for h in heads]
    qdec = [q[h] * eg[h] for h in heads]
    hb = [(h, b) for h in heads for b in seqs]
    s_prev = {(h, b): (s_scr[h] if carry else s_in_ref[b, h]) for h, b in hb}
    wq = {(h, b): jnp.concatenate([part(sol[h][:, HEAD_DIM:], b), part(qdec[h], b)], axis=0).astype(BF16)
          for h, b in hb}
    ws = {i: _dot(wq[i], s_prev[i].astype(BF16)) for i in hb}
    vn = {(h, b): part(sol[h][:, :HEAD_DIM], b) - ws[h, b][:c] for h, b in hb}
    v_new = [jnp.concatenate([vn[h, b] for b in seqs], axis=0).astype(BF16) for h in heads]
    o = [jnp.concatenate([ws[h, b][c:] for b in seqs], axis=0) + _dot(attn[h].astype(BF16), v_new[h])
         for h in heads]
    kdec = {(h, b): (part(k[h], b) * jnp.exp(glast[h][b] - part(gc[h], b))).astype(BF16) for h, b in hb}
    s_new = {(h, b): s_prev[h, b] * jnp.exp(glast[h][b]) + _dot_tn(kdec[h, b], vn[h, b].astype(BF16))
             for h, b in hb}
    for h in heads:
        if carry:
            s_scr[h] = s_new[h, 0]
            s_out_ref[h] = s_new[h, 0]
        else:
            for b in seqs:
                s_out_ref[b, h] = s_new[h, b]
        y = o[h] * lax.rsqrt(jnp.mean(o[h] * o[h], axis=-1, keepdims=True) + EPS) * gain_ref[...]
        o_ref[:, hs(0, h)] = (y * _silu(z_ref[:, hs(0, h)])).astype(o_ref.dtype)


def _delta_scan(proj, w_conv, a_log, dt_bias, norm_delta, s0, conv_buf, chunk, riders=()):
    m = proj.shape[0]
    carry = s0 is None
    nseq = 1 if carry else math.gcd(m // chunk, DELTA_SEQS_PER_STEP)
    rows = nseq * chunk
    n = m // rows
    width = 3 * DN_WIDTH
    assert chunk >= HALO and B_OFF % width == 0
    pad = lambda v_, off: jnp.zeros((1, LANES), F32).at[0, off:off + DN_HEADS].set(v_.astype(F32))
    consts = [w_conv, pad(a_log, DN_HEADS), pad(dt_bias, DN_HEADS), norm_delta.reshape(1, HEAD_DIM).astype(F32)]
    const_specs = [pl.BlockSpec((CONV_WIDTH, width), lambda i: (0, 0))] + [pl.BlockSpec((1, LANES), lambda i: (0, 0))] * 3
    in_specs = [pl.BlockSpec((rows, width), lambda i: (i, B_OFF // width)),
                pl.BlockSpec((rows, DN_WIDTH), lambda i: (i, Z_OFF // DN_WIDTH)),
                pl.BlockSpec((rows, LANES), lambda i: (i, P_BA // LANES))] + const_specs
    args = [proj, proj, proj] + consts
    state = (DN_HEADS, HEAD_DIM, HEAD_DIM)
    if carry:
        out_dtype = BF16
        s_shape, s_spec = state, pl.BlockSpec(state, lambda i: (0, 0, 0))
        scratch = [pltpu.VMEM(state, F32), pltpu.VMEM((HALO, width), F32)]
    else:
        out_dtype = F32
        halo = jnp.pad(conv_buf, ((0, 0), (HALO - (CONV_WIDTH - 1), 0), (0, 0))).reshape(-1, width)
        in_specs += [pl.BlockSpec((nseq,) + state, lambda i: (i, 0, 0, 0)),
                     pl.BlockSpec((nseq * HALO, width), lambda i: (i, 0))]
        args += [s0, halo]
        s_shape, s_spec = (m // chunk,) + state, pl.BlockSpec((nseq,) + state, lambda i: (i, 0, 0, 0))
        scratch = []
    out_specs = [pl.BlockSpec((rows, DN_WIDTH), lambda i: (i, 0)), s_spec]
    out_shape = [jax.ShapeDtypeStruct((m, DN_WIDTH), out_dtype), jax.ShapeDtypeStruct(s_shape, F32)]
    host = functools.partial(_delta_scan_kernel, chunk=chunk, nseq=nseq, carry=carry)
    assert all(t.steps == n for t in riders)
    body = functools.partial(_hosted_kernel, host=host, n_host_in=len(in_specs), n_host_out=len(out_specs),
                             riders=tuple(riders))
    for t in riders:
        in_specs, args = in_specs + t.in_specs, args + t.args
        out_specs, out_shape = out_specs + t.out_specs, out_shape + t.out_shapes
    return pl.pallas_call(
        body, grid=(n,), in_specs=in_specs, out_specs=out_specs, out_shape=out_shape, scratch_shapes=scratch,
        compiler_params=_params(1), name="delta_scan")(*args)


def _hosted_kernel(*refs, host, n_host_in, n_host_out, riders):
    n_in = n_host_in + sum(len(t.in_specs) for t in riders)
    n_out = n_host_out + sum(len(t.out_specs) for t in riders)
    ins, outs, scratch = refs[:n_in], refs[n_in:n_in + n_out], refs[n_in + n_out:]
    host(*ins[:n_host_in], *outs[:n_host_out], *scratch)
    pi, po = n_host_in, n_host_out
    for t in riders:
        ni, no = len(t.in_specs), len(t.out_specs)
        t.body(*ins[pi:pi + ni], *outs[po:po + no])
        pi, po = pi + ni, po + no


def _mem_attn_kernel(q_ref, k_ref, v_ref, o_ref):
    scale = MEM_HEAD_DIM ** -0.5
    parts = MEM_HEAD_DIM // LANES
    period = MEM_HEADS * parts
    tokens = k_ref.shape[2] // period
    pick = lambda b, c, h: (0, b, pl.ds(c * MEM_HEADS + h, tokens, stride=period), slice(None))
    nb = k_ref.shape[1]
    rows = q_ref.shape[0] // nb
    items = [(b, h) for b in range(nb) for h in range(MEM_HEADS)]
    idx = range(len(items))
    q = [[q_ref[b * rows:(b + 1) * rows, h * MEM_HEAD_DIM + c * LANES:h * MEM_HEAD_DIM + (c + 1) * LANES]
          .astype(BF16) for c in range(parts)] for b, h in items]
    s = [functools.reduce(lambda x, y: x + y, [_dot_nt(q[i][c], k_ref[pick(b, c, h)].astype(BF16))
                                               for c in range(parts)]) * scale for i, (b, h) in enumerate(items)]
    p = [jnp.exp(s[i] - jnp.max(s[i], axis=-1, keepdims=True)) for i in idx]
    den = [jnp.sum(p[i], axis=-1, keepdims=True) for i in idx]
    for i, (b, h) in enumerate(items):
        for c in range(parts):
            col = h * MEM_HEAD_DIM + c * LANES
            o_ref[b * rows:(b + 1) * rows, col:col + LANES] = (
                _dot(p[i].astype(BF16), v_ref[pick(b, c, h)].astype(BF16)) / den[i]).astype(o_ref.dtype)


MEM_PER_STEP = 4


def _mem_attn_task(proj, mem_k, mem_v, layer, tm, out_dtype):
    m = proj.shape[0]
    assert P_CQ % MEM_WIDTH == 0
    depth, nb, tokens, heads, hd = mem_k.shape
    per = m // nb
    parts = hd // LANES
    mem_k, mem_v = (t.reshape(depth, nb, tokens, heads, parts, LANES).transpose(0, 1, 2, 4, 3, 5)
                    .reshape(depth, nb, tokens * heads * parts, LANES) for t in (mem_k, mem_v))
    mems = max(tm // per, 1)
    kv_spec = pl.BlockSpec((1, mems, tokens * heads * parts, LANES), lambda i: (layer, i * tm // (per * mems), 0, 0))
    return _Task(
        "mem_attn", _mem_attn_kernel, m // tm, [proj, mem_k, mem_v],
        [pl.BlockSpec((tm, MEM_WIDTH), lambda i: (i, P_CQ // MEM_WIDTH)), kv_spec, kv_spec],
        [jax.ShapeDtypeStruct((m, MEM_WIDTH), out_dtype)], [pl.BlockSpec((tm, MEM_WIDTH), lambda i: (i, 0))])


def _merge_kernel(a_ref, b_ref, c_ref, ga_ref, gb_ref, gc_ref, wa_ref, wb_ref, wc_ref, o_ref):
    def branch(x_ref, g_ref, w_ref):
        y = _dot(x_ref[...].astype(BF16), w_ref[...])
        return y * (1.0 / (1.0 + jnp.exp(-g_ref[...])))

    o_ref[...] = (branch(a_ref, ga_ref, wa_ref) + branch(b_ref, gb_ref, wb_ref)
                  + branch(c_ref, gc_ref, wc_ref)).astype(o_ref.dtype)


def _merge(a, b, c, tail, w_a, w_b, w_c, tm, tn):
    m = a.shape[0]
    d = w_a.shape[1]
    gate0 = P_GATE // tn
    assert P_GATE % tn == 0 and d % tn == 0
    row = lambda x: pl.BlockSpec((tm, x.shape[1]), lambda i, j: (i, 0))
    gate = lambda br: pl.BlockSpec((tm, tn), lambda i, j, br=br: (i, gate0 + br * (d // tn) + j))
    wsp = lambda w: pl.BlockSpec((w.shape[0], tn), lambda i, j: (0, j))
    return pl.pallas_call(
        _merge_kernel, grid=(m // tm, d // tn),
        in_specs=[row(a), row(b), row(c), gate(0), gate(1), gate(2), wsp(w_a), wsp(w_b), wsp(w_c)],
        out_specs=pl.BlockSpec((tm, tn), lambda i, j: (i, j)),
        out_shape=jax.ShapeDtypeStruct((m, d), BF16),
        compiler_params=_params(2), name="merge")(a, b, c, tail, tail, tail, w_a, w_b, w_c)


def _rms(y, g):
    return y * lax.rsqrt(jnp.mean(y * y, axis=-1, keepdims=True) + EPS) * g


def _proj_norm_residual_kernel(y_ref, w_ref, x_ref, g_post_ref, g_next_ref, o_ref, h_ref):
    x1 = x_ref[...] + _rms(_dot(y_ref[...], w_ref[...]), g_post_ref[...])
    o_ref[...] = x1
    h_ref[...] = _rms(x1, g_next_ref[...]).astype(h_ref.dtype)


def _proj_norm_residual(y, w, x, g_post, g_next, tm):
    m, d = x.shape
    k = y.shape[1]
    row = lambda width: pl.BlockSpec((tm, width), lambda i: (i, 0))
    gsp = pl.BlockSpec((1, d), lambda i: (0, 0))
    return pl.pallas_call(
        _proj_norm_residual_kernel, grid=(m // tm,),
        in_specs=[row(k), pl.BlockSpec((k, d), lambda i: (0, 0)), row(d), gsp, gsp],
        out_specs=[row(d), row(d)],
        out_shape=[jax.ShapeDtypeStruct((m, d), F32), jax.ShapeDtypeStruct((m, d), BF16)],
        compiler_params=_params(1), name="proj_norm_residual")(y, w, x, g_post.reshape(1, d), g_next.reshape(1, d))


FFN_SPLIT = 4


def _ffn_kernel(h_ref, w1_ref, w2_ref, x_ref, g_ref, o_ref, a_ref):
    j = pl.program_id(1)
    last = pl.num_programs(1) - 1
    h = h_ref[...]
    piece = w1_ref.shape[1] // FFN_SPLIT
    for s in range(FFN_SPLIT):
        cols = slice(s * piece, (s + 1) * piece)
        a_ref[:, cols] = jnp.square(jnp.maximum(_dot(h, w1_ref[:, cols]), 0.0)).astype(BF16)
    y = _dot(a_ref[...], w2_ref[...])

    @pl.when(j == 0)
    def _():
        o_ref[...] = y

    @pl.when(jnp.logical_and(j > 0, j < last))
    def _():
        o_ref[...] += y

    @pl.when(j == last)
    def _():
        o_ref[...] = x_ref[...] + _rms(o_ref[...] + y, g_ref[...])


def _ffn(h, w1, w2, x, g, tm, tf):
    m, d = h.shape
    f = w1.shape[1]
    assert f // tf >= 2
    once = lambda shape: pl.BlockSpec(shape, lambda i, j: (i, 0), pipeline_mode=pl.Buffered(1))
    return pl.pallas_call(
        _ffn_kernel, grid=(m // tm, f // tf),
        in_specs=[pl.BlockSpec((tm, d), lambda i, j: (i, 0)), pl.BlockSpec((d, tf), lambda i, j: (0, j)),
                  pl.BlockSpec((tf, d), lambda i, j: (j, 0)), once((tm, d)), pl.BlockSpec((1, d), lambda i, j: (0, 0))],
        out_specs=once((tm, d)),
        out_shape=jax.ShapeDtypeStruct((m, d), F32),
        scratch_shapes=[pltpu.VMEM((tm, tf), BF16)],
        compiler_params=_params(2), name="ffn")(h, w1, w2, x, g.reshape(1, d))


def _row_tile(m, cap):
    t = min(m, cap)
    assert m % t == 0
    return t


def _layer(xp, xs, lw, layer, rel_bias, bias_prompt, mem_p, swa_caches, mem_s, conv_buf, s0):
    (w_in_t, w_conv, a_log, dt_bias, norm_delta, w_o_swa, w_o_delta, w_o_mem, w_out,
     norm_pre_mix, norm_post_mix, norm_pre_ffn, norm_post_ffn, w_ff1, w_ff2) = lw
    mp, ms, n_seq = xp.shape[0], xs.shape[0], s0.shape[0]
    seq = ms // n_seq
    proj_p = _in_proj(xp, norm_pre_mix, w_in_t, _row_tile(mp, 2048))
    proj_s = _in_proj(xs, norm_pre_mix, w_in_t, _row_tile(ms, 2048))
    chunk = DN_CHUNK if mp % DN_CHUNK == 0 else mp
    riding = mp // chunk == n_seq
    swa_s = _swa_sample_task(proj_s, swa_caches, layer, rel_bias, seq)
    mem_s_task = _mem_attn_task(proj_s, *mem_s, layer, seq * (1 if riding else math.gcd(n_seq, MEM_PER_STEP)), F32)
    riders = [swa_s, mem_s_task] if riding else []
    b_p, state_p, *rode = _delta_scan(proj_p, w_conv, a_log, dt_bias, norm_delta, None, None, chunk, riders)
    a_s, c_s = rode if riding else (_run_task(swa_s)[0], _run_task(mem_s_task)[0])
    a_p = _swa_prompt(proj_p, bias_prompt)
    c_p = _run_task(_mem_attn_task(proj_p, *mem_p, 0, _row_tile(mp, 512), BF16))[0]
    b_s, state_s = _delta_scan(proj_s, w_conv, a_log, dt_bias, norm_delta, s0, conv_buf, seq)
    outs = []
    for x, proj, a, b, c in ((xp, proj_p, a_p, b_p, c_p), (xs, proj_s, a_s, b_s, c_s)):
        m = x.shape[0]
        tm = _row_tile(m, 1024)
        merged = _merge(a, b, c, proj, w_o_swa, w_o_delta, w_o_mem, tm, 512)
        x1, h2 = _proj_norm_residual(merged, w_out, x, norm_post_mix, norm_pre_ffn, _row_tile(m, 512))
        outs.append(_ffn(h2, w_ff1, w_ff2, x1, norm_post_ffn, tm, 1024))
    return outs[0], outs[1], proj_p, proj_s, state_p, state_s


def kernel(x_prompt, x_sample, cache_swa0_k, cache_swa0_v, cache_swa1_k, cache_swa1_v, cache_swa2_k, cache_swa2_v, state_delta, state_conv, cache_mem_k, cache_mem_v, mem_prompt, rel_bias, w_in, w_conv, A_log, dt_bias, norm_delta, norm_mem, w_mem_kv, w_o_swa, w_o_delta, w_o_mem, w_out, norm_pre_mix, norm_post_mix, norm_pre_ffn, norm_post_ffn, w_ff1, w_ff2):
    depth = w_in.shape[0]
    bp, tp, d = x_prompt.shape
    bs, ts, _ = x_sample.shape
    assert bp == 1 and ts > CONV_WIDTH - 1 and ts % 8 == 0
    sample_swa = (cache_swa0_k, cache_swa0_v, cache_swa1_k, cache_swa1_v, cache_swa2_k, cache_swa2_v)
    bias_prompt = _prompt_bias_tables(rel_bias)
    xp = x_prompt.reshape(bp * tp, d)
    xs = x_sample.reshape(bs * ts, d)
    new_p = [[] for _ in range(10)]
    new_s = [[] for _ in range(8)]
    hw = SWA_HPG * HEAD_DIM
    for l in range(depth):
        lw = (jnp.swapaxes(w_in, 1, 2)[l], w_conv[l], A_log[l], dt_bias[l], norm_delta[l], w_o_swa[l].astype(BF16),
              w_o_delta[l].astype(BF16), w_o_mem[l].astype(BF16), w_out[l].astype(BF16),
              norm_pre_mix[l], norm_post_mix[l], norm_pre_ffn[l], norm_post_ffn[l],
              w_ff1[l].astype(BF16), w_ff2[l].astype(BF16))
        mem = mem_prompt.reshape(-1, d)
        mkv = _matmul(_rmsnorm_cast(mem, norm_mem[l], _row_tile(mem.shape[0], 256)), w_mem_kv[l],
                      2 * MEM_WIDTH, _row_tile(mem.shape[0], 256), 512, "mem_kv")
        mk = mkv[:, :MEM_WIDTH].reshape(1, bp, -1, MEM_HEADS, MEM_HEAD_DIM)
        mv = mkv[:, MEM_WIDTH:].reshape(1, bp, -1, MEM_HEADS, MEM_HEAD_DIM)
        xp, xs, main_p, main_s, s_p, s_s = _layer(
            xp, xs, lw, l, rel_bias, bias_prompt, (mk, mv), sample_swa, (cache_mem_k, cache_mem_v),
            state_conv[l], state_delta[l])
        vals_p = []
        for g, (window, _) in enumerate(SWA_GROUPS):
            keep = min(window, tp)
            for sec in (1, 2):
                c0 = sec * SWA_WIDTH + g * hw
                vals_p.append(main_p[tp - keep:, c0:c0 + hw].reshape(bp, keep, SWA_HPG, HEAD_DIM))
        vals_p.append(s_p.reshape(bp, DN_HEADS, HEAD_DIM, HEAD_DIM))
        vals_p.append(main_p[tp - (CONV_WIDTH - 1):, B_OFF:B_OFF + 3 * DN_WIDTH].reshape(bp, CONV_WIDTH - 1, -1))
        vals_p += [mk[0], mv[0]]
        vals_s = []
        main_s3 = main_s.reshape(bs, ts, -1)
        for g in range(len(SWA_GROUPS)):
            for sec in (1, 2):
                c0 = sec * SWA_WIDTH + g * hw
                vals_s.append(main_s3[:, :, c0:c0 + hw].reshape(bs, ts, SWA_HPG, HEAD_DIM))
        vals_s.append(s_s)
        vals_s.append(main_s3[:, ts - (CONV_WIDTH - 1):, B_OFF:B_OFF + 3 * DN_WIDTH])
        for lst, val in zip(new_p, vals_p):
            lst.append(val)
        for lst, val in zip(new_s, vals_s):
            lst.append(val)
    outs_p = [jnp.stack(t) for t in new_p]
    outs_s = [jnp.stack(t) for t in new_s]
    return (xp.reshape(bp, tp, d), xs.reshape(bs, ts, d), *outs_p, *outs_s)
```

```python
import functools
import math

import numpy as np
import jax
import jax.numpy as jnp
from jax import lax
from jax.experimental import pallas as pl
from jax.experimental.pallas import tpu as pltpu

EPS = 1e-6
HEAD_DIM = 128
SWA_GROUPS = ((128, 1), (512, 4), (2048, 16))
SWA_SPAN = 128
SWA_HPG = 4
SWA_WIDTH = SWA_HPG * len(SWA_GROUPS) * HEAD_DIM
DN_HEADS = 12
DN_WIDTH = DN_HEADS * HEAD_DIM
CONV_WIDTH = 4
DN_CHUNK = 64
MEM_HEADS = 4
MEM_HEAD_DIM = 256
MEM_WIDTH = MEM_HEADS * MEM_HEAD_DIM
N_BUCKETS = 32
MAX_DISTANCE = 2048
N_BRANCHES = 3

A_OFF = 0
B_OFF = 3 * SWA_WIDTH
Z_OFF = B_OFF + 3 * DN_WIDTH
BA_OFF = Z_OFF + DN_WIDTH
CQ_OFF = BA_OFF + 2 * DN_HEADS
GATE_OFF = CQ_OFF + MEM_WIDTH
LANES = 128
SUBLANES = 8
PROJ_TILE = 512
P_BA = BA_OFF
P_CQ = P_BA + PROJ_TILE
P_GATE = P_CQ + MEM_WIDTH
SWA_ROWS = 2048
SWA_UNROLL = 8
NEG = -1e30
VMEM_LIMIT = 58 * 1024 * 1024

BF16 = jnp.bfloat16
F32 = jnp.float32


def _params(n_grid):
    return pltpu.CompilerParams(dimension_semantics=("arbitrary",) * n_grid, vmem_limit_bytes=VMEM_LIMIT)


def _dot(a, b):
    return jnp.dot(a, b, preferred_element_type=F32)


def _dot_nt(a, b):
    return lax.dot_general(a, b, (((1,), (1,)), ((), ())), preferred_element_type=F32)


def _dot_tn(a, b):
    return lax.dot_general(a, b, (((0,), (0,)), ((), ())), preferred_element_type=F32)


def _split2(x):
    hi = x.astype(BF16)
    lo = (x - hi.astype(F32)).astype(BF16)
    return hi, lo


def _dot3(a, b, dot=_dot):
    ah, al = _split2(a)
    bh, bl = _split2(b)
    return dot(ah, bh) + (dot(ah, bl) + dot(al, bh))


def _rmsnorm_cast_kernel(x_ref, g_ref, o_ref):
    x = x_ref[...]
    y = x * lax.rsqrt(jnp.mean(x * x, axis=-1, keepdims=True) + EPS)
    o_ref[...] = (y * g_ref[...]).astype(o_ref.dtype)


def _rmsnorm_cast(x, g, tm):
    m, d = x.shape
    return pl.pallas_call(
        _rmsnorm_cast_kernel, grid=(m // tm,),
        in_specs=[pl.BlockSpec((tm, d), lambda i: (i, 0)), pl.BlockSpec((1, d), lambda i: (0, 0))],
        out_specs=pl.BlockSpec((tm, d), lambda i: (i, 0)),
        out_shape=jax.ShapeDtypeStruct((m, d), BF16),
        compiler_params=_params(1), name="rmsnorm_cast")(x, g.reshape(1, d))


def _matmul_kernel(x_ref, w_ref, o_ref):
    o_ref[...] = _dot(x_ref[...], w_ref[...].astype(BF16)).astype(o_ref.dtype)


def _matmul(x, w, n, tm, tn, name):
    m, k = x.shape
    assert n % tn == 0 and m % tm == 0
    return pl.pallas_call(
        _matmul_kernel, grid=(m // tm, n // tn),
        in_specs=[pl.BlockSpec((tm, k), lambda i, j: (i, 0)), pl.BlockSpec((k, tn), lambda i, j: (0, j))],
        out_specs=pl.BlockSpec((tm, tn), lambda i, j: (i, j)),
        out_shape=jax.ShapeDtypeStruct((m, n), F32),
        compiler_params=_params(2), name=name)(x, w)


def _in_proj_kernel(x_ref, g_ref, w_ref, o_ref, h_ref):
    @pl.when(pl.program_id(1) == 0)
    def _():
        h_ref[...] = _rms(x_ref[...], g_ref[...]).astype(BF16)

    o_ref[...] = _dot_nt(h_ref[...], w_ref[...].astype(BF16))


def _in_proj(x, g, wt, tm, riders=()):
    m, k = x.shape
    tn = PROJ_TILE
    n = P_CQ + wt.shape[0] - CQ_OFF
    assert n % tn == 0 and m % tm == 0 and P_CQ % tn == 0 and CQ_OFF % SUBLANES == 0

    def w_rows(i, j):
        return (pl.multiple_of(jnp.where(j * tn < P_CQ, j * tn, j * tn - P_CQ + CQ_OFF), SUBLANES), 0)

    return _hosted_call(
        "in_proj", _in_proj_kernel, (m // tm, n // tn), [x, g.reshape(1, k), wt],
        [pl.BlockSpec((tm, k), lambda i, j: (i, 0), pipeline_mode=pl.Buffered(1)),
         pl.BlockSpec((1, k), lambda i, j: (0, 0)),
         pl.BlockSpec((pl.Element(tn), pl.Element(k)), w_rows)],
        [jax.ShapeDtypeStruct((m, n), F32)], [pl.BlockSpec((tm, tn), lambda i, j: (i, j))],
        [pltpu.VMEM((tm, k), BF16)], riders)


def _t5_bucket(dist):
    max_exact = N_BUCKETS // 2
    df = jnp.maximum(dist, 1).astype(F32)
    large = max_exact + (jnp.log(df / max_exact) / math.log(MAX_DISTANCE / max_exact)
                         * (N_BUCKETS - max_exact)).astype(jnp.int32)
    return jnp.where(dist < max_exact, dist, jnp.minimum(large, N_BUCKETS - 1))


def _group_bias(rel_bias, g):
    _, dil = SWA_GROUPS[g]
    dist = jnp.arange(SWA_SPAN + 1, dtype=jnp.int32) * dil
    onehot = _t5_bucket(dist)[None, :, None] == jnp.arange(N_BUCKETS, dtype=jnp.int32)[None, None, :]
    heads = jnp.transpose(rel_bias[:, g * SWA_HPG:(g + 1) * SWA_HPG].astype(F32))
    return jnp.sum(jnp.where(onehot, heads[:, None, :], 0.0), axis=-1)


def _prompt_bias_tables(rel_bias):
    span = SWA_SPAN
    first = (np.arange(2 * span) >= span)[None, None, :]
    out = []
    for g in range(len(SWA_GROUPS)):
        w = _group_bias(rel_bias, g)
        h = w.shape[0]
        p = 3 * span
        e = jnp.concatenate([jnp.full((h, span - 1), NEG, F32), w[:, ::-1], jnp.full((h, p - 2 * span), NEG, F32)], 1)
        skew = jnp.broadcast_to(e[:, None, :], (h, span, p)).reshape(h, span * p)[:, :span * (p - 1)]
        tab = skew.reshape(h, span, p - 1)[:, :, span - 1:3 * span - 1]
        out.append(jnp.stack([jnp.where(first, tab, NEG), tab], axis=1))
    return jnp.stack(out)


def _sample_bias_tables(rel_bias, g, n_res, n_new, cache_len):
    _, dil = SWA_GROUPS[g]
    w = _group_bias(rel_bias, g)
    h = w.shape[0]
    lm = cache_len // dil
    assert cache_len % dil == 0
    wpad = jnp.concatenate([w, jnp.full((h, lm + n_new), NEG, F32)], axis=1)
    res = np.arange(n_res)[None, None, :]
    rows = []
    for s in range(n_new):
        vec = wpad[:, s // dil + 1:lm + s // dil + 1][:, ::-1]
        rows.append(jnp.where(res == s % dil, vec[:, :, None], NEG).reshape(h, lm * n_res))
    s = np.arange(n_new)
    dist = s[:, None] - s[None, :]
    new = jnp.full((h, n_new, n_new), NEG, F32)
    for j in range(min((n_new - 1) // dil, SWA_SPAN) + 1):
        new = jnp.where((dist == j * dil)[None], w[:, j][:, None, None], new)
    return jnp.stack(rows, axis=1), new


def _swa_prompt_kernel(*refs, n_groups):
    ins = refs[:5 * n_groups]
    bias_ref = refs[5 * n_groups]
    o_ref = refs[5 * n_groups + 1]
    scr = refs[5 * n_groups + 2:]
    kext, vext, og, lg = scr[:n_groups], scr[n_groups:2 * n_groups], scr[2 * n_groups:3 * n_groups], scr[3 * n_groups:]
    n = pl.program_id(0)
    scale = HEAD_DIM ** -0.5
    span = SWA_SPAN
    for g in range(n_groups):
        _, dil = SWA_GROUPS[g]
        q_ref, kc_ref, vc_ref, kp_ref, vp_ref = ins[5 * g:5 * g + 5]
        blk = span * dil
        kext[g][0:blk, :] = kp_ref[...]
        kext[g][blk:, :] = kc_ref[...]
        vext[g][0:blk, :] = vp_ref[...]
        vext[g][blk:, :] = vc_ref[...]

        def body(it, carry, g=g, dil=dil, blk=blk, q_ref=q_ref):
            items = range(SWA_UNROLL)
            starts, sels = [], []
            for u in items:
                idx = it * SWA_UNROLL + u
                b = idx // dil
                starts.append(b * blk + (idx - b * dil))
                sels.append(jnp.where(jnp.logical_and(n == 0, b == 0), 0, 1))
            q = [q_ref[pl.ds(starts[u], span, stride=dil), :].astype(BF16) for u in items]
            kk = [kext[g][pl.ds(starts[u], 2 * span, stride=dil), :].astype(BF16) for u in items]
            s = [_dot_nt(q[u], kk[u]) * scale + bias_ref[g, 0, sels[u]] for u in items]
            m = [jnp.max(s[u], axis=-1, keepdims=True) for u in items]
            p = [jnp.exp(s[u] - m[u]) for u in items]
            den = [jnp.sum(p[u], axis=-1, keepdims=True) for u in items]
            vv = [vext[g][pl.ds(starts[u], 2 * span, stride=dil), :].astype(BF16) for u in items]
            o = [_dot(p[u].astype(BF16), vv[u]) / den[u] for u in items]
            for u in items:
                og[g][pl.ds(starts[u], span, stride=dil), :] = o[u]
                lg[g][pl.ds(starts[u], span, stride=dil), :] = jnp.broadcast_to(
                    m[u] + jnp.log(den[u]), (span, HEAD_DIM))
            return carry

        lax.fori_loop(0, SWA_ROWS // span // SWA_UNROLL, body, 0)
    lses = [lg[g][...] for g in range(n_groups)]
    mx = functools.reduce(jnp.maximum, lses)
    ws = [jnp.exp(l - mx) for l in lses]
    num = functools.reduce(lambda a, b: a + b, [w * og[g][...] for g, w in enumerate(ws)])
    o_ref[...] = (num / functools.reduce(lambda a, b: a + b, ws)).astype(o_ref.dtype)


def _swa_prompt(proj, bias_tables):
    t = proj.shape[0]
    n_groups = len(SWA_GROUPS)
    assert t % SWA_ROWS == 0
    in_specs, scratch_k, scratch_o = [], [], []
    for g, (_, dil) in enumerate(SWA_GROUPS):
        blk = SWA_SPAN * dil
        per = SWA_ROWS // blk
        assert SWA_ROWS % blk == 0
        qc, kc, vc = (sec * (SWA_WIDTH // HEAD_DIM) + g * SWA_HPG for sec in range(3))
        cur = lambda col: pl.BlockSpec((SWA_ROWS, HEAD_DIM), lambda n, h, col=col: (n, col + h))
        prev = lambda col, blk=blk, per=per: pl.BlockSpec(
            (blk, HEAD_DIM), lambda n, h, col=col, per=per: (jnp.maximum(n * per - 1, 0), col + h))
        in_specs += [cur(qc), cur(kc), cur(vc), prev(kc), prev(vc)]
        scratch_k.append(pltpu.VMEM((SWA_ROWS + blk, HEAD_DIM), F32))
        scratch_o.append(pltpu.VMEM((SWA_ROWS, HEAD_DIM), F32))
    in_specs.append(pl.BlockSpec((n_groups, 1, 2, SWA_SPAN, 2 * SWA_SPAN), lambda n, h: (0, h, 0, 0, 0)))
    return pl.pallas_call(
        functools.partial(_swa_prompt_kernel, n_groups=n_groups),
        grid=(t // SWA_ROWS, SWA_HPG),
        in_specs=in_specs,
        out_specs=pl.BlockSpec((SWA_ROWS, HEAD_DIM), lambda n, h: (n, h)),
        out_shape=jax.ShapeDtypeStruct((t, SWA_HPG * HEAD_DIM), BF16),
        scratch_shapes=scratch_k + scratch_k + scratch_o + scratch_o,
        compiler_params=_params(2), name="swa_prompt")(*([proj] * (5 * n_groups)), bias_tables)


def _swa_sample_kernel(*refs, n_groups):
    qkv_ref = refs[0]
    caches = refs[1:1 + 2 * n_groups]
    biases = refs[1 + 2 * n_groups:1 + 4 * n_groups]
    o_ref = refs[1 + 4 * n_groups]
    scale = HEAD_DIM ** -0.5
    hw = SWA_HPG * HEAD_DIM
    items = [(g, h) for h in range(SWA_HPG) for g in range(n_groups)]
    idx = range(len(items))

    def new(sec, g, h):
        col = sec * SWA_WIDTH + (g * SWA_HPG + h) * HEAD_DIM
        return qkv_ref[:, col:col + HEAD_DIM].astype(BF16)

    def cached(ref, h):
        nk = math.prod(ref.shape[2:-1]) // SWA_HPG
        if len(ref.shape) == 5:
            pick = (0, 0, slice(None), pl.ds(h, ref.shape[3] // SWA_HPG, stride=SWA_HPG), slice(None))
        else:
            pick = (0, 0, pl.ds(h, nk, stride=SWA_HPG), slice(None))
        return ref[pick].reshape(nk, HEAD_DIM).astype(BF16)

    q = [new(0, g, h) for g, h in items]
    sc = [_dot_nt(q[i], cached(caches[2 * g], h)) * scale + biases[2 * g][h] for i, (g, h) in enumerate(items)]
    sn = [_dot_nt(q[i], new(1, g, h)) * scale + biases[2 * g + 1][h] for i, (g, h) in enumerate(items)]
    m = [jnp.maximum(jnp.max(sc[i], axis=-1, keepdims=True), jnp.max(sn[i], axis=-1, keepdims=True)) for i in idx]
    pc = [jnp.exp(sc[i] - m[i]) for i in idx]
    pn = [jnp.exp(sn[i] - m[i]) for i in idx]
    den = [jnp.sum(pc[i], axis=-1, keepdims=True) + jnp.sum(pn[i], axis=-1, keepdims=True) for i in idx]
    o = [(_dot(pc[i].astype(BF16), cached(caches[2 * g + 1], h)) + _dot(pn[i].astype(BF16), new(2, g, h))) / den[i]
         for i, (g, h) in enumerate(items)]
    lse = [m[i] + jnp.log(den[i]) for i in idx]
    for h in range(SWA_HPG):
        mine = [i for i in idx if items[i][1] == h]
        mx = functools.reduce(jnp.maximum, [lse[i] for i in mine])
        ws = [jnp.exp(lse[i] - mx) for i in mine]
        num = functools.reduce(lambda a, b: a + b, [w * o[i] for w, i in zip(ws, mine)])
        o_ref[:, h * HEAD_DIM:(h + 1) * HEAD_DIM] = num / functools.reduce(lambda a, b: a + b, ws)


class _Task:
    def __init__(self, name, body, steps, args, in_specs, out_shapes, out_specs):
        self.name, self.body, self.steps = name, body, steps
        self.args, self.in_specs, self.out_shapes, self.out_specs = args, in_specs, out_shapes, out_specs


def _run_task(task):
    return pl.pallas_call(
        task.body, grid=(task.steps,), in_specs=task.in_specs, out_specs=task.out_specs,
        out_shape=task.out_shapes, compiler_params=_params(1), name=task.name)(*task.args)


def _riding_specs(task, grid):
    assert math.prod(grid) >= task.steps

    def step_of(*idx):
        flat = functools.reduce(lambda acc, t: acc * t[1] + t[0], zip(idx[1:], grid[1:]), idx[0])
        return flat if math.prod(grid) == task.steps else jnp.minimum(flat, task.steps - 1)

    re = lambda spec: pl.BlockSpec(spec.block_shape, lambda *idx: spec.index_map(step_of(*idx)))
    return [re(s) for s in task.in_specs], [re(s) for s in task.out_specs]


BF16_ROWS = 16


def _cast_task(w, max_steps):
    rows, cols = w.shape
    steps = max(s for s in range(1, min(max_steps, rows // BF16_ROWS) + 1) if rows % (s * BF16_ROWS) == 0)
    slab = pl.BlockSpec((rows // steps, cols), lambda i: (i, 0))

    def body(x_ref, o_ref):
        o_ref[...] = x_ref[...].astype(BF16)

    return _Task("cast", body, steps, [w], [slab], [jax.ShapeDtypeStruct((rows, cols), BF16)], [slab])


def _swa_sample_task(proj, caches, layer, rel_bias, n_new):
    rows_total = proj.shape[0]
    nb = rows_total // n_new
    n_groups = len(SWA_GROUPS)
    hw = SWA_HPG * HEAD_DIM
    cache_in, cache_specs, bias_in, bias_specs = [], [], [], []
    for g, (_, dil) in enumerate(SWA_GROUPS):
        depth, _, cache_len, heads, hd = caches[2 * g].shape
        if dil >= 2 * n_new and n_new % 8 == 0:
            n_res = n_new
            shape = (depth, nb, cache_len // dil, dil * heads, hd)
            spec = pl.BlockSpec((1, 1, cache_len // dil, n_res * heads, hd), lambda b: (layer, b, 0, 0, 0))
        else:
            n_res = dil
            shape = (depth, nb, cache_len * heads, hd)
            spec = pl.BlockSpec((1, 1, cache_len * heads, hd), lambda b: (layer, b, 0, 0))
        for c in caches[2 * g:2 * g + 2]:
            cache_in.append(c.reshape(shape))
            cache_specs.append(spec)
        for tab in _sample_bias_tables(rel_bias, g, n_res, n_new, cache_len):
            bias_in.append(tab)
            bias_specs.append(pl.BlockSpec(tab.shape, lambda b: (0, 0, 0)))
    return _Task(
        "swa_sample", functools.partial(_swa_sample_kernel, n_groups=n_groups), nb,
        [proj, *cache_in, *bias_in],
        [pl.BlockSpec((n_new, 3 * SWA_WIDTH), lambda b: (b, 0))] + cache_specs + bias_specs,
        [jax.ShapeDtypeStruct((rows_total, hw), F32)], [pl.BlockSpec((n_new, hw), lambda b: (b, 0))])


def _silu(x):
    return x * (1.0 / (1.0 + jnp.exp(-x)))


def _softplus(x):
    return jnp.maximum(x, 0.0) + jnp.log(1.0 + jnp.exp(-jnp.abs(x)))


def _inv_dot(a, b):
    return _dot(a.astype(BF16), b.astype(BF16))


HALO = 8
DELTA_SEQS_PER_STEP = 8


def _delta_scan_kernel(*refs, chunk, nseq, carry):
    if carry:
        x_ref, z_ref, ba_ref, wc_ref, alog_ref, dt_ref, gain_ref, o_ref, s_out_ref, s_scr, halo_scr = refs
        s_in_ref = halo_ref = None
    else:
        x_ref, z_ref, ba_ref, wc_ref, alog_ref, dt_ref, gain_ref, s_in_ref, halo_ref, o_ref, s_out_ref = refs
        s_scr = halo_scr = None
    c = chunk
    rows = nseq * c
    seqs = range(nseq)
    part = lambda t, b: t[b * c:(b + 1) * c]
    if carry:
        @pl.when(pl.program_id(0) == 0)
        def _():
            s_scr[...] = jnp.zeros_like(s_scr)
            halo_scr[...] = jnp.zeros_like(halo_scr)

    x = x_ref[...]
    hist = halo_scr[...] if carry else halo_ref[...]
    xp = jnp.concatenate([t for b in seqs for t in (hist[b * HALO:(b + 1) * HALO], part(x, b))], axis=0)
    acc = x * wc_ref[CONV_WIDTH - 1:CONV_WIDTH, :]
    for t in range(1, CONV_WIDTH):
        rolled = pltpu.roll(xp, t, 0)
        shifted = jnp.concatenate([rolled[b * (HALO + c) + HALO:(b + 1) * (HALO + c)] for b in seqs], axis=0)
        acc = acc + shifted * wc_ref[CONV_WIDTH - 1 - t:CONV_WIDTH - t, :]
    if carry:
        halo_scr[...] = x[c - HALO:, :]
    qkv = _silu(acc)

    ba = ba_ref[...]
    beta_all = 1.0 / (1.0 + jnp.exp(-ba))
    g_all = -jnp.exp(alog_ref[...]) * _softplus(ba + dt_ref[...])
    row = lax.broadcasted_iota(jnp.int32, (rows, rows), 0)
    col = lax.broadcasted_iota(jnp.int32, (rows, rows), 1)
    same = (row // c) == (col // c)
    tri = (row >= col) & same
    strict = (row > col) & same
    tril = tri.astype(BF16)
    g1 = g_all.astype(BF16)
    r1 = g_all - g1.astype(F32)
    g2 = r1.astype(BF16)
    g3 = (r1 - g2.astype(F32)).astype(BF16)
    gcum_all = _dot(tril, g1) + (_dot(tril, g2) + _dot(tril, g3))
    gcum_t = gcum_all.T
    eye = (row == col).astype(F32)

    heads = range(DN_HEADS)
    hs = lambda sec, h: slice(sec * DN_WIDTH + h * HEAD_DIM, sec * DN_WIDTH + (h + 1) * HEAD_DIM)
    lane = lambda h: slice(DN_HEADS + h, DN_HEADS + h + 1)
    l2 = lambda t: t * lax.rsqrt(jnp.sum(t * t, axis=-1, keepdims=True) + EPS)
    q = [l2(qkv[:, hs(0, h)]) * (HEAD_DIM ** -0.5) for h in heads]
    k = [l2(qkv[:, hs(1, h)]) for h in heads]
    v = [qkv[:, hs(2, h)] for h in heads]
    bcol = [beta_all[:, h:h + 1] for h in heads]
    gc = [gcum_all[:, lane(h)] for h in heads]
    glast = [[gcum_all[(b + 1) * c - 1:(b + 1) * c, lane(h)] for b in seqs] for h in heads]
    decay = [jnp.exp(jnp.where(tri, gc[h] - gcum_t[lane(h), :], NEG)) for h in heads]
    kb = [k[h] * bcol[h] for h in heads]
    k16 = [k[h].astype(BF16) for h in heads]
    nmat = [jnp.where(strict, _dot_nt(kb[h].astype(BF16), k16[h]) * decay[h], 0.0) for h in heads]
    attn = [jnp.where(tri, _dot_nt(q[h].astype(BF16), k16[h]) * decay[h], 0.0) for h in heads]
    inv = [eye] * DN_HEADS
    blk = 1
    while blk < c:
        pair = (row // (2 * blk) == col // (2 * blk)) & (row % (2 * blk) >= blk) & (col % (2 * blk) < blk)
        low = [jnp.where(pair, nmat[h], 0.0) for h in heads]
        if blk == 1:
            inv = [eye - low[h] for h in heads]
        else:
            t = [_inv_dot(low[h], inv[h]) for h in heads]
            inv = [inv[h] - _inv_dot(inv[h], t[h]) for h in heads]
        blk *= 2
    eg = [jnp.exp(gc[h]) for h in heads]
    rhs = [jnp.concatenate([v[h] * bcol[h], kb[h] * eg[h]], axis=1) for h in heads]
    sol = [_inv_dot(inv[h], rhs[h]) for h in heads]
    qdec = [q[h] * eg[h] for h in heads]
    hb = [(h, b) for h in heads for b in seqs]
    s_prev = {(h, b): (s_scr[h] if carry else s_in_ref[b, h]) for h, b in hb}
    wq = {(h, b): jnp.concatenate([part(sol[h][:, HEAD_DIM:], b), part(qdec[h], b)], axis=0).astype(BF16)
          for h, b in hb}
    ws = {i: _dot(wq[i], s_prev[i].astype(BF16)) for i in hb}
    vn = {(h, b): part(sol[h][:, :HEAD_DIM], b) - ws[h, b][:c] for h, b in hb}
    v_new = [jnp.concatenate([vn[h, b] for b in seqs], axis=0).astype(BF16) for h in heads]
    o = [jnp.concatenate([ws[h, b][c:] for b in seqs], axis=0) + _dot(attn[h].astype(BF16), v_new[h])
         for h in heads]
    kdec = {(h, b): (part(k[h], b) * jnp.exp(glast[h][b] - part(gc[h], b))).astype(BF16) for h, b in hb}
    s_new = {(h, b): s_prev[h, b] * jnp.exp(glast[h][b]) + _dot_tn(kdec[h, b], vn[h, b].astype(BF16))
             for h, b in hb}
    for h in heads:
        if carry:
            s_scr[h] = s_new[h, 0]
            s_out_ref[h] = s_new[h, 0]
        else:
            for b in seqs:
                s_out_ref[b, h] = s_new[h, b]
        y = o[h] * lax.rsqrt(jnp.mean(o[h] * o[h], axis=-1, keepdims=True) + EPS) * gain_ref[...]
        o_ref[:, hs(0, h)] = (y * _silu(z_ref[:, hs(0, h)])).astype(o_ref.dtype)


def _delta_scan(proj, w_conv, a_log, dt_bias, norm_delta, s0, conv_buf, chunk, riders=()):
    m = proj.shape[0]
    carry = s0 is None
    nseq = 1 if carry else math.gcd(m // chunk, DELTA_SEQS_PER_STEP)
    rows = nseq * chunk
    n = m // rows
    width = 3 * DN_WIDTH
    assert chunk >= HALO and B_OFF % width == 0
    pad = lambda v_, off: jnp.zeros((1, LANES), F32).at[0, off:off + DN_HEADS].set(v_.astype(F32))
    consts = [w_conv, pad(a_log, DN_HEADS), pad(dt_bias, DN_HEADS), norm_delta.reshape(1, HEAD_DIM).astype(F32)]
    const_specs = [pl.BlockSpec((CONV_WIDTH, width), lambda i: (0, 0))] + [pl.BlockSpec((1, LANES), lambda i: (0, 0))] * 3
    in_specs = [pl.BlockSpec((rows, width), lambda i: (i, B_OFF // width)),
                pl.BlockSpec((rows, DN_WIDTH), lambda i: (i, Z_OFF // DN_WIDTH)),
                pl.BlockSpec((rows, LANES), lambda i: (i, P_BA // LANES))] + const_specs
    args = [proj, proj, proj] + consts
    state = (DN_HEADS, HEAD_DIM, HEAD_DIM)
    if carry:
        out_dtype = BF16
        s_shape, s_spec = state, pl.BlockSpec(state, lambda i: (0, 0, 0))
        scratch = [pltpu.VMEM(state, F32), pltpu.VMEM((HALO, width), F32)]
    else:
        out_dtype = F32
        halo = jnp.pad(conv_buf, ((0, 0), (HALO - (CONV_WIDTH - 1), 0), (0, 0))).reshape(-1, width)
        in_specs += [pl.BlockSpec((nseq,) + state, lambda i: (i, 0, 0, 0)),
                     pl.BlockSpec((nseq * HALO, width), lambda i: (i, 0))]
        args += [s0, halo]
        s_shape, s_spec = (m // chunk,) + state, pl.BlockSpec((nseq,) + state, lambda i: (i, 0, 0, 0))
        scratch = []
    out_specs = [pl.BlockSpec((rows, DN_WIDTH), lambda i: (i, 0)), s_spec]
    out_shape = [jax.ShapeDtypeStruct((m, DN_WIDTH), out_dtype), jax.ShapeDtypeStruct(s_shape, F32)]
    host = functools.partial(_delta_scan_kernel, chunk=chunk, nseq=nseq, carry=carry)
    return _hosted_call("delta_scan", host, (n,), args, in_specs, out_shape, out_specs, scratch, riders)


def _hosted_kernel(*refs, host, n_host_in, n_host_out, riders):
    n_in = n_host_in + sum(len(t.in_specs) for t in riders)
    n_out = n_host_out + sum(len(t.out_specs) for t in riders)
    ins, outs, scratch = refs[:n_in], refs[n_in:n_in + n_out], refs[n_in + n_out:]
    host(*ins[:n_host_in], *outs[:n_host_out], *scratch)
    pi, po = n_host_in, n_host_out
    for t in riders:
        ni, no = len(t.in_specs), len(t.out_specs)
        t.body(*ins[pi:pi + ni], *outs[po:po + no])
        pi, po = pi + ni, po + no


def _hosted_call(name, host, grid, args, in_specs, out_shape, out_specs, scratch, riders):
    body = functools.partial(_hosted_kernel, host=host, n_host_in=len(in_specs), n_host_out=len(out_specs),
                             riders=tuple(riders))
    args, in_specs, out_shape, out_specs = list(args), list(in_specs), list(out_shape), list(out_specs)
    for t in riders:
        r_in, r_out = _riding_specs(t, grid)
        args, in_specs = args + list(t.args), in_specs + r_in
        out_shape, out_specs = out_shape + list(t.out_shapes), out_specs + r_out
    return pl.pallas_call(
        body, grid=grid, in_specs=in_specs, out_specs=out_specs, out_shape=out_shape, scratch_shapes=scratch,
        compiler_params=_params(len(grid)), name=name)(*args)


def _mem_attn_kernel(q_ref, k_ref, v_ref, o_ref):
    scale = MEM_HEAD_DIM ** -0.5
    parts = MEM_HEAD_DIM // LANES
    period = MEM_HEADS * parts
    tokens = k_ref.shape[2] // period
    pick = lambda b, c, h: (0, b, pl.ds(c * MEM_HEADS + h, tokens, stride=period), slice(None))
    nb = k_ref.shape[1]
    rows = q_ref.shape[0] // nb
    items = [(b, h) for b in range(nb) for h in range(MEM_HEADS)]
    idx = range(len(items))
    q = [[q_ref[b * rows:(b + 1) * rows, h * MEM_HEAD_DIM + c * LANES:h * MEM_HEAD_DIM + (c + 1) * LANES]
          .astype(BF16) for c in range(parts)] for b, h in items]
    s = [functools.reduce(lambda x, y: x + y, [_dot_nt(q[i][c], k_ref[pick(b, c, h)].astype(BF16))
                                               for c in range(parts)]) * scale for i, (b, h) in enumerate(items)]
    p = [jnp.exp(s[i] - jnp.max(s[i], axis=-1, keepdims=True)) for i in idx]
    den = [jnp.sum(p[i], axis=-1, keepdims=True) for i in idx]
    for i, (b, h) in enumerate(items):
        for c in range(parts):
            col = h * MEM_HEAD_DIM + c * LANES
            o_ref[b * rows:(b + 1) * rows, col:col + LANES] = (
                _dot(p[i].astype(BF16), v_ref[pick(b, c, h)].astype(BF16)) / den[i]).astype(o_ref.dtype)


MEM_PER_STEP = 4


def _mem_attn_task(proj, mem_k, mem_v, layer, tm, out_dtype):
    m = proj.shape[0]
    assert P_CQ % MEM_WIDTH == 0
    depth, nb, tokens, heads, hd = mem_k.shape
    per = m // nb
    parts = hd // LANES
    mem_k, mem_v = (t.reshape(depth, nb, tokens, heads, parts, LANES).transpose(0, 1, 2, 4, 3, 5)
                    .reshape(depth, nb, tokens * heads * parts, LANES) for t in (mem_k, mem_v))
    mems = max(tm // per, 1)
    kv_spec = pl.BlockSpec((1, mems, tokens * heads * parts, LANES), lambda i: (layer, i * tm // (per * mems), 0, 0))
    return _Task(
        "mem_attn", _mem_attn_kernel, m // tm, [proj, mem_k, mem_v],
        [pl.BlockSpec((tm, MEM_WIDTH), lambda i: (i, P_CQ // MEM_WIDTH)), kv_spec, kv_spec],
        [jax.ShapeDtypeStruct((m, MEM_WIDTH), out_dtype)], [pl.BlockSpec((tm, MEM_WIDTH), lambda i: (i, 0))])


def _merge_kernel(a_ref, b_ref, c_ref, ga_ref, gb_ref, gc_ref, wa_ref, wb_ref, wc_ref, o_ref):
    def branch(x_ref, g_ref, w_ref):
        y = _dot(x_ref[...].astype(BF16), w_ref[...])
        return y * (1.0 / (1.0 + jnp.exp(-g_ref[...])))

    o_ref[...] = (branch(a_ref, ga_ref, wa_ref) + branch(b_ref, gb_ref, wb_ref)
                  + branch(c_ref, gc_ref, wc_ref)).astype(o_ref.dtype)


def _merge(a, b, c, tail, w_a, w_b, w_c, tm, tn):
    m = a.shape[0]
    d = w_a.shape[1]
    gate0 = P_GATE // tn
    assert P_GATE % tn == 0 and d % tn == 0
    row = lambda x: pl.BlockSpec((tm, x.shape[1]), lambda i, j: (i, 0))
    gate = lambda br: pl.BlockSpec((tm, tn), lambda i, j, br=br: (i, gate0 + br * (d // tn) + j))
    wsp = lambda w: pl.BlockSpec((w.shape[0], tn), lambda i, j: (0, j))
    return pl.pallas_call(
        _merge_kernel, grid=(m // tm, d // tn),
        in_specs=[row(a), row(b), row(c), gate(0), gate(1), gate(2), wsp(w_a), wsp(w_b), wsp(w_c)],
        out_specs=pl.BlockSpec((tm, tn), lambda i, j: (i, j)),
        out_shape=jax.ShapeDtypeStruct((m, d), BF16),
        compiler_params=_params(2), name="merge")(a, b, c, tail, tail, tail, w_a, w_b, w_c)


def _rms(y, g):
    return y * lax.rsqrt(jnp.mean(y * y, axis=-1, keepdims=True) + EPS) * g


def _proj_norm_residual_kernel(y_ref, w_ref, x_ref, g_post_ref, g_next_ref, o_ref, h_ref):
    x1 = x_ref[...] + _rms(_dot(y_ref[...], w_ref[...]), g_post_ref[...])
    o_ref[...] = x1
    h_ref[...] = _rms(x1, g_next_ref[...]).astype(h_ref.dtype)


def _proj_norm_residual(y, w, x, g_post, g_next, tm):
    m, d = x.shape
    k = y.shape[1]
    row = lambda width: pl.BlockSpec((tm, width), lambda i: (i, 0))
    gsp = pl.BlockSpec((1, d), lambda i: (0, 0))
    return pl.pallas_call(
        _proj_norm_residual_kernel, grid=(m // tm,),
        in_specs=[row(k), pl.BlockSpec((k, d), lambda i: (0, 0)), row(d), gsp, gsp],
        out_specs=[row(d), row(d)],
        out_shape=[jax.ShapeDtypeStruct((m, d), F32), jax.ShapeDtypeStruct((m, d), BF16)],
        compiler_params=_params(1), name="proj_norm_residual")(y, w, x, g_post.reshape(1, d), g_next.reshape(1, d))


FFN_SPLIT = 4


def _ffn_kernel(h_ref, w1_ref, w2_ref, x_ref, g_ref, o_ref, a_ref):
    j = pl.program_id(1)
    last = pl.num_programs(1) - 1
    h = h_ref[...]
    piece = w1_ref.shape[1] // FFN_SPLIT
    for s in range(FFN_SPLIT):
        cols = slice(s * piece, (s + 1) * piece)
        a_ref[:, cols] = jnp.square(jnp.maximum(_dot(h, w1_ref[:, cols]), 0.0)).astype(BF16)
    y = _dot(a_ref[...], w2_ref[...])

    @pl.when(j == 0)
    def _():
        o_ref[...] = y

    @pl.when(jnp.logical_and(j > 0, j < last))
    def _():
        o_ref[...] += y

    @pl.when(j == last)
    def _():
        o_ref[...] = x_ref[...] + _rms(o_ref[...] + y, g_ref[...])


def _ffn(h, w1, w2, x, g, tm, tf):
    m, d = h.shape
    f = w1.shape[1]
    assert f // tf >= 2
    once = lambda shape: pl.BlockSpec(shape, lambda i, j: (i, 0), pipeline_mode=pl.Buffered(1))
    return pl.pallas_call(
        _ffn_kernel, grid=(m // tm, f // tf),
        in_specs=[pl.BlockSpec((tm, d), lambda i, j: (i, 0)), pl.BlockSpec((d, tf), lambda i, j: (0, j)),
                  pl.BlockSpec((tf, d), lambda i, j: (j, 0)), once((tm, d)), pl.BlockSpec((1, d), lambda i, j: (0, 0))],
        out_specs=once((tm, d)),
        out_shape=jax.ShapeDtypeStruct((m, d), F32),
        scratch_shapes=[pltpu.VMEM((tm, tf), BF16)],
        compiler_params=_params(2), name="ffn")(h, w1, w2, x, g.reshape(1, d))


def _row_tile(m, cap):
    t = min(m, cap)
    assert m % t == 0
    return t


def _layer(xp, xs, lw, layer, rel_bias, bias_prompt, mem_p, swa_caches, mem_s, conv_buf, s0):
    (w_in_t, w_conv, a_log, dt_bias, norm_delta, w_o_swa, w_o_delta, w_o_mem, w_out,
     norm_pre_mix, norm_post_mix, norm_pre_ffn, norm_post_ffn, w_ff1, w_ff2) = lw
    mp, ms, n_seq = xp.shape[0], xs.shape[0], s0.shape[0]
    seq = ms // n_seq
    tm_p = _row_tile(mp, 2048)
    host_steps = (mp // tm_p) * ((P_CQ + w_in_t.shape[0] - CQ_OFF) // PROJ_TILE)
    casts = [_cast_task(w, host_steps) for w in (w_o_swa, w_o_delta, w_o_mem, w_out, w_ff1, w_ff2)]
    proj_p, w_o_swa, w_o_delta, w_o_mem, w_out, w_ff1, w_ff2 = _in_proj(xp, norm_pre_mix, w_in_t, tm_p, casts)
    proj_s, = _in_proj(xs, norm_pre_mix, w_in_t, _row_tile(ms, 2048))
    chunk = DN_CHUNK if mp % DN_CHUNK == 0 else mp
    riding = mp // chunk == n_seq
    swa_s = _swa_sample_task(proj_s, swa_caches, layer, rel_bias, seq)
    mem_s_task = _mem_attn_task(proj_s, *mem_s, layer, seq * (1 if riding else math.gcd(n_seq, MEM_PER_STEP)), F32)
    riders = [swa_s, mem_s_task] if riding else []
    b_p, state_p, *rode = _delta_scan(proj_p, w_conv, a_log, dt_bias, norm_delta, None, None, chunk, riders)
    a_s, c_s = rode if riding else (_run_task(swa_s)[0], _run_task(mem_s_task)[0])
    a_p = _swa_prompt(proj_p, bias_prompt)
    c_p = _run_task(_mem_attn_task(proj_p, *mem_p, 0, _row_tile(mp, 512), BF16))[0]
    b_s, state_s = _delta_scan(proj_s, w_conv, a_log, dt_bias, norm_delta, s0, conv_buf, seq)
    outs = []
    for x, proj, a, b, c in ((xp, proj_p, a_p, b_p, c_p), (xs, proj_s, a_s, b_s, c_s)):
        m = x.shape[0]
        tm = _row_tile(m, 1024)
        merged = _merge(a, b, c, proj, w_o_swa, w_o_delta, w_o_mem, tm, 512)
        x1, h2 = _proj_norm_residual(merged, w_out, x, norm_post_mix, norm_pre_ffn, _row_tile(m, 512))
        outs.append(_ffn(h2, w_ff1, w_ff2, x1, norm_post_ffn, tm, 1024))
    return outs[0], outs[1], proj_p, proj_s, state_p, state_s


def kernel(x_prompt, x_sample, cache_swa0_k, cache_swa0_v, cache_swa1_k, cache_swa1_v, cache_swa2_k, cache_swa2_v, state_delta, state_conv, cache_mem_k, cache_mem_v, mem_prompt, rel_bias, w_in, w_conv, A_log, dt_bias, norm_delta, norm_mem, w_mem_kv, w_o_swa, w_o_delta, w_o_mem, w_out, norm_pre_mix, norm_post_mix, norm_pre_ffn, norm_post_ffn, w_ff1, w_ff2):
    depth = w_in.shape[0]
    bp, tp, d = x_prompt.shape
    bs, ts, _ = x_sample.shape
    assert bp == 1 and ts > CONV_WIDTH - 1 and ts % 8 == 0
    sample_swa = (cache_swa0_k, cache_swa0_v, cache_swa1_k, cache_swa1_v, cache_swa2_k, cache_swa2_v)
    bias_prompt = _prompt_bias_tables(rel_bias)
    xp = x_prompt.reshape(bp * tp, d)
    xs = x_sample.reshape(bs * ts, d)
    new_p = [[] for _ in range(10)]
    new_s = [[] for _ in range(8)]
    hw = SWA_HPG * HEAD_DIM
    for l in range(depth):
        lw = (jnp.swapaxes(w_in, 1, 2)[l], w_conv[l], A_log[l], dt_bias[l], norm_delta[l], w_o_swa[l],
              w_o_delta[l], w_o_mem[l], w_out[l],
              norm_pre_mix[l], norm_post_mix[l], norm_pre_ffn[l], norm_post_ffn[l],
              w_ff1[l], w_ff2[l])
        mem = mem_prompt.reshape(-1, d)
        mkv = _matmul(_rmsnorm_cast(mem, norm_mem[l], _row_tile(mem.shape[0], 256)), w_mem_kv[l],
                      2 * MEM_WIDTH, _row_tile(mem.shape[0], 256), 512, "mem_kv")
        mk = mkv[:, :MEM_WIDTH].reshape(1, bp, -1, MEM_HEADS, MEM_HEAD_DIM)
        mv = mkv[:, MEM_WIDTH:].reshape(1, bp, -1, MEM_HEADS, MEM_HEAD_DIM)
        xp, xs, main_p, main_s, s_p, s_s = _layer(
            xp, xs, lw, l, rel_bias, bias_prompt, (mk, mv), sample_swa, (cache_mem_k, cache_mem_v),
            state_conv[l], state_delta[l])
        vals_p = []
        for g, (window, _) in enumerate(SWA_GROUPS):
            keep = min(window, tp)
            for sec in (1, 2):
                c0 = sec * SWA_WIDTH + g * hw
                vals_p.append(main_p[tp - keep:, c0:c0 + hw].reshape(bp, keep, SWA_HPG, HEAD_DIM))
        vals_p.append(s_p.reshape(bp, DN_HEADS, HEAD_DIM, HEAD_DIM))
        vals_p.append(main_p[tp - (CONV_WIDTH - 1):, B_OFF:B_OFF + 3 * DN_WIDTH].reshape(bp, CONV_WIDTH - 1, -1))
        vals_p += [mk[0], mv[0]]
        vals_s = []
        main_s3 = main_s.reshape(bs, ts, -1)
        for g in range(len(SWA_GROUPS)):
            for sec in (1, 2):
                c0 = sec * SWA_WIDTH + g * hw
                vals_s.append(main_s3[:, :, c0:c0 + hw].reshape(bs, ts, SWA_HPG, HEAD_DIM))
        vals_s.append(s_s)
        vals_s.append(main_s3[:, ts - (CONV_WIDTH - 1):, B_OFF:B_OFF + 3 * DN_WIDTH])
        for lst, val in zip(new_p, vals_p):
            lst.append(val)
        for lst, val in zip(new_s, vals_s):
            lst.append(val)
    outs_p = [jnp.stack(t) for t in new_p]
    outs_s = [jnp.stack(t) for t in new_s]
    return (xp.reshape(bp, tp, d), xs.reshape(bs, ts, d), *outs_p, *outs_s)
```

```python
import functools
import math

import numpy as np
import jax
import jax.numpy as jnp
from jax import lax
from jax.experimental import pallas as pl
from jax.experimental.pallas import tpu as pltpu

EPS = 1e-6
HEAD_DIM = 128
SWA_GROUPS = ((128, 1), (512, 4), (2048, 16))
SWA_SPAN = 128
SWA_HPG = 4
SWA_WIDTH = SWA_HPG * len(SWA_GROUPS) * HEAD_DIM
DN_HEADS = 12
DN_WIDTH = DN_HEADS * HEAD_DIM
CONV_WIDTH = 4
DN_CHUNK = 128
MEM_HEADS = 4
MEM_HEAD_DIM = 256
MEM_WIDTH = MEM_HEADS * MEM_HEAD_DIM
N_BUCKETS = 32
MAX_DISTANCE = 2048
N_BRANCHES = 3

A_OFF = 0
B_OFF = 3 * SWA_WIDTH
Z_OFF = B_OFF + 3 * DN_WIDTH
BA_OFF = Z_OFF + DN_WIDTH
CQ_OFF = BA_OFF + 2 * DN_HEADS
GATE_OFF = CQ_OFF + MEM_WIDTH
LANES = 128
SUBLANES = 8
PROJ_TILE = 512
P_BA = BA_OFF
P_CQ = P_BA + PROJ_TILE
P_GATE = P_CQ + MEM_WIDTH
SWA_ROWS = 2048
SWA_UNROLL = 8
NEG = -1e30
VMEM_LIMIT = 58 * 1024 * 1024

BF16 = jnp.bfloat16
F32 = jnp.float32


def _params(n_grid):
    return pltpu.CompilerParams(dimension_semantics=("arbitrary",) * n_grid, vmem_limit_bytes=VMEM_LIMIT)


def _dot(a, b):
    return jnp.dot(a, b, preferred_element_type=F32)


def _dot_nt(a, b):
    return lax.dot_general(a, b, (((1,), (1,)), ((), ())), preferred_element_type=F32)


def _dot_tn(a, b):
    return lax.dot_general(a, b, (((0,), (0,)), ((), ())), preferred_element_type=F32)


def _split2(x):
    hi = x.astype(BF16)
    lo = (x - hi.astype(F32)).astype(BF16)
    return hi, lo


def _dot3(a, b, dot=_dot):
    ah, al = _split2(a)
    bh, bl = _split2(b)
    return dot(ah, bh) + (dot(ah, bl) + dot(al, bh))


def _rmsnorm_cast_kernel(x_ref, g_ref, o_ref):
    x = x_ref[...]
    y = x * lax.rsqrt(jnp.mean(x * x, axis=-1, keepdims=True) + EPS)
    o_ref[...] = (y * g_ref[...]).astype(o_ref.dtype)


def _rmsnorm_cast(x, g, tm):
    m, d = x.shape
    return pl.pallas_call(
        _rmsnorm_cast_kernel, grid=(m // tm,),
        in_specs=[pl.BlockSpec((tm, d), lambda i: (i, 0)), pl.BlockSpec((1, d), lambda i: (0, 0))],
        out_specs=pl.BlockSpec((tm, d), lambda i: (i, 0)),
        out_shape=jax.ShapeDtypeStruct((m, d), BF16),
        compiler_params=_params(1), name="rmsnorm_cast")(x, g.reshape(1, d))


def _matmul_kernel(x_ref, w_ref, o_ref):
    o_ref[...] = _dot(x_ref[...], w_ref[...].astype(BF16)).astype(o_ref.dtype)


def _matmul(x, w, n, tm, tn, name):
    m, k = x.shape
    assert n % tn == 0 and m % tm == 0
    return pl.pallas_call(
        _matmul_kernel, grid=(m // tm, n // tn),
        in_specs=[pl.BlockSpec((tm, k), lambda i, j: (i, 0)), pl.BlockSpec((k, tn), lambda i, j: (0, j))],
        out_specs=pl.BlockSpec((tm, tn), lambda i, j: (i, j)),
        out_shape=jax.ShapeDtypeStruct((m, n), F32),
        compiler_params=_params(2), name=name)(x, w)


def _in_proj_kernel(x_ref, g_ref, w_ref, o_ref, h_ref):
    @pl.when(pl.program_id(1) == 0)
    def _():
        h_ref[...] = _rms(x_ref[...], g_ref[...]).astype(BF16)

    o_ref[...] = _dot_nt(h_ref[...], w_ref[...].astype(BF16))


def _in_proj(x, g, wt, tm, riders=()):
    m, k = x.shape
    tn = PROJ_TILE
    n = P_CQ + wt.shape[0] - CQ_OFF
    assert n % tn == 0 and m % tm == 0 and P_CQ % tn == 0 and CQ_OFF % SUBLANES == 0

    def w_rows(i, j):
        return (pl.multiple_of(jnp.where(j * tn < P_CQ, j * tn, j * tn - P_CQ + CQ_OFF), SUBLANES), 0)

    return _hosted_call(
        "in_proj", _in_proj_kernel, (m // tm, n // tn), [x, g.reshape(1, k), wt],
        [pl.BlockSpec((tm, k), lambda i, j: (i, 0), pipeline_mode=pl.Buffered(1)),
         pl.BlockSpec((1, k), lambda i, j: (0, 0)),
         pl.BlockSpec((pl.Element(tn), pl.Element(k)), w_rows)],
        [jax.ShapeDtypeStruct((m, n), F32)], [pl.BlockSpec((tm, tn), lambda i, j: (i, j))],
        [pltpu.VMEM((tm, k), BF16)], riders)


def _t5_bucket(dist):
    max_exact = N_BUCKETS // 2
    df = jnp.maximum(dist, 1).astype(F32)
    large = max_exact + (jnp.log(df / max_exact) / math.log(MAX_DISTANCE / max_exact)
                         * (N_BUCKETS - max_exact)).astype(jnp.int32)
    return jnp.where(dist < max_exact, dist, jnp.minimum(large, N_BUCKETS - 1))


def _group_bias(rel_bias, g):
    _, dil = SWA_GROUPS[g]
    dist = jnp.arange(SWA_SPAN + 1, dtype=jnp.int32) * dil
    onehot = _t5_bucket(dist)[None, :, None] == jnp.arange(N_BUCKETS, dtype=jnp.int32)[None, None, :]
    heads = jnp.transpose(rel_bias[:, g * SWA_HPG:(g + 1) * SWA_HPG].astype(F32))
    return jnp.sum(jnp.where(onehot, heads[:, None, :], 0.0), axis=-1)


def _prompt_bias_tables(rel_bias):
    span = SWA_SPAN
    first = (np.arange(2 * span) >= span)[None, None, :]
    out = []
    for g in range(len(SWA_GROUPS)):
        w = _group_bias(rel_bias, g)
        h = w.shape[0]
        p = 3 * span
        e = jnp.concatenate([jnp.full((h, span - 1), NEG, F32), w[:, ::-1], jnp.full((h, p - 2 * span), NEG, F32)], 1)
        skew = jnp.broadcast_to(e[:, None, :], (h, span, p)).reshape(h, span * p)[:, :span * (p - 1)]
        tab = skew.reshape(h, span, p - 1)[:, :, span - 1:3 * span - 1]
        out.append(jnp.stack([jnp.where(first, tab, NEG), tab], axis=1))
    return jnp.stack(out)


def _sample_bias_tables(rel_bias, g, n_res, n_new, cache_len):
    _, dil = SWA_GROUPS[g]
    w = _group_bias(rel_bias, g)
    h = w.shape[0]
    lm = cache_len // dil
    assert cache_len % dil == 0
    wpad = jnp.concatenate([w, jnp.full((h, lm + n_new), NEG, F32)], axis=1)
    res = np.arange(n_res)[None, None, :]
    rows = []
    for s in range(n_new):
        vec = wpad[:, s // dil + 1:lm + s // dil + 1][:, ::-1]
        rows.append(jnp.where(res == s % dil, vec[:, :, None], NEG).reshape(h, lm * n_res))
    s = np.arange(n_new)
    dist = s[:, None] - s[None, :]
    new = jnp.full((h, n_new, n_new), NEG, F32)
    for j in range(min((n_new - 1) // dil, SWA_SPAN) + 1):
        new = jnp.where((dist == j * dil)[None], w[:, j][:, None, None], new)
    return jnp.stack(rows, axis=1), new


def _swa_prompt_kernel(*refs, n_groups):
    ins = refs[:5 * n_groups]
    bias_ref = refs[5 * n_groups]
    o_ref = refs[5 * n_groups + 1]
    scr = refs[5 * n_groups + 2:]
    kext, vext, og, lg = scr[:n_groups], scr[n_groups:2 * n_groups], scr[2 * n_groups:3 * n_groups], scr[3 * n_groups:]
    n = pl.program_id(0)
    scale = HEAD_DIM ** -0.5
    span = SWA_SPAN
    for g in range(n_groups):
        _, dil = SWA_GROUPS[g]
        q_ref, kc_ref, vc_ref, kp_ref, vp_ref = ins[5 * g:5 * g + 5]
        blk = span * dil
        kext[g][0:blk, :] = kp_ref[...]
        kext[g][blk:, :] = kc_ref[...]
        vext[g][0:blk, :] = vp_ref[...]
        vext[g][blk:, :] = vc_ref[...]

        def body(it, carry, g=g, dil=dil, blk=blk, q_ref=q_ref):
            items = range(SWA_UNROLL)
            starts, sels = [], []
            for u in items:
                idx = it * SWA_UNROLL + u
                b = idx // dil
                starts.append(b * blk + (idx - b * dil))
                sels.append(jnp.where(jnp.logical_and(n == 0, b == 0), 0, 1))
            q = [q_ref[pl.ds(starts[u], span, stride=dil), :].astype(BF16) for u in items]
            kk = [kext[g][pl.ds(starts[u], 2 * span, stride=dil), :].astype(BF16) for u in items]
            s = [_dot_nt(q[u], kk[u]) * scale + bias_ref[g, 0, sels[u]] for u in items]
            m = [jnp.max(s[u], axis=-1, keepdims=True) for u in items]
            p = [jnp.exp(s[u] - m[u]) for u in items]
            den = [jnp.sum(p[u], axis=-1, keepdims=True) for u in items]
            vv = [vext[g][pl.ds(starts[u], 2 * span, stride=dil), :].astype(BF16) for u in items]
            o = [_dot(p[u].astype(BF16), vv[u]) / den[u] for u in items]
            for u in items:
                og[g][pl.ds(starts[u], span, stride=dil), :] = o[u]
                lg[g][pl.ds(starts[u], span, stride=dil), :] = jnp.broadcast_to(
                    m[u] + jnp.log(den[u]), (span, HEAD_DIM))
            return carry

        lax.fori_loop(0, SWA_ROWS // span // SWA_UNROLL, body, 0)
    lses = [lg[g][...] for g in range(n_groups)]
    mx = functools.reduce(jnp.maximum, lses)
    ws = [jnp.exp(l - mx) for l in lses]
    num = functools.reduce(lambda a, b: a + b, [w * og[g][...] for g, w in enumerate(ws)])
    o_ref[...] = (num / functools.reduce(lambda a, b: a + b, ws)).astype(o_ref.dtype)


def _swa_prompt(proj, bias_tables):
    t = proj.shape[0]
    n_groups = len(SWA_GROUPS)
    assert t % SWA_ROWS == 0
    in_specs, scratch_k, scratch_o = [], [], []
    for g, (_, dil) in enumerate(SWA_GROUPS):
        blk = SWA_SPAN * dil
        per = SWA_ROWS // blk
        assert SWA_ROWS % blk == 0
        qc, kc, vc = (sec * (SWA_WIDTH // HEAD_DIM) + g * SWA_HPG for sec in range(3))
        cur = lambda col: pl.BlockSpec((SWA_ROWS, HEAD_DIM), lambda n, h, col=col: (n, col + h))
        prev = lambda col, blk=blk, per=per: pl.BlockSpec(
            (blk, HEAD_DIM), lambda n, h, col=col, per=per: (jnp.maximum(n * per - 1, 0), col + h))
        in_specs += [cur(qc), cur(kc), cur(vc), prev(kc), prev(vc)]
        scratch_k.append(pltpu.VMEM((SWA_ROWS + blk, HEAD_DIM), F32))
        scratch_o.append(pltpu.VMEM((SWA_ROWS, HEAD_DIM), F32))
    in_specs.append(pl.BlockSpec((n_groups, 1, 2, SWA_SPAN, 2 * SWA_SPAN), lambda n, h: (0, h, 0, 0, 0)))
    return pl.pallas_call(
        functools.partial(_swa_prompt_kernel, n_groups=n_groups),
        grid=(t // SWA_ROWS, SWA_HPG),
        in_specs=in_specs,
        out_specs=pl.BlockSpec((SWA_ROWS, HEAD_DIM), lambda n, h: (n, h)),
        out_shape=jax.ShapeDtypeStruct((t, SWA_HPG * HEAD_DIM), BF16),
        scratch_shapes=scratch_k + scratch_k + scratch_o + scratch_o,
        compiler_params=_params(2), name="swa_prompt")(*([proj] * (5 * n_groups)), bias_tables)


def _swa_sample_kernel(*refs, n_groups):
    qkv_ref = refs[0]
    caches = refs[1:1 + 2 * n_groups]
    biases = refs[1 + 2 * n_groups:1 + 4 * n_groups]
    o_ref = refs[1 + 4 * n_groups]
    scale = HEAD_DIM ** -0.5
    nseq = caches[0].shape[1]
    n_new = qkv_ref.shape[0] // nseq
    items = [(b, g, h) for b in range(nseq) for h in range(SWA_HPG) for g in range(n_groups)]
    idx = range(len(items))

    def new(sec, b, g, h):
        col = sec * SWA_WIDTH + (g * SWA_HPG + h) * HEAD_DIM
        return qkv_ref[b * n_new:(b + 1) * n_new, col:col + HEAD_DIM].astype(BF16)

    def cached(ref, b, h):
        nk = math.prod(ref.shape[2:-1]) // SWA_HPG
        if len(ref.shape) == 5:
            pick = (0, b, slice(None), pl.ds(h, ref.shape[3] // SWA_HPG, stride=SWA_HPG), slice(None))
        else:
            pick = (0, b, pl.ds(h, nk, stride=SWA_HPG), slice(None))
        return ref[pick].reshape(nk, HEAD_DIM).astype(BF16)

    q = [new(0, *it) for it in items]
    sc = [_dot_nt(q[i], cached(caches[2 * g], b, h)) * scale + biases[2 * g][h]
          for i, (b, g, h) in enumerate(items)]
    sn = [_dot_nt(q[i], new(1, b, g, h)) * scale + biases[2 * g + 1][h] for i, (b, g, h) in enumerate(items)]
    m = [jnp.maximum(jnp.max(sc[i], axis=-1, keepdims=True), jnp.max(sn[i], axis=-1, keepdims=True)) for i in idx]
    pc = [jnp.exp(sc[i] - m[i]) for i in idx]
    pn = [jnp.exp(sn[i] - m[i]) for i in idx]
    den = [jnp.sum(pc[i], axis=-1, keepdims=True) + jnp.sum(pn[i], axis=-1, keepdims=True) for i in idx]
    o = [(_dot(pc[i].astype(BF16), cached(caches[2 * g + 1], b, h)) + _dot(pn[i].astype(BF16), new(2, b, g, h)))
         / den[i] for i, (b, g, h) in enumerate(items)]
    lse = [m[i] + jnp.log(den[i]) for i in idx]
    for b in range(nseq):
        for h in range(SWA_HPG):
            mine = [i for i in idx if items[i][0] == b and items[i][2] == h]
            mx = functools.reduce(jnp.maximum, [lse[i] for i in mine])
            ws = [jnp.exp(lse[i] - mx) for i in mine]
            num = functools.reduce(lambda x, y: x + y, [w * o[i] for w, i in zip(ws, mine)])
            o_ref[b * n_new:(b + 1) * n_new, h * HEAD_DIM:(h + 1) * HEAD_DIM] = (
                num / functools.reduce(lambda x, y: x + y, ws))


class _Task:
    def __init__(self, name, body, steps, args, in_specs, out_shapes, out_specs):
        self.name, self.body, self.steps = name, body, steps
        self.args, self.in_specs, self.out_shapes, self.out_specs = args, in_specs, out_shapes, out_specs


def _run_task(task):
    return pl.pallas_call(
        task.body, grid=(task.steps,), in_specs=task.in_specs, out_specs=task.out_specs,
        out_shape=task.out_shapes, compiler_params=_params(1), name=task.name)(*task.args)


def _riding_specs(task, grid):
    assert math.prod(grid) >= task.steps

    def step_of(*idx):
        flat = functools.reduce(lambda acc, t: acc * t[1] + t[0], zip(idx[1:], grid[1:]), idx[0])
        return flat if math.prod(grid) == task.steps else jnp.minimum(flat, task.steps - 1)

    re = lambda spec: pl.BlockSpec(spec.block_shape, lambda *idx: spec.index_map(step_of(*idx)))
    return [re(s) for s in task.in_specs], [re(s) for s in task.out_specs]


BF16_ROWS = 16


def _cast_task(w, max_steps):
    rows, cols = w.shape
    steps = max(s for s in range(1, min(max_steps, rows // BF16_ROWS) + 1) if rows % (s * BF16_ROWS) == 0)
    slab = pl.BlockSpec((rows // steps, cols), lambda i: (i, 0))

    def body(x_ref, o_ref):
        o_ref[...] = x_ref[...].astype(BF16)

    return _Task("cast", body, steps, [w], [slab], [jax.ShapeDtypeStruct((rows, cols), BF16)], [slab])


def _swa_sample_task(proj, caches, layer, rel_bias, n_new, per):
    rows_total = proj.shape[0]
    nb = rows_total // n_new
    assert nb % per == 0
    n_groups = len(SWA_GROUPS)
    hw = SWA_HPG * HEAD_DIM
    cache_in, cache_specs, bias_in, bias_specs = [], [], [], []
    for g, (_, dil) in enumerate(SWA_GROUPS):
        depth, _, cache_len, heads, hd = caches[2 * g].shape
        if dil >= 2 * n_new and n_new % 8 == 0:
            n_res = n_new
            shape = (depth, nb, cache_len // dil, dil * heads, hd)
            spec = pl.BlockSpec((1, per, cache_len // dil, n_res * heads, hd), lambda b: (layer, b, 0, 0, 0))
        else:
            n_res = dil
            shape = (depth, nb, cache_len * heads, hd)
            spec = pl.BlockSpec((1, per, cache_len * heads, hd), lambda b: (layer, b, 0, 0))
        for c in caches[2 * g:2 * g + 2]:
            cache_in.append(c.reshape(shape))
            cache_specs.append(spec)
        for tab in _sample_bias_tables(rel_bias, g, n_res, n_new, cache_len):
            bias_in.append(tab)
            bias_specs.append(pl.BlockSpec(tab.shape, lambda b: (0, 0, 0)))
    return _Task(
        "swa_sample", functools.partial(_swa_sample_kernel, n_groups=n_groups), nb // per,
        [proj, *cache_in, *bias_in],
        [pl.BlockSpec((per * n_new, 3 * SWA_WIDTH), lambda b: (b, 0))] + cache_specs + bias_specs,
        [jax.ShapeDtypeStruct((rows_total, hw), F32)], [pl.BlockSpec((per * n_new, hw), lambda b: (b, 0))])


def _silu(x):
    return x * (1.0 / (1.0 + jnp.exp(-x)))


def _softplus(x):
    return jnp.maximum(x, 0.0) + jnp.log(1.0 + jnp.exp(-jnp.abs(x)))


def _inv_dot(a, b):
    return _dot(a.astype(BF16), b.astype(BF16))


HALO = 8
DELTA_SEQS_PER_STEP = 8


def _delta_scan_kernel(*refs, chunk, nseq, carry):
    if carry:
        x_ref, z_ref, ba_ref, wc_ref, alog_ref, dt_ref, gain_ref, o_ref, s_out_ref, s_scr, halo_scr = refs
        s_in_ref = halo_ref = None
    else:
        x_ref, z_ref, ba_ref, wc_ref, alog_ref, dt_ref, gain_ref, s_in_ref, halo_ref, o_ref, s_out_ref = refs
        s_scr = halo_scr = None
    c = chunk
    rows = nseq * c
    seqs = range(nseq)
    part = lambda t, b: t[b * c:(b + 1) * c]
    if carry:
        @pl.when(pl.program_id(0) == 0)
        def _():
            s_scr[...] = jnp.zeros_like(s_scr)
            halo_scr[...] = jnp.zeros_like(halo_scr)

    x = x_ref[...]
    hist = halo_scr[...] if carry else halo_ref[...]
    xp = jnp.concatenate([t for b in seqs for t in (hist[b * HALO:(b + 1) * HALO], part(x, b))], axis=0)
    acc = x * wc_ref[CONV_WIDTH - 1:CONV_WIDTH, :]
    for t in range(1, CONV_WIDTH):
        rolled = pltpu.roll(xp, t, 0)
        shifted = jnp.concatenate([rolled[b * (HALO + c) + HALO:(b + 1) * (HALO + c)] for b in seqs], axis=0)
        acc = acc + shifted * wc_ref[CONV_WIDTH - 1 - t:CONV_WIDTH - t, :]
    if carry:
        halo_scr[...] = x[c - HALO:, :]
    qkv = _silu(acc)

    ba = ba_ref[...]
    beta_all = 1.0 / (1.0 + jnp.exp(-ba))
    g_all = -jnp.exp(alog_ref[...]) * _softplus(ba + dt_ref[...])
    row = lax.broadcasted_iota(jnp.int32, (rows, rows), 0)
    col = lax.broadcasted_iota(jnp.int32, (rows, rows), 1)
    same = (row // c) == (col // c)
    tri = (row >= col) & same
    strict = (row > col) & same
    tril = tri.astype(BF16)
    g1 = g_all.astype(BF16)
    r1 = g_all - g1.astype(F32)
    g2 = r1.astype(BF16)
    g3 = (r1 - g2.astype(F32)).astype(BF16)
    gcum_all = _dot(tril, g1) + (_dot(tril, g2) + _dot(tril, g3))
    gcum_t = gcum_all.T
    eye = (row == col).astype(F32)

    heads = range(DN_HEADS)
    hs = lambda sec, h: slice(sec * DN_WIDTH + h * HEAD_DIM, sec * DN_WIDTH + (h + 1) * HEAD_DIM)
    lane = lambda h: slice(DN_HEADS + h, DN_HEADS + h + 1)
    l2 = lambda t: t * lax.rsqrt(jnp.sum(t * t, axis=-1, keepdims=True) + EPS)
    q = [l2(qkv[:, hs(0, h)]) * (HEAD_DIM ** -0.5) for h in heads]
    k = [l2(qkv[:, hs(1, h)]) for h in heads]
    v = [qkv[:, hs(2, h)] for h in heads]
    bcol = [beta_all[:, h:h + 1] for h in heads]
    gc = [gcum_all[:, lane(h)] for h in heads]
    glast = [[gcum_all[(b + 1) * c - 1:(b + 1) * c, lane(h)] for b in seqs] for h in heads]
    decay = [jnp.exp(jnp.where(tri, gc[h] - gcum_t[lane(h), :], NEG)) for h in heads]
    kb = [k[h] * bcol[h] for h in heads]
    k16 = [k[h].astype(BF16) for h in heads]
    nmat = [jnp.where(strict, _dot_nt(kb[h].astype(BF16), k16[h]) * decay[h], 0.0) for h in heads]
    attn = [jnp.where(tri, _dot_nt(q[h].astype(BF16), k16[h]) * decay[h], 0.0) for h in heads]
    inv = [eye] * DN_HEADS
    blk = 1
    while blk < c:
        pair = (row // (2 * blk) == col // (2 * blk)) & (row % (2 * blk) >= blk) & (col % (2 * blk) < blk)
        low = [jnp.where(pair, nmat[h], 0.0) for h in heads]
        if blk == 1:
            inv = [eye - low[h] for h in heads]
        else:
            t = [_inv_dot(low[h], inv[h]) for h in heads]
            inv = [inv[h] - _inv_dot(inv[h], t[h]) for h in heads]
        blk *= 2
    eg = [jnp.exp(gc[h]) for h in heads]
    rhs = [jnp.concatenate([v[h] * bcol[h], kb[h] * eg[h]], axis=1) for h in heads]
    sol = [_inv_dot(inv[h], rhs[h]) for h in heads]
    qdec = [q[h] * eg[h] for h in heads]
    hb = [(h, b) for h in heads for b in seqs]
    s_prev = {(h, b): (s_scr[h] if carry else s_in_ref[b, h]) for h, b in hb}
    wq = {(h, b): jnp.concatenate([part(sol[h][:, HEAD_DIM:], b), part(qdec[h], b)], axis=0).astype(BF16)
          for h, b in hb}
    ws = {i: _dot(wq[i], s_prev[i].astype(BF16)) for i in hb}
    vn = {(h, b): part(sol[h][:, :HEAD_DIM], b) - ws[h, b][:c] for h, b in hb}
    v_new = [jnp.concatenate([vn[h, b] for b in seqs], axis=0).astype(BF16) for h in heads]
    o = [jnp.concatenate([ws[h, b][c:] for b in seqs], axis=0) + _dot(attn[h].astype(BF16), v_new[h])
         for h in heads]
    kdec = {(h, b): (part(k[h], b) * jnp.exp(glast[h][b] - part(gc[h], b))).astype(BF16) for h, b in hb}
    s_new = {(h, b): s_prev[h, b] * jnp.exp(glast[h][b]) + _dot_tn(kdec[h, b], vn[h, b].astype(BF16))
             for h, b in hb}
    for h in heads:
        if carry:
            s_scr[h] = s_new[h, 0]
            s_out_ref[h] = s_new[h, 0]
        else:
            for b in seqs:
                s_out_ref[b, h] = s_new[h, b]
        y = o[h] * lax.rsqrt(jnp.mean(o[h] * o[h], axis=-1, keepdims=True) + EPS) * gain_ref[...]
        o_ref[:, hs(0, h)] = (y * _silu(z_ref[:, hs(0, h)])).astype(o_ref.dtype)


def _delta_scan(proj, w_conv, a_log, dt_bias, norm_delta, s0, conv_buf, chunk, riders=()):
    m = proj.shape[0]
    carry = s0 is None
    nseq = 1 if carry else math.gcd(m // chunk, DELTA_SEQS_PER_STEP)
    rows = nseq * chunk
    n = m // rows
    width = 3 * DN_WIDTH
    assert chunk >= HALO and B_OFF % width == 0
    pad = lambda v_, off: jnp.zeros((1, LANES), F32).at[0, off:off + DN_HEADS].set(v_.astype(F32))
    consts = [w_conv, pad(a_log, DN_HEADS), pad(dt_bias, DN_HEADS), norm_delta.reshape(1, HEAD_DIM).astype(F32)]
    const_specs = [pl.BlockSpec((CONV_WIDTH, width), lambda i: (0, 0))] + [pl.BlockSpec((1, LANES), lambda i: (0, 0))] * 3
    in_specs = [pl.BlockSpec((rows, width), lambda i: (i, B_OFF // width)),
                pl.BlockSpec((rows, DN_WIDTH), lambda i: (i, Z_OFF // DN_WIDTH)),
                pl.BlockSpec((rows, LANES), lambda i: (i, P_BA // LANES))] + const_specs
    args = [proj, proj, proj] + consts
    state = (DN_HEADS, HEAD_DIM, HEAD_DIM)
    if carry:
        out_dtype = BF16
        s_shape, s_spec = state, pl.BlockSpec(state, lambda i: (0, 0, 0))
        scratch = [pltpu.VMEM(state, F32), pltpu.VMEM((HALO, width), F32)]
    else:
        out_dtype = F32
        halo = jnp.pad(conv_buf, ((0, 0), (HALO - (CONV_WIDTH - 1), 0), (0, 0))).reshape(-1, width)
        in_specs += [pl.BlockSpec((nseq,) + state, lambda i: (i, 0, 0, 0)),
                     pl.BlockSpec((nseq * HALO, width), lambda i: (i, 0))]
        args += [s0, halo]
        s_shape, s_spec = (m // chunk,) + state, pl.BlockSpec((nseq,) + state, lambda i: (i, 0, 0, 0))
        scratch = []
    out_specs = [pl.BlockSpec((rows, DN_WIDTH), lambda i: (i, 0)), s_spec]
    out_shape = [jax.ShapeDtypeStruct((m, DN_WIDTH), out_dtype), jax.ShapeDtypeStruct(s_shape, F32)]
    host = functools.partial(_delta_scan_kernel, chunk=chunk, nseq=nseq, carry=carry)
    return _hosted_call("delta_scan", host, (n,), args, in_specs, out_shape, out_specs, scratch, riders)


def _hosted_kernel(*refs, host, n_host_in, n_host_out, riders):
    n_in = n_host_in + sum(len(t.in_specs) for t in riders)
    n_out = n_host_out + sum(len(t.out_specs) for t in riders)
    ins, outs, scratch = refs[:n_in], refs[n_in:n_in + n_out], refs[n_in + n_out:]
    host(*ins[:n_host_in], *outs[:n_host_out], *scratch)
    pi, po = n_host_in, n_host_out
    for t in riders:
        ni, no = len(t.in_specs), len(t.out_specs)
        t.body(*ins[pi:pi + ni], *outs[po:po + no])
        pi, po = pi + ni, po + no


def _hosted_call(name, host, grid, args, in_specs, out_shape, out_specs, scratch, riders):
    body = functools.partial(_hosted_kernel, host=host, n_host_in=len(in_specs), n_host_out=len(out_specs),
                             riders=tuple(riders))
    args, in_specs, out_shape, out_specs = list(args), list(in_specs), list(out_shape), list(out_specs)
    for t in riders:
        r_in, r_out = _riding_specs(t, grid)
        args, in_specs = args + list(t.args), in_specs + r_in
        out_shape, out_specs = out_shape + list(t.out_shapes), out_specs + r_out
    return pl.pallas_call(
        body, grid=grid, in_specs=in_specs, out_specs=out_specs, out_shape=out_shape, scratch_shapes=scratch,
        compiler_params=_params(len(grid)), name=name)(*args)


def _mem_attn_kernel(q_ref, k_ref, v_ref, o_ref):
    scale = MEM_HEAD_DIM ** -0.5
    parts = MEM_HEAD_DIM // LANES
    period = MEM_HEADS * parts
    tokens = k_ref.shape[2] // period
    pick = lambda b, c, h: (0, b, pl.ds(c * MEM_HEADS + h, tokens, stride=period), slice(None))
    nb = k_ref.shape[1]
    rows = q_ref.shape[0] // nb
    items = [(b, h) for b in range(nb) for h in range(MEM_HEADS)]
    idx = range(len(items))
    q = [[q_ref[b * rows:(b + 1) * rows, h * MEM_HEAD_DIM + c * LANES:h * MEM_HEAD_DIM + (c + 1) * LANES]
          .astype(BF16) for c in range(parts)] for b, h in items]
    s = [functools.reduce(lambda x, y: x + y, [_dot_nt(q[i][c], k_ref[pick(b, c, h)].astype(BF16))
                                               for c in range(parts)]) * scale for i, (b, h) in enumerate(items)]
    p = [jnp.exp(s[i] - jnp.max(s[i], axis=-1, keepdims=True)) for i in idx]
    den = [jnp.sum(p[i], axis=-1, keepdims=True) for i in idx]
    for i, (b, h) in enumerate(items):
        for c in range(parts):
            col = h * MEM_HEAD_DIM + c * LANES
            o_ref[b * rows:(b + 1) * rows, col:col + LANES] = (
                _dot(p[i].astype(BF16), v_ref[pick(b, c, h)].astype(BF16)) / den[i]).astype(o_ref.dtype)


MEM_PER_STEP = 4
RIDER_SEQS_MAX = 2


def _mem_attn_task(proj, mem_k, mem_v, layer, tm, out_dtype):
    m = proj.shape[0]
    assert P_CQ % MEM_WIDTH == 0
    depth, nb, tokens, heads, hd = mem_k.shape
    per = m // nb
    parts = hd // LANES
    mem_k, mem_v = (t.reshape(depth, nb, tokens, heads, parts, LANES).transpose(0, 1, 2, 4, 3, 5)
                    .reshape(depth, nb, tokens * heads * parts, LANES) for t in (mem_k, mem_v))
    mems = max(tm // per, 1)
    kv_spec = pl.BlockSpec((1, mems, tokens * heads * parts, LANES), lambda i: (layer, i * tm // (per * mems), 0, 0))
    return _Task(
        "mem_attn", _mem_attn_kernel, m // tm, [proj, mem_k, mem_v],
        [pl.BlockSpec((tm, MEM_WIDTH), lambda i: (i, P_CQ // MEM_WIDTH)), kv_spec, kv_spec],
        [jax.ShapeDtypeStruct((m, MEM_WIDTH), out_dtype)], [pl.BlockSpec((tm, MEM_WIDTH), lambda i: (i, 0))])


def _merge_kernel(a_ref, b_ref, c_ref, ga_ref, gb_ref, gc_ref, wa_ref, wb_ref, wc_ref, o_ref):
    def branch(x_ref, g_ref, w_ref):
        y = _dot(x_ref[...].astype(BF16), w_ref[...])
        return y * (1.0 / (1.0 + jnp.exp(-g_ref[...])))

    o_ref[...] = (branch(a_ref, ga_ref, wa_ref) + branch(b_ref, gb_ref, wb_ref)
                  + branch(c_ref, gc_ref, wc_ref)).astype(o_ref.dtype)


def _merge(a, b, c, tail, w_a, w_b, w_c, tm, tn):
    m = a.shape[0]
    d = w_a.shape[1]
    gate0 = P_GATE // tn
    assert P_GATE % tn == 0 and d % tn == 0
    row = lambda x: pl.BlockSpec((tm, x.shape[1]), lambda i, j: (i, 0))
    gate = lambda br: pl.BlockSpec((tm, tn), lambda i, j, br=br: (i, gate0 + br * (d // tn) + j))
    wsp = lambda w: pl.BlockSpec((w.shape[0], tn), lambda i, j: (0, j))
    return pl.pallas_call(
        _merge_kernel, grid=(m // tm, d // tn),
        in_specs=[row(a), row(b), row(c), gate(0), gate(1), gate(2), wsp(w_a), wsp(w_b), wsp(w_c)],
        out_specs=pl.BlockSpec((tm, tn), lambda i, j: (i, j)),
        out_shape=jax.ShapeDtypeStruct((m, d), BF16),
        compiler_params=_params(2), name="merge")(a, b, c, tail, tail, tail, w_a, w_b, w_c)


def _rms(y, g):
    return y * lax.rsqrt(jnp.mean(y * y, axis=-1, keepdims=True) + EPS) * g


def _proj_norm_residual_kernel(y_ref, w_ref, x_ref, g_post_ref, g_next_ref, o_ref, h_ref):
    x1 = x_ref[...] + _rms(_dot(y_ref[...], w_ref[...]), g_post_ref[...])
    o_ref[...] = x1
    h_ref[...] = _rms(x1, g_next_ref[...]).astype(h_ref.dtype)


def _proj_norm_residual(y, w, x, g_post, g_next, tm):
    m, d = x.shape
    k = y.shape[1]
    row = lambda width: pl.BlockSpec((tm, width), lambda i: (i, 0))
    gsp = pl.BlockSpec((1, d), lambda i: (0, 0))
    return pl.pallas_call(
        _proj_norm_residual_kernel, grid=(m // tm,),
        in_specs=[row(k), pl.BlockSpec((k, d), lambda i: (0, 0)), row(d), gsp, gsp],
        out_specs=[row(d), row(d)],
        out_shape=[jax.ShapeDtypeStruct((m, d), F32), jax.ShapeDtypeStruct((m, d), BF16)],
        compiler_params=_params(1), name="proj_norm_residual")(y, w, x, g_post.reshape(1, d), g_next.reshape(1, d))


FFN_SPLIT = 4


def _ffn_kernel(h_ref, w1_ref, w2_ref, x_ref, g_ref, o_ref, a_ref):
    j = pl.program_id(1)
    last = pl.num_programs(1) - 1
    h = h_ref[...]
    piece = w1_ref.shape[1] // FFN_SPLIT
    for s in range(FFN_SPLIT):
        cols = slice(s * piece, (s + 1) * piece)
        a_ref[:, cols] = jnp.square(jnp.maximum(_dot(h, w1_ref[:, cols]), 0.0)).astype(BF16)
    y = _dot(a_ref[...], w2_ref[...])

    @pl.when(j == 0)
    def _():
        o_ref[...] = y

    @pl.when(jnp.logical_and(j > 0, j < last))
    def _():
        o_ref[...] += y

    @pl.when(j == last)
    def _():
        o_ref[...] = x_ref[...] + _rms(o_ref[...] + y, g_ref[...])


def _ffn(h, w1, w2, x, g, tm, tf):
    m, d = h.shape
    f = w1.shape[1]
    assert f // tf >= 2
    once = lambda shape: pl.BlockSpec(shape, lambda i, j: (i, 0), pipeline_mode=pl.Buffered(1))
    return pl.pallas_call(
        _ffn_kernel, grid=(m // tm, f // tf),
        in_specs=[pl.BlockSpec((tm, d), lambda i, j: (i, 0)), pl.BlockSpec((d, tf), lambda i, j: (0, j)),
                  pl.BlockSpec((tf, d), lambda i, j: (j, 0)), once((tm, d)), pl.BlockSpec((1, d), lambda i, j: (0, 0))],
        out_specs=once((tm, d)),
        out_shape=jax.ShapeDtypeStruct((m, d), F32),
        scratch_shapes=[pltpu.VMEM((tm, tf), BF16)],
        compiler_params=_params(2), name="ffn")(h, w1, w2, x, g.reshape(1, d))


def _row_tile(m, cap):
    t = min(m, cap)
    assert m % t == 0
    return t


def _layer(xp, xs, lw, layer, rel_bias, bias_prompt, mem_p, swa_caches, mem_s, conv_buf, s0):
    (w_in_t, w_conv, a_log, dt_bias, norm_delta, w_o_swa, w_o_delta, w_o_mem, w_out,
     norm_pre_mix, norm_post_mix, norm_pre_ffn, norm_post_ffn, w_ff1, w_ff2) = lw
    mp, ms, n_seq = xp.shape[0], xs.shape[0], s0.shape[0]
    seq = ms // n_seq
    tm_p = _row_tile(mp, 2048)
    host_steps = (mp // tm_p) * ((P_CQ + w_in_t.shape[0] - CQ_OFF) // PROJ_TILE)
    casts = [_cast_task(w, host_steps) for w in (w_o_swa, w_o_delta, w_o_mem, w_out, w_ff1, w_ff2)]
    proj_p, w_o_swa, w_o_delta, w_o_mem, w_out, w_ff1, w_ff2 = _in_proj(xp, norm_pre_mix, w_in_t, tm_p, casts)
    proj_s, = _in_proj(xs, norm_pre_mix, w_in_t, _row_tile(ms, 2048))
    chunk = DN_CHUNK if mp % DN_CHUNK == 0 else mp
    scan_steps = mp // chunk
    per = -(-n_seq // scan_steps)
    riding = n_seq % per == 0 and per <= RIDER_SEQS_MAX
    swa_s = _swa_sample_task(proj_s, swa_caches, layer, rel_bias, seq, per if riding else 1)
    mem_s_task = _mem_attn_task(proj_s, *mem_s, layer, seq * (per if riding else math.gcd(n_seq, MEM_PER_STEP)), F32)
    riders = [swa_s, mem_s_task] if riding else []
    b_p, state_p, *rode = _delta_scan(proj_p, w_conv, a_log, dt_bias, norm_delta, None, None, chunk, riders)
    a_s, c_s = rode if riding else (_run_task(swa_s)[0], _run_task(mem_s_task)[0])
    a_p = _swa_prompt(proj_p, bias_prompt)
    c_p = _run_task(_mem_attn_task(proj_p, *mem_p, 0, _row_tile(mp, 512), BF16))[0]
    b_s, state_s = _delta_scan(proj_s, w_conv, a_log, dt_bias, norm_delta, s0, conv_buf, seq)
    outs = []
    for x, proj, a, b, c in ((xp, proj_p, a_p, b_p, c_p), (xs, proj_s, a_s, b_s, c_s)):
        m = x.shape[0]
        tm = _row_tile(m, 1024)
        merged = _merge(a, b, c, proj, w_o_swa, w_o_delta, w_o_mem, tm, 512)
        x1, h2 = _proj_norm_residual(merged, w_out, x, norm_post_mix, norm_pre_ffn, _row_tile(m, 512))
        outs.append(_ffn(h2, w_ff1, w_ff2, x1, norm_post_ffn, tm, 1024))
    return outs[0], outs[1], proj_p, proj_s, state_p, state_s


def kernel(x_prompt, x_sample, cache_swa0_k, cache_swa0_v, cache_swa1_k, cache_swa1_v, cache_swa2_k, cache_swa2_v, state_delta, state_conv, cache_mem_k, cache_mem_v, mem_prompt, rel_bias, w_in, w_conv, A_log, dt_bias, norm_delta, norm_mem, w_mem_kv, w_o_swa, w_o_delta, w_o_mem, w_out, norm_pre_mix, norm_post_mix, norm_pre_ffn, norm_post_ffn, w_ff1, w_ff2):
    depth = w_in.shape[0]
    bp, tp, d = x_prompt.shape
    bs, ts, _ = x_sample.shape
    assert bp == 1 and ts > CONV_WIDTH - 1 and ts % 8 == 0
    sample_swa = (cache_swa0_k, cache_swa0_v, cache_swa1_k, cache_swa1_v, cache_swa2_k, cache_swa2_v)
    bias_prompt = _prompt_bias_tables(rel_bias)
    xp = x_prompt.reshape(bp * tp, d)
    xs = x_sample.reshape(bs * ts, d)
    new_p = [[] for _ in range(10)]
    new_s = [[] for _ in range(8)]
    hw = SWA_HPG * HEAD_DIM
    for l in range(depth):
        lw = (jnp.swapaxes(w_in, 1, 2)[l], w_conv[l], A_log[l], dt_bias[l], norm_delta[l], w_o_swa[l],
              w_o_delta[l], w_o_mem[l], w_out[l],
              norm_pre_mix[l], norm_post_mix[l], norm_pre_ffn[l], norm_post_ffn[l],
              w_ff1[l], w_ff2[l])
        mem = mem_prompt.reshape(-1, d)
        mkv = _matmul(_rmsnorm_cast(mem, norm_mem[l], _row_tile(mem.shape[0], 256)), w_mem_kv[l],
                      2 * MEM_WIDTH, _row_tile(mem.shape[0], 256), 512, "mem_kv")
        mk = mkv[:, :MEM_WIDTH].reshape(1, bp, -1, MEM_HEADS, MEM_HEAD_DIM)
        mv = mkv[:, MEM_WIDTH:].reshape(1, bp, -1, MEM_HEADS, MEM_HEAD_DIM)
        xp, xs, main_p, main_s, s_p, s_s = _layer(
            xp, xs, lw, l, rel_bias, bias_prompt, (mk, mv), sample_swa, (cache_mem_k, cache_mem_v),
            state_conv[l], state_delta[l])
        vals_p = []
        for g, (window, _) in enumerate(SWA_GROUPS):
            keep = min(window, tp)
            for sec in (1, 2):
                c0 = sec * SWA_WIDTH + g * hw
                vals_p.append(main_p[tp - keep:, c0:c0 + hw].reshape(bp, keep, SWA_HPG, HEAD_DIM))
        vals_p.append(s_p.reshape(bp, DN_HEADS, HEAD_DIM, HEAD_DIM))
        vals_p.append(main_p[tp - (CONV_WIDTH - 1):, B_OFF:B_OFF + 3 * DN_WIDTH].reshape(bp, CONV_WIDTH - 1, -1))
        vals_p += [mk[0], mv[0]]
        vals_s = []
        main_s3 = main_s.reshape(bs, ts, -1)
        for g in range(len(SWA_GROUPS)):
            for sec in (1, 2):
                c0 = sec * SWA_WIDTH + g * hw
                vals_s.append(main_s3[:, :, c0:c0 + hw].reshape(bs, ts, SWA_HPG, HEAD_DIM))
        vals_s.append(s_s)
        vals_s.append(main_s3[:, ts - (CONV_WIDTH - 1):, B_OFF:B_OFF + 3 * DN_WIDTH])
        for lst, val in zip(new_p, vals_p):
            lst.append(val)
        for lst, val in zip(new_s, vals_s):
            lst.append(val)
    outs_p = [jnp.stack(t) for t in new_p]
    outs_s = [jnp.stack(t) for t in new_s]
    return (xp.reshape(bp, tp, d), xs.reshape(bs, ts, d), *outs_p, *outs_s)
```

```python
import functools
import math

import numpy as np
import jax
import jax.numpy as jnp
from jax import lax
from jax.experimental import pallas as pl
from jax.experimental.pallas import tpu as pltpu

EPS = 1e-6
HEAD_DIM = 128
SWA_GROUPS = ((128, 1), (512, 4), (2048, 16))
SWA_SPAN = 128
SWA_HPG = 4
SWA_WIDTH = SWA_HPG * len(SWA_GROUPS) * HEAD_DIM
DN_HEADS = 12
DN_WIDTH = DN_HEADS * HEAD_DIM
CONV_WIDTH = 4
DN_CHUNK = 128
MEM_HEADS = 4
MEM_HEAD_DIM = 256
MEM_WIDTH = MEM_HEADS * MEM_HEAD_DIM
N_BUCKETS = 32
MAX_DISTANCE = 2048
N_BRANCHES = 3

A_OFF = 0
B_OFF = 3 * SWA_WIDTH
Z_OFF = B_OFF + 3 * DN_WIDTH
BA_OFF = Z_OFF + DN_WIDTH
CQ_OFF = BA_OFF + 2 * DN_HEADS
GATE_OFF = CQ_OFF + MEM_WIDTH
LANES = 128
SUBLANES = 8
PROJ_TILE = 512
P_BA = BA_OFF
P_CQ = P_BA + PROJ_TILE
P_GATE = P_CQ + MEM_WIDTH
SWA_ROWS = 2048
SWA_UNROLL = 8
NEG = -1e30
VMEM_LIMIT = 58 * 1024 * 1024

BF16 = jnp.bfloat16
F32 = jnp.float32


def _params(n_grid):
    return pltpu.CompilerParams(dimension_semantics=("arbitrary",) * n_grid, vmem_limit_bytes=VMEM_LIMIT)


def _dot(a, b):
    return jnp.dot(a, b, preferred_element_type=F32)


def _dot_nt(a, b):
    return lax.dot_general(a, b, (((1,), (1,)), ((), ())), preferred_element_type=F32)


def _dot_tn(a, b):
    return lax.dot_general(a, b, (((0,), (0,)), ((), ())), preferred_element_type=F32)


def _split2(x):
    hi = x.astype(BF16)
    lo = (x - hi.astype(F32)).astype(BF16)
    return hi, lo


def _dot3(a, b, dot=_dot):
    ah, al = _split2(a)
    bh, bl = _split2(b)
    return dot(ah, bh) + (dot(ah, bl) + dot(al, bh))


def _rmsnorm_cast_kernel(x_ref, g_ref, o_ref):
    x = x_ref[...]
    y = x * lax.rsqrt(jnp.mean(x * x, axis=-1, keepdims=True) + EPS)
    o_ref[...] = (y * g_ref[...]).astype(o_ref.dtype)


def _rmsnorm_cast(x, g, tm):
    m, d = x.shape
    return pl.pallas_call(
        _rmsnorm_cast_kernel, grid=(m // tm,),
        in_specs=[pl.BlockSpec((tm, d), lambda i: (i, 0)), pl.BlockSpec((1, d), lambda i: (0, 0))],
        out_specs=pl.BlockSpec((tm, d), lambda i: (i, 0)),
        out_shape=jax.ShapeDtypeStruct((m, d), BF16),
        compiler_params=_params(1), name="rmsnorm_cast")(x, g.reshape(1, d))


def _matmul_kernel(x_ref, w_ref, o_ref):
    o_ref[...] = _dot(x_ref[...], w_ref[...].astype(BF16)).astype(o_ref.dtype)


def _matmul(x, w, n, tm, tn, name):
    m, k = x.shape
    assert n % tn == 0 and m % tm == 0
    return pl.pallas_call(
        _matmul_kernel, grid=(m // tm, n // tn),
        in_specs=[pl.BlockSpec((tm, k), lambda i, j: (i, 0)), pl.BlockSpec((k, tn), lambda i, j: (0, j))],
        out_specs=pl.BlockSpec((tm, tn), lambda i, j: (i, j)),
        out_shape=jax.ShapeDtypeStruct((m, n), F32),
        compiler_params=_params(2), name=name)(x, w)


def _in_proj_kernel(x_ref, g_ref, w_ref, o_ref, h_ref):
    @pl.when(pl.program_id(1) == 0)
    def _():
        h_ref[...] = _rms(x_ref[...], g_ref[...]).astype(BF16)

    o_ref[...] = _dot_nt(h_ref[...], w_ref[...].astype(BF16))


def _in_proj(x, g, wt, tm, tn, riders=()):
    m, k = x.shape
    n = P_CQ + wt.shape[0] - CQ_OFF
    assert n % tn == 0 and m % tm == 0 and P_CQ % tn == 0 and CQ_OFF % SUBLANES == 0

    def w_rows(i, j):
        return (pl.multiple_of(jnp.where(j * tn < P_CQ, j * tn, j * tn - P_CQ + CQ_OFF), SUBLANES), 0)

    return _hosted_call(
        "in_proj", _in_proj_kernel, (m // tm, n // tn), [x, g.reshape(1, k), wt],
        [pl.BlockSpec((tm, k), lambda i, j: (i, 0), pipeline_mode=pl.Buffered(1)),
         pl.BlockSpec((1, k), lambda i, j: (0, 0)),
         pl.BlockSpec((pl.Element(tn), pl.Element(k)), w_rows)],
        [jax.ShapeDtypeStruct((m, n), F32)], [pl.BlockSpec((tm, tn), lambda i, j: (i, j))],
        [pltpu.VMEM((tm, k), BF16)], riders)


def _t5_bucket(dist):
    max_exact = N_BUCKETS // 2
    df = jnp.maximum(dist, 1).astype(F32)
    large = max_exact + (jnp.log(df / max_exact) / math.log(MAX_DISTANCE / max_exact)
                         * (N_BUCKETS - max_exact)).astype(jnp.int32)
    return jnp.where(dist < max_exact, dist, jnp.minimum(large, N_BUCKETS - 1))


def _group_bias(rel_bias, g):
    _, dil = SWA_GROUPS[g]
    dist = jnp.arange(SWA_SPAN + 1, dtype=jnp.int32) * dil
    onehot = _t5_bucket(dist)[None, :, None] == jnp.arange(N_BUCKETS, dtype=jnp.int32)[None, None, :]
    heads = jnp.transpose(rel_bias[:, g * SWA_HPG:(g + 1) * SWA_HPG].astype(F32))
    return jnp.sum(jnp.where(onehot, heads[:, None, :], 0.0), axis=-1)


def _prompt_bias_tables(rel_bias):
    span = SWA_SPAN
    first = (np.arange(2 * span) >= span)[None, None, :]
    out = []
    for g in range(len(SWA_GROUPS)):
        w = _group_bias(rel_bias, g)
        h = w.shape[0]
        p = 3 * span
        e = jnp.concatenate([jnp.full((h, span - 1), NEG, F32), w[:, ::-1], jnp.full((h, p - 2 * span), NEG, F32)], 1)
        skew = jnp.broadcast_to(e[:, None, :], (h, span, p)).reshape(h, span * p)[:, :span * (p - 1)]
        tab = skew.reshape(h, span, p - 1)[:, :, span - 1:3 * span - 1]
        out.append(jnp.stack([jnp.where(first, tab, NEG), tab], axis=1))
    return jnp.stack(out)


def _sample_bias_tables(rel_bias, g, n_res, n_new, cache_len):
    _, dil = SWA_GROUPS[g]
    w = _group_bias(rel_bias, g)
    h = w.shape[0]
    lm = cache_len // dil
    assert cache_len % dil == 0
    wpad = jnp.concatenate([w, jnp.full((h, lm + n_new), NEG, F32)], axis=1)
    res = np.arange(n_res)[None, None, :]
    rows = []
    for s in range(n_new):
        vec = wpad[:, s // dil + 1:lm + s // dil + 1][:, ::-1]
        rows.append(jnp.where(res == s % dil, vec[:, :, None], NEG).reshape(h, lm * n_res))
    s = np.arange(n_new)
    dist = s[:, None] - s[None, :]
    new = jnp.full((h, n_new, n_new), NEG, F32)
    for j in range(min((n_new - 1) // dil, SWA_SPAN) + 1):
        new = jnp.where((dist == j * dil)[None], w[:, j][:, None, None], new)
    return jnp.stack(rows, axis=1), new


def _swa_prompt_kernel(*refs, n_groups):
    ins = refs[:5 * n_groups]
    bias_ref = refs[5 * n_groups]
    o_ref = refs[5 * n_groups + 1]
    scr = refs[5 * n_groups + 2:]
    kext, vext, og, lg = scr[:n_groups], scr[n_groups:2 * n_groups], scr[2 * n_groups:3 * n_groups], scr[3 * n_groups:]
    n = pl.program_id(0)
    scale = HEAD_DIM ** -0.5
    span = SWA_SPAN
    for g in range(n_groups):
        _, dil = SWA_GROUPS[g]
        q_ref, kc_ref, vc_ref, kp_ref, vp_ref = ins[5 * g:5 * g + 5]
        blk = span * dil
        kext[g][0:blk, :] = kp_ref[...]
        kext[g][blk:, :] = kc_ref[...]
        vext[g][0:blk, :] = vp_ref[...]
        vext[g][blk:, :] = vc_ref[...]

        def body(it, carry, g=g, dil=dil, blk=blk, q_ref=q_ref):
            items = range(SWA_UNROLL)
            starts, sels = [], []
            for u in items:
                idx = it * SWA_UNROLL + u
                b = idx // dil
                starts.append(b * blk + (idx - b * dil))
                sels.append(jnp.where(jnp.logical_and(n == 0, b == 0), 0, 1))
            q = [q_ref[pl.ds(starts[u], span, stride=dil), :].astype(BF16) for u in items]
            kk = [kext[g][pl.ds(starts[u], 2 * span, stride=dil), :].astype(BF16) for u in items]
            s = [_dot_nt(q[u], kk[u]) * scale + bias_ref[g, 0, sels[u]] for u in items]
            m = [jnp.max(s[u], axis=-1, keepdims=True) for u in items]
            p = [jnp.exp(s[u] - m[u]) for u in items]
            den = [jnp.sum(p[u], axis=-1, keepdims=True) for u in items]
            vv = [vext[g][pl.ds(starts[u], 2 * span, stride=dil), :].astype(BF16) for u in items]
            o = [_dot(p[u].astype(BF16), vv[u]) / den[u] for u in items]
            for u in items:
                og[g][pl.ds(starts[u], span, stride=dil), :] = o[u]
                lg[g][pl.ds(starts[u], span, stride=dil), :] = jnp.broadcast_to(
                    m[u] + jnp.log(den[u]), (span, HEAD_DIM))
            return carry

        lax.fori_loop(0, SWA_ROWS // span // SWA_UNROLL, body, 0)
    lses = [lg[g][...] for g in range(n_groups)]
    mx = functools.reduce(jnp.maximum, lses)
    ws = [jnp.exp(l - mx) for l in lses]
    num = functools.reduce(lambda a, b: a + b, [w * og[g][...] for g, w in enumerate(ws)])
    o_ref[...] = (num / functools.reduce(lambda a, b: a + b, ws)).astype(o_ref.dtype)


def _swa_prompt(proj, bias_tables):
    t = proj.shape[0]
    n_groups = len(SWA_GROUPS)
    assert t % SWA_ROWS == 0
    in_specs, scratch_k, scratch_o = [], [], []
    for g, (_, dil) in enumerate(SWA_GROUPS):
        blk = SWA_SPAN * dil
        per = SWA_ROWS // blk
        assert SWA_ROWS % blk == 0
        qc, kc, vc = (sec * (SWA_WIDTH // HEAD_DIM) + g * SWA_HPG for sec in range(3))
        cur = lambda col: pl.BlockSpec((SWA_ROWS, HEAD_DIM), lambda n, h, col=col: (n, col + h))
        prev = lambda col, blk=blk, per=per: pl.BlockSpec(
            (blk, HEAD_DIM), lambda n, h, col=col, per=per: (jnp.maximum(n * per - 1, 0), col + h))
        in_specs += [cur(qc), cur(kc), cur(vc), prev(kc), prev(vc)]
        scratch_k.append(pltpu.VMEM((SWA_ROWS + blk, HEAD_DIM), F32))
        scratch_o.append(pltpu.VMEM((SWA_ROWS, HEAD_DIM), F32))
    in_specs.append(pl.BlockSpec((n_groups, 1, 2, SWA_SPAN, 2 * SWA_SPAN), lambda n, h: (0, h, 0, 0, 0)))
    return pl.pallas_call(
        functools.partial(_swa_prompt_kernel, n_groups=n_groups),
        grid=(t // SWA_ROWS, SWA_HPG),
        in_specs=in_specs,
        out_specs=pl.BlockSpec((SWA_ROWS, HEAD_DIM), lambda n, h: (n, h)),
        out_shape=jax.ShapeDtypeStruct((t, SWA_HPG * HEAD_DIM), BF16),
        scratch_shapes=scratch_k + scratch_k + scratch_o + scratch_o,
        compiler_params=_params(2), name="swa_prompt")(*([proj] * (5 * n_groups)), bias_tables)


def _swa_sample_kernel(*refs, n_groups):
    qkv_ref = refs[0]
    caches = refs[1:1 + 2 * n_groups]
    biases = refs[1 + 2 * n_groups:1 + 4 * n_groups]
    o_ref = refs[1 + 4 * n_groups]
    new_refs = refs[2 + 4 * n_groups:]
    scale = HEAD_DIM ** -0.5
    nseq = caches[0].shape[1]
    n_new = qkv_ref.shape[0] // nseq
    items = [(b, g, h) for b in range(nseq) for h in range(SWA_HPG) for g in range(n_groups)]
    idx = range(len(items))

    def new(sec, b, g, h):
        col = sec * SWA_WIDTH + (g * SWA_HPG + h) * HEAD_DIM
        return qkv_ref[b * n_new:(b + 1) * n_new, col:col + HEAD_DIM].astype(BF16)

    def cached(ref, b, h):
        nk = math.prod(ref.shape[2:-1]) // SWA_HPG
        if len(ref.shape) == 5:
            pick = (0, b, slice(None), pl.ds(h, ref.shape[3] // SWA_HPG, stride=SWA_HPG), slice(None))
        else:
            pick = (0, b, pl.ds(h, nk, stride=SWA_HPG), slice(None))
        return ref[pick].reshape(nk, HEAD_DIM).astype(BF16)

    q = [new(0, *it) for it in items]
    sc = [_dot_nt(q[i], cached(caches[2 * g], b, h)) * scale + biases[2 * g][h]
          for i, (b, g, h) in enumerate(items)]
    sn = [_dot_nt(q[i], new(1, b, g, h)) * scale + biases[2 * g + 1][h] for i, (b, g, h) in enumerate(items)]
    m = [jnp.maximum(jnp.max(sc[i], axis=-1, keepdims=True), jnp.max(sn[i], axis=-1, keepdims=True)) for i in idx]
    pc = [jnp.exp(sc[i] - m[i]) for i in idx]
    pn = [jnp.exp(sn[i] - m[i]) for i in idx]
    den = [jnp.sum(pc[i], axis=-1, keepdims=True) + jnp.sum(pn[i], axis=-1, keepdims=True) for i in idx]
    o = [(_dot(pc[i].astype(BF16), cached(caches[2 * g + 1], b, h)) + _dot(pn[i].astype(BF16), new(2, b, g, h)))
         / den[i] for i, (b, g, h) in enumerate(items)]
    lse = [m[i] + jnp.log(den[i]) for i in idx]
    for b in range(nseq):
        for h in range(SWA_HPG):
            mine = [i for i in idx if items[i][0] == b and items[i][2] == h]
            mx = functools.reduce(jnp.maximum, [lse[i] for i in mine])
            ws = [jnp.exp(lse[i] - mx) for i in mine]
            num = functools.reduce(lambda x, y: x + y, [w * o[i] for w, i in zip(ws, mine)])
            o_ref[b * n_new:(b + 1) * n_new, h * HEAD_DIM:(h + 1) * HEAD_DIM] = (
                num / functools.reduce(lambda x, y: x + y, ws))
    for b, g, h in items:
        for sec in (1, 2):
            col = sec * SWA_WIDTH + (g * SWA_HPG + h) * HEAD_DIM
            new_refs[2 * g + sec - 1][b, pl.ds(h, n_new, stride=SWA_HPG), :] = (
                qkv_ref[b * n_new:(b + 1) * n_new, col:col + HEAD_DIM])


class _Task:
    def __init__(self, name, body, steps, args, in_specs, out_shapes, out_specs):
        self.name, self.body, self.steps = name, body, steps
        self.args, self.in_specs, self.out_shapes, self.out_specs = args, in_specs, out_shapes, out_specs


def _run_task(task):
    return pl.pallas_call(
        task.body, grid=(task.steps,), in_specs=task.in_specs, out_specs=task.out_specs,
        out_shape=task.out_shapes, compiler_params=_params(1), name=task.name)(*task.args)


def _riding_specs(task, grid):
    assert math.prod(grid) >= task.steps

    def step_of(*idx):
        flat = functools.reduce(lambda acc, t: acc * t[1] + t[0], zip(idx[1:], grid[1:]), idx[0])
        return flat if math.prod(grid) == task.steps else jnp.minimum(flat, task.steps - 1)

    re = lambda spec: pl.BlockSpec(spec.block_shape, lambda *idx: spec.index_map(step_of(*idx)))
    return [re(s) for s in task.in_specs], [re(s) for s in task.out_specs]


BF16_ROWS = 16


def _cast_task(w, max_steps):
    rows, cols = w.shape
    steps = max(s for s in range(1, min(max_steps, rows // BF16_ROWS) + 1) if rows % (s * BF16_ROWS) == 0)
    slab = pl.BlockSpec((rows // steps, cols), lambda i: (i, 0))

    def body(x_ref, o_ref):
        o_ref[...] = x_ref[...].astype(BF16)

    return _Task("cast", body, steps, [w], [slab], [jax.ShapeDtypeStruct((rows, cols), BF16)], [slab])


def _swa_sample_task(proj, caches, layer, rel_bias, n_new, per):
    rows_total = proj.shape[0]
    nb = rows_total // n_new
    assert nb % per == 0
    n_groups = len(SWA_GROUPS)
    hw = SWA_HPG * HEAD_DIM
    cache_in, cache_specs, bias_in, bias_specs = [], [], [], []
    for g, (_, dil) in enumerate(SWA_GROUPS):
        depth, _, cache_len, heads, hd = caches[2 * g].shape
        if dil >= 2 * n_new and n_new % 8 == 0:
            n_res = n_new
            shape = (depth, nb, cache_len // dil, dil * heads, hd)
            spec = pl.BlockSpec((1, per, cache_len // dil, n_res * heads, hd), lambda b: (layer, b, 0, 0, 0))
        else:
            n_res = dil
            shape = (depth, nb, cache_len * heads, hd)
            spec = pl.BlockSpec((1, per, cache_len * heads, hd), lambda b: (layer, b, 0, 0))
        for c in caches[2 * g:2 * g + 2]:
            cache_in.append(c.reshape(shape))
            cache_specs.append(spec)
        for tab in _sample_bias_tables(rel_bias, g, n_res, n_new, cache_len):
            bias_in.append(tab)
            bias_specs.append(pl.BlockSpec(tab.shape, lambda b: (0, 0, 0)))
    return _Task(
        "swa_sample", functools.partial(_swa_sample_kernel, n_groups=n_groups), nb // per,
        [proj, *cache_in, *bias_in],
        [pl.BlockSpec((per * n_new, 3 * SWA_WIDTH), lambda b: (b, 0))] + cache_specs + bias_specs,
        [jax.ShapeDtypeStruct((rows_total, hw), F32)]
        + [jax.ShapeDtypeStruct((nb, n_new * SWA_HPG, HEAD_DIM), F32)] * (2 * n_groups),
        [pl.BlockSpec((per * n_new, hw), lambda b: (b, 0))]
        + [pl.BlockSpec((per, n_new * SWA_HPG, HEAD_DIM), lambda b: (b, 0, 0))] * (2 * n_groups))


def _silu(x):
    return x * (1.0 / (1.0 + jnp.exp(-x)))


def _softplus(x):
    return jnp.maximum(x, 0.0) + jnp.log(1.0 + jnp.exp(-jnp.abs(x)))


def _inv_dot(a, b):
    return _dot(a.astype(BF16), b.astype(BF16))


HALO = 8
DELTA_SEQS_PER_STEP = 8


def _delta_scan_kernel(*refs, chunk, nseq, carry):
    if carry:
        x_ref, z_ref, ba_ref, wc_ref, alog_ref, dt_ref, gain_ref, o_ref, s_out_ref, s_scr, halo_scr = refs
        s_in_ref = halo_ref = None
    else:
        (x_ref, z_ref, ba_ref, wc_ref, alog_ref, dt_ref, gain_ref, s_in_ref, halo_ref,
         o_ref, s_out_ref, hist_out_ref) = refs
        s_scr = halo_scr = None
        x3 = x_ref[...].reshape(nseq, chunk, x_ref.shape[1])
        for t in range(CONV_WIDTH - 1):
            hist_out_ref[t] = x3[:, chunk - (CONV_WIDTH - 1) + t, :]
    c = chunk
    rows = nseq * c
    seqs = range(nseq)
    part = lambda t, b: t[b * c:(b + 1) * c]
    if carry:
        @pl.when(pl.program_id(0) == 0)
        def _():
            s_scr[...] = jnp.zeros_like(s_scr)
            halo_scr[...] = jnp.zeros_like(halo_scr)

    x = x_ref[...]
    hist = halo_scr[...] if carry else halo_ref[...]
    xp = jnp.concatenate([t for b in seqs for t in (hist[b * HALO:(b + 1) * HALO], part(x, b))], axis=0)
    acc = x * wc_ref[CONV_WIDTH - 1:CONV_WIDTH, :]
    for t in range(1, CONV_WIDTH):
        rolled = pltpu.roll(xp, t, 0)
        shifted = jnp.concatenate([rolled[b * (HALO + c) + HALO:(b + 1) * (HALO + c)] for b in seqs], axis=0)
        acc = acc + shifted * wc_ref[CONV_WIDTH - 1 - t:CONV_WIDTH - t, :]
    if carry:
        halo_scr[...] = x[c - HALO:, :]
    qkv = _silu(acc)

    ba = ba_ref[...]
    beta_all = 1.0 / (1.0 + jnp.exp(-ba))
    g_all = -jnp.exp(alog_ref[...]) * _softplus(ba + dt_ref[...])
    row = lax.broadcasted_iota(jnp.int32, (rows, rows), 0)
    col = lax.broadcasted_iota(jnp.int32, (rows, rows), 1)
    same = (row // c) == (col // c)
    tri = (row >= col) & same
    strict = (row > col) & same
    tril = tri.astype(BF16)
    g1 = g_all.astype(BF16)
    r1 = g_all - g1.astype(F32)
    g2 = r1.astype(BF16)
    g3 = (r1 - g2.astype(F32)).astype(BF16)
    gcum_all = _dot(tril, g1) + (_dot(tril, g2) + _dot(tril, g3))
    gcum_t = gcum_all.T
    eye = (row == col).astype(F32)

    heads = range(DN_HEADS)
    hs = lambda sec, h: slice(sec * DN_WIDTH + h * HEAD_DIM, sec * DN_WIDTH + (h + 1) * HEAD_DIM)
    lane = lambda h: slice(DN_HEADS + h, DN_HEADS + h + 1)
    l2 = lambda t: t * lax.rsqrt(jnp.sum(t * t, axis=-1, keepdims=True) + EPS)
    q = [l2(qkv[:, hs(0, h)]) * (HEAD_DIM ** -0.5) for h in heads]
    k = [l2(qkv[:, hs(1, h)]) for h in heads]
    v = [qkv[:, hs(2, h)] for h in heads]
    bcol = [beta_all[:, h:h + 1] for h in heads]
    gc = [gcum_all[:, lane(h)] for h in heads]
    glast = [[gcum_all[(b + 1) * c - 1:(b + 1) * c, lane(h)] for b in seqs] for h in heads]
    decay = [jnp.exp(jnp.where(tri, gc[h] - gcum_t[lane(h), :], NEG)) for h in heads]
    kb = [k[h] * bcol[h] for h in heads]
    k16 = [k[h].astype(BF16) for h in heads]
    nmat = [jnp.where(strict, _dot_nt(kb[h].astype(BF16), k16[h]) * decay[h], 0.0) for h in heads]
    attn = [jnp.where(tri, _dot_nt(q[h].astype(BF16), k16[h]) * decay[h], 0.0) for h in heads]
    inv = [eye] * DN_HEADS
    blk = 1
    while blk < c:
        pair = (row // (2 * blk) == col // (2 * blk)) & (row % (2 * blk) >= blk) & (col % (2 * blk) < blk)
        low = [jnp.where(pair, nmat[h], 0.0) for h in heads]
        if blk == 1:
            inv = [eye - low[h] for h in heads]
        else:
            t = [_inv_dot(low[h], inv[h]) for h in heads]
            inv = [inv[h] - _inv_dot(inv[h], t[h]) for h in heads]
        blk *= 2
    eg = [jnp.exp(gc[h]) for h in heads]
    rhs = [jnp.concatenate([v[h] * bcol[h], kb[h] * eg[h]], axis=1) for h in heads]
    sol = [_inv_dot(inv[h], rhs[h]) for h in heads]
    qdec = [q[h] * eg[h] for h in heads]
    hb = [(h, b) for h in heads for b in seqs]
    s_prev = {(h, b): (s_scr[h] if carry else s_in_ref[b, h]) for h, b in hb}
    wq = {(h, b): jnp.concatenate([part(sol[h][:, HEAD_DIM:], b), part(qdec[h], b)], axis=0).astype(BF16)
          for h, b in hb}
    ws = {i: _dot(wq[i], s_prev[i].astype(BF16)) for i in hb}
    vn = {(h, b): part(sol[h][:, :HEAD_DIM], b) - ws[h, b][:c] for h, b in hb}
    v_new = [jnp.concatenate([vn[h, b] for b in seqs], axis=0).astype(BF16) for h in heads]
    o = [jnp.concatenate([ws[h, b][c:] for b in seqs], axis=0) + _dot(attn[h].astype(BF16), v_new[h])
         for h in heads]
    kdec = {(h, b): (part(k[h], b) * jnp.exp(glast[h][b] - part(gc[h], b))).astype(BF16) for h, b in hb}
    s_new = {(h, b): s_prev[h, b] * jnp.exp(glast[h][b]) + _dot_tn(kdec[h, b], vn[h, b].astype(BF16))
             for h, b in hb}
    for h in heads:
        if carry:
            s_scr[h] = s_new[h, 0]
            s_out_ref[h] = s_new[h, 0]
        else:
            for b in seqs:
                s_out_ref[b, h] = s_new[h, b]
        y = o[h] * lax.rsqrt(jnp.mean(o[h] * o[h], axis=-1, keepdims=True) + EPS) * gain_ref[...]
        o_ref[:, hs(0, h)] = (y * _silu(z_ref[:, hs(0, h)])).astype(o_ref.dtype)


def _delta_scan(proj, w_conv, a_log, dt_bias, norm_delta, s0, conv_buf, chunk, riders=()):
    m = proj.shape[0]
    carry = s0 is None
    nseq = 1 if carry else math.gcd(m // chunk, DELTA_SEQS_PER_STEP)
    rows = nseq * chunk
    n = m // rows
    width = 3 * DN_WIDTH
    assert chunk >= HALO and B_OFF % width == 0
    pad = lambda v_, off: jnp.zeros((1, LANES), F32).at[0, off:off + DN_HEADS].set(v_.astype(F32))
    consts = [w_conv, pad(a_log, DN_HEADS), pad(dt_bias, DN_HEADS), norm_delta.reshape(1, HEAD_DIM).astype(F32)]
    const_specs = [pl.BlockSpec((CONV_WIDTH, width), lambda i: (0, 0))] + [pl.BlockSpec((1, LANES), lambda i: (0, 0))] * 3
    in_specs = [pl.BlockSpec((rows, width), lambda i: (i, B_OFF // width)),
                pl.BlockSpec((rows, DN_WIDTH), lambda i: (i, Z_OFF // DN_WIDTH)),
                pl.BlockSpec((rows, LANES), lambda i: (i, P_BA // LANES))] + const_specs
    args = [proj, proj, proj] + consts
    state = (DN_HEADS, HEAD_DIM, HEAD_DIM)
    if carry:
        out_dtype = BF16
        s_shape, s_spec = state, pl.BlockSpec(state, lambda i: (0, 0, 0))
        scratch = [pltpu.VMEM(state, F32), pltpu.VMEM((HALO, width), F32)]
    else:
        out_dtype = F32
        halo = jnp.pad(conv_buf, ((0, 0), (HALO - (CONV_WIDTH - 1), 0), (0, 0))).reshape(-1, width)
        in_specs += [pl.BlockSpec((nseq,) + state, lambda i: (i, 0, 0, 0)),
                     pl.BlockSpec((nseq * HALO, width), lambda i: (i, 0))]
        args += [s0, halo]
        s_shape, s_spec = (m // chunk,) + state, pl.BlockSpec((nseq,) + state, lambda i: (i, 0, 0, 0))
        scratch = []
    out_specs = [pl.BlockSpec((rows, DN_WIDTH), lambda i: (i, 0)), s_spec]
    out_shape = [jax.ShapeDtypeStruct((m, DN_WIDTH), out_dtype), jax.ShapeDtypeStruct(s_shape, F32)]
    if not carry:
        out_specs.append(pl.BlockSpec((CONV_WIDTH - 1, nseq, width), lambda i: (0, i, 0)))
        out_shape.append(jax.ShapeDtypeStruct((CONV_WIDTH - 1, m // chunk, width), F32))
    host = functools.partial(_delta_scan_kernel, chunk=chunk, nseq=nseq, carry=carry)
    return _hosted_call("delta_scan", host, (n,), args, in_specs, out_shape, out_specs, scratch, riders)


def _hosted_kernel(*refs, host, n_host_in, n_host_out, riders):
    n_in = n_host_in + sum(len(t.in_specs) for t in riders)
    n_out = n_host_out + sum(len(t.out_specs) for t in riders)
    ins, outs, scratch = refs[:n_in], refs[n_in:n_in + n_out], refs[n_in + n_out:]
    host(*ins[:n_host_in], *outs[:n_host_out], *scratch)
    pi, po = n_host_in, n_host_out
    for t in riders:
        ni, no = len(t.in_specs), len(t.out_specs)
        t.body(*ins[pi:pi + ni], *outs[po:po + no])
        pi, po = pi + ni, po + no


def _hosted_call(name, host, grid, args, in_specs, out_shape, out_specs, scratch, riders):
    body = functools.partial(_hosted_kernel, host=host, n_host_in=len(in_specs), n_host_out=len(out_specs),
                             riders=tuple(riders))
    args, in_specs, out_shape, out_specs = list(args), list(in_specs), list(out_shape), list(out_specs)
    for t in riders:
        r_in, r_out = _riding_specs(t, grid)
        args, in_specs = args + list(t.args), in_specs + r_in
        out_shape, out_specs = out_shape + list(t.out_shapes), out_specs + r_out
    return pl.pallas_call(
        body, grid=grid, in_specs=in_specs, out_specs=out_specs, out_shape=out_shape, scratch_shapes=scratch,
        compiler_params=_params(len(grid)), name=name)(*args)


def _mem_attn_kernel(q_ref, k_ref, v_ref, o_ref):
    scale = MEM_HEAD_DIM ** -0.5
    parts = MEM_HEAD_DIM // LANES
    period = MEM_HEADS * parts
    tokens = k_ref.shape[2] // period
    pick = lambda b, c, h: (0, b, pl.ds(c * MEM_HEADS + h, tokens, stride=period), slice(None))
    nb = k_ref.shape[1]
    rows = q_ref.shape[0] // nb
    items = [(b, h) for b in range(nb) for h in range(MEM_HEADS)]
    idx = range(len(items))
    q = [[q_ref[b * rows:(b + 1) * rows, h * MEM_HEAD_DIM + c * LANES:h * MEM_HEAD_DIM + (c + 1) * LANES]
          .astype(BF16) for c in range(parts)] for b, h in items]
    s = [functools.reduce(lambda x, y: x + y, [_dot_nt(q[i][c], k_ref[pick(b, c, h)].astype(BF16))
                                               for c in range(parts)]) * scale for i, (b, h) in enumerate(items)]
    p = [jnp.exp(s[i] - jnp.max(s[i], axis=-1, keepdims=True)) for i in idx]
    den = [jnp.sum(p[i], axis=-1, keepdims=True) for i in idx]
    for i, (b, h) in enumerate(items):
        for c in range(parts):
            col = h * MEM_HEAD_DIM + c * LANES
            o_ref[b * rows:(b + 1) * rows, col:col + LANES] = (
                _dot(p[i].astype(BF16), v_ref[pick(b, c, h)].astype(BF16)) / den[i]).astype(o_ref.dtype)


MEM_PER_STEP = 4
RIDER_SEQS_MAX = 2


def _mem_attn_task(proj, mem_k, mem_v, layer, tm, out_dtype):
    m = proj.shape[0]
    assert P_CQ % MEM_WIDTH == 0
    depth, nb, tokens, heads, hd = mem_k.shape
    per = m // nb
    parts = hd // LANES
    mem_k, mem_v = (t.reshape(depth, nb, tokens, heads, parts, LANES).transpose(0, 1, 2, 4, 3, 5)
                    .reshape(depth, nb, tokens * heads * parts, LANES) for t in (mem_k, mem_v))
    mems = max(tm // per, 1)
    kv_spec = pl.BlockSpec((1, mems, tokens * heads * parts, LANES), lambda i: (layer, i * tm // (per * mems), 0, 0))
    return _Task(
        "mem_attn", _mem_attn_kernel, m // tm, [proj, mem_k, mem_v],
        [pl.BlockSpec((tm, MEM_WIDTH), lambda i: (i, P_CQ // MEM_WIDTH)), kv_spec, kv_spec],
        [jax.ShapeDtypeStruct((m, MEM_WIDTH), out_dtype)], [pl.BlockSpec((tm, MEM_WIDTH), lambda i: (i, 0))])


def _merge_kernel(a_ref, b_ref, c_ref, ga_ref, gb_ref, gc_ref, wa_ref, wb_ref, wc_ref, o_ref):
    def branch(x_ref, g_ref, w_ref):
        y = _dot(x_ref[...].astype(BF16), w_ref[...])
        return y * (1.0 / (1.0 + jnp.exp(-g_ref[...])))

    o_ref[...] = (branch(a_ref, ga_ref, wa_ref) + branch(b_ref, gb_ref, wb_ref)
                  + branch(c_ref, gc_ref, wc_ref)).astype(o_ref.dtype)


def _merge(a, b, c, tail, w_a, w_b, w_c, tm, tn):
    m = a.shape[0]
    d = w_a.shape[1]
    gate0 = P_GATE // tn
    assert P_GATE % tn == 0 and d % tn == 0
    row = lambda x: pl.BlockSpec((tm, x.shape[1]), lambda i, j: (i, 0))
    gate = lambda br: pl.BlockSpec((tm, tn), lambda i, j, br=br: (i, gate0 + br * (d // tn) + j))
    wsp = lambda w: pl.BlockSpec((w.shape[0], tn), lambda i, j: (0, j))
    return pl.pallas_call(
        _merge_kernel, grid=(m // tm, d // tn),
        in_specs=[row(a), row(b), row(c), gate(0), gate(1), gate(2), wsp(w_a), wsp(w_b), wsp(w_c)],
        out_specs=pl.BlockSpec((tm, tn), lambda i, j: (i, j)),
        out_shape=jax.ShapeDtypeStruct((m, d), BF16),
        compiler_params=_params(2), name="merge")(a, b, c, tail, tail, tail, w_a, w_b, w_c)


def _rms(y, g):
    return y * lax.rsqrt(jnp.mean(y * y, axis=-1, keepdims=True) + EPS) * g


def _proj_norm_residual_kernel(y_ref, w_ref, x_ref, g_post_ref, g_next_ref, o_ref, h_ref):
    x1 = x_ref[...] + _rms(_dot(y_ref[...], w_ref[...]), g_post_ref[...])
    o_ref[...] = x1
    h_ref[...] = _rms(x1, g_next_ref[...]).astype(h_ref.dtype)


def _proj_norm_residual(y, w, x, g_post, g_next, tm):
    m, d = x.shape
    k = y.shape[1]
    row = lambda width: pl.BlockSpec((tm, width), lambda i: (i, 0))
    gsp = pl.BlockSpec((1, d), lambda i: (0, 0))
    return pl.pallas_call(
        _proj_norm_residual_kernel, grid=(m // tm,),
        in_specs=[row(k), pl.BlockSpec((k, d), lambda i: (0, 0)), row(d), gsp, gsp],
        out_specs=[row(d), row(d)],
        out_shape=[jax.ShapeDtypeStruct((m, d), F32), jax.ShapeDtypeStruct((m, d), BF16)],
        compiler_params=_params(1), name="proj_norm_residual")(y, w, x, g_post.reshape(1, d), g_next.reshape(1, d))


FFN_SPLIT = 4


def _ffn_kernel(h_ref, w1_ref, w2_ref, x_ref, g_ref, o_ref, a_ref):
    j = pl.program_id(1)
    last = pl.num_programs(1) - 1
    h = h_ref[...]
    piece = w1_ref.shape[1] // FFN_SPLIT
    for s in range(FFN_SPLIT):
        cols = slice(s * piece, (s + 1) * piece)
        a_ref[:, cols] = jnp.square(jnp.maximum(_dot(h, w1_ref[:, cols]), 0.0)).astype(BF16)
    y = _dot(a_ref[...], w2_ref[...])

    @pl.when(j == 0)
    def _():
        o_ref[...] = y

    @pl.when(jnp.logical_and(j > 0, j < last))
    def _():
        o_ref[...] += y

    @pl.when(j == last)
    def _():
        o_ref[...] = x_ref[...] + _rms(o_ref[...] + y, g_ref[...])


def _ffn(h, w1, w2, x, g, tm, tf):
    m, d = h.shape
    f = w1.shape[1]
    assert f // tf >= 2
    once = lambda shape: pl.BlockSpec(shape, lambda i, j: (i, 0), pipeline_mode=pl.Buffered(1))
    return pl.pallas_call(
        _ffn_kernel, grid=(m // tm, f // tf),
        in_specs=[pl.BlockSpec((tm, d), lambda i, j: (i, 0)), pl.BlockSpec((d, tf), lambda i, j: (0, j)),
                  pl.BlockSpec((tf, d), lambda i, j: (j, 0)), once((tm, d)), pl.BlockSpec((1, d), lambda i, j: (0, 0))],
        out_specs=once((tm, d)),
        out_shape=jax.ShapeDtypeStruct((m, d), F32),
        scratch_shapes=[pltpu.VMEM((tm, tf), BF16)],
        compiler_params=_params(2), name="ffn")(h, w1, w2, x, g.reshape(1, d))


def _row_tile(m, cap):
    t = min(m, cap)
    assert m % t == 0
    return t


def _layer(xp, xs, lw, layer, rel_bias, bias_prompt, mem_p, swa_caches, mem_s, conv_buf, s0):
    (w_in_t, w_conv, a_log, dt_bias, norm_delta, w_o_swa, w_o_delta, w_o_mem, w_out,
     norm_pre_mix, norm_post_mix, norm_pre_ffn, norm_post_ffn, w_ff1, w_ff2) = lw
    mp, ms, n_seq = xp.shape[0], xs.shape[0], s0.shape[0]
    seq = ms // n_seq
    proj_s, = _in_proj(xs, norm_pre_mix, w_in_t, _row_tile(ms, 1024), 2 * PROJ_TILE)
    tm_p = _row_tile(mp, 2048)
    host_steps = (mp // tm_p) * ((P_CQ + w_in_t.shape[0] - CQ_OFF) // PROJ_TILE)
    casts = [_cast_task(w, host_steps) for w in (w_o_swa, w_o_delta, w_o_mem, w_out, w_ff1, w_ff2)]
    proj_p, w_o_swa, w_o_delta, w_o_mem, w_out, w_ff1, w_ff2 = _in_proj(
        xp, norm_pre_mix, w_in_t, tm_p, PROJ_TILE, casts)
    chunk = DN_CHUNK if mp % DN_CHUNK == 0 else mp
    scan_steps = mp // chunk
    per = -(-n_seq // scan_steps)
    riding = n_seq % per == 0 and per <= RIDER_SEQS_MAX
    swa_s = _swa_sample_task(proj_s, swa_caches, layer, rel_bias, seq, per if riding else 1)
    mem_s_task = _mem_attn_task(proj_s, *mem_s, layer, seq * (per if riding else math.gcd(n_seq, MEM_PER_STEP)), F32)
    riders = [swa_s, mem_s_task] if riding else []
    b_p, state_p, *rode = _delta_scan(proj_p, w_conv, a_log, dt_bias, norm_delta, None, None, chunk, riders)
    a_s, *kv_new_s, c_s = rode if riding else (*_run_task(swa_s), *_run_task(mem_s_task))
    a_p = _swa_prompt(proj_p, bias_prompt)
    c_p = _run_task(_mem_attn_task(proj_p, *mem_p, 0, _row_tile(mp, 512), BF16))[0]
    b_s, state_s, hist_s = _delta_scan(proj_s, w_conv, a_log, dt_bias, norm_delta, s0, conv_buf, seq)
    outs = []
    for x, proj, a, b, c in ((xp, proj_p, a_p, b_p, c_p), (xs, proj_s, a_s, b_s, c_s)):
        m = x.shape[0]
        tm = _row_tile(m, 1024)
        merged = _merge(a, b, c, proj, w_o_swa, w_o_delta, w_o_mem, tm, 512)
        x1, h2 = _proj_norm_residual(merged, w_out, x, norm_post_mix, norm_pre_ffn, _row_tile(m, 512))
        outs.append(_ffn(h2, w_ff1, w_ff2, x1, norm_post_ffn, tm, 1024))
    new_s = [t.reshape(n_seq, seq, SWA_HPG, HEAD_DIM) for t in kv_new_s] + [state_s, jnp.transpose(hist_s, (1, 0, 2))]
    return outs[0], outs[1], proj_p, state_p, new_s


def kernel(x_prompt, x_sample, cache_swa0_k, cache_swa0_v, cache_swa1_k, cache_swa1_v, cache_swa2_k, cache_swa2_v, state_delta, state_conv, cache_mem_k, cache_mem_v, mem_prompt, rel_bias, w_in, w_conv, A_log, dt_bias, norm_delta, norm_mem, w_mem_kv, w_o_swa, w_o_delta, w_o_mem, w_out, norm_pre_mix, norm_post_mix, norm_pre_ffn, norm_post_ffn, w_ff1, w_ff2):
    depth = w_in.shape[0]
    bp, tp, d = x_prompt.shape
    bs, ts, _ = x_sample.shape
    assert bp == 1 and ts > CONV_WIDTH - 1 and ts % 8 == 0
    sample_swa = (cache_swa0_k, cache_swa0_v, cache_swa1_k, cache_swa1_v, cache_swa2_k, cache_swa2_v)
    bias_prompt = _prompt_bias_tables(rel_bias)
    xp = x_prompt.reshape(bp * tp, d)
    xs = x_sample.reshape(bs * ts, d)
    new_p = [[] for _ in range(10)]
    new_s = [[] for _ in range(8)]
    hw = SWA_HPG * HEAD_DIM
    for l in range(depth):
        lw = (jnp.swapaxes(w_in, 1, 2)[l], w_conv[l], A_log[l], dt_bias[l], norm_delta[l], w_o_swa[l],
              w_o_delta[l], w_o_mem[l], w_out[l],
              norm_pre_mix[l], norm_post_mix[l], norm_pre_ffn[l], norm_post_ffn[l],
              w_ff1[l], w_ff2[l])
        mem = mem_prompt.reshape(-1, d)
        mkv = _matmul(_rmsnorm_cast(mem, norm_mem[l], _row_tile(mem.shape[0], 256)), w_mem_kv[l],
                      2 * MEM_WIDTH, _row_tile(mem.shape[0], 256), 512, "mem_kv")
        mk = mkv[:, :MEM_WIDTH].reshape(1, bp, -1, MEM_HEADS, MEM_HEAD_DIM)
        mv = mkv[:, MEM_WIDTH:].reshape(1, bp, -1, MEM_HEADS, MEM_HEAD_DIM)
        xp, xs, main_p, s_p, vals_s = _layer(
            xp, xs, lw, l, rel_bias, bias_prompt, (mk, mv), sample_swa, (cache_mem_k, cache_mem_v),
            state_conv[l], state_delta[l])
        vals_p = []
        for g, (window, _) in enumerate(SWA_GROUPS):
            keep = min(window, tp)
            for sec in (1, 2):
                c0 = sec * SWA_WIDTH + g * hw
                vals_p.append(main_p[tp - keep:, c0:c0 + hw].reshape(bp, keep, SWA_HPG, HEAD_DIM))
        vals_p.append(s_p.reshape(bp, DN_HEADS, HEAD_DIM, HEAD_DIM))
        vals_p.append(main_p[tp - (CONV_WIDTH - 1):, B_OFF:B_OFF + 3 * DN_WIDTH].reshape(bp, CONV_WIDTH - 1, -1))
        vals_p += [mk[0], mv[0]]
        for lst, val in zip(new_p, vals_p):
            lst.append(val)
        for lst, val in zip(new_s, vals_s):
            lst.append(val)
    outs_p = [jnp.stack(t) for t in new_p]
    outs_s = [jnp.stack(t) for t in new_s]
    return (xp.reshape(bp, tp, d), xs.reshape(bs, ts, d), *outs_p, *outs_s)
```

```python
import functools
import math

import numpy as np
import jax
import jax.numpy as jnp
from jax import lax
from jax.experimental import pallas as pl
from jax.experimental.pallas import tpu as pltpu

EPS = 1e-6
HEAD_DIM = 128
SWA_GROUPS = ((128, 1), (512, 4), (2048, 16))
SWA_SPAN = 128
SWA_HPG = 4
SWA_WIDTH = SWA_HPG * len(SWA_GROUPS) * HEAD_DIM
DN_HEADS = 12
DN_WIDTH = DN_HEADS * HEAD_DIM
CONV_WIDTH = 4
DN_CHUNK = 128
MEM_HEADS = 4
MEM_HEAD_DIM = 256
MEM_WIDTH = MEM_HEADS * MEM_HEAD_DIM
N_BUCKETS = 32
MAX_DISTANCE = 2048
N_BRANCHES = 3

A_OFF = 0
B_OFF = 3 * SWA_WIDTH
Z_OFF = B_OFF + 3 * DN_WIDTH
BA_OFF = Z_OFF + DN_WIDTH
CQ_OFF = BA_OFF + 2 * DN_HEADS
GATE_OFF = CQ_OFF + MEM_WIDTH
LANES = 128
SUBLANES = 8
PROJ_TILE = 512
P_BA = BA_OFF
P_CQ = P_BA + PROJ_TILE
P_GATE = P_CQ + MEM_WIDTH
SWA_ROWS = 2048
SWA_UNROLL = 8
NEG = -1e30
VMEM_LIMIT = 61 * 1024 * 1024

BF16 = jnp.bfloat16
F32 = jnp.float32


def _params(n_grid):
    return pltpu.CompilerParams(dimension_semantics=("arbitrary",) * n_grid, vmem_limit_bytes=VMEM_LIMIT)


def _dot(a, b):
    return jnp.dot(a, b, preferred_element_type=F32)


def _dot_nt(a, b):
    return lax.dot_general(a, b, (((1,), (1,)), ((), ())), preferred_element_type=F32)


def _dot_tn(a, b):
    return lax.dot_general(a, b, (((0,), (0,)), ((), ())), preferred_element_type=F32)


def _split2(x):
    hi = x.astype(BF16)
    lo = (x - hi.astype(F32)).astype(BF16)
    return hi, lo


def _dot3(a, b, dot=_dot):
    ah, al = _split2(a)
    bh, bl = _split2(b)
    return dot(ah, bh) + (dot(ah, bl) + dot(al, bh))


def _rmsnorm_cast_kernel(x_ref, g_ref, o_ref):
    x = x_ref[...]
    y = x * lax.rsqrt(jnp.mean(x * x, axis=-1, keepdims=True) + EPS)
    o_ref[...] = (y * g_ref[...]).astype(o_ref.dtype)


def _rmsnorm_cast(x, g, tm):
    m, d = x.shape
    return pl.pallas_call(
        _rmsnorm_cast_kernel, grid=(m // tm,),
        in_specs=[pl.BlockSpec((tm, d), lambda i: (i, 0)), pl.BlockSpec((1, d), lambda i: (0, 0))],
        out_specs=pl.BlockSpec((tm, d), lambda i: (i, 0)),
        out_shape=jax.ShapeDtypeStruct((m, d), BF16),
        compiler_params=_params(1), name="rmsnorm_cast")(x, g.reshape(1, d))


def _matmul_kernel(x_ref, w_ref, o_ref):
    o_ref[...] = _dot(x_ref[...], w_ref[...].astype(BF16)).astype(o_ref.dtype)


def _matmul(x, w, n, tm, tn, name):
    m, k = x.shape
    assert n % tn == 0 and m % tm == 0
    return pl.pallas_call(
        _matmul_kernel, grid=(m // tm, n // tn),
        in_specs=[pl.BlockSpec((tm, k), lambda i, j: (i, 0)), pl.BlockSpec((k, tn), lambda i, j: (0, j))],
        out_specs=pl.BlockSpec((tm, tn), lambda i, j: (i, j)),
        out_shape=jax.ShapeDtypeStruct((m, n), F32),
        compiler_params=_params(2), name=name)(x, w)


def _in_proj_kernel(x_ref, g_ref, w_ref, o_ref, h_ref):
    @pl.when(pl.program_id(1) == 0)
    def _():
        h_ref[...] = _rms(x_ref[...], g_ref[...]).astype(BF16)

    o_ref[...] = _dot_nt(h_ref[...], w_ref[...].astype(BF16))


def _in_proj_normed_kernel(h_ref, w_ref, o_ref):
    o_ref[...] = _dot_nt(h_ref[...], w_ref[...].astype(BF16))


def _in_proj(x, g, wt, tm, tn, riders=()):
    m, k = x.shape
    n = P_CQ + wt.shape[0] - CQ_OFF
    assert n % tn == 0 and m % tm == 0 and P_CQ % tn == 0 and CQ_OFF % SUBLANES == 0

    def w_rows(i, j):
        return (pl.multiple_of(jnp.where(j * tn < P_CQ, j * tn, j * tn - P_CQ + CQ_OFF), SUBLANES), 0)

    x_spec = pl.BlockSpec((tm, k), lambda i, j: (i, 0), pipeline_mode=pl.Buffered(1))
    w_spec = pl.BlockSpec((pl.Element(tn), pl.Element(k)), w_rows)
    out = ([jax.ShapeDtypeStruct((m, n), F32)], [pl.BlockSpec((tm, tn), lambda i, j: (i, j))])
    if g is None:
        return _hosted_call("in_proj", _in_proj_normed_kernel, (m // tm, n // tn), [x, wt], [x_spec, w_spec],
                            *out, [], riders)
    return _hosted_call(
        "in_proj", _in_proj_kernel, (m // tm, n // tn), [x, g.reshape(1, k), wt],
        [x_spec, pl.BlockSpec((1, k), lambda i, j: (0, 0)), w_spec], *out, [pltpu.VMEM((tm, k), BF16)], riders)


def _rmsnorm_task(x, g, max_steps):
    rows, k = x.shape
    steps = max(s for s in range(1, min(max_steps, rows // BF16_ROWS) + 1) if rows % (s * BF16_ROWS) == 0)
    slab = pl.BlockSpec((rows // steps, k), lambda i: (i, 0))

    def body(x_ref, g_ref, o_ref):
        o_ref[...] = _rms(x_ref[...], g_ref[...]).astype(BF16)

    return _Task("rmsnorm", body, steps, [x, g.reshape(1, k)], [slab, pl.BlockSpec((1, k), lambda i: (0, 0))],
                 [jax.ShapeDtypeStruct((rows, k), BF16)], [slab])


def _t5_bucket(dist):
    max_exact = N_BUCKETS // 2
    df = jnp.maximum(dist, 1).astype(F32)
    large = max_exact + (jnp.log(df / max_exact) / math.log(MAX_DISTANCE / max_exact)
                         * (N_BUCKETS - max_exact)).astype(jnp.int32)
    return jnp.where(dist < max_exact, dist, jnp.minimum(large, N_BUCKETS - 1))


def _group_bias(rel_bias, g):
    _, dil = SWA_GROUPS[g]
    dist = jnp.arange(SWA_SPAN + 1, dtype=jnp.int32) * dil
    onehot = _t5_bucket(dist)[None, :, None] == jnp.arange(N_BUCKETS, dtype=jnp.int32)[None, None, :]
    heads = jnp.transpose(rel_bias[:, g * SWA_HPG:(g + 1) * SWA_HPG].astype(F32))
    return jnp.sum(jnp.where(onehot, heads[:, None, :], 0.0), axis=-1)


def _prompt_bias_tables(rel_bias):
    span = SWA_SPAN
    first = (np.arange(2 * span) >= span)[None, None, :]
    out = []
    for g in range(len(SWA_GROUPS)):
        w = _group_bias(rel_bias, g)
        h = w.shape[0]
        p = 3 * span
        e = jnp.concatenate([jnp.full((h, span - 1), NEG, F32), w[:, ::-1], jnp.full((h, p - 2 * span), NEG, F32)], 1)
        skew = jnp.broadcast_to(e[:, None, :], (h, span, p)).reshape(h, span * p)[:, :span * (p - 1)]
        tab = skew.reshape(h, span, p - 1)[:, :, span - 1:3 * span - 1]
        out.append(jnp.stack([jnp.where(first, tab, NEG), tab], axis=1))
    return jnp.stack(out)


def _sample_bias_tables(rel_bias, g, n_res, n_new, cache_len):
    _, dil = SWA_GROUPS[g]
    w = _group_bias(rel_bias, g)
    h = w.shape[0]
    lm = cache_len // dil
    assert cache_len % dil == 0
    wpad = jnp.concatenate([w, jnp.full((h, lm + n_new), NEG, F32)], axis=1)
    res = np.arange(n_res)[None, None, :]
    rows = []
    for s in range(n_new):
        vec = wpad[:, s // dil + 1:lm + s // dil + 1][:, ::-1]
        rows.append(jnp.where(res == s % dil, vec[:, :, None], NEG).reshape(h, lm * n_res))
    s = np.arange(n_new)
    dist = s[:, None] - s[None, :]
    new = jnp.full((h, n_new, n_new), NEG, F32)
    for j in range(min((n_new - 1) // dil, SWA_SPAN) + 1):
        new = jnp.where((dist == j * dil)[None], w[:, j][:, None, None], new)
    return jnp.stack(rows, axis=1), new


def _swa_prompt_kernel(*refs, n_groups):
    ins = refs[:5 * n_groups]
    bias_ref = refs[5 * n_groups]
    o_ref = refs[5 * n_groups + 1]
    scr = refs[5 * n_groups + 2:]
    kext, vext, og, lg = scr[:n_groups], scr[n_groups:2 * n_groups], scr[2 * n_groups:3 * n_groups], scr[3 * n_groups:]
    n = pl.program_id(0)
    scale = HEAD_DIM ** -0.5
    span = SWA_SPAN
    for g in range(n_groups):
        _, dil = SWA_GROUPS[g]
        q_ref, kc_ref, vc_ref, kp_ref, vp_ref = ins[5 * g:5 * g + 5]
        blk = span * dil
        kext[g][0:blk, :] = kp_ref[...]
        kext[g][blk:, :] = kc_ref[...]
        vext[g][0:blk, :] = vp_ref[...]
        vext[g][blk:, :] = vc_ref[...]

        def body(it, carry, g=g, dil=dil, blk=blk, q_ref=q_ref):
            items = range(SWA_UNROLL)
            starts, sels = [], []
            for u in items:
                idx = it * SWA_UNROLL + u
                b = idx // dil
                starts.append(b * blk + (idx - b * dil))
                sels.append(jnp.where(jnp.logical_and(n == 0, b == 0), 0, 1))
            q = [q_ref[pl.ds(starts[u], span, stride=dil), :].astype(BF16) for u in items]
            kk = [kext[g][pl.ds(starts[u], 2 * span, stride=dil), :].astype(BF16) for u in items]
            s = [_dot_nt(q[u], kk[u]) * scale + bias_ref[g, 0, sels[u]] for u in items]
            m = [jnp.max(s[u], axis=-1, keepdims=True) for u in items]
            p = [jnp.exp(s[u] - m[u]) for u in items]
            den = [jnp.sum(p[u], axis=-1, keepdims=True) for u in items]
            vv = [vext[g][pl.ds(starts[u], 2 * span, stride=dil), :].astype(BF16) for u in items]
            o = [_dot(p[u].astype(BF16), vv[u]) / den[u] for u in items]
            for u in items:
                og[g][pl.ds(starts[u], span, stride=dil), :] = o[u]
                lg[g][pl.ds(starts[u], span, stride=dil), :] = jnp.broadcast_to(
                    m[u] + jnp.log(den[u]), (span, HEAD_DIM))
            return carry

        lax.fori_loop(0, SWA_ROWS // span // SWA_UNROLL, body, 0)
    lses = [lg[g][...] for g in range(n_groups)]
    mx = functools.reduce(jnp.maximum, lses)
    ws = [jnp.exp(l - mx) for l in lses]
    num = functools.reduce(lambda a, b: a + b, [w * og[g][...] for g, w in enumerate(ws)])
    o_ref[...] = (num / functools.reduce(lambda a, b: a + b, ws)).astype(o_ref.dtype)


def _swa_prompt(proj, bias_tables):
    t = proj.shape[0]
    n_groups = len(SWA_GROUPS)
    assert t % SWA_ROWS == 0
    in_specs, scratch_k, scratch_o = [], [], []
    for g, (_, dil) in enumerate(SWA_GROUPS):
        blk = SWA_SPAN * dil
        per = SWA_ROWS // blk
        assert SWA_ROWS % blk == 0
        qc, kc, vc = (sec * (SWA_WIDTH // HEAD_DIM) + g * SWA_HPG for sec in range(3))
        cur = lambda col: pl.BlockSpec((SWA_ROWS, HEAD_DIM), lambda n, h, col=col: (n, col + h))
        prev = lambda col, blk=blk, per=per: pl.BlockSpec(
            (blk, HEAD_DIM), lambda n, h, col=col, per=per: (jnp.maximum(n * per - 1, 0), col + h))
        in_specs += [cur(qc), cur(kc), cur(vc), prev(kc), prev(vc)]
        scratch_k.append(pltpu.VMEM((SWA_ROWS + blk, HEAD_DIM), F32))
        scratch_o.append(pltpu.VMEM((SWA_ROWS, HEAD_DIM), F32))
    in_specs.append(pl.BlockSpec((n_groups, 1, 2, SWA_SPAN, 2 * SWA_SPAN), lambda n, h: (0, h, 0, 0, 0)))
    return pl.pallas_call(
        functools.partial(_swa_prompt_kernel, n_groups=n_groups),
        grid=(t // SWA_ROWS, SWA_HPG),
        in_specs=in_specs,
        out_specs=pl.BlockSpec((SWA_ROWS, HEAD_DIM), lambda n, h: (n, h)),
        out_shape=jax.ShapeDtypeStruct((t, SWA_HPG * HEAD_DIM), BF16),
        scratch_shapes=scratch_k + scratch_k + scratch_o + scratch_o,
        compiler_params=_params(2), name="swa_prompt")(*([proj] * (5 * n_groups)), bias_tables)


def _swa_sample_kernel(*refs, n_groups):
    qkv_ref = refs[0]
    caches = refs[1:1 + 2 * n_groups]
    biases = refs[1 + 2 * n_groups:1 + 4 * n_groups]
    o_ref = refs[1 + 4 * n_groups]
    new_refs = refs[2 + 4 * n_groups:]
    scale = HEAD_DIM ** -0.5
    nseq = caches[0].shape[1]
    n_new = qkv_ref.shape[0] // nseq
    items = [(b, g, h) for b in range(nseq) for h in range(SWA_HPG) for g in range(n_groups)]
    idx = range(len(items))

    def new(sec, b, g, h):
        col = sec * SWA_WIDTH + (g * SWA_HPG + h) * HEAD_DIM
        return qkv_ref[b * n_new:(b + 1) * n_new, col:col + HEAD_DIM].astype(BF16)

    def cached(ref, b, h):
        nk = math.prod(ref.shape[2:-1]) // SWA_HPG
        if len(ref.shape) == 5:
            pick = (0, b, slice(None), pl.ds(h, ref.shape[3] // SWA_HPG, stride=SWA_HPG), slice(None))
        else:
            pick = (0, b, pl.ds(h, nk, stride=SWA_HPG), slice(None))
        return ref[pick].reshape(nk, HEAD_DIM).astype(BF16)

    q = [new(0, *it) for it in items]
    sc = [_dot_nt(q[i], cached(caches[2 * g], b, h)) * scale + biases[2 * g][h]
          for i, (b, g, h) in enumerate(items)]
    sn = [_dot_nt(q[i], new(1, b, g, h)) * scale + biases[2 * g + 1][h] for i, (b, g, h) in enumerate(items)]
    m = [jnp.maximum(jnp.max(sc[i], axis=-1, keepdims=True), jnp.max(sn[i], axis=-1, keepdims=True)) for i in idx]
    pc = [jnp.exp(sc[i] - m[i]) for i in idx]
    pn = [jnp.exp(sn[i] - m[i]) for i in idx]
    den = [jnp.sum(pc[i], axis=-1, keepdims=True) + jnp.sum(pn[i], axis=-1, keepdims=True) for i in idx]
    o = [(_dot(pc[i].astype(BF16), cached(caches[2 * g + 1], b, h)) + _dot(pn[i].astype(BF16), new(2, b, g, h)))
         / den[i] for i, (b, g, h) in enumerate(items)]
    lse = [m[i] + jnp.log(den[i]) for i in idx]
    for b in range(nseq):
        for h in range(SWA_HPG):
            mine = [i for i in idx if items[i][0] == b and items[i][2] == h]
            mx = functools.reduce(jnp.maximum, [lse[i] for i in mine])
            ws = [jnp.exp(lse[i] - mx) for i in mine]
            num = functools.reduce(lambda x, y: x + y, [w * o[i] for w, i in zip(ws, mine)])
            o_ref[b * n_new:(b + 1) * n_new, h * HEAD_DIM:(h + 1) * HEAD_DIM] = (
                num / functools.reduce(lambda x, y: x + y, ws))
    for b, g, h in items:
        for sec in (1, 2):
            col = sec * SWA_WIDTH + (g * SWA_HPG + h) * HEAD_DIM
            new_refs[2 * g + sec - 1][b, pl.ds(h, n_new, stride=SWA_HPG), :] = (
                qkv_ref[b * n_new:(b + 1) * n_new, col:col + HEAD_DIM])


class _Task:
    def __init__(self, name, body, steps, args, in_specs, out_shapes, out_specs):
        self.name, self.body, self.steps = name, body, steps
        self.args, self.in_specs, self.out_shapes, self.out_specs = args, in_specs, out_shapes, out_specs


def _run_task(task):
    return pl.pallas_call(
        task.body, grid=(task.steps,), in_specs=task.in_specs, out_specs=task.out_specs,
        out_shape=task.out_shapes, compiler_params=_params(1), name=task.name)(*task.args)


def _riding_specs(task, grid):
    assert math.prod(grid) >= task.steps

    def step_of(*idx):
        flat = functools.reduce(lambda acc, t: acc * t[1] + t[0], zip(idx[1:], grid[1:]), idx[0])
        return flat if math.prod(grid) == task.steps else jnp.minimum(flat, task.steps - 1)

    re = lambda spec: pl.BlockSpec(spec.block_shape, lambda *idx: spec.index_map(step_of(*idx)))
    return [re(s) for s in task.in_specs], [re(s) for s in task.out_specs]


BF16_ROWS = 16


def _cast_task(w, max_steps):
    rows, cols = w.shape
    steps = max(s for s in range(1, min(max_steps, rows // BF16_ROWS) + 1) if rows % (s * BF16_ROWS) == 0)
    slab = pl.BlockSpec((rows // steps, cols), lambda i: (i, 0))

    def body(x_ref, o_ref):
        o_ref[...] = x_ref[...].astype(BF16)

    return _Task("cast", body, steps, [w], [slab], [jax.ShapeDtypeStruct((rows, cols), BF16)], [slab])


def _swa_sample_task(proj, caches, layer, rel_bias, n_new, per):
    rows_total = proj.shape[0]
    nb = rows_total // n_new
    assert nb % per == 0
    n_groups = len(SWA_GROUPS)
    hw = SWA_HPG * HEAD_DIM
    cache_in, cache_specs, bias_in, bias_specs = [], [], [], []
    for g, (_, dil) in enumerate(SWA_GROUPS):
        depth, _, cache_len, heads, hd = caches[2 * g].shape
        if dil >= 2 * n_new and n_new % 8 == 0:
            n_res = n_new
            shape = (depth, nb, cache_len // dil, dil * heads, hd)
            spec = pl.BlockSpec((1, per, cache_len // dil, n_res * heads, hd), lambda b: (layer, b, 0, 0, 0))
        else:
            n_res = dil
            shape = (depth, nb, cache_len * heads, hd)
            spec = pl.BlockSpec((1, per, cache_len * heads, hd), lambda b: (layer, b, 0, 0))
        for c in caches[2 * g:2 * g + 2]:
            cache_in.append(c.reshape(shape))
            cache_specs.append(spec)
        for tab in _sample_bias_tables(rel_bias, g, n_res, n_new, cache_len):
            bias_in.append(tab)
            bias_specs.append(pl.BlockSpec(tab.shape, lambda b: (0, 0, 0)))
    return _Task(
        "swa_sample", functools.partial(_swa_sample_kernel, n_groups=n_groups), nb // per,
        [proj, *cache_in, *bias_in],
        [pl.BlockSpec((per * n_new, 3 * SWA_WIDTH), lambda b: (b, 0))] + cache_specs + bias_specs,
        [jax.ShapeDtypeStruct((rows_total, hw), F32)]
        + [jax.ShapeDtypeStruct((nb, n_new * SWA_HPG, HEAD_DIM), F32)] * (2 * n_groups),
        [pl.BlockSpec((per * n_new, hw), lambda b: (b, 0))]
        + [pl.BlockSpec((per, n_new * SWA_HPG, HEAD_DIM), lambda b: (b, 0, 0))] * (2 * n_groups))


def _silu(x):
    return x * (1.0 / (1.0 + jnp.exp(-x)))


def _softplus(x):
    return jnp.maximum(x, 0.0) + jnp.log(1.0 + jnp.exp(-jnp.abs(x)))


def _inv_dot(a, b):
    return _dot(a.astype(BF16), b.astype(BF16))


HALO = 8
DELTA_SEQS_PER_STEP = 8


def _delta_scan_kernel(*refs, chunk, nseq, carry):
    if carry:
        x_ref, z_ref, ba_ref, wc_ref, alog_ref, dt_ref, gain_ref, o_ref, s_out_ref, s_scr, halo_scr = refs
        s_in_ref = halo_ref = None
    else:
        (x_ref, z_ref, ba_ref, wc_ref, alog_ref, dt_ref, gain_ref, s_in_ref, halo_ref,
         o_ref, s_out_ref, hist_out_ref) = refs
        s_scr = halo_scr = None
        x3 = x_ref[...].reshape(nseq, chunk, x_ref.shape[1])
        for t in range(CONV_WIDTH - 1):
            hist_out_ref[t] = x3[:, chunk - (CONV_WIDTH - 1) + t, :]
    c = chunk
    rows = nseq * c
    seqs = range(nseq)
    part = lambda t, b: t[b * c:(b + 1) * c]
    if carry:
        @pl.when(pl.program_id(0) == 0)
        def _():
            s_scr[...] = jnp.zeros_like(s_scr)
            halo_scr[...] = jnp.zeros_like(halo_scr)

    x = x_ref[...]
    hist = halo_scr[...] if carry else halo_ref[...]
    xp = jnp.concatenate([t for b in seqs for t in (hist[b * HALO:(b + 1) * HALO], part(x, b))], axis=0)
    acc = x * wc_ref[CONV_WIDTH - 1:CONV_WIDTH, :]
    for t in range(1, CONV_WIDTH):
        rolled = pltpu.roll(xp, t, 0)
        shifted = jnp.concatenate([rolled[b * (HALO + c) + HALO:(b + 1) * (HALO + c)] for b in seqs], axis=0)
        acc = acc + shifted * wc_ref[CONV_WIDTH - 1 - t:CONV_WIDTH - t, :]
    if carry:
        halo_scr[...] = x[c - HALO:, :]
    qkv = _silu(acc)

    ba = ba_ref[...]
    beta_all = 1.0 / (1.0 + jnp.exp(-ba))
    g_all = -jnp.exp(alog_ref[...]) * _softplus(ba + dt_ref[...])
    row = lax.broadcasted_iota(jnp.int32, (rows, rows), 0)
    col = lax.broadcasted_iota(jnp.int32, (rows, rows), 1)
    same = (row // c) == (col // c)
    tri = (row >= col) & same
    strict = (row > col) & same
    tril = tri.astype(BF16)
    g1 = g_all.astype(BF16)
    r1 = g_all - g1.astype(F32)
    g2 = r1.astype(BF16)
    g3 = (r1 - g2.astype(F32)).astype(BF16)
    gcum_all = _dot(tril, g1) + (_dot(tril, g2) + _dot(tril, g3))
    gcum_t = gcum_all.T
    eye = (row == col).astype(F32)

    heads = range(DN_HEADS)
    hs = lambda sec, h: slice(sec * DN_WIDTH + h * HEAD_DIM, sec * DN_WIDTH + (h + 1) * HEAD_DIM)
    lane = lambda h: slice(DN_HEADS + h, DN_HEADS + h + 1)
    l2 = lambda t: t * lax.rsqrt(jnp.sum(t * t, axis=-1, keepdims=True) + EPS)
    q = [l2(qkv[:, hs(0, h)]) * (HEAD_DIM ** -0.5) for h in heads]
    k = [l2(qkv[:, hs(1, h)]) for h in heads]
    v = [qkv[:, hs(2, h)] for h in heads]
    bcol = [beta_all[:, h:h + 1] for h in heads]
    gc = [gcum_all[:, lane(h)] for h in heads]
    glast = [[gcum_all[(b + 1) * c - 1:(b + 1) * c, lane(h)] for b in seqs] for h in heads]
    decay = [jnp.exp(jnp.where(tri, gc[h] - gcum_t[lane(h), :], NEG)) for h in heads]
    kb = [k[h] * bcol[h] for h in heads]
    k16 = [k[h].astype(BF16) for h in heads]
    nmat = [jnp.where(strict, _dot_nt(kb[h].astype(BF16), k16[h]) * decay[h], 0.0) for h in heads]
    attn = [jnp.where(tri, _dot_nt(q[h].astype(BF16), k16[h]) * decay[h], 0.0) for h in heads]
    inv = [eye] * DN_HEADS
    blk = 1
    while blk < c:
        pair = (row // (2 * blk) == col // (2 * blk)) & (row % (2 * blk) >= blk) & (col % (2 * blk) < blk)
        low = [jnp.where(pair, nmat[h], 0.0) for h in heads]
        if blk == 1:
            inv = [eye - low[h] for h in heads]
        else:
            t = [_inv_dot(low[h], inv[h]) for h in heads]
            inv = [inv[h] - _inv_dot(inv[h], t[h]) for h in heads]
        blk *= 2
    eg = [jnp.exp(gc[h]) for h in heads]
    rhs = [jnp.concatenate([v[h] * bcol[h], kb[h] * eg[h]], axis=1) for h in heads]
    sol = [_inv_dot(inv[h], rhs[h]) for h in heads]
    qdec = [q[h] * eg[h] for h in heads]
    hb = [(h, b) for h in heads for b in seqs]
    s_prev = {(h, b): (s_scr[h] if carry else s_in_ref[b, h]) for h, b in hb}
    wq = {(h, b): jnp.concatenate([part(sol[h][:, HEAD_DIM:], b), part(qdec[h], b)], axis=0).astype(BF16)
          for h, b in hb}
    ws = {i: _dot(wq[i], s_prev[i].astype(BF16)) for i in hb}
    vn = {(h, b): part(sol[h][:, :HEAD_DIM], b) - ws[h, b][:c] for h, b in hb}
    v_new = [jnp.concatenate([vn[h, b] for b in seqs], axis=0).astype(BF16) for h in heads]
    o = [jnp.concatenate([ws[h, b][c:] for b in seqs], axis=0) + _dot(attn[h].astype(BF16), v_new[h])
         for h in heads]
    kdec = {(h, b): (part(k[h], b) * jnp.exp(glast[h][b] - part(gc[h], b))).astype(BF16) for h, b in hb}
    s_new = {(h, b): s_prev[h, b] * jnp.exp(glast[h][b]) + _dot_tn(kdec[h, b], vn[h, b].astype(BF16))
             for h, b in hb}
    for h in heads:
        if carry:
            s_scr[h] = s_new[h, 0]
            s_out_ref[h] = s_new[h, 0]
        else:
            for b in seqs:
                s_out_ref[b, h] = s_new[h, b]
        y = o[h] * lax.rsqrt(jnp.mean(o[h] * o[h], axis=-1, keepdims=True) + EPS) * gain_ref[...]
        o_ref[:, hs(0, h)] = (y * _silu(z_ref[:, hs(0, h)])).astype(o_ref.dtype)


def _delta_scan(proj, w_conv, a_log, dt_bias, norm_delta, s0, conv_buf, chunk, riders=()):
    m = proj.shape[0]
    carry = s0 is None
    nseq = 1 if carry else math.gcd(m // chunk, DELTA_SEQS_PER_STEP)
    rows = nseq * chunk
    n = m // rows
    width = 3 * DN_WIDTH
    assert chunk >= HALO and B_OFF % width == 0
    pad = lambda v_, off: jnp.zeros((1, LANES), F32).at[0, off:off + DN_HEADS].set(v_.astype(F32))
    consts = [w_conv, pad(a_log, DN_HEADS), pad(dt_bias, DN_HEADS), norm_delta.reshape(1, HEAD_DIM).astype(F32)]
    const_specs = [pl.BlockSpec((CONV_WIDTH, width), lambda i: (0, 0))] + [pl.BlockSpec((1, LANES), lambda i: (0, 0))] * 3
    in_specs = [pl.BlockSpec((rows, width), lambda i: (i, B_OFF // width)),
                pl.BlockSpec((rows, DN_WIDTH), lambda i: (i, Z_OFF // DN_WIDTH)),
                pl.BlockSpec((rows, LANES), lambda i: (i, P_BA // LANES))] + const_specs
    args = [proj, proj, proj] + consts
    state = (DN_HEADS, HEAD_DIM, HEAD_DIM)
    if carry:
        out_dtype = BF16
        s_shape, s_spec = state, pl.BlockSpec(state, lambda i: (0, 0, 0))
        scratch = [pltpu.VMEM(state, F32), pltpu.VMEM((HALO, width), F32)]
    else:
        out_dtype = F32
        halo = jnp.pad(conv_buf, ((0, 0), (HALO - (CONV_WIDTH - 1), 0), (0, 0))).reshape(-1, width)
        in_specs += [pl.BlockSpec((nseq,) + state, lambda i: (i, 0, 0, 0)),
                     pl.BlockSpec((nseq * HALO, width), lambda i: (i, 0))]
        args += [s0, halo]
        s_shape, s_spec = (m // chunk,) + state, pl.BlockSpec((nseq,) + state, lambda i: (i, 0, 0, 0))
        scratch = []
    out_specs = [pl.BlockSpec((rows, DN_WIDTH), lambda i: (i, 0)), s_spec]
    out_shape = [jax.ShapeDtypeStruct((m, DN_WIDTH), out_dtype), jax.ShapeDtypeStruct(s_shape, F32)]
    if not carry:
        out_specs.append(pl.BlockSpec((CONV_WIDTH - 1, nseq, width), lambda i: (0, i, 0)))
        out_shape.append(jax.ShapeDtypeStruct((CONV_WIDTH - 1, m // chunk, width), F32))
    host = functools.partial(_delta_scan_kernel, chunk=chunk, nseq=nseq, carry=carry)
    return _hosted_call("delta_scan", host, (n,), args, in_specs, out_shape, out_specs, scratch, riders)


def _hosted_kernel(*refs, host, n_host_in, n_host_out, riders):
    n_in = n_host_in + sum(len(t.in_specs) for t in riders)
    n_out = n_host_out + sum(len(t.out_specs) for t in riders)
    ins, outs, scratch = refs[:n_in], refs[n_in:n_in + n_out], refs[n_in + n_out:]
    host(*ins[:n_host_in], *outs[:n_host_out], *scratch)
    pi, po = n_host_in, n_host_out
    for t in riders:
        ni, no = len(t.in_specs), len(t.out_specs)
        t.body(*ins[pi:pi + ni], *outs[po:po + no])
        pi, po = pi + ni, po + no


def _hosted_call(name, host, grid, args, in_specs, out_shape, out_specs, scratch, riders):
    body = functools.partial(_hosted_kernel, host=host, n_host_in=len(in_specs), n_host_out=len(out_specs),
                             riders=tuple(riders))
    args, in_specs, out_shape, out_specs = list(args), list(in_specs), list(out_shape), list(out_specs)
    for t in riders:
        r_in, r_out = _riding_specs(t, grid)
        args, in_specs = args + list(t.args), in_specs + r_in
        out_shape, out_specs = out_shape + list(t.out_shapes), out_specs + r_out
    return pl.pallas_call(
        body, grid=grid, in_specs=in_specs, out_specs=out_specs, out_shape=out_shape, scratch_shapes=scratch,
        compiler_params=_params(len(grid)), name=name)(*args)


def _mem_attn_kernel(q_ref, k_ref, v_ref, o_ref):
    scale = MEM_HEAD_DIM ** -0.5
    parts = MEM_HEAD_DIM // LANES
    period = MEM_HEADS * parts
    tokens = k_ref.shape[2] // period
    pick = lambda b, c, h: (0, b, pl.ds(c * MEM_HEADS + h, tokens, stride=period), slice(None))
    nb = k_ref.shape[1]
    rows = q_ref.shape[0] // nb
    items = [(b, h) for b in range(nb) for h in range(MEM_HEADS)]
    idx = range(len(items))
    q = [[q_ref[b * rows:(b + 1) * rows, h * MEM_HEAD_DIM + c * LANES:h * MEM_HEAD_DIM + (c + 1) * LANES]
          .astype(BF16) for c in range(parts)] for b, h in items]
    s = [functools.reduce(lambda x, y: x + y, [_dot_nt(q[i][c], k_ref[pick(b, c, h)].astype(BF16))
                                               for c in range(parts)]) * scale for i, (b, h) in enumerate(items)]
    p = [jnp.exp(s[i] - jnp.max(s[i], axis=-1, keepdims=True)) for i in idx]
    den = [jnp.sum(p[i], axis=-1, keepdims=True) for i in idx]
    for i, (b, h) in enumerate(items):
        for c in range(parts):
            col = h * MEM_HEAD_DIM + c * LANES
            o_ref[b * rows:(b + 1) * rows, col:col + LANES] = (
                _dot(p[i].astype(BF16), v_ref[pick(b, c, h)].astype(BF16)) / den[i]).astype(o_ref.dtype)


MEM_PER_STEP = 4
RIDER_SEQS_MAX = 2


def _mem_attn_task(proj, mem_k, mem_v, layer, tm, out_dtype):
    m = proj.shape[0]
    assert P_CQ % MEM_WIDTH == 0
    depth, nb, tokens, heads, hd = mem_k.shape
    per = m // nb
    parts = hd // LANES
    mem_k, mem_v = (t.reshape(depth, nb, tokens, heads, parts, LANES).transpose(0, 1, 2, 4, 3, 5)
                    .reshape(depth, nb, tokens * heads * parts, LANES) for t in (mem_k, mem_v))
    mems = max(tm // per, 1)
    kv_spec = pl.BlockSpec((1, mems, tokens * heads * parts, LANES), lambda i: (layer, i * tm // (per * mems), 0, 0))
    return _Task(
        "mem_attn", _mem_attn_kernel, m // tm, [proj, mem_k, mem_v],
        [pl.BlockSpec((tm, MEM_WIDTH), lambda i: (i, P_CQ // MEM_WIDTH)), kv_spec, kv_spec],
        [jax.ShapeDtypeStruct((m, MEM_WIDTH), out_dtype)], [pl.BlockSpec((tm, MEM_WIDTH), lambda i: (i, 0))])


def _merge_kernel(a_ref, b_ref, c_ref, ga_ref, gb_ref, gc_ref, wa_ref, wb_ref, wc_ref, o_ref):
    def branch(x_ref, g_ref, w_ref):
        y = _dot(x_ref[...].astype(BF16), w_ref[...])
        return y * (1.0 / (1.0 + jnp.exp(-g_ref[...])))

    o_ref[...] = (branch(a_ref, ga_ref, wa_ref) + branch(b_ref, gb_ref, wb_ref)
                  + branch(c_ref, gc_ref, wc_ref)).astype(o_ref.dtype)


def _merge(a, b, c, tail, w_a, w_b, w_c, tm, tn):
    m = a.shape[0]
    d = w_a.shape[1]
    gate0 = P_GATE // tn
    assert P_GATE % tn == 0 and d % tn == 0
    row = lambda x: pl.BlockSpec((tm, x.shape[1]), lambda i, j: (i, 0))
    gate = lambda br: pl.BlockSpec((tm, tn), lambda i, j, br=br: (i, gate0 + br * (d // tn) + j))
    wsp = lambda w: pl.BlockSpec((w.shape[0], tn), lambda i, j: (0, j))
    return pl.pallas_call(
        _merge_kernel, grid=(m // tm, d // tn),
        in_specs=[row(a), row(b), row(c), gate(0), gate(1), gate(2), wsp(w_a), wsp(w_b), wsp(w_c)],
        out_specs=pl.BlockSpec((tm, tn), lambda i, j: (i, j)),
        out_shape=jax.ShapeDtypeStruct((m, d), BF16),
        compiler_params=_params(2), name="merge")(a, b, c, tail, tail, tail, w_a, w_b, w_c)


def _rms(y, g):
    return y * lax.rsqrt(jnp.mean(y * y, axis=-1, keepdims=True) + EPS) * g


def _proj_norm_residual_kernel(y_ref, w_ref, x_ref, g_post_ref, g_next_ref, o_ref, h_ref):
    x1 = x_ref[...] + _rms(_dot(y_ref[...], w_ref[...]), g_post_ref[...])
    o_ref[...] = x1
    h_ref[...] = _rms(x1, g_next_ref[...]).astype(h_ref.dtype)


def _proj_norm_residual(y, w, x, g_post, g_next, tm):
    m, d = x.shape
    k = y.shape[1]
    row = lambda width: pl.BlockSpec((tm, width), lambda i: (i, 0))
    gsp = pl.BlockSpec((1, d), lambda i: (0, 0))
    return pl.pallas_call(
        _proj_norm_residual_kernel, grid=(m // tm,),
        in_specs=[row(k), pl.BlockSpec((k, d), lambda i: (0, 0)), row(d), gsp, gsp],
        out_specs=[row(d), row(d)],
        out_shape=[jax.ShapeDtypeStruct((m, d), F32), jax.ShapeDtypeStruct((m, d), BF16)],
        compiler_params=_params(1), name="proj_norm_residual")(y, w, x, g_post.reshape(1, d), g_next.reshape(1, d))


FFN_SPLIT = 4


def _ffn_kernel(h_ref, w1_ref, w2_ref, x_ref, g_ref, o_ref, a_ref):
    j = pl.program_id(1)
    last = pl.num_programs(1) - 1
    h = h_ref[...]
    piece = w1_ref.shape[1] // FFN_SPLIT
    for s in range(FFN_SPLIT):
        cols = slice(s * piece, (s + 1) * piece)
        a_ref[:, cols] = jnp.square(jnp.maximum(_dot(h, w1_ref[:, cols]), 0.0)).astype(BF16)
    y = _dot(a_ref[...], w2_ref[...])

    @pl.when(j == 0)
    def _():
        o_ref[...] = y

    @pl.when(jnp.logical_and(j > 0, j < last))
    def _():
        o_ref[...] += y

    @pl.when(j == last)
    def _():
        o_ref[...] = x_ref[...] + _rms(o_ref[...] + y, g_ref[...])


def _ffn(h, w1, w2, x, g, tm, tf):
    m, d = h.shape
    f = w1.shape[1]
    assert f // tf >= 2
    once = lambda shape: pl.BlockSpec(shape, lambda i, j: (i, 0), pipeline_mode=pl.Buffered(1))
    return pl.pallas_call(
        _ffn_kernel, grid=(m // tm, f // tf),
        in_specs=[pl.BlockSpec((tm, d), lambda i, j: (i, 0)), pl.BlockSpec((d, tf), lambda i, j: (0, j)),
                  pl.BlockSpec((tf, d), lambda i, j: (j, 0)), once((tm, d)), pl.BlockSpec((1, d), lambda i, j: (0, 0))],
        out_specs=once((tm, d)),
        out_shape=jax.ShapeDtypeStruct((m, d), F32),
        scratch_shapes=[pltpu.VMEM((tm, tf), BF16)],
        compiler_params=_params(2), name="ffn")(h, w1, w2, x, g.reshape(1, d))


def _row_tile(m, cap):
    t = min(m, cap)
    assert m % t == 0
    return t


def _layer(xp, xs, lw, layer, rel_bias, bias_prompt, mem_p, swa_caches, mem_s, conv_buf, s0):
    (w_in_t, w_conv, a_log, dt_bias, norm_delta, w_o_swa, w_o_delta, w_o_mem, w_out,
     norm_pre_mix, norm_post_mix, norm_pre_ffn, norm_post_ffn, w_ff1, w_ff2) = lw
    mp, ms, n_seq = xp.shape[0], xs.shape[0], s0.shape[0]
    seq = ms // n_seq
    p_width = P_CQ + w_in_t.shape[0] - CQ_OFF
    tm_s = _row_tile(ms, 1024)
    proj_s, h_p = _in_proj(xs, norm_pre_mix, w_in_t, tm_s, PROJ_TILE,
                           [_rmsnorm_task(xp, norm_pre_mix, (ms // tm_s) * (p_width // PROJ_TILE))])
    tm_p = _row_tile(mp, 2048)
    host_steps = (mp // tm_p) * (p_width // (2 * PROJ_TILE))
    casts = [_cast_task(w, host_steps) for w in (w_o_swa, w_o_delta, w_o_mem, w_out, w_ff1, w_ff2)]
    proj_p, w_o_swa, w_o_delta, w_o_mem, w_out, w_ff1, w_ff2 = _in_proj(
        h_p, None, w_in_t, tm_p, 2 * PROJ_TILE, casts)
    chunk = DN_CHUNK if mp % DN_CHUNK == 0 else mp
    scan_steps = mp // chunk
    per = -(-n_seq // scan_steps)
    riding = n_seq % per == 0 and per <= RIDER_SEQS_MAX
    swa_s = _swa_sample_task(proj_s, swa_caches, layer, rel_bias, seq, per if riding else 1)
    mem_s_task = _mem_attn_task(proj_s, *mem_s, layer, seq * (per if riding else math.gcd(n_seq, MEM_PER_STEP)), F32)
    riders = [swa_s, mem_s_task] if riding else []
    b_p, state_p, *rode = _delta_scan(proj_p, w_conv, a_log, dt_bias, norm_delta, None, None, chunk, riders)
    a_s, *kv_new_s, c_s = rode if riding else (*_run_task(swa_s), *_run_task(mem_s_task))
    a_p = _swa_prompt(proj_p, bias_prompt)
    c_p = _run_task(_mem_attn_task(proj_p, *mem_p, 0, _row_tile(mp, 512), BF16))[0]
    b_s, state_s, hist_s = _delta_scan(proj_s, w_conv, a_log, dt_bias, norm_delta, s0, conv_buf, seq)
    outs = []
    for x, proj, a, b, c in ((xp, proj_p, a_p, b_p, c_p), (xs, proj_s, a_s, b_s, c_s)):
        m = x.shape[0]
        tm = _row_tile(m, 1024)
        merged = _merge(a, b, c, proj, w_o_swa, w_o_delta, w_o_mem, tm, 512)
        x1, h2 = _proj_norm_residual(merged, w_out, x, norm_post_mix, norm_pre_ffn, _row_tile(m, 512))
        outs.append(_ffn(h2, w_ff1, w_ff2, x1, norm_post_ffn, tm, 1024))
    new_s = [t.reshape(n_seq, seq, SWA_HPG, HEAD_DIM) for t in kv_new_s] + [state_s, jnp.transpose(hist_s, (1, 0, 2))]
    return outs[0], outs[1], proj_p, state_p, new_s


def kernel(x_prompt, x_sample, cache_swa0_k, cache_swa0_v, cache_swa1_k, cache_swa1_v, cache_swa2_k, cache_swa2_v, state_delta, state_conv, cache_mem_k, cache_mem_v, mem_prompt, rel_bias, w_in, w_conv, A_log, dt_bias, norm_delta, norm_mem, w_mem_kv, w_o_swa, w_o_delta, w_o_mem, w_out, norm_pre_mix, norm_post_mix, norm_pre_ffn, norm_post_ffn, w_ff1, w_ff2):
    depth = w_in.shape[0]
    bp, tp, d = x_prompt.shape
    bs, ts, _ = x_sample.shape
    assert bp == 1 and ts > CONV_WIDTH - 1 and ts % 8 == 0
    sample_swa = (cache_swa0_k, cache_swa0_v, cache_swa1_k, cache_swa1_v, cache_swa2_k, cache_swa2_v)
    bias_prompt = _prompt_bias_tables(rel_bias)
    xp = x_prompt.reshape(bp * tp, d)
    xs = x_sample.reshape(bs * ts, d)
    new_p = [[] for _ in range(10)]
    new_s = [[] for _ in range(8)]
    hw = SWA_HPG * HEAD_DIM
    for l in range(depth):
        lw = (jnp.swapaxes(w_in, 1, 2)[l], w_conv[l], A_log[l], dt_bias[l], norm_delta[l], w_o_swa[l],
              w_o_delta[l], w_o_mem[l], w_out[l],
              norm_pre_mix[l], norm_post_mix[l], norm_pre_ffn[l], norm_post_ffn[l],
              w_ff1[l], w_ff2[l])
        mem = mem_prompt.reshape(-1, d)
        mkv = _matmul(_rmsnorm_cast(mem, norm_mem[l], _row_tile(mem.shape[0], 256)), w_mem_kv[l],
                      2 * MEM_WIDTH, _row_tile(mem.shape[0], 256), 512, "mem_kv")
        mk = mkv[:, :MEM_WIDTH].reshape(1, bp, -1, MEM_HEADS, MEM_HEAD_DIM)
        mv = mkv[:, MEM_WIDTH:].reshape(1, bp, -1, MEM_HEADS, MEM_HEAD_DIM)
        xp, xs, main_p, s_p, vals_s = _layer(
            xp, xs, lw, l, rel_bias, bias_prompt, (mk, mv), sample_swa, (cache_mem_k, cache_mem_v),
            state_conv[l], state_delta[l])
        vals_p = []
        for g, (window, _) in enumerate(SWA_GROUPS):
            keep = min(window, tp)
            for sec in (1, 2):
                c0 = sec * SWA_WIDTH + g * hw
                vals_p.append(main_p[tp - keep:, c0:c0 + hw].reshape(bp, keep, SWA_HPG, HEAD_DIM))
        vals_p.append(s_p.reshape(bp, DN_HEADS, HEAD_DIM, HEAD_DIM))
        vals_p.append(main_p[tp - (CONV_WIDTH - 1):, B_OFF:B_OFF + 3 * DN_WIDTH].reshape(bp, CONV_WIDTH - 1, -1))
        vals_p += [mk[0], mv[0]]
        for lst, val in zip(new_p, vals_p):
            lst.append(val)
        for lst, val in zip(new_s, vals_s):
            lst.append(val)
    outs_p = [jnp.stack(t) for t in new_p]
    outs_s = [jnp.stack(t) for t in new_s]
    return (xp.reshape(bp, tp, d), xs.reshape(bs, ts, d), *outs_p, *outs_s)
```

```python
import functools
import math

import numpy as np
import jax
import jax.numpy as jnp
from jax import lax
from jax.experimental import pallas as pl
from jax.experimental.pallas import tpu as pltpu

EPS = 1e-6
HEAD_DIM = 128
SWA_GROUPS = ((128, 1), (512, 4), (2048, 16))
SWA_SPAN = 128
SWA_HPG = 4
SWA_WIDTH = SWA_HPG * len(SWA_GROUPS) * HEAD_DIM
DN_HEADS = 12
DN_WIDTH = DN_HEADS * HEAD_DIM
CONV_WIDTH = 4
DN_CHUNK = 128
MEM_HEADS = 4
MEM_HEAD_DIM = 256
MEM_WIDTH = MEM_HEADS * MEM_HEAD_DIM
N_BUCKETS = 32
MAX_DISTANCE = 2048
N_BRANCHES = 3

A_OFF = 0
B_OFF = 3 * SWA_WIDTH
Z_OFF = B_OFF + 3 * DN_WIDTH
BA_OFF = Z_OFF + DN_WIDTH
CQ_OFF = BA_OFF + 2 * DN_HEADS
GATE_OFF = CQ_OFF + MEM_WIDTH
LANES = 128
SUBLANES = 8
PROJ_TILE = 512
P_BA = BA_OFF
P_CQ = P_BA + PROJ_TILE
P_GATE = P_CQ + MEM_WIDTH
SWA_ROWS = 2048
SWA_UNROLL = 8
NEG = -1e30
VMEM_LIMIT = 61 * 1024 * 1024

BF16 = jnp.bfloat16
F32 = jnp.float32


def _params(n_grid):
    return pltpu.CompilerParams(dimension_semantics=("arbitrary",) * n_grid, vmem_limit_bytes=VMEM_LIMIT)


def _dot(a, b):
    return jnp.dot(a, b, preferred_element_type=F32)


def _dot_nt(a, b):
    return lax.dot_general(a, b, (((1,), (1,)), ((), ())), preferred_element_type=F32)


def _dot_tn(a, b):
    return lax.dot_general(a, b, (((0,), (0,)), ((), ())), preferred_element_type=F32)


def _split2(x):
    hi = x.astype(BF16)
    lo = (x - hi.astype(F32)).astype(BF16)
    return hi, lo


def _dot3(a, b, dot=_dot):
    ah, al = _split2(a)
    bh, bl = _split2(b)
    return dot(ah, bh) + (dot(ah, bl) + dot(al, bh))


def _rmsnorm_cast_kernel(x_ref, g_ref, o_ref):
    x = x_ref[...]
    y = x * lax.rsqrt(jnp.mean(x * x, axis=-1, keepdims=True) + EPS)
    o_ref[...] = (y * g_ref[...]).astype(o_ref.dtype)


def _rmsnorm_cast(x, g, tm):
    m, d = x.shape
    return pl.pallas_call(
        _rmsnorm_cast_kernel, grid=(m // tm,),
        in_specs=[pl.BlockSpec((tm, d), lambda i: (i, 0)), pl.BlockSpec((1, d), lambda i: (0, 0))],
        out_specs=pl.BlockSpec((tm, d), lambda i: (i, 0)),
        out_shape=jax.ShapeDtypeStruct((m, d), BF16),
        compiler_params=_params(1), name="rmsnorm_cast")(x, g.reshape(1, d))


def _matmul_kernel(x_ref, w_ref, o_ref):
    o_ref[...] = _dot(x_ref[...], w_ref[...].astype(BF16)).astype(o_ref.dtype)


def _matmul(x, w, n, tm, tn, name):
    m, k = x.shape
    assert n % tn == 0 and m % tm == 0
    return pl.pallas_call(
        _matmul_kernel, grid=(m // tm, n // tn),
        in_specs=[pl.BlockSpec((tm, k), lambda i, j: (i, 0)), pl.BlockSpec((k, tn), lambda i, j: (0, j))],
        out_specs=pl.BlockSpec((tm, tn), lambda i, j: (i, j)),
        out_shape=jax.ShapeDtypeStruct((m, n), F32),
        compiler_params=_params(2), name=name)(x, w)


def _in_proj_kernel(x_ref, g_ref, w_ref, o_ref, h_ref):
    @pl.when(pl.program_id(1) == 0)
    def _():
        h_ref[...] = _rms(x_ref[...], g_ref[...]).astype(BF16)

    o_ref[...] = _dot_nt(h_ref[...], w_ref[...].astype(BF16))


def _in_proj_normed_kernel(h_ref, w_ref, o_ref):
    o_ref[...] = _dot_nt(h_ref[...], w_ref[...].astype(BF16))


def _in_proj(x, g, wt, tm, tn, riders=()):
    m, k = x.shape
    n = P_CQ + wt.shape[0] - CQ_OFF
    assert n % tn == 0 and m % tm == 0 and P_CQ % tn == 0 and CQ_OFF % SUBLANES == 0

    def w_rows(i, j):
        return (pl.multiple_of(jnp.where(j * tn < P_CQ, j * tn, j * tn - P_CQ + CQ_OFF), SUBLANES), 0)

    x_spec = pl.BlockSpec((tm, k), lambda i, j: (i, 0), pipeline_mode=pl.Buffered(1))
    w_spec = pl.BlockSpec((pl.Element(tn), pl.Element(k)), w_rows)
    out = ([jax.ShapeDtypeStruct((m, n), F32)], [pl.BlockSpec((tm, tn), lambda i, j: (i, j))])
    if g is None:
        return _hosted_call("in_proj", _in_proj_normed_kernel, (m // tm, n // tn), [x, wt], [x_spec, w_spec],
                            *out, [], riders)
    return _hosted_call(
        "in_proj", _in_proj_kernel, (m // tm, n // tn), [x, g.reshape(1, k), wt],
        [x_spec, pl.BlockSpec((1, k), lambda i, j: (0, 0)), w_spec], *out, [pltpu.VMEM((tm, k), BF16)], riders)


def _rmsnorm_task(x, g, max_steps):
    rows, k = x.shape
    steps = max(s for s in range(1, min(max_steps, rows // BF16_ROWS) + 1) if rows % (s * BF16_ROWS) == 0)
    slab = pl.BlockSpec((rows // steps, k), lambda i: (i, 0))

    def body(x_ref, g_ref, o_ref):
        o_ref[...] = _rms(x_ref[...], g_ref[...]).astype(BF16)

    return _Task("rmsnorm", body, steps, [x, g.reshape(1, k)], [slab, pl.BlockSpec((1, k), lambda i: (0, 0))],
                 [jax.ShapeDtypeStruct((rows, k), BF16)], [slab])


def _t5_bucket(dist):
    max_exact = N_BUCKETS // 2
    df = jnp.maximum(dist, 1).astype(F32)
    large = max_exact + (jnp.log(df / max_exact) / math.log(MAX_DISTANCE / max_exact)
                         * (N_BUCKETS - max_exact)).astype(jnp.int32)
    return jnp.where(dist < max_exact, dist, jnp.minimum(large, N_BUCKETS - 1))


def _group_bias(rel_bias, g):
    _, dil = SWA_GROUPS[g]
    dist = jnp.arange(SWA_SPAN + 1, dtype=jnp.int32) * dil
    onehot = _t5_bucket(dist)[None, :, None] == jnp.arange(N_BUCKETS, dtype=jnp.int32)[None, None, :]
    heads = jnp.transpose(rel_bias[:, g * SWA_HPG:(g + 1) * SWA_HPG].astype(F32))
    return jnp.sum(jnp.where(onehot, heads[:, None, :], 0.0), axis=-1)


def _prompt_bias_tables(rel_bias):
    span = SWA_SPAN
    first = (np.arange(2 * span) >= span)[None, None, :]
    out = []
    for g in range(len(SWA_GROUPS)):
        w = _group_bias(rel_bias, g)
        h = w.shape[0]
        p = 3 * span
        e = jnp.concatenate([jnp.full((h, span - 1), NEG, F32), w[:, ::-1], jnp.full((h, p - 2 * span), NEG, F32)], 1)
        skew = jnp.broadcast_to(e[:, None, :], (h, span, p)).reshape(h, span * p)[:, :span * (p - 1)]
        tab = skew.reshape(h, span, p - 1)[:, :, span - 1:3 * span - 1]
        out.append(jnp.stack([jnp.where(first, tab, NEG), tab], axis=1))
    return jnp.stack(out)


def _sample_bias_tables(rel_bias, g, n_res, n_new, cache_len):
    _, dil = SWA_GROUPS[g]
    w = _group_bias(rel_bias, g)
    h = w.shape[0]
    lm = cache_len // dil
    assert cache_len % dil == 0
    p = w.shape[1] + lm + n_new
    wrev = jnp.concatenate([w, jnp.full((h, lm + n_new), NEG, F32)], axis=1)[:, ::-1]
    s = np.arange(n_new)
    res = np.arange(n_res)
    tab = jnp.full((h, n_new, lm, n_res), NEG, F32)
    for q in range((n_new - 1) // dil + 1):
        vec = wrev[:, p - 1 - lm - q:p - 1 - q]
        mask = (s // dil == q)[:, None] & (res[None, :] == (s % dil)[:, None])
        tab = jnp.where(mask[None, :, None, :], vec[:, None, :, None], tab)
    dist = s[:, None] - s[None, :]
    new = jnp.full((h, n_new, n_new), NEG, F32)
    for j in range(min((n_new - 1) // dil, SWA_SPAN) + 1):
        new = jnp.where((dist == j * dil)[None], w[:, j][:, None, None], new)
    return tab.reshape(h, n_new, lm * n_res), new


def _swa_prompt_kernel(*refs, n_groups):
    ins = refs[:5 * n_groups]
    bias_ref = refs[5 * n_groups]
    o_ref = refs[5 * n_groups + 1]
    scr = refs[5 * n_groups + 2:]
    kext, vext, og, lg = scr[:n_groups], scr[n_groups:2 * n_groups], scr[2 * n_groups:3 * n_groups], scr[3 * n_groups:]
    n = pl.program_id(0)
    scale = HEAD_DIM ** -0.5
    span = SWA_SPAN
    for g in range(n_groups):
        _, dil = SWA_GROUPS[g]
        q_ref, kc_ref, vc_ref, kp_ref, vp_ref = ins[5 * g:5 * g + 5]
        blk = span * dil
        kext[g][0:blk, :] = kp_ref[...]
        kext[g][blk:, :] = kc_ref[...]
        vext[g][0:blk, :] = vp_ref[...]
        vext[g][blk:, :] = vc_ref[...]

        def body(it, carry, g=g, dil=dil, blk=blk, q_ref=q_ref):
            items = range(SWA_UNROLL)
            starts, sels = [], []
            for u in items:
                idx = it * SWA_UNROLL + u
                b = idx // dil
                starts.append(b * blk + (idx - b * dil))
                sels.append(jnp.where(jnp.logical_and(n == 0, b == 0), 0, 1))
            q = [q_ref[pl.ds(starts[u], span, stride=dil), :].astype(BF16) for u in items]
            kk = [kext[g][pl.ds(starts[u], 2 * span, stride=dil), :].astype(BF16) for u in items]
            s = [_dot_nt(q[u], kk[u]) * scale + bias_ref[g, 0, sels[u]] for u in items]
            m = [jnp.max(s[u], axis=-1, keepdims=True) for u in items]
            p = [jnp.exp(s[u] - m[u]) for u in items]
            den = [jnp.sum(p[u], axis=-1, keepdims=True) for u in items]
            vv = [vext[g][pl.ds(starts[u], 2 * span, stride=dil), :].astype(BF16) for u in items]
            o = [_dot(p[u].astype(BF16), vv[u]) / den[u] for u in items]
            for u in items:
                og[g][pl.ds(starts[u], span, stride=dil), :] = o[u]
                lg[g][pl.ds(starts[u], span, stride=dil), :] = jnp.broadcast_to(
                    m[u] + jnp.log(den[u]), (span, HEAD_DIM))
            return carry

        lax.fori_loop(0, SWA_ROWS // span // SWA_UNROLL, body, 0)
    lses = [lg[g][...] for g in range(n_groups)]
    mx = functools.reduce(jnp.maximum, lses)
    ws = [jnp.exp(l - mx) for l in lses]
    num = functools.reduce(lambda a, b: a + b, [w * og[g][...] for g, w in enumerate(ws)])
    o_ref[...] = (num / functools.reduce(lambda a, b: a + b, ws)).astype(o_ref.dtype)


def _swa_prompt(proj, bias_tables):
    t = proj.shape[0]
    n_groups = len(SWA_GROUPS)
    assert t % SWA_ROWS == 0
    in_specs, scratch_k, scratch_o = [], [], []
    for g, (_, dil) in enumerate(SWA_GROUPS):
        blk = SWA_SPAN * dil
        per = SWA_ROWS // blk
        assert SWA_ROWS % blk == 0
        qc, kc, vc = (sec * (SWA_WIDTH // HEAD_DIM) + g * SWA_HPG for sec in range(3))
        cur = lambda col: pl.BlockSpec((SWA_ROWS, HEAD_DIM), lambda n, h, col=col: (n, col + h))
        prev = lambda col, blk=blk, per=per: pl.BlockSpec(
            (blk, HEAD_DIM), lambda n, h, col=col, per=per: (jnp.maximum(n * per - 1, 0), col + h))
        in_specs += [cur(qc), cur(kc), cur(vc), prev(kc), prev(vc)]
        scratch_k.append(pltpu.VMEM((SWA_ROWS + blk, HEAD_DIM), F32))
        scratch_o.append(pltpu.VMEM((SWA_ROWS, HEAD_DIM), F32))
    in_specs.append(pl.BlockSpec((n_groups, 1, 2, SWA_SPAN, 2 * SWA_SPAN), lambda n, h: (0, h, 0, 0, 0)))
    return pl.pallas_call(
        functools.partial(_swa_prompt_kernel, n_groups=n_groups),
        grid=(t // SWA_ROWS, SWA_HPG),
        in_specs=in_specs,
        out_specs=pl.BlockSpec((SWA_ROWS, HEAD_DIM), lambda n, h: (n, h)),
        out_shape=jax.ShapeDtypeStruct((t, SWA_HPG * HEAD_DIM), BF16),
        scratch_shapes=scratch_k + scratch_k + scratch_o + scratch_o,
        compiler_params=_params(2), name="swa_prompt")(*([proj] * (5 * n_groups)), bias_tables)


def _swa_sample_kernel(*refs, n_groups):
    qkv_ref = refs[0]
    caches = refs[1:1 + 2 * n_groups]
    biases = refs[1 + 2 * n_groups:1 + 4 * n_groups]
    o_ref = refs[1 + 4 * n_groups]
    new_refs = refs[2 + 4 * n_groups:]
    scale = HEAD_DIM ** -0.5
    nseq = caches[0].shape[1]
    n_new = qkv_ref.shape[0] // nseq
    items = [(b, g, h) for b in range(nseq) for h in range(SWA_HPG) for g in range(n_groups)]
    idx = range(len(items))

    def new(sec, b, g, h):
        col = sec * SWA_WIDTH + (g * SWA_HPG + h) * HEAD_DIM
        return qkv_ref[b * n_new:(b + 1) * n_new, col:col + HEAD_DIM].astype(BF16)

    def cached(ref, b, h):
        nk = math.prod(ref.shape[2:-1]) // SWA_HPG
        if len(ref.shape) == 5:
            pick = (0, b, slice(None), pl.ds(h, ref.shape[3] // SWA_HPG, stride=SWA_HPG), slice(None))
        else:
            pick = (0, b, pl.ds(h, nk, stride=SWA_HPG), slice(None))
        return ref[pick].reshape(nk, HEAD_DIM).astype(BF16)

    q = [new(0, *it) for it in items]
    sc = [_dot_nt(q[i], cached(caches[2 * g], b, h)) * scale + biases[2 * g][h]
          for i, (b, g, h) in enumerate(items)]
    sn = [_dot_nt(q[i], new(1, b, g, h)) * scale + biases[2 * g + 1][h] for i, (b, g, h) in enumerate(items)]
    m = [jnp.maximum(jnp.max(sc[i], axis=-1, keepdims=True), jnp.max(sn[i], axis=-1, keepdims=True)) for i in idx]
    pc = [jnp.exp(sc[i] - m[i]) for i in idx]
    pn = [jnp.exp(sn[i] - m[i]) for i in idx]
    den = [jnp.sum(pc[i], axis=-1, keepdims=True) + jnp.sum(pn[i], axis=-1, keepdims=True) for i in idx]
    o = [(_dot(pc[i].astype(BF16), cached(caches[2 * g + 1], b, h)) + _dot(pn[i].astype(BF16), new(2, b, g, h)))
         / den[i] for i, (b, g, h) in enumerate(items)]
    lse = [m[i] + jnp.log(den[i]) for i in idx]
    for b in range(nseq):
        for h in range(SWA_HPG):
            mine = [i for i in idx if items[i][0] == b and items[i][2] == h]
            mx = functools.reduce(jnp.maximum, [lse[i] for i in mine])
            ws = [jnp.exp(lse[i] - mx) for i in mine]
            num = functools.reduce(lambda x, y: x + y, [w * o[i] for w, i in zip(ws, mine)])
            o_ref[b * n_new:(b + 1) * n_new, h * HEAD_DIM:(h + 1) * HEAD_DIM] = (
                num / functools.reduce(lambda x, y: x + y, ws))
    for b, g, h in items:
        for sec in (1, 2):
            col = sec * SWA_WIDTH + (g * SWA_HPG + h) * HEAD_DIM
            new_refs[2 * g + sec - 1][b, pl.ds(h, n_new, stride=SWA_HPG), :] = (
                qkv_ref[b * n_new:(b + 1) * n_new, col:col + HEAD_DIM])


class _Task:
    def __init__(self, name, body, steps, args, in_specs, out_shapes, out_specs):
        self.name, self.body, self.steps = name, body, steps
        self.args, self.in_specs, self.out_shapes, self.out_specs = args, in_specs, out_shapes, out_specs


def _run_task(task):
    return pl.pallas_call(
        task.body, grid=(task.steps,), in_specs=task.in_specs, out_specs=task.out_specs,
        out_shape=task.out_shapes, compiler_params=_params(1), name=task.name)(*task.args)


def _riding_specs(task, grid):
    assert math.prod(grid) >= task.steps

    def step_of(*idx):
        flat = functools.reduce(lambda acc, t: acc * t[1] + t[0], zip(idx[1:], grid[1:]), idx[0])
        return flat if math.prod(grid) == task.steps else jnp.minimum(flat, task.steps - 1)

    re = lambda spec: pl.BlockSpec(spec.block_shape, lambda *idx: spec.index_map(step_of(*idx)))
    return [re(s) for s in task.in_specs], [re(s) for s in task.out_specs]


BF16_ROWS = 16


def _cast_task(w, max_steps):
    rows, cols = w.shape
    steps = max(s for s in range(1, min(max_steps, rows // BF16_ROWS) + 1) if rows % (s * BF16_ROWS) == 0)
    slab = pl.BlockSpec((rows // steps, cols), lambda i: (i, 0))

    def body(x_ref, o_ref):
        o_ref[...] = x_ref[...].astype(BF16)

    return _Task("cast", body, steps, [w], [slab], [jax.ShapeDtypeStruct((rows, cols), BF16)], [slab])


def _swa_sample_task(proj, caches, layer, rel_bias, n_new, per):
    rows_total = proj.shape[0]
    nb = rows_total // n_new
    assert nb % per == 0
    n_groups = len(SWA_GROUPS)
    hw = SWA_HPG * HEAD_DIM
    cache_in, cache_specs, bias_in, bias_specs = [], [], [], []
    for g, (_, dil) in enumerate(SWA_GROUPS):
        depth, _, cache_len, heads, hd = caches[2 * g].shape
        if dil >= 2 * n_new and n_new % 8 == 0:
            n_res = n_new
            shape = (depth, nb, cache_len // dil, dil * heads, hd)
            spec = pl.BlockSpec((1, per, cache_len // dil, n_res * heads, hd), lambda b: (layer, b, 0, 0, 0))
        else:
            n_res = dil
            shape = (depth, nb, cache_len * heads, hd)
            spec = pl.BlockSpec((1, per, cache_len * heads, hd), lambda b: (layer, b, 0, 0))
        for c in caches[2 * g:2 * g + 2]:
            cache_in.append(c.reshape(shape))
            cache_specs.append(spec)
        for tab in _sample_bias_tables(rel_bias, g, n_res, n_new, cache_len):
            bias_in.append(tab)
            bias_specs.append(pl.BlockSpec(tab.shape, lambda b: (0, 0, 0)))
    return _Task(
        "swa_sample", functools.partial(_swa_sample_kernel, n_groups=n_groups), nb // per,
        [proj, *cache_in, *bias_in],
        [pl.BlockSpec((per * n_new, 3 * SWA_WIDTH), lambda b: (b, 0))] + cache_specs + bias_specs,
        [jax.ShapeDtypeStruct((rows_total, hw), F32)]
        + [jax.ShapeDtypeStruct((nb, n_new * SWA_HPG, HEAD_DIM), F32)] * (2 * n_groups),
        [pl.BlockSpec((per * n_new, hw), lambda b: (b, 0))]
        + [pl.BlockSpec((per, n_new * SWA_HPG, HEAD_DIM), lambda b: (b, 0, 0))] * (2 * n_groups))


def _silu(x):
    return x * (1.0 / (1.0 + jnp.exp(-x)))


def _softplus(x):
    return jnp.maximum(x, 0.0) + jnp.log(1.0 + jnp.exp(-jnp.abs(x)))


def _inv_dot(a, b):
    return _dot(a.astype(BF16), b.astype(BF16))


HALO = 8
DELTA_SEQS_PER_STEP = 8


def _delta_scan_kernel(*refs, chunk, nseq, carry):
    if carry:
        x_ref, z_ref, ba_ref, wc_ref, alog_ref, dt_ref, gain_ref, o_ref, s_out_ref, s_scr, halo_scr = refs
        s_in_ref = halo_ref = None
    else:
        (x_ref, z_ref, ba_ref, wc_ref, alog_ref, dt_ref, gain_ref, s_in_ref, halo_ref,
         o_ref, s_out_ref, hist_out_ref) = refs
        s_scr = halo_scr = None
        x3 = x_ref[...].reshape(nseq, chunk, x_ref.shape[1])
        for t in range(CONV_WIDTH - 1):
            hist_out_ref[t] = x3[:, chunk - (CONV_WIDTH - 1) + t, :]
    c = chunk
    rows = nseq * c
    seqs = range(nseq)
    part = lambda t, b: t[b * c:(b + 1) * c]
    if carry:
        @pl.when(pl.program_id(0) == 0)
        def _():
            s_scr[...] = jnp.zeros_like(s_scr)
            halo_scr[...] = jnp.zeros_like(halo_scr)

    x = x_ref[...]
    acc = x * wc_ref[CONV_WIDTH - 1:CONV_WIDTH, :]
    if carry:
        xp = jnp.concatenate([halo_scr[...], x], axis=0)
        for t in range(1, CONV_WIDTH):
            acc = acc + pltpu.roll(xp, t, 0)[HALO:] * wc_ref[CONV_WIDTH - 1 - t:CONV_WIDTH - t, :]
        halo_scr[...] = x[c - HALO:, :]
    else:
        width = x.shape[1]
        rep = [jnp.broadcast_to(halo_ref[k][:, None, :], (nseq, c, width)).reshape(rows, width)
               for k in range(CONV_WIDTH - 1)]
        pos = lax.broadcasted_iota(jnp.int32, (rows, 1), 0) % c
        for t in range(1, CONV_WIDTH):
            hv = rep[CONV_WIDTH - 1 - t]
            for s in range(1, t):
                hv = jnp.where(pos == s, rep[CONV_WIDTH - 1 + s - t], hv)
            acc = acc + jnp.where(pos >= t, pltpu.roll(x, t, 0), hv) * wc_ref[CONV_WIDTH - 1 - t:CONV_WIDTH - t, :]
    qkv = _silu(acc)

    ba = ba_ref[...]
    beta_all = 1.0 / (1.0 + jnp.exp(-ba))
    g_all = -jnp.exp(alog_ref[...]) * _softplus(ba + dt_ref[...])
    row = lax.broadcasted_iota(jnp.int32, (rows, rows), 0)
    col = lax.broadcasted_iota(jnp.int32, (rows, rows), 1)
    same = (row // c) == (col // c)
    tri = (row >= col) & same
    strict = (row > col) & same
    tril = tri.astype(BF16)
    g1 = g_all.astype(BF16)
    r1 = g_all - g1.astype(F32)
    g2 = r1.astype(BF16)
    g3 = (r1 - g2.astype(F32)).astype(BF16)
    gcum_all = _dot(tril, g1) + (_dot(tril, g2) + _dot(tril, g3))
    gcum_t = gcum_all.T
    eye = (row == col).astype(F32)

    heads = range(DN_HEADS)
    hs = lambda sec, h: slice(sec * DN_WIDTH + h * HEAD_DIM, sec * DN_WIDTH + (h + 1) * HEAD_DIM)
    lane = lambda h: slice(DN_HEADS + h, DN_HEADS + h + 1)
    l2 = lambda t: t * lax.rsqrt(jnp.sum(t * t, axis=-1, keepdims=True) + EPS)
    q = [l2(qkv[:, hs(0, h)]) * (HEAD_DIM ** -0.5) for h in heads]
    k = [l2(qkv[:, hs(1, h)]) for h in heads]
    v = [qkv[:, hs(2, h)] for h in heads]
    bcol = [beta_all[:, h:h + 1] for h in heads]
    gc = [gcum_all[:, lane(h)] for h in heads]
    glast = [[gcum_all[(b + 1) * c - 1:(b + 1) * c, lane(h)] for b in seqs] for h in heads]
    decay = [jnp.exp(jnp.where(tri, gc[h] - gcum_t[lane(h), :], NEG)) for h in heads]
    kb = [k[h] * bcol[h] for h in heads]
    k16 = [k[h].astype(BF16) for h in heads]
    nmat = [jnp.where(strict, _dot_nt(kb[h].astype(BF16), k16[h]) * decay[h], 0.0) for h in heads]
    attn = [jnp.where(tri, _dot_nt(q[h].astype(BF16), k16[h]) * decay[h], 0.0) for h in heads]
    inv = [eye] * DN_HEADS
    blk = 1
    while blk < c:
        pair = (row // (2 * blk) == col // (2 * blk)) & (row % (2 * blk) >= blk) & (col % (2 * blk) < blk)
        low = [jnp.where(pair, nmat[h], 0.0) for h in heads]
        if blk == 1:
            inv = [eye - low[h] for h in heads]
        else:
            t = [_inv_dot(low[h], inv[h]) for h in heads]
            inv = [inv[h] - _inv_dot(inv[h], t[h]) for h in heads]
        blk *= 2
    eg = [jnp.exp(gc[h]) for h in heads]
    rhs = [jnp.concatenate([v[h] * bcol[h], kb[h] * eg[h]], axis=1) for h in heads]
    sol = [_inv_dot(inv[h], rhs[h]) for h in heads]
    qdec = [q[h] * eg[h] for h in heads]
    hb = [(h, b) for h in heads for b in seqs]
    s_prev = {(h, b): (s_scr[h] if carry else s_in_ref[b, h]) for h, b in hb}
    wq = {(h, b): jnp.concatenate([part(sol[h][:, HEAD_DIM:], b), part(qdec[h], b)], axis=0).astype(BF16)
          for h, b in hb}
    ws = {i: _dot(wq[i], s_prev[i].astype(BF16)) for i in hb}
    vn = {(h, b): part(sol[h][:, :HEAD_DIM], b) - ws[h, b][:c] for h, b in hb}
    v_new = [jnp.concatenate([vn[h, b] for b in seqs], axis=0).astype(BF16) for h in heads]
    o = [jnp.concatenate([ws[h, b][c:] for b in seqs], axis=0) + _dot(attn[h].astype(BF16), v_new[h])
         for h in heads]
    kdec = {(h, b): (part(k[h], b) * jnp.exp(glast[h][b] - part(gc[h], b))).astype(BF16) for h, b in hb}
    s_new = {(h, b): s_prev[h, b] * jnp.exp(glast[h][b]) + _dot_tn(kdec[h, b], vn[h, b].astype(BF16))
             for h, b in hb}
    for h in heads:
        if carry:
            s_scr[h] = s_new[h, 0]
            s_out_ref[h] = s_new[h, 0]
        else:
            for b in seqs:
                s_out_ref[b, h] = s_new[h, b]
        y = o[h] * lax.rsqrt(jnp.mean(o[h] * o[h], axis=-1, keepdims=True) + EPS) * gain_ref[...]
        o_ref[:, hs(0, h)] = (y * _silu(z_ref[:, hs(0, h)])).astype(o_ref.dtype)


def _delta_scan(proj, w_conv, a_log, dt_bias, norm_delta, s0, conv_buf, chunk, riders=()):
    m = proj.shape[0]
    carry = s0 is None
    nseq = 1 if carry else math.gcd(m // chunk, DELTA_SEQS_PER_STEP)
    rows = nseq * chunk
    n = m // rows
    width = 3 * DN_WIDTH
    assert chunk >= HALO and B_OFF % width == 0
    pad = lambda v_, off: jnp.zeros((1, LANES), F32).at[0, off:off + DN_HEADS].set(v_.astype(F32))
    consts = [w_conv, pad(a_log, DN_HEADS), pad(dt_bias, DN_HEADS), norm_delta.reshape(1, HEAD_DIM).astype(F32)]
    const_specs = [pl.BlockSpec((CONV_WIDTH, width), lambda i: (0, 0))] + [pl.BlockSpec((1, LANES), lambda i: (0, 0))] * 3
    in_specs = [pl.BlockSpec((rows, width), lambda i: (i, B_OFF // width)),
                pl.BlockSpec((rows, DN_WIDTH), lambda i: (i, Z_OFF // DN_WIDTH)),
                pl.BlockSpec((rows, LANES), lambda i: (i, P_BA // LANES))] + const_specs
    args = [proj, proj, proj] + consts
    state = (DN_HEADS, HEAD_DIM, HEAD_DIM)
    if carry:
        out_dtype = BF16
        s_shape, s_spec = state, pl.BlockSpec(state, lambda i: (0, 0, 0))
        scratch = [pltpu.VMEM(state, F32), pltpu.VMEM((HALO, width), F32)]
    else:
        out_dtype = F32
        in_specs += [pl.BlockSpec((nseq,) + state, lambda i: (i, 0, 0, 0)),
                     pl.BlockSpec((CONV_WIDTH - 1, nseq, width), lambda i: (0, i, 0))]
        args += [s0, jnp.transpose(conv_buf, (1, 0, 2))]
        s_shape, s_spec = (m // chunk,) + state, pl.BlockSpec((nseq,) + state, lambda i: (i, 0, 0, 0))
        scratch = []
    out_specs = [pl.BlockSpec((rows, DN_WIDTH), lambda i: (i, 0)), s_spec]
    out_shape = [jax.ShapeDtypeStruct((m, DN_WIDTH), out_dtype), jax.ShapeDtypeStruct(s_shape, F32)]
    if not carry:
        out_specs.append(pl.BlockSpec((CONV_WIDTH - 1, nseq, width), lambda i: (0, i, 0)))
        out_shape.append(jax.ShapeDtypeStruct((CONV_WIDTH - 1, m // chunk, width), F32))
    host = functools.partial(_delta_scan_kernel, chunk=chunk, nseq=nseq, carry=carry)
    return _hosted_call("delta_scan", host, (n,), args, in_specs, out_shape, out_specs, scratch, riders)


def _hosted_kernel(*refs, host, n_host_in, n_host_out, riders):
    n_in = n_host_in + sum(len(t.in_specs) for t in riders)
    n_out = n_host_out + sum(len(t.out_specs) for t in riders)
    ins, outs, scratch = refs[:n_in], refs[n_in:n_in + n_out], refs[n_in + n_out:]
    host(*ins[:n_host_in], *outs[:n_host_out], *scratch)
    pi, po = n_host_in, n_host_out
    for t in riders:
        ni, no = len(t.in_specs), len(t.out_specs)
        t.body(*ins[pi:pi + ni], *outs[po:po + no])
        pi, po = pi + ni, po + no


def _hosted_call(name, host, grid, args, in_specs, out_shape, out_specs, scratch, riders):
    body = functools.partial(_hosted_kernel, host=host, n_host_in=len(in_specs), n_host_out=len(out_specs),
                             riders=tuple(riders))
    args, in_specs, out_shape, out_specs = list(args), list(in_specs), list(out_shape), list(out_specs)
    for t in riders:
        r_in, r_out = _riding_specs(t, grid)
        args, in_specs = args + list(t.args), in_specs + r_in
        out_shape, out_specs = out_shape + list(t.out_shapes), out_specs + r_out
    return pl.pallas_call(
        body, grid=grid, in_specs=in_specs, out_specs=out_specs, out_shape=out_shape, scratch_shapes=scratch,
        compiler_params=_params(len(grid)), name=name)(*args)


def _mem_attn_kernel(q_ref, k_ref, v_ref, o_ref):
    scale = MEM_HEAD_DIM ** -0.5
    parts = MEM_HEAD_DIM // LANES
    period = MEM_HEADS * parts
    tokens = k_ref.shape[2] // period
    pick = lambda b, c, h: (0, b, pl.ds(c * MEM_HEADS + h, tokens, stride=period), slice(None))
    nb = k_ref.shape[1]
    rows = q_ref.shape[0] // nb
    items = [(b, h) for b in range(nb) for h in range(MEM_HEADS)]
    idx = range(len(items))
    q = [[q_ref[b * rows:(b + 1) * rows, h * MEM_HEAD_DIM + c * LANES:h * MEM_HEAD_DIM + (c + 1) * LANES]
          .astype(BF16) for c in range(parts)] for b, h in items]
    s = [functools.reduce(lambda x, y: x + y, [_dot_nt(q[i][c], k_ref[pick(b, c, h)].astype(BF16))
                                               for c in range(parts)]) * scale for i, (b, h) in enumerate(items)]
    p = [jnp.exp(s[i] - jnp.max(s[i], axis=-1, keepdims=True)) for i in idx]
    den = [jnp.sum(p[i], axis=-1, keepdims=True) for i in idx]
    for i, (b, h) in enumerate(items):
        for c in range(parts):
            col = h * MEM_HEAD_DIM + c * LANES
            o_ref[b * rows:(b + 1) * rows, col:col + LANES] = (
                _dot(p[i].astype(BF16), v_ref[pick(b, c, h)].astype(BF16)) / den[i]).astype(o_ref.dtype)


MEM_PER_STEP = 4
RIDER_SEQS_MAX = 2


def _mem_attn_task(proj, mem_k, mem_v, layer, tm, out_dtype):
    m = proj.shape[0]
    assert P_CQ % MEM_WIDTH == 0
    depth, nb, tokens, heads, hd = mem_k.shape
    per = m // nb
    parts = hd // LANES
    mem_k, mem_v = (t.reshape(depth, nb, tokens, heads, parts, LANES).transpose(0, 1, 2, 4, 3, 5)
                    .reshape(depth, nb, tokens * heads * parts, LANES) for t in (mem_k, mem_v))
    mems = max(tm // per, 1)
    kv_spec = pl.BlockSpec((1, mems, tokens * heads * parts, LANES), lambda i: (layer, i * tm // (per * mems), 0, 0))
    return _Task(
        "mem_attn", _mem_attn_kernel, m // tm, [proj, mem_k, mem_v],
        [pl.BlockSpec((tm, MEM_WIDTH), lambda i: (i, P_CQ // MEM_WIDTH)), kv_spec, kv_spec],
        [jax.ShapeDtypeStruct((m, MEM_WIDTH), out_dtype)], [pl.BlockSpec((tm, MEM_WIDTH), lambda i: (i, 0))])


def _merge_kernel(a_ref, b_ref, c_ref, ga_ref, gb_ref, gc_ref, wa_ref, wb_ref, wc_ref, o_ref):
    def branch(x_ref, g_ref, w_ref):
        y = _dot(x_ref[...].astype(BF16), w_ref[...])
        return y * (1.0 / (1.0 + jnp.exp(-g_ref[...])))

    o_ref[...] = (branch(a_ref, ga_ref, wa_ref) + branch(b_ref, gb_ref, wb_ref)
                  + branch(c_ref, gc_ref, wc_ref)).astype(o_ref.dtype)


def _merge(a, b, c, tail, w_a, w_b, w_c, tm, tn):
    m = a.shape[0]
    d = w_a.shape[1]
    gate0 = P_GATE // tn
    assert P_GATE % tn == 0 and d % tn == 0
    row = lambda x: pl.BlockSpec((tm, x.shape[1]), lambda i, j: (i, 0))
    gate = lambda br: pl.BlockSpec((tm, tn), lambda i, j, br=br: (i, gate0 + br * (d // tn) + j))
    wsp = lambda w: pl.BlockSpec((w.shape[0], tn), lambda i, j: (0, j))
    return pl.pallas_call(
        _merge_kernel, grid=(m // tm, d // tn),
        in_specs=[row(a), row(b), row(c), gate(0), gate(1), gate(2), wsp(w_a), wsp(w_b), wsp(w_c)],
        out_specs=pl.BlockSpec((tm, tn), lambda i, j: (i, j)),
        out_shape=jax.ShapeDtypeStruct((m, d), BF16),
        compiler_params=_params(2), name="merge")(a, b, c, tail, tail, tail, w_a, w_b, w_c)


def _rms(y, g):
    return y * lax.rsqrt(jnp.mean(y * y, axis=-1, keepdims=True) + EPS) * g


def _proj_norm_residual_kernel(y_ref, w_ref, x_ref, g_post_ref, g_next_ref, o_ref, h_ref):
    x1 = x_ref[...] + _rms(_dot(y_ref[...], w_ref[...]), g_post_ref[...])
    o_ref[...] = x1
    h_ref[...] = _rms(x1, g_next_ref[...]).astype(h_ref.dtype)


def _proj_norm_residual(y, w, x, g_post, g_next, tm):
    m, d = x.shape
    k = y.shape[1]
    row = lambda width: pl.BlockSpec((tm, width), lambda i: (i, 0))
    gsp = pl.BlockSpec((1, d), lambda i: (0, 0))
    return pl.pallas_call(
        _proj_norm_residual_kernel, grid=(m // tm,),
        in_specs=[row(k), pl.BlockSpec((k, d), lambda i: (0, 0)), row(d), gsp, gsp],
        out_specs=[row(d), row(d)],
        out_shape=[jax.ShapeDtypeStruct((m, d), F32), jax.ShapeDtypeStruct((m, d), BF16)],
        compiler_params=_params(1), name="proj_norm_residual")(y, w, x, g_post.reshape(1, d), g_next.reshape(1, d))


FFN_SPLIT = 4


def _ffn_kernel(h_ref, w1_ref, w2_ref, x_ref, g_ref, o_ref, a_ref):
    j = pl.program_id(1)
    last = pl.num_programs(1) - 1
    h = h_ref[...]
    piece = w1_ref.shape[1] // FFN_SPLIT
    for s in range(FFN_SPLIT):
        cols = slice(s * piece, (s + 1) * piece)
        a_ref[:, cols] = jnp.square(jnp.maximum(_dot(h, w1_ref[:, cols]), 0.0)).astype(BF16)
    y = _dot(a_ref[...], w2_ref[...])

    @pl.when(j == 0)
    def _():
        o_ref[...] = y

    @pl.when(jnp.logical_and(j > 0, j < last))
    def _():
        o_ref[...] += y

    @pl.when(j == last)
    def _():
        o_ref[...] = x_ref[...] + _rms(o_ref[...] + y, g_ref[...])


def _ffn(h, w1, w2, x, g, tm, tf):
    m, d = h.shape
    f = w1.shape[1]
    assert f // tf >= 2
    once = lambda shape: pl.BlockSpec(shape, lambda i, j: (i, 0), pipeline_mode=pl.Buffered(1))
    return pl.pallas_call(
        _ffn_kernel, grid=(m // tm, f // tf),
        in_specs=[pl.BlockSpec((tm, d), lambda i, j: (i, 0)), pl.BlockSpec((d, tf), lambda i, j: (0, j)),
                  pl.BlockSpec((tf, d), lambda i, j: (j, 0)), once((tm, d)), pl.BlockSpec((1, d), lambda i, j: (0, 0))],
        out_specs=once((tm, d)),
        out_shape=jax.ShapeDtypeStruct((m, d), F32),
        scratch_shapes=[pltpu.VMEM((tm, tf), BF16)],
        compiler_params=_params(2), name="ffn")(h, w1, w2, x, g.reshape(1, d))


def _row_tile(m, cap):
    t = min(m, cap)
    assert m % t == 0
    return t


def _layer(xp, xs, lw, layer, rel_bias, bias_prompt, mem_p, swa_caches, mem_s, conv_buf, s0):
    (w_in_t, w_conv, a_log, dt_bias, norm_delta, w_o_swa, w_o_delta, w_o_mem, w_out,
     norm_pre_mix, norm_post_mix, norm_pre_ffn, norm_post_ffn, w_ff1, w_ff2) = lw
    mp, ms, n_seq = xp.shape[0], xs.shape[0], s0.shape[0]
    seq = ms // n_seq
    p_width = P_CQ + w_in_t.shape[0] - CQ_OFF
    tm_s = _row_tile(ms, 1024)
    proj_s, h_p = _in_proj(xs, norm_pre_mix, w_in_t, tm_s, PROJ_TILE,
                           [_rmsnorm_task(xp, norm_pre_mix, (ms // tm_s) * (p_width // PROJ_TILE))])
    tm_p = _row_tile(mp, 2048)
    host_steps = (mp // tm_p) * (p_width // (2 * PROJ_TILE))
    casts = [_cast_task(w, host_steps) for w in (w_o_swa, w_o_delta, w_o_mem, w_out, w_ff1, w_ff2)]
    proj_p, w_o_swa, w_o_delta, w_o_mem, w_out, w_ff1, w_ff2 = _in_proj(
        h_p, None, w_in_t, tm_p, 2 * PROJ_TILE, casts)
    chunk = DN_CHUNK if mp % DN_CHUNK == 0 else mp
    scan_steps = mp // chunk
    per = -(-n_seq // scan_steps)
    riding = n_seq % per == 0 and per <= RIDER_SEQS_MAX
    swa_s = _swa_sample_task(proj_s, swa_caches, layer, rel_bias, seq, per if riding else 1)
    mem_s_task = _mem_attn_task(proj_s, *mem_s, layer, seq * (per if riding else math.gcd(n_seq, MEM_PER_STEP)), F32)
    riders = [swa_s, mem_s_task] if riding else []
    b_p, state_p, *rode = _delta_scan(proj_p, w_conv, a_log, dt_bias, norm_delta, None, None, chunk, riders)
    a_s, *kv_new_s, c_s = rode if riding else (*_run_task(swa_s), *_run_task(mem_s_task))
    a_p = _swa_prompt(proj_p, bias_prompt)
    c_p = _run_task(_mem_attn_task(proj_p, *mem_p, 0, _row_tile(mp, 512), BF16))[0]
    b_s, state_s, hist_s = _delta_scan(proj_s, w_conv, a_log, dt_bias, norm_delta, s0, conv_buf, seq)
    outs = []
    for x, proj, a, b, c in ((xp, proj_p, a_p, b_p, c_p), (xs, proj_s, a_s, b_s, c_s)):
        m = x.shape[0]
        tm = _row_tile(m, 1024)
        merged = _merge(a, b, c, proj, w_o_swa, w_o_delta, w_o_mem, tm, 512)
        x1, h2 = _proj_norm_residual(merged, w_out, x, norm_post_mix, norm_pre_ffn, _row_tile(m, 512))
        outs.append(_ffn(h2, w_ff1, w_ff2, x1, norm_post_ffn, tm, 1024))
    new_s = [t.reshape(n_seq, seq, SWA_HPG, HEAD_DIM) for t in kv_new_s] + [state_s, jnp.transpose(hist_s, (1, 0, 2))]
    return outs[0], outs[1], proj_p, state_p, new_s


def kernel(x_prompt, x_sample, cache_swa0_k, cache_swa0_v, cache_swa1_k, cache_swa1_v, cache_swa2_k, cache_swa2_v, state_delta, state_conv, cache_mem_k, cache_mem_v, mem_prompt, rel_bias, w_in, w_conv, A_log, dt_bias, norm_delta, norm_mem, w_mem_kv, w_o_swa, w_o_delta, w_o_mem, w_out, norm_pre_mix, norm_post_mix, norm_pre_ffn, norm_post_ffn, w_ff1, w_ff2):
    depth = w_in.shape[0]
    bp, tp, d = x_prompt.shape
    bs, ts, _ = x_sample.shape
    assert bp == 1 and ts > CONV_WIDTH - 1 and ts % 8 == 0
    sample_swa = (cache_swa0_k, cache_swa0_v, cache_swa1_k, cache_swa1_v, cache_swa2_k, cache_swa2_v)
    bias_prompt = _prompt_bias_tables(rel_bias)
    xp = x_prompt.reshape(bp * tp, d)
    xs = x_sample.reshape(bs * ts, d)
    new_p = [[] for _ in range(10)]
    new_s = [[] for _ in range(8)]
    hw = SWA_HPG * HEAD_DIM
    for l in range(depth):
        lw = (jnp.swapaxes(w_in, 1, 2)[l], w_conv[l], A_log[l], dt_bias[l], norm_delta[l], w_o_swa[l],
              w_o_delta[l], w_o_mem[l], w_out[l],
              norm_pre_mix[l], norm_post_mix[l], norm_pre_ffn[l], norm_post_ffn[l],
              w_ff1[l], w_ff2[l])
        mem = mem_prompt.reshape(-1, d)
        mkv = _matmul(_rmsnorm_cast(mem, norm_mem[l], _row_tile(mem.shape[0], 256)), w_mem_kv[l],
                      2 * MEM_WIDTH, _row_tile(mem.shape[0], 256), 512, "mem_kv")
        mk = mkv[:, :MEM_WIDTH].reshape(1, bp, -1, MEM_HEADS, MEM_HEAD_DIM)
        mv = mkv[:, MEM_WIDTH:].reshape(1, bp, -1, MEM_HEADS, MEM_HEAD_DIM)
        xp, xs, main_p, s_p, vals_s = _layer(
            xp, xs, lw, l, rel_bias, bias_prompt, (mk, mv), sample_swa, (cache_mem_k, cache_mem_v),
            state_conv[l], state_delta[l])
        vals_p = []
        for g, (window, _) in enumerate(SWA_GROUPS):
            keep = min(window, tp)
            for sec in (1, 2):
                c0 = sec * SWA_WIDTH + g * hw
                vals_p.append(main_p[tp - keep:, c0:c0 + hw].reshape(bp, keep, SWA_HPG, HEAD_DIM))
        vals_p.append(s_p.reshape(bp, DN_HEADS, HEAD_DIM, HEAD_DIM))
        vals_p.append(main_p[tp - (CONV_WIDTH - 1):, B_OFF:B_OFF + 3 * DN_WIDTH].reshape(bp, CONV_WIDTH - 1, -1))
        vals_p += [mk[0], mv[0]]
        for lst, val in zip(new_p, vals_p):
            lst.append(val)
        for lst, val in zip(new_s, vals_s):
            lst.append(val)
    outs_p = [jnp.stack(t) for t in new_p]
    outs_s = [jnp.stack(t) for t in new_s]
    return (xp.reshape(bp, tp, d), xs.reshape(bs, ts, d), *outs_p, *outs_s)
```

```python
import functools
import math

import numpy as np
import jax
import jax.numpy as jnp
from jax import lax
from jax.experimental import pallas as pl
from jax.experimental.pallas import tpu as pltpu

EPS = 1e-6
HEAD_DIM = 128
SWA_GROUPS = ((128, 1), (512, 4), (2048, 16))
SWA_SPAN = 128
SWA_HPG = 4
SWA_WIDTH = SWA_HPG * len(SWA_GROUPS) * HEAD_DIM
DN_HEADS = 12
DN_WIDTH = DN_HEADS * HEAD_DIM
CONV_WIDTH = 4
DN_CHUNK = 128
MEM_HEADS = 4
MEM_HEAD_DIM = 256
MEM_WIDTH = MEM_HEADS * MEM_HEAD_DIM
N_BUCKETS = 32
MAX_DISTANCE = 2048
N_BRANCHES = 3

A_OFF = 0
B_OFF = 3 * SWA_WIDTH
Z_OFF = B_OFF + 3 * DN_WIDTH
BA_OFF = Z_OFF + DN_WIDTH
CQ_OFF = BA_OFF + 2 * DN_HEADS
GATE_OFF = CQ_OFF + MEM_WIDTH
LANES = 128
SUBLANES = 8
PROJ_TILE = 512
P_BA = BA_OFF
P_CQ = P_BA + PROJ_TILE
P_GATE = P_CQ + MEM_WIDTH
SWA_ROWS = 2048
SWA_UNROLL = 16
NEG = -1e30
VMEM_LIMIT = 61 * 1024 * 1024

BF16 = jnp.bfloat16
F32 = jnp.float32


def _params(n_grid):
    return pltpu.CompilerParams(dimension_semantics=("arbitrary",) * n_grid, vmem_limit_bytes=VMEM_LIMIT)


def _dot(a, b):
    return jnp.dot(a, b, preferred_element_type=F32)


def _dot_nt(a, b):
    return lax.dot_general(a, b, (((1,), (1,)), ((), ())), preferred_element_type=F32)


def _dot_tn(a, b):
    return lax.dot_general(a, b, (((0,), (0,)), ((), ())), preferred_element_type=F32)


def _split2(x):
    hi = x.astype(BF16)
    lo = (x - hi.astype(F32)).astype(BF16)
    return hi, lo


def _dot3(a, b, dot=_dot):
    ah, al = _split2(a)
    bh, bl = _split2(b)
    return dot(ah, bh) + (dot(ah, bl) + dot(al, bh))


def _rmsnorm_cast_kernel(x_ref, g_ref, o_ref):
    x = x_ref[...]
    y = x * lax.rsqrt(jnp.mean(x * x, axis=-1, keepdims=True) + EPS)
    o_ref[...] = (y * g_ref[...]).astype(o_ref.dtype)


def _rmsnorm_cast(x, g, tm):
    m, d = x.shape
    return pl.pallas_call(
        _rmsnorm_cast_kernel, grid=(m // tm,),
        in_specs=[pl.BlockSpec((tm, d), lambda i: (i, 0)), pl.BlockSpec((1, d), lambda i: (0, 0))],
        out_specs=pl.BlockSpec((tm, d), lambda i: (i, 0)),
        out_shape=jax.ShapeDtypeStruct((m, d), BF16),
        compiler_params=_params(1), name="rmsnorm_cast")(x, g.reshape(1, d))


def _matmul_kernel(x_ref, w_ref, o_ref):
    o_ref[...] = _dot(x_ref[...], w_ref[...].astype(BF16)).astype(o_ref.dtype)


def _matmul(x, w, n, tm, tn, name):
    m, k = x.shape
    assert n % tn == 0 and m % tm == 0
    return pl.pallas_call(
        _matmul_kernel, grid=(m // tm, n // tn),
        in_specs=[pl.BlockSpec((tm, k), lambda i, j: (i, 0)), pl.BlockSpec((k, tn), lambda i, j: (0, j))],
        out_specs=pl.BlockSpec((tm, tn), lambda i, j: (i, j)),
        out_shape=jax.ShapeDtypeStruct((m, n), F32),
        compiler_params=_params(2), name=name)(x, w)


def _in_proj_kernel(x_ref, g_ref, w_ref, o_ref, h_ref):
    @pl.when(pl.program_id(1) == 0)
    def _():
        h_ref[...] = _rms(x_ref[...], g_ref[...]).astype(BF16)

    o_ref[...] = _dot_nt(h_ref[...], w_ref[...].astype(BF16))


def _in_proj_normed_kernel(h_ref, w_ref, o_ref):
    o_ref[...] = _dot_nt(h_ref[...], w_ref[...].astype(BF16))


def _in_proj(x, g, wt, tm, tn, riders=()):
    m, k = x.shape
    n = P_CQ + wt.shape[0] - CQ_OFF
    assert n % tn == 0 and m % tm == 0 and P_CQ % tn == 0 and CQ_OFF % SUBLANES == 0

    def w_rows(i, j):
        return (pl.multiple_of(jnp.where(j * tn < P_CQ, j * tn, j * tn - P_CQ + CQ_OFF), SUBLANES), 0)

    x_spec = pl.BlockSpec((tm, k), lambda i, j: (i, 0), pipeline_mode=pl.Buffered(1))
    w_spec = pl.BlockSpec((pl.Element(tn), pl.Element(k)), w_rows)
    out = ([jax.ShapeDtypeStruct((m, n), F32)], [pl.BlockSpec((tm, tn), lambda i, j: (i, j))])
    if g is None:
        return _hosted_call("in_proj", _in_proj_normed_kernel, (m // tm, n // tn), [x, wt], [x_spec, w_spec],
                            *out, [], riders)
    return _hosted_call(
        "in_proj", _in_proj_kernel, (m // tm, n // tn), [x, g.reshape(1, k), wt],
        [x_spec, pl.BlockSpec((1, k), lambda i, j: (0, 0)), w_spec], *out, [pltpu.VMEM((tm, k), BF16)], riders)


def _rmsnorm_task(x, g, max_steps):
    rows, k = x.shape
    steps = max(s for s in range(1, min(max_steps, rows // BF16_ROWS) + 1) if rows % (s * BF16_ROWS) == 0)
    slab = pl.BlockSpec((rows // steps, k), lambda i: (i, 0))

    def body(x_ref, g_ref, o_ref):
        o_ref[...] = _rms(x_ref[...], g_ref[...]).astype(BF16)

    return _Task("rmsnorm", body, steps, [x, g.reshape(1, k)], [slab, pl.BlockSpec((1, k), lambda i: (0, 0))],
                 [jax.ShapeDtypeStruct((rows, k), BF16)], [slab])


def _t5_bucket(dist):
    max_exact = N_BUCKETS // 2
    df = jnp.maximum(dist, 1).astype(F32)
    large = max_exact + (jnp.log(df / max_exact) / math.log(MAX_DISTANCE / max_exact)
                         * (N_BUCKETS - max_exact)).astype(jnp.int32)
    return jnp.where(dist < max_exact, dist, jnp.minimum(large, N_BUCKETS - 1))


def _group_bias(rel_bias, g):
    _, dil = SWA_GROUPS[g]
    dist = jnp.arange(SWA_SPAN + 1, dtype=jnp.int32) * dil
    onehot = _t5_bucket(dist)[None, :, None] == jnp.arange(N_BUCKETS, dtype=jnp.int32)[None, None, :]
    heads = jnp.transpose(rel_bias[:, g * SWA_HPG:(g + 1) * SWA_HPG].astype(F32))
    return jnp.sum(jnp.where(onehot, heads[:, None, :], 0.0), axis=-1)


def _prompt_bias_tables(rel_bias):
    span = SWA_SPAN
    first = (np.arange(2 * span) >= span)[None, None, :]
    out = []
    for g in range(len(SWA_GROUPS)):
        w = _group_bias(rel_bias, g)
        h = w.shape[0]
        p = 3 * span
        e = jnp.concatenate([jnp.full((h, span - 1), NEG, F32), w[:, ::-1], jnp.full((h, p - 2 * span), NEG, F32)], 1)
        skew = jnp.broadcast_to(e[:, None, :], (h, span, p)).reshape(h, span * p)[:, :span * (p - 1)]
        tab = skew.reshape(h, span, p - 1)[:, :, span - 1:3 * span - 1]
        out.append(jnp.stack([jnp.where(first, tab, NEG), tab], axis=1))
    return jnp.stack(out)


def _sample_bias_tables(rel_bias, g, n_res, n_new, cache_len):
    _, dil = SWA_GROUPS[g]
    w = _group_bias(rel_bias, g)
    h = w.shape[0]
    lm = cache_len // dil
    assert cache_len % dil == 0
    p = w.shape[1] + lm + n_new
    wrev = jnp.concatenate([w, jnp.full((h, lm + n_new), NEG, F32)], axis=1)[:, ::-1]
    s = np.arange(n_new)
    res = np.arange(n_res)
    tab = jnp.full((h, n_new, lm, n_res), NEG, F32)
    for q in range((n_new - 1) // dil + 1):
        vec = wrev[:, p - 1 - lm - q:p - 1 - q]
        mask = (s // dil == q)[:, None] & (res[None, :] == (s % dil)[:, None])
        tab = jnp.where(mask[None, :, None, :], vec[:, None, :, None], tab)
    dist = s[:, None] - s[None, :]
    new = jnp.full((h, n_new, n_new), NEG, F32)
    for j in range(min((n_new - 1) // dil, SWA_SPAN) + 1):
        new = jnp.where((dist == j * dil)[None], w[:, j][:, None, None], new)
    return tab.reshape(h, n_new, lm * n_res), new


def _swa_prompt_kernel(*refs, n_groups):
    ins = refs[:5 * n_groups]
    bias_ref = refs[5 * n_groups]
    o_ref = refs[5 * n_groups + 1]
    state_refs = refs[5 * n_groups + 2:7 * n_groups + 2]
    scr = refs[7 * n_groups + 2:]
    kext, vext, og, lg = scr[:n_groups], scr[n_groups:2 * n_groups], scr[2 * n_groups:3 * n_groups], scr[3 * n_groups:]
    n = pl.program_id(0)
    scale = HEAD_DIM ** -0.5
    span = SWA_SPAN
    for g in range(n_groups):
        _, dil = SWA_GROUPS[g]
        q_ref, kc_ref, vc_ref, kp_ref, vp_ref = ins[5 * g:5 * g + 5]
        blk = span * dil
        kext[g][0:blk, :] = kp_ref[...]
        kext[g][blk:, :] = kc_ref[...]
        vext[g][0:blk, :] = vp_ref[...]
        vext[g][blk:, :] = vc_ref[...]

        def body(it, carry, g=g, dil=dil, blk=blk, q_ref=q_ref):
            items = range(SWA_UNROLL)
            starts, sels = [], []
            for u in items:
                idx = it * SWA_UNROLL + u
                b = idx // dil
                starts.append(b * blk + (idx - b * dil))
                sels.append(jnp.where(jnp.logical_and(n == 0, b == 0), 0, 1))
            q = [q_ref[pl.ds(starts[u], span, stride=dil), :].astype(BF16) for u in items]
            kk = [kext[g][pl.ds(starts[u], 2 * span, stride=dil), :].astype(BF16) for u in items]
            s = [_dot_nt(q[u], kk[u]) * scale + bias_ref[g, 0, sels[u]] for u in items]
            m = [jnp.max(s[u], axis=-1, keepdims=True) for u in items]
            p = [jnp.exp(s[u] - m[u]) for u in items]
            den = [jnp.sum(p[u], axis=-1, keepdims=True) for u in items]
            vv = [vext[g][pl.ds(starts[u], 2 * span, stride=dil), :].astype(BF16) for u in items]
            o = [_dot(p[u].astype(BF16), vv[u]) / den[u] for u in items]
            for u in items:
                og[g][pl.ds(starts[u], span, stride=dil), :] = o[u]
                lg[g][pl.ds(starts[u], span, stride=dil), :] = jnp.broadcast_to(
                    m[u] + jnp.log(den[u]), (span, HEAD_DIM))
            return carry

        lax.fori_loop(0, SWA_ROWS // span // SWA_UNROLL, body, 0)
    lses = [lg[g][...] for g in range(n_groups)]
    mx = functools.reduce(jnp.maximum, lses)
    ws = [jnp.exp(l - mx) for l in lses]
    num = functools.reduce(lambda a, b: a + b, [w * og[g][...] for g, w in enumerate(ws)])
    o_ref[...] = (num / functools.reduce(lambda a, b: a + b, ws)).astype(o_ref.dtype)

    @pl.when(n == pl.num_programs(0) - 1)
    def _():
        head = pl.program_id(1)
        for g in range(n_groups):
            for sec in (1, 2):
                dst = state_refs[2 * g + sec - 1]
                keep = dst.shape[0] // SWA_HPG
                dst[pl.ds(head, keep, stride=SWA_HPG), :] = ins[5 * g + sec][SWA_ROWS - keep:, :]


def _swa_prompt(proj, bias_tables):
    t = proj.shape[0]
    n_groups = len(SWA_GROUPS)
    assert t % SWA_ROWS == 0 and all(w <= SWA_ROWS for w, _ in SWA_GROUPS)
    state_shape = [(min(w, t) * SWA_HPG, HEAD_DIM) for w, _ in SWA_GROUPS for _ in range(2)]
    in_specs, scratch_k, scratch_o = [], [], []
    for g, (_, dil) in enumerate(SWA_GROUPS):
        blk = SWA_SPAN * dil
        per = SWA_ROWS // blk
        assert SWA_ROWS % blk == 0
        qc, kc, vc = (sec * (SWA_WIDTH // HEAD_DIM) + g * SWA_HPG for sec in range(3))
        cur = lambda col: pl.BlockSpec((SWA_ROWS, HEAD_DIM), lambda n, h, col=col: (n, col + h))
        prev = lambda col, blk=blk, per=per: pl.BlockSpec(
            (blk, HEAD_DIM), lambda n, h, col=col, per=per: (jnp.maximum(n * per - 1, 0), col + h))
        in_specs += [cur(qc), cur(kc), cur(vc), prev(kc), prev(vc)]
        scratch_k.append(pltpu.VMEM((SWA_ROWS + blk, HEAD_DIM), F32))
        scratch_o.append(pltpu.VMEM((SWA_ROWS, HEAD_DIM), F32))
    in_specs.append(pl.BlockSpec((n_groups, 1, 2, SWA_SPAN, 2 * SWA_SPAN), lambda n, h: (0, h, 0, 0, 0)))
    return pl.pallas_call(
        functools.partial(_swa_prompt_kernel, n_groups=n_groups),
        grid=(t // SWA_ROWS, SWA_HPG),
        in_specs=in_specs,
        out_specs=[pl.BlockSpec((SWA_ROWS, HEAD_DIM), lambda n, h: (n, h))]
        + [pl.BlockSpec(s, lambda n, h: (0, 0)) for s in state_shape],
        out_shape=[jax.ShapeDtypeStruct((t, SWA_HPG * HEAD_DIM), BF16)]
        + [jax.ShapeDtypeStruct(s, F32) for s in state_shape],
        scratch_shapes=scratch_k + scratch_k + scratch_o + scratch_o,
        compiler_params=_params(2), name="swa_prompt")(*([proj] * (5 * n_groups)), bias_tables)


def _swa_sample_kernel(*refs, n_groups):
    qkv_ref = refs[0]
    caches = refs[1:1 + 2 * n_groups]
    biases = refs[1 + 2 * n_groups:1 + 4 * n_groups]
    o_ref = refs[1 + 4 * n_groups]
    new_refs = refs[2 + 4 * n_groups:]
    scale = HEAD_DIM ** -0.5
    nseq = caches[0].shape[1]
    n_new = qkv_ref.shape[0] // nseq
    items = [(b, g, h) for b in range(nseq) for h in range(SWA_HPG) for g in range(n_groups)]
    idx = range(len(items))

    def new(sec, b, g, h):
        col = sec * SWA_WIDTH + (g * SWA_HPG + h) * HEAD_DIM
        return qkv_ref[b * n_new:(b + 1) * n_new, col:col + HEAD_DIM].astype(BF16)

    def cached(ref, b, h):
        nk = math.prod(ref.shape[2:-1]) // SWA_HPG
        if len(ref.shape) == 5:
            pick = (0, b, slice(None), pl.ds(h, ref.shape[3] // SWA_HPG, stride=SWA_HPG), slice(None))
        else:
            pick = (0, b, pl.ds(h, nk, stride=SWA_HPG), slice(None))
        return ref[pick].reshape(nk, HEAD_DIM).astype(BF16)

    q = [new(0, *it) for it in items]
    sc = [_dot_nt(q[i], cached(caches[2 * g], b, h)) * scale + biases[2 * g][h]
          for i, (b, g, h) in enumerate(items)]
    sn = [_dot_nt(q[i], new(1, b, g, h)) * scale + biases[2 * g + 1][h] for i, (b, g, h) in enumerate(items)]
    m = [jnp.maximum(jnp.max(sc[i], axis=-1, keepdims=True), jnp.max(sn[i], axis=-1, keepdims=True)) for i in idx]
    pc = [jnp.exp(sc[i] - m[i]) for i in idx]
    pn = [jnp.exp(sn[i] - m[i]) for i in idx]
    den = [jnp.sum(pc[i], axis=-1, keepdims=True) + jnp.sum(pn[i], axis=-1, keepdims=True) for i in idx]
    o = [(_dot(pc[i].astype(BF16), cached(caches[2 * g + 1], b, h)) + _dot(pn[i].astype(BF16), new(2, b, g, h)))
         / den[i] for i, (b, g, h) in enumerate(items)]
    lse = [m[i] + jnp.log(den[i]) for i in idx]
    for b in range(nseq):
        for h in range(SWA_HPG):
            mine = [i for i in idx if items[i][0] == b and items[i][2] == h]
            mx = functools.reduce(jnp.maximum, [lse[i] for i in mine])
            ws = [jnp.exp(lse[i] - mx) for i in mine]
            num = functools.reduce(lambda x, y: x + y, [w * o[i] for w, i in zip(ws, mine)])
            o_ref[b * n_new:(b + 1) * n_new, h * HEAD_DIM:(h + 1) * HEAD_DIM] = (
                num / functools.reduce(lambda x, y: x + y, ws))
    for b, g, h in items:
        for sec in (1, 2):
            col = sec * SWA_WIDTH + (g * SWA_HPG + h) * HEAD_DIM
            new_refs[2 * g + sec - 1][b, pl.ds(h, n_new, stride=SWA_HPG), :] = (
                qkv_ref[b * n_new:(b + 1) * n_new, col:col + HEAD_DIM])


class _Task:
    def __init__(self, name, body, steps, args, in_specs, out_shapes, out_specs):
        self.name, self.body, self.steps = name, body, steps
        self.args, self.in_specs, self.out_shapes, self.out_specs = args, in_specs, out_shapes, out_specs


def _run_task(task):
    return pl.pallas_call(
        task.body, grid=(task.steps,), in_specs=task.in_specs, out_specs=task.out_specs,
        out_shape=task.out_shapes, compiler_params=_params(1), name=task.name)(*task.args)


def _riding_specs(task, grid):
    assert math.prod(grid) >= task.steps

    def step_of(*idx):
        flat = functools.reduce(lambda acc, t: acc * t[1] + t[0], zip(idx[1:], grid[1:]), idx[0])
        return flat if math.prod(grid) == task.steps else jnp.minimum(flat, task.steps - 1)

    re = lambda spec: pl.BlockSpec(spec.block_shape, lambda *idx: spec.index_map(step_of(*idx)))
    return [re(s) for s in task.in_specs], [re(s) for s in task.out_specs]


BF16_ROWS = 16


def _cast_task(w, max_steps):
    rows, cols = w.shape
    steps = max(s for s in range(1, min(max_steps, rows // BF16_ROWS) + 1) if rows % (s * BF16_ROWS) == 0)
    slab = pl.BlockSpec((rows // steps, cols), lambda i: (i, 0))

    def body(x_ref, o_ref):
        o_ref[...] = x_ref[...].astype(BF16)

    return _Task("cast", body, steps, [w], [slab], [jax.ShapeDtypeStruct((rows, cols), BF16)], [slab])


def _swa_sample_task(proj, caches, layer, rel_bias, n_new, per):
    rows_total = proj.shape[0]
    nb = rows_total // n_new
    assert nb % per == 0
    n_groups = len(SWA_GROUPS)
    hw = SWA_HPG * HEAD_DIM
    cache_in, cache_specs, bias_in, bias_specs = [], [], [], []
    for g, (_, dil) in enumerate(SWA_GROUPS):
        depth, _, cache_len, heads, hd = caches[2 * g].shape
        if dil >= 2 * n_new and n_new % 8 == 0:
            n_res = n_new
            shape = (depth, nb, cache_len // dil, dil * heads, hd)
            spec = pl.BlockSpec((1, per, cache_len // dil, n_res * heads, hd), lambda b: (layer, b, 0, 0, 0))
        else:
            n_res = dil
            shape = (depth, nb, cache_len * heads, hd)
            spec = pl.BlockSpec((1, per, cache_len * heads, hd), lambda b: (layer, b, 0, 0))
        for c in caches[2 * g:2 * g + 2]:
            cache_in.append(c.reshape(shape))
            cache_specs.append(spec)
        for tab in _sample_bias_tables(rel_bias, g, n_res, n_new, cache_len):
            bias_in.append(tab)
            bias_specs.append(pl.BlockSpec(tab.shape, lambda b: (0, 0, 0)))
    return _Task(
        "swa_sample", functools.partial(_swa_sample_kernel, n_groups=n_groups), nb // per,
        [proj, *cache_in, *bias_in],
        [pl.BlockSpec((per * n_new, 3 * SWA_WIDTH), lambda b: (b, 0))] + cache_specs + bias_specs,
        [jax.ShapeDtypeStruct((rows_total, hw), F32)]
        + [jax.ShapeDtypeStruct((nb, n_new * SWA_HPG, HEAD_DIM), F32)] * (2 * n_groups),
        [pl.BlockSpec((per * n_new, hw), lambda b: (b, 0))]
        + [pl.BlockSpec((per, n_new * SWA_HPG, HEAD_DIM), lambda b: (b, 0, 0))] * (2 * n_groups))


def _silu(x):
    return x * (1.0 / (1.0 + jnp.exp(-x)))


def _softplus(x):
    return jnp.maximum(x, 0.0) + jnp.log(1.0 + jnp.exp(-jnp.abs(x)))


def _inv_dot(a, b):
    return _dot(a.astype(BF16), b.astype(BF16))


HALO = 8
DELTA_SEQS_PER_STEP = 8


def _delta_scan_kernel(*refs, chunk, nseq, carry):
    if carry:
        x_ref, z_ref, ba_ref, wc_ref, alog_ref, dt_ref, gain_ref, o_ref, s_out_ref, s_scr, halo_scr = refs
        s_in_ref = halo_ref = None
    else:
        (x_ref, z_ref, ba_ref, wc_ref, alog_ref, dt_ref, gain_ref, s_in_ref, halo_ref,
         o_ref, s_out_ref, hist_out_ref) = refs
        s_scr = halo_scr = None
        x3 = x_ref[...].reshape(nseq, chunk, x_ref.shape[1])
        for t in range(CONV_WIDTH - 1):
            hist_out_ref[t] = x3[:, chunk - (CONV_WIDTH - 1) + t, :]
    c = chunk
    rows = nseq * c
    seqs = range(nseq)
    part = lambda t, b: t[b * c:(b + 1) * c]
    if carry:
        @pl.when(pl.program_id(0) == 0)
        def _():
            s_scr[...] = jnp.zeros_like(s_scr)
            halo_scr[...] = jnp.zeros_like(halo_scr)

    x = x_ref[...]
    acc = x * wc_ref[CONV_WIDTH - 1:CONV_WIDTH, :]
    if carry:
        xp = jnp.concatenate([halo_scr[...], x], axis=0)
        for t in range(1, CONV_WIDTH):
            acc = acc + pltpu.roll(xp, t, 0)[HALO:] * wc_ref[CONV_WIDTH - 1 - t:CONV_WIDTH - t, :]
        halo_scr[...] = x[c - HALO:, :]
    else:
        width = x.shape[1]
        rep = [jnp.broadcast_to(halo_ref[k][:, None, :], (nseq, c, width)).reshape(rows, width)
               for k in range(CONV_WIDTH - 1)]
        pos = lax.broadcasted_iota(jnp.int32, (rows, 1), 0) % c
        for t in range(1, CONV_WIDTH):
            hv = rep[CONV_WIDTH - 1 - t]
            for s in range(1, t):
                hv = jnp.where(pos == s, rep[CONV_WIDTH - 1 + s - t], hv)
            acc = acc + jnp.where(pos >= t, pltpu.roll(x, t, 0), hv) * wc_ref[CONV_WIDTH - 1 - t:CONV_WIDTH - t, :]
    qkv = _silu(acc)

    ba = ba_ref[...]
    beta_all = 1.0 / (1.0 + jnp.exp(-ba))
    g_all = -jnp.exp(alog_ref[...]) * _softplus(ba + dt_ref[...])
    row = lax.broadcasted_iota(jnp.int32, (rows, rows), 0)
    col = lax.broadcasted_iota(jnp.int32, (rows, rows), 1)
    same = (row // c) == (col // c)
    tri = (row >= col) & same
    strict = (row > col) & same
    tril = tri.astype(BF16)
    g1 = g_all.astype(BF16)
    r1 = g_all - g1.astype(F32)
    g2 = r1.astype(BF16)
    g3 = (r1 - g2.astype(F32)).astype(BF16)
    gcum_all = _dot(tril, g1) + (_dot(tril, g2) + _dot(tril, g3))
    gcum_t = gcum_all.T
    eye = (row == col).astype(F32)

    heads = range(DN_HEADS)
    hs = lambda sec, h: slice(sec * DN_WIDTH + h * HEAD_DIM, sec * DN_WIDTH + (h + 1) * HEAD_DIM)
    lane = lambda h: slice(DN_HEADS + h, DN_HEADS + h + 1)
    l2 = lambda t: t * lax.rsqrt(jnp.sum(t * t, axis=-1, keepdims=True) + EPS)
    q = [l2(qkv[:, hs(0, h)]) * (HEAD_DIM ** -0.5) for h in heads]
    k = [l2(qkv[:, hs(1, h)]) for h in heads]
    v = [qkv[:, hs(2, h)] for h in heads]
    bcol = [beta_all[:, h:h + 1] for h in heads]
    gc = [gcum_all[:, lane(h)] for h in heads]
    glast = [[gcum_all[(b + 1) * c - 1:(b + 1) * c, lane(h)] for b in seqs] for h in heads]
    decay = [jnp.exp(jnp.where(tri, gc[h] - gcum_t[lane(h), :], NEG)) for h in heads]
    kb = [k[h] * bcol[h] for h in heads]
    k16 = [k[h].astype(BF16) for h in heads]
    nmat = [jnp.where(strict, _dot_nt(kb[h].astype(BF16), k16[h]) * decay[h], 0.0) for h in heads]
    attn = [jnp.where(tri, _dot_nt(q[h].astype(BF16), k16[h]) * decay[h], 0.0) for h in heads]
    inv = [eye] * DN_HEADS
    blk = 1
    while blk < c:
        pair = (row // (2 * blk) == col // (2 * blk)) & (row % (2 * blk) >= blk) & (col % (2 * blk) < blk)
        low = [jnp.where(pair, nmat[h], 0.0) for h in heads]
        if blk == 1:
            inv = [eye - low[h] for h in heads]
        else:
            t = [_inv_dot(low[h], inv[h]) for h in heads]
            inv = [inv[h] - _inv_dot(inv[h], t[h]) for h in heads]
        blk *= 2
    eg = [jnp.exp(gc[h]) for h in heads]
    rhs = [jnp.concatenate([v[h] * bcol[h], kb[h] * eg[h]], axis=1) for h in heads]
    sol = [_inv_dot(inv[h], rhs[h]) for h in heads]
    qdec = [q[h] * eg[h] for h in heads]
    hb = [(h, b) for h in heads for b in seqs]
    s_prev = {(h, b): (s_scr[h] if carry else s_in_ref[b, h]) for h, b in hb}
    wq = {(h, b): jnp.concatenate([part(sol[h][:, HEAD_DIM:], b), part(qdec[h], b)], axis=0).astype(BF16)
          for h, b in hb}
    ws = {i: _dot(wq[i], s_prev[i].astype(BF16)) for i in hb}
    vn = {(h, b): part(sol[h][:, :HEAD_DIM], b) - ws[h, b][:c] for h, b in hb}
    v_new = [jnp.concatenate([vn[h, b] for b in seqs], axis=0).astype(BF16) for h in heads]
    o = [jnp.concatenate([ws[h, b][c:] for b in seqs], axis=0) + _dot(attn[h].astype(BF16), v_new[h])
         for h in heads]
    kdec = {(h, b): (part(k[h], b) * jnp.exp(glast[h][b] - part(gc[h], b))).astype(BF16) for h, b in hb}
    s_new = {(h, b): s_prev[h, b] * jnp.exp(glast[h][b]) + _dot_tn(kdec[h, b], vn[h, b].astype(BF16))
             for h, b in hb}
    for h in heads:
        if carry:
            s_scr[h] = s_new[h, 0]
            s_out_ref[h] = s_new[h, 0]
        else:
            for b in seqs:
                s_out_ref[b, h] = s_new[h, b]
        y = o[h] * lax.rsqrt(jnp.mean(o[h] * o[h], axis=-1, keepdims=True) + EPS) * gain_ref[...]
        o_ref[:, hs(0, h)] = (y * _silu(z_ref[:, hs(0, h)])).astype(o_ref.dtype)


def _delta_scan(proj, w_conv, a_log, dt_bias, norm_delta, s0, conv_buf, chunk, riders=()):
    m = proj.shape[0]
    carry = s0 is None
    nseq = 1 if carry else math.gcd(m // chunk, DELTA_SEQS_PER_STEP)
    rows = nseq * chunk
    n = m // rows
    width = 3 * DN_WIDTH
    assert chunk >= HALO and B_OFF % width == 0
    pad = lambda v_, off: jnp.zeros((1, LANES), F32).at[0, off:off + DN_HEADS].set(v_.astype(F32))
    consts = [w_conv, pad(a_log, DN_HEADS), pad(dt_bias, DN_HEADS), norm_delta.reshape(1, HEAD_DIM).astype(F32)]
    const_specs = [pl.BlockSpec((CONV_WIDTH, width), lambda i: (0, 0))] + [pl.BlockSpec((1, LANES), lambda i: (0, 0))] * 3
    in_specs = [pl.BlockSpec((rows, width), lambda i: (i, B_OFF // width)),
                pl.BlockSpec((rows, DN_WIDTH), lambda i: (i, Z_OFF // DN_WIDTH)),
                pl.BlockSpec((rows, LANES), lambda i: (i, P_BA // LANES))] + const_specs
    args = [proj, proj, proj] + consts
    state = (DN_HEADS, HEAD_DIM, HEAD_DIM)
    if carry:
        out_dtype = BF16
        s_shape, s_spec = state, pl.BlockSpec(state, lambda i: (0, 0, 0))
        scratch = [pltpu.VMEM(state, F32), pltpu.VMEM((HALO, width), F32)]
    else:
        out_dtype = F32
        in_specs += [pl.BlockSpec((nseq,) + state, lambda i: (i, 0, 0, 0)),
                     pl.BlockSpec((CONV_WIDTH - 1, nseq, width), lambda i: (0, i, 0))]
        args += [s0, jnp.transpose(conv_buf, (1, 0, 2))]
        s_shape, s_spec = (m // chunk,) + state, pl.BlockSpec((nseq,) + state, lambda i: (i, 0, 0, 0))
        scratch = []
    out_specs = [pl.BlockSpec((rows, DN_WIDTH), lambda i: (i, 0)), s_spec]
    out_shape = [jax.ShapeDtypeStruct((m, DN_WIDTH), out_dtype), jax.ShapeDtypeStruct(s_shape, F32)]
    if not carry:
        out_specs.append(pl.BlockSpec((CONV_WIDTH - 1, nseq, width), lambda i: (0, i, 0)))
        out_shape.append(jax.ShapeDtypeStruct((CONV_WIDTH - 1, m // chunk, width), F32))
    host = functools.partial(_delta_scan_kernel, chunk=chunk, nseq=nseq, carry=carry)
    return _hosted_call("delta_scan", host, (n,), args, in_specs, out_shape, out_specs, scratch, riders)


def _hosted_kernel(*refs, host, n_host_in, n_host_out, riders):
    n_in = n_host_in + sum(len(t.in_specs) for t in riders)
    n_out = n_host_out + sum(len(t.out_specs) for t in riders)
    ins, outs, scratch = refs[:n_in], refs[n_in:n_in + n_out], refs[n_in + n_out:]
    host(*ins[:n_host_in], *outs[:n_host_out], *scratch)
    pi, po = n_host_in, n_host_out
    for t in riders:
        ni, no = len(t.in_specs), len(t.out_specs)
        t.body(*ins[pi:pi + ni], *outs[po:po + no])
        pi, po = pi + ni, po + no


def _hosted_call(name, host, grid, args, in_specs, out_shape, out_specs, scratch, riders):
    body = functools.partial(_hosted_kernel, host=host, n_host_in=len(in_specs), n_host_out=len(out_specs),
                             riders=tuple(riders))
    args, in_specs, out_shape, out_specs = list(args), list(in_specs), list(out_shape), list(out_specs)
    for t in riders:
        r_in, r_out = _riding_specs(t, grid)
        args, in_specs = args + list(t.args), in_specs + r_in
        out_shape, out_specs = out_shape + list(t.out_shapes), out_specs + r_out
    return pl.pallas_call(
        body, grid=grid, in_specs=in_specs, out_specs=out_specs, out_shape=out_shape, scratch_shapes=scratch,
        compiler_params=_params(len(grid)), name=name)(*args)


def _mem_attn_kernel(q_ref, k_ref, v_ref, o_ref):
    scale = MEM_HEAD_DIM ** -0.5
    parts = MEM_HEAD_DIM // LANES
    period = MEM_HEADS * parts
    tokens = k_ref.shape[2] // period
    pick = lambda b, c, h: (0, b, pl.ds(c * MEM_HEADS + h, tokens, stride=period), slice(None))
    nb = k_ref.shape[1]
    rows = q_ref.shape[0] // nb
    items = [(b, h) for b in range(nb) for h in range(MEM_HEADS)]
    idx = range(len(items))
    q = [[q_ref[b * rows:(b + 1) * rows, h * MEM_HEAD_DIM + c * LANES:h * MEM_HEAD_DIM + (c + 1) * LANES]
          .astype(BF16) for c in range(parts)] for b, h in items]
    s = [functools.reduce(lambda x, y: x + y, [_dot_nt(q[i][c], k_ref[pick(b, c, h)].astype(BF16))
                                               for c in range(parts)]) * scale for i, (b, h) in enumerate(items)]
    p = [jnp.exp(s[i] - jnp.max(s[i], axis=-1, keepdims=True)) for i in idx]
    den = [jnp.sum(p[i], axis=-1, keepdims=True) for i in idx]
    for i, (b, h) in enumerate(items):
        for c in range(parts):
            col = h * MEM_HEAD_DIM + c * LANES
            o_ref[b * rows:(b + 1) * rows, col:col + LANES] = (
                _dot(p[i].astype(BF16), v_ref[pick(b, c, h)].astype(BF16)) / den[i]).astype(o_ref.dtype)


MEM_PER_STEP = 4
RIDER_SEQS_MAX = 2


def _mem_attn_task(proj, mem_k, mem_v, layer, tm, out_dtype):
    m = proj.shape[0]
    assert P_CQ % MEM_WIDTH == 0
    depth, nb, tokens, heads, hd = mem_k.shape
    per = m // nb
    parts = hd // LANES
    mem_k, mem_v = (t.reshape(depth, nb, tokens, heads, parts, LANES).transpose(0, 1, 2, 4, 3, 5)
                    .reshape(depth, nb, tokens * heads * parts, LANES) for t in (mem_k, mem_v))
    mems = max(tm // per, 1)
    kv_spec = pl.BlockSpec((1, mems, tokens * heads * parts, LANES), lambda i: (layer, i * tm // (per * mems), 0, 0))
    return _Task(
        "mem_attn", _mem_attn_kernel, m // tm, [proj, mem_k, mem_v],
        [pl.BlockSpec((tm, MEM_WIDTH), lambda i: (i, P_CQ // MEM_WIDTH)), kv_spec, kv_spec],
        [jax.ShapeDtypeStruct((m, MEM_WIDTH), out_dtype)], [pl.BlockSpec((tm, MEM_WIDTH), lambda i: (i, 0))])


def _merge_kernel(a_ref, b_ref, c_ref, ga_ref, gb_ref, gc_ref, wa_ref, wb_ref, wc_ref, o_ref):
    def branch(x_ref, g_ref, w_ref):
        y = _dot(x_ref[...].astype(BF16), w_ref[...])
        return y * (1.0 / (1.0 + jnp.exp(-g_ref[...])))

    o_ref[...] = (branch(a_ref, ga_ref, wa_ref) + branch(b_ref, gb_ref, wb_ref)
                  + branch(c_ref, gc_ref, wc_ref)).astype(o_ref.dtype)


def _merge(a, b, c, tail, w_a, w_b, w_c, tm, tn):
    m = a.shape[0]
    d = w_a.shape[1]
    gate0 = P_GATE // tn
    assert P_GATE % tn == 0 and d % tn == 0
    row = lambda x: pl.BlockSpec((tm, x.shape[1]), lambda i, j: (i, 0))
    gate = lambda br: pl.BlockSpec((tm, tn), lambda i, j, br=br: (i, gate0 + br * (d // tn) + j))
    wsp = lambda w: pl.BlockSpec((w.shape[0], tn), lambda i, j: (0, j))
    return pl.pallas_call(
        _merge_kernel, grid=(m // tm, d // tn),
        in_specs=[row(a), row(b), row(c), gate(0), gate(1), gate(2), wsp(w_a), wsp(w_b), wsp(w_c)],
        out_specs=pl.BlockSpec((tm, tn), lambda i, j: (i, j)),
        out_shape=jax.ShapeDtypeStruct((m, d), BF16),
        compiler_params=_params(2), name="merge")(a, b, c, tail, tail, tail, w_a, w_b, w_c)


def _rms(y, g):
    return y * lax.rsqrt(jnp.mean(y * y, axis=-1, keepdims=True) + EPS) * g


def _proj_norm_residual_kernel(y_ref, w_ref, x_ref, g_post_ref, g_next_ref, o_ref, h_ref):
    x1 = x_ref[...] + _rms(_dot(y_ref[...], w_ref[...]), g_post_ref[...])
    o_ref[...] = x1
    h_ref[...] = _rms(x1, g_next_ref[...]).astype(h_ref.dtype)


def _proj_norm_residual(y, w, x, g_post, g_next, tm):
    m, d = x.shape
    k = y.shape[1]
    row = lambda width: pl.BlockSpec((tm, width), lambda i: (i, 0))
    gsp = pl.BlockSpec((1, d), lambda i: (0, 0))
    return pl.pallas_call(
        _proj_norm_residual_kernel, grid=(m // tm,),
        in_specs=[row(k), pl.BlockSpec((k, d), lambda i: (0, 0)), row(d), gsp, gsp],
        out_specs=[row(d), row(d)],
        out_shape=[jax.ShapeDtypeStruct((m, d), F32), jax.ShapeDtypeStruct((m, d), BF16)],
        compiler_params=_params(1), name="proj_norm_residual")(y, w, x, g_post.reshape(1, d), g_next.reshape(1, d))


FFN_SPLIT = 4


def _ffn_kernel(h_ref, w1_ref, w2_ref, x_ref, g_ref, o_ref, a_ref):
    j = pl.program_id(1)
    last = pl.num_programs(1) - 1
    h = h_ref[...]
    piece = w1_ref.shape[1] // FFN_SPLIT
    for s in range(FFN_SPLIT):
        cols = slice(s * piece, (s + 1) * piece)
        a_ref[:, cols] = jnp.square(jnp.maximum(_dot(h, w1_ref[:, cols]), 0.0)).astype(BF16)
    y = _dot(a_ref[...], w2_ref[...])

    @pl.when(j == 0)
    def _():
        o_ref[...] = y

    @pl.when(jnp.logical_and(j > 0, j < last))
    def _():
        o_ref[...] += y

    @pl.when(j == last)
    def _():
        o_ref[...] = x_ref[...] + _rms(o_ref[...] + y, g_ref[...])


def _ffn(h, w1, w2, x, g, tm, tf):
    m, d = h.shape
    f = w1.shape[1]
    assert f // tf >= 2
    once = lambda shape: pl.BlockSpec(shape, lambda i, j: (i, 0), pipeline_mode=pl.Buffered(1))
    return pl.pallas_call(
        _ffn_kernel, grid=(m // tm, f // tf),
        in_specs=[pl.BlockSpec((tm, d), lambda i, j: (i, 0)), pl.BlockSpec((d, tf), lambda i, j: (0, j)),
                  pl.BlockSpec((tf, d), lambda i, j: (j, 0)), once((tm, d)), pl.BlockSpec((1, d), lambda i, j: (0, 0))],
        out_specs=once((tm, d)),
        out_shape=jax.ShapeDtypeStruct((m, d), F32),
        scratch_shapes=[pltpu.VMEM((tm, tf), BF16)],
        compiler_params=_params(2), name="ffn")(h, w1, w2, x, g.reshape(1, d))


def _row_tile(m, cap):
    t = min(m, cap)
    assert m % t == 0
    return t


def _layer(xp, xs, lw, layer, rel_bias, bias_prompt, mem_p, swa_caches, mem_s, conv_buf, s0):
    (w_in_t, w_conv, a_log, dt_bias, norm_delta, w_o_swa, w_o_delta, w_o_mem, w_out,
     norm_pre_mix, norm_post_mix, norm_pre_ffn, norm_post_ffn, w_ff1, w_ff2) = lw
    mp, ms, n_seq = xp.shape[0], xs.shape[0], s0.shape[0]
    seq = ms // n_seq
    p_width = P_CQ + w_in_t.shape[0] - CQ_OFF
    tm_s = _row_tile(ms, 1024)
    proj_s, h_p = _in_proj(xs, norm_pre_mix, w_in_t, tm_s, 2 * PROJ_TILE,
                           [_rmsnorm_task(xp, norm_pre_mix, (ms // tm_s) * (p_width // (2 * PROJ_TILE)))])
    tm_p = _row_tile(mp, 2048)
    host_steps = (mp // tm_p) * (p_width // (2 * PROJ_TILE))
    casts = [_cast_task(w, host_steps) for w in (w_o_swa, w_o_delta, w_o_mem, w_out, w_ff1, w_ff2)]
    proj_p, w_o_swa, w_o_delta, w_o_mem, w_out, w_ff1, w_ff2 = _in_proj(
        h_p, None, w_in_t, tm_p, 2 * PROJ_TILE, casts)
    chunk = DN_CHUNK if mp % DN_CHUNK == 0 else mp
    scan_steps = mp // chunk
    per = -(-n_seq // scan_steps)
    riding = n_seq % per == 0 and per <= RIDER_SEQS_MAX
    swa_s = _swa_sample_task(proj_s, swa_caches, layer, rel_bias, seq, per if riding else 1)
    mem_s_task = _mem_attn_task(proj_s, *mem_s, layer, seq * (per if riding else math.gcd(n_seq, MEM_PER_STEP)), F32)
    riders = [swa_s, mem_s_task] if riding else []
    b_p, state_p, *rode = _delta_scan(proj_p, w_conv, a_log, dt_bias, norm_delta, None, None, chunk, riders)
    a_s, *kv_new_s, c_s = rode if riding else (*_run_task(swa_s), *_run_task(mem_s_task))
    a_p, *kv_state_p = _swa_prompt(proj_p, bias_prompt)
    c_p = _run_task(_mem_attn_task(proj_p, *mem_p, 0, _row_tile(mp, 512), BF16))[0]
    b_s, state_s, hist_s = _delta_scan(proj_s, w_conv, a_log, dt_bias, norm_delta, s0, conv_buf, seq)
    outs = []
    for x, proj, a, b, c in ((xp, proj_p, a_p, b_p, c_p), (xs, proj_s, a_s, b_s, c_s)):
        m = x.shape[0]
        tm = _row_tile(m, 1024)
        merged = _merge(a, b, c, proj, w_o_swa, w_o_delta, w_o_mem, tm, 512)
        x1, h2 = _proj_norm_residual(merged, w_out, x, norm_post_mix, norm_pre_ffn, _row_tile(m, 512))
        outs.append(_ffn(h2, w_ff1, w_ff2, x1, norm_post_ffn, tm, 1024))
    new_s = [t.reshape(n_seq, seq, SWA_HPG, HEAD_DIM) for t in kv_new_s] + [state_s, jnp.transpose(hist_s, (1, 0, 2))]
    kv_state_p = [t.reshape(1, -1, SWA_HPG, HEAD_DIM) for t in kv_state_p]
    return outs[0], outs[1], proj_p, kv_state_p, state_p, new_s


def kernel(x_prompt, x_sample, cache_swa0_k, cache_swa0_v, cache_swa1_k, cache_swa1_v, cache_swa2_k, cache_swa2_v, state_delta, state_conv, cache_mem_k, cache_mem_v, mem_prompt, rel_bias, w_in, w_conv, A_log, dt_bias, norm_delta, norm_mem, w_mem_kv, w_o_swa, w_o_delta, w_o_mem, w_out, norm_pre_mix, norm_post_mix, norm_pre_ffn, norm_post_ffn, w_ff1, w_ff2):
    depth = w_in.shape[0]
    bp, tp, d = x_prompt.shape
    bs, ts, _ = x_sample.shape
    assert bp == 1 and ts > CONV_WIDTH - 1 and ts % 8 == 0
    sample_swa = (cache_swa0_k, cache_swa0_v, cache_swa1_k, cache_swa1_v, cache_swa2_k, cache_swa2_v)
    bias_prompt = _prompt_bias_tables(rel_bias)
    xp = x_prompt.reshape(bp * tp, d)
    xs = x_sample.reshape(bs * ts, d)
    new_p = [[] for _ in range(10)]
    new_s = [[] for _ in range(8)]
    hw = SWA_HPG * HEAD_DIM
    for l in range(depth):
        lw = (jnp.swapaxes(w_in, 1, 2)[l], w_conv[l], A_log[l], dt_bias[l], norm_delta[l], w_o_swa[l],
              w_o_delta[l], w_o_mem[l], w_out[l],
              norm_pre_mix[l], norm_post_mix[l], norm_pre_ffn[l], norm_post_ffn[l],
              w_ff1[l], w_ff2[l])
        mem = mem_prompt.reshape(-1, d)
        mkv = _matmul(_rmsnorm_cast(mem, norm_mem[l], _row_tile(mem.shape[0], 256)), w_mem_kv[l],
                      2 * MEM_WIDTH, _row_tile(mem.shape[0], 256), 512, "mem_kv")
        mk = mkv[:, :MEM_WIDTH].reshape(1, bp, -1, MEM_HEADS, MEM_HEAD_DIM)
        mv = mkv[:, MEM_WIDTH:].reshape(1, bp, -1, MEM_HEADS, MEM_HEAD_DIM)
        xp, xs, main_p, kv_p, s_p, vals_s = _layer(
            xp, xs, lw, l, rel_bias, bias_prompt, (mk, mv), sample_swa, (cache_mem_k, cache_mem_v),
            state_conv[l], state_delta[l])
        vals_p = list(kv_p)
        vals_p.append(s_p.reshape(bp, DN_HEADS, HEAD_DIM, HEAD_DIM))
        vals_p.append(main_p[tp - (CONV_WIDTH - 1):, B_OFF:B_OFF + 3 * DN_WIDTH].reshape(bp, CONV_WIDTH - 1, -1))
        vals_p += [mk[0], mv[0]]
        for lst, val in zip(new_p, vals_p):
            lst.append(val)
        for lst, val in zip(new_s, vals_s):
            lst.append(val)
    outs_p = [jnp.stack(t) for t in new_p]
    outs_s = [jnp.stack(t) for t in new_s]
    return (xp.reshape(bp, tp, d), xs.reshape(bs, ts, d), *outs_p, *outs_s)
```

```python
import functools
import math

import numpy as np
import jax
import jax.numpy as jnp
from jax import lax
from jax.experimental import pallas as pl
from jax.experimental.pallas import tpu as pltpu

EPS = 1e-6
HEAD_DIM = 128
SWA_GROUPS = ((128, 1), (512, 4), (2048, 16))
SWA_SPAN = 128
SWA_HPG = 4
SWA_WIDTH = SWA_HPG * len(SWA_GROUPS) * HEAD_DIM
DN_HEADS = 12
DN_WIDTH = DN_HEADS * HEAD_DIM
CONV_WIDTH = 4
DN_CHUNK = 128
MEM_HEADS = 4
MEM_HEAD_DIM = 256
MEM_WIDTH = MEM_HEADS * MEM_HEAD_DIM
N_BUCKETS = 32
MAX_DISTANCE = 2048
N_BRANCHES = 3

A_OFF = 0
B_OFF = 3 * SWA_WIDTH
Z_OFF = B_OFF + 3 * DN_WIDTH
BA_OFF = Z_OFF + DN_WIDTH
CQ_OFF = BA_OFF + 2 * DN_HEADS
GATE_OFF = CQ_OFF + MEM_WIDTH
LANES = 128
SUBLANES = 8
PROJ_TILE = 512
P_BA = BA_OFF
P_CQ = P_BA + PROJ_TILE
P_GATE = P_CQ + MEM_WIDTH
SWA_ROWS = 2048
SWA_UNROLL = 16
NEG = -1e30
VMEM_LIMIT = 61 * 1024 * 1024

BF16 = jnp.bfloat16
F32 = jnp.float32


def _params(n_grid):
    return pltpu.CompilerParams(dimension_semantics=("arbitrary",) * n_grid, vmem_limit_bytes=VMEM_LIMIT)


def _dot(a, b):
    return jnp.dot(a, b, preferred_element_type=F32)


def _dot_nt(a, b):
    return lax.dot_general(a, b, (((1,), (1,)), ((), ())), preferred_element_type=F32)


def _dot_tn(a, b):
    return lax.dot_general(a, b, (((0,), (0,)), ((), ())), preferred_element_type=F32)


def _split2(x):
    hi = x.astype(BF16)
    lo = (x - hi.astype(F32)).astype(BF16)
    return hi, lo


def _dot3(a, b, dot=_dot):
    ah, al = _split2(a)
    bh, bl = _split2(b)
    return dot(ah, bh) + (dot(ah, bl) + dot(al, bh))


def _rmsnorm_cast_kernel(x_ref, g_ref, o_ref):
    x = x_ref[...]
    y = x * lax.rsqrt(jnp.mean(x * x, axis=-1, keepdims=True) + EPS)
    o_ref[...] = (y * g_ref[...]).astype(o_ref.dtype)


def _rmsnorm_cast(x, g, tm):
    m, d = x.shape
    return pl.pallas_call(
        _rmsnorm_cast_kernel, grid=(m // tm,),
        in_specs=[pl.BlockSpec((tm, d), lambda i: (i, 0)), pl.BlockSpec((1, d), lambda i: (0, 0))],
        out_specs=pl.BlockSpec((tm, d), lambda i: (i, 0)),
        out_shape=jax.ShapeDtypeStruct((m, d), BF16),
        compiler_params=_params(1), name="rmsnorm_cast")(x, g.reshape(1, d))


def _matmul_kernel(x_ref, w_ref, o_ref):
    o_ref[...] = _dot(x_ref[...], w_ref[...].astype(BF16)).astype(o_ref.dtype)


def _matmul(x, w, n, tm, tn, name):
    m, k = x.shape
    assert n % tn == 0 and m % tm == 0
    return pl.pallas_call(
        _matmul_kernel, grid=(m // tm, n // tn),
        in_specs=[pl.BlockSpec((tm, k), lambda i, j: (i, 0)), pl.BlockSpec((k, tn), lambda i, j: (0, j))],
        out_specs=pl.BlockSpec((tm, tn), lambda i, j: (i, j)),
        out_shape=jax.ShapeDtypeStruct((m, n), F32),
        compiler_params=_params(2), name=name)(x, w)


def _in_proj_kernel(x_ref, g_ref, w_ref, o_ref, h_ref):
    @pl.when(pl.program_id(1) == 0)
    def _():
        h_ref[...] = _rms(x_ref[...], g_ref[...]).astype(BF16)

    o_ref[...] = _dot_nt(h_ref[...], w_ref[...].astype(BF16))


def _in_proj_normed_kernel(h_ref, w_ref, o_ref):
    o_ref[...] = _dot_nt(h_ref[...], w_ref[...].astype(BF16))


def _in_proj(x, g, wt, tm, tn, riders=()):
    m, k = x.shape
    n = P_CQ + wt.shape[0] - CQ_OFF
    assert n % tn == 0 and m % tm == 0 and P_CQ % tn == 0 and CQ_OFF % SUBLANES == 0

    def w_rows(i, j):
        return (pl.multiple_of(jnp.where(j * tn < P_CQ, j * tn, j * tn - P_CQ + CQ_OFF), SUBLANES), 0)

    x_spec = pl.BlockSpec((tm, k), lambda i, j: (i, 0), pipeline_mode=pl.Buffered(1))
    w_spec = pl.BlockSpec((pl.Element(tn), pl.Element(k)), w_rows)
    out = ([jax.ShapeDtypeStruct((m, n), F32)], [pl.BlockSpec((tm, tn), lambda i, j: (i, j))])
    if g is None:
        return _hosted_call("in_proj", _in_proj_normed_kernel, (m // tm, n // tn), [x, wt], [x_spec, w_spec],
                            *out, [], riders)
    return _hosted_call(
        "in_proj", _in_proj_kernel, (m // tm, n // tn), [x, g.reshape(1, k), wt],
        [x_spec, pl.BlockSpec((1, k), lambda i, j: (0, 0)), w_spec], *out, [pltpu.VMEM((tm, k), BF16)], riders)


def _rmsnorm_task(x, g, max_steps):
    rows, k = x.shape
    steps = max(s for s in range(1, min(max_steps, rows // BF16_ROWS) + 1) if rows % (s * BF16_ROWS) == 0)
    slab = pl.BlockSpec((rows // steps, k), lambda i: (i, 0))

    def body(x_ref, g_ref, o_ref):
        o_ref[...] = _rms(x_ref[...], g_ref[...]).astype(BF16)

    return _Task("rmsnorm", body, steps, [x, g.reshape(1, k)], [slab, pl.BlockSpec((1, k), lambda i: (0, 0))],
                 [jax.ShapeDtypeStruct((rows, k), BF16)], [slab])


def _t5_bucket(dist):
    max_exact = N_BUCKETS // 2
    df = jnp.maximum(dist, 1).astype(F32)
    large = max_exact + (jnp.log(df / max_exact) / math.log(MAX_DISTANCE / max_exact)
                         * (N_BUCKETS - max_exact)).astype(jnp.int32)
    return jnp.where(dist < max_exact, dist, jnp.minimum(large, N_BUCKETS - 1))


def _group_bias(rel_bias, g):
    _, dil = SWA_GROUPS[g]
    dist = jnp.arange(SWA_SPAN + 1, dtype=jnp.int32) * dil
    onehot = _t5_bucket(dist)[None, :, None] == jnp.arange(N_BUCKETS, dtype=jnp.int32)[None, None, :]
    heads = jnp.transpose(rel_bias[:, g * SWA_HPG:(g + 1) * SWA_HPG].astype(F32))
    return jnp.sum(jnp.where(onehot, heads[:, None, :], 0.0), axis=-1)


def _prompt_bias_tables(rel_bias):
    span = SWA_SPAN
    first = (np.arange(2 * span) >= span)[None, None, :]
    out = []
    for g in range(len(SWA_GROUPS)):
        w = _group_bias(rel_bias, g)
        h = w.shape[0]
        p = 3 * span
        e = jnp.concatenate([jnp.full((h, span - 1), NEG, F32), w[:, ::-1], jnp.full((h, p - 2 * span), NEG, F32)], 1)
        skew = jnp.broadcast_to(e[:, None, :], (h, span, p)).reshape(h, span * p)[:, :span * (p - 1)]
        tab = skew.reshape(h, span, p - 1)[:, :, span - 1:3 * span - 1]
        out.append(jnp.stack([jnp.where(first, tab, NEG), tab], axis=1))
    return jnp.stack(out)


def _sample_bias_tables(rel_bias, g, n_res, n_new, cache_len):
    _, dil = SWA_GROUPS[g]
    w = _group_bias(rel_bias, g)
    h = w.shape[0]
    lm = cache_len // dil
    assert cache_len % dil == 0
    p = w.shape[1] + lm + n_new
    wrev = jnp.concatenate([w, jnp.full((h, lm + n_new), NEG, F32)], axis=1)[:, ::-1]
    s = np.arange(n_new)
    res = np.arange(n_res)
    tab = jnp.full((h, n_new, lm, n_res), NEG, F32)
    for q in range((n_new - 1) // dil + 1):
        vec = wrev[:, p - 1 - lm - q:p - 1 - q]
        mask = (s // dil == q)[:, None] & (res[None, :] == (s % dil)[:, None])
        tab = jnp.where(mask[None, :, None, :], vec[:, None, :, None], tab)
    dist = s[:, None] - s[None, :]
    new = jnp.full((h, n_new, n_new), NEG, F32)
    for j in range(min((n_new - 1) // dil, SWA_SPAN) + 1):
        new = jnp.where((dist == j * dil)[None], w[:, j][:, None, None], new)
    return tab.reshape(h, n_new, lm * n_res), new


def _swa_prompt_kernel(*refs, n_groups):
    ins = refs[:5 * n_groups]
    bias_ref = refs[5 * n_groups]
    o_ref = refs[5 * n_groups + 1]
    state_refs = refs[5 * n_groups + 2:7 * n_groups + 2]
    scr = refs[7 * n_groups + 2:]
    kext, vext, og, lg = scr[:n_groups], scr[n_groups:2 * n_groups], scr[2 * n_groups:3 * n_groups], scr[3 * n_groups:]
    n = pl.program_id(0)
    scale = HEAD_DIM ** -0.5
    span = SWA_SPAN
    for g in range(n_groups):
        _, dil = SWA_GROUPS[g]
        q_ref, kc_ref, vc_ref, kp_ref, vp_ref = ins[5 * g:5 * g + 5]
        blk = span * dil
        kext[g][0:blk, :] = kp_ref[...]
        kext[g][blk:, :] = kc_ref[...]
        vext[g][0:blk, :] = vp_ref[...]
        vext[g][blk:, :] = vc_ref[...]

        def body(it, carry, g=g, dil=dil, blk=blk, q_ref=q_ref):
            items = range(SWA_UNROLL)
            starts, sels = [], []
            for u in items:
                idx = it * SWA_UNROLL + u
                b = idx // dil
                starts.append(b * blk + (idx - b * dil))
                sels.append(jnp.where(jnp.logical_and(n == 0, b == 0), 0, 1))
            q = [q_ref[pl.ds(starts[u], span, stride=dil), :].astype(BF16) for u in items]
            kk = [kext[g][pl.ds(starts[u], 2 * span, stride=dil), :].astype(BF16) for u in items]
            s = [_dot_nt(q[u], kk[u]) * scale + bias_ref[g, 0, sels[u]] for u in items]
            m = [jnp.max(s[u], axis=-1, keepdims=True) for u in items]
            p = [jnp.exp(s[u] - m[u]) for u in items]
            den = [jnp.sum(p[u], axis=-1, keepdims=True) for u in items]
            vv = [vext[g][pl.ds(starts[u], 2 * span, stride=dil), :].astype(BF16) for u in items]
            o = [_dot(p[u].astype(BF16), vv[u]) / den[u] for u in items]
            for u in items:
                og[g][pl.ds(starts[u], span, stride=dil), :] = o[u]
                lg[g][pl.ds(starts[u], span, stride=dil), :] = jnp.broadcast_to(
                    m[u] + jnp.log(den[u]), (span, HEAD_DIM))
            return carry

        lax.fori_loop(0, SWA_ROWS // span // SWA_UNROLL, body, 0)
    lses = [lg[g][...] for g in range(n_groups)]
    mx = functools.reduce(jnp.maximum, lses)
    ws = [jnp.exp(l - mx) for l in lses]
    num = functools.reduce(lambda a, b: a + b, [w * og[g][...] for g, w in enumerate(ws)])
    o_ref[...] = (num / functools.reduce(lambda a, b: a + b, ws)).astype(o_ref.dtype)

    @pl.when(n == pl.num_programs(0) - 1)
    def _():
        head = pl.program_id(1)
        for g in range(n_groups):
            for sec in (1, 2):
                dst = state_refs[2 * g + sec - 1]
                keep = dst.shape[0] // SWA_HPG
                dst[pl.ds(head, keep, stride=SWA_HPG), :] = ins[5 * g + sec][SWA_ROWS - keep:, :]


def _swa_prompt(proj, bias_tables):
    t = proj.shape[0]
    n_groups = len(SWA_GROUPS)
    assert t % SWA_ROWS == 0 and all(w <= SWA_ROWS for w, _ in SWA_GROUPS)
    state_shape = [(min(w, t) * SWA_HPG, HEAD_DIM) for w, _ in SWA_GROUPS for _ in range(2)]
    in_specs, scratch_k, scratch_o = [], [], []
    for g, (_, dil) in enumerate(SWA_GROUPS):
        blk = SWA_SPAN * dil
        per = SWA_ROWS // blk
        assert SWA_ROWS % blk == 0
        qc, kc, vc = (sec * (SWA_WIDTH // HEAD_DIM) + g * SWA_HPG for sec in range(3))
        cur = lambda col: pl.BlockSpec((SWA_ROWS, HEAD_DIM), lambda n, h, col=col: (n, col + h))
        prev = lambda col, blk=blk, per=per: pl.BlockSpec(
            (blk, HEAD_DIM), lambda n, h, col=col, per=per: (jnp.maximum(n * per - 1, 0), col + h))
        in_specs += [cur(qc), cur(kc), cur(vc), prev(kc), prev(vc)]
        scratch_k.append(pltpu.VMEM((SWA_ROWS + blk, HEAD_DIM), F32))
        scratch_o.append(pltpu.VMEM((SWA_ROWS, HEAD_DIM), F32))
    in_specs.append(pl.BlockSpec((n_groups, 1, 2, SWA_SPAN, 2 * SWA_SPAN), lambda n, h: (0, h, 0, 0, 0)))
    return pl.pallas_call(
        functools.partial(_swa_prompt_kernel, n_groups=n_groups),
        grid=(t // SWA_ROWS, SWA_HPG),
        in_specs=in_specs,
        out_specs=[pl.BlockSpec((SWA_ROWS, HEAD_DIM), lambda n, h: (n, h))]
        + [pl.BlockSpec(s, lambda n, h: (0, 0)) for s in state_shape],
        out_shape=[jax.ShapeDtypeStruct((t, SWA_HPG * HEAD_DIM), BF16)]
        + [jax.ShapeDtypeStruct(s, F32) for s in state_shape],
        scratch_shapes=scratch_k + scratch_k + scratch_o + scratch_o,
        compiler_params=_params(2), name="swa_prompt")(*([proj] * (5 * n_groups)), bias_tables)


def _swa_sample_kernel(*refs, n_groups):
    qkv_ref = refs[0]
    caches = refs[1:1 + 2 * n_groups]
    biases = refs[1 + 2 * n_groups:1 + 4 * n_groups]
    o_ref = refs[1 + 4 * n_groups]
    new_refs = refs[2 + 4 * n_groups:]
    scale = HEAD_DIM ** -0.5
    nseq = caches[0].shape[1]
    n_new = qkv_ref.shape[0] // nseq
    items = [(b, g, h) for b in range(nseq) for h in range(SWA_HPG) for g in range(n_groups)]
    idx = range(len(items))

    def new(sec, b, g, h):
        col = sec * SWA_WIDTH + (g * SWA_HPG + h) * HEAD_DIM
        return qkv_ref[b * n_new:(b + 1) * n_new, col:col + HEAD_DIM].astype(BF16)

    def cached(ref, b, h):
        nk = math.prod(ref.shape[2:-1]) // SWA_HPG
        if len(ref.shape) == 5:
            pick = (0, b, slice(None), pl.ds(h, ref.shape[3] // SWA_HPG, stride=SWA_HPG), slice(None))
        else:
            pick = (0, b, pl.ds(h, nk, stride=SWA_HPG), slice(None))
        return ref[pick].reshape(nk, HEAD_DIM).astype(BF16)

    q = [new(0, *it) for it in items]
    sc = [_dot_nt(q[i], cached(caches[2 * g], b, h)) * scale + biases[2 * g][h]
          for i, (b, g, h) in enumerate(items)]
    sn = [_dot_nt(q[i], new(1, b, g, h)) * scale + biases[2 * g + 1][h] for i, (b, g, h) in enumerate(items)]
    m = [jnp.maximum(jnp.max(sc[i], axis=-1, keepdims=True), jnp.max(sn[i], axis=-1, keepdims=True)) for i in idx]
    pc = [jnp.exp(sc[i] - m[i]) for i in idx]
    pn = [jnp.exp(sn[i] - m[i]) for i in idx]
    den = [jnp.sum(pc[i], axis=-1, keepdims=True) + jnp.sum(pn[i], axis=-1, keepdims=True) for i in idx]
    o = [(_dot(pc[i].astype(BF16), cached(caches[2 * g + 1], b, h)) + _dot(pn[i].astype(BF16), new(2, b, g, h)))
         / den[i] for i, (b, g, h) in enumerate(items)]
    lse = [m[i] + jnp.log(den[i]) for i in idx]
    for b in range(nseq):
        for h in range(SWA_HPG):
            mine = [i for i in idx if items[i][0] == b and items[i][2] == h]
            mx = functools.reduce(jnp.maximum, [lse[i] for i in mine])
            ws = [jnp.exp(lse[i] - mx) for i in mine]
            num = functools.reduce(lambda x, y: x + y, [w * o[i] for w, i in zip(ws, mine)])
            o_ref[b * n_new:(b + 1) * n_new, h * HEAD_DIM:(h + 1) * HEAD_DIM] = (
                num / functools.reduce(lambda x, y: x + y, ws))
    for b, g, h in items:
        for sec in (1, 2):
            col = sec * SWA_WIDTH + (g * SWA_HPG + h) * HEAD_DIM
            new_refs[2 * g + sec - 1][b, pl.ds(h, n_new, stride=SWA_HPG), :] = (
                qkv_ref[b * n_new:(b + 1) * n_new, col:col + HEAD_DIM])


class _Task:
    def __init__(self, name, body, steps, args, in_specs, out_shapes, out_specs):
        self.name, self.body, self.steps = name, body, steps
        self.args, self.in_specs, self.out_shapes, self.out_specs = args, in_specs, out_shapes, out_specs


def _run_task(task):
    return pl.pallas_call(
        task.body, grid=(task.steps,), in_specs=task.in_specs, out_specs=task.out_specs,
        out_shape=task.out_shapes, compiler_params=_params(1), name=task.name)(*task.args)


def _riding_specs(task, grid):
    assert math.prod(grid) >= task.steps

    def step_of(*idx):
        flat = functools.reduce(lambda acc, t: acc * t[1] + t[0], zip(idx[1:], grid[1:]), idx[0])
        return flat if math.prod(grid) == task.steps else jnp.minimum(flat, task.steps - 1)

    re = lambda spec: pl.BlockSpec(spec.block_shape, lambda *idx: spec.index_map(step_of(*idx)))
    return [re(s) for s in task.in_specs], [re(s) for s in task.out_specs]


BF16_ROWS = 16


def _cast_task(w, max_steps):
    rows, cols = w.shape
    steps = max(s for s in range(1, min(max_steps, rows // BF16_ROWS) + 1) if rows % (s * BF16_ROWS) == 0)
    slab = pl.BlockSpec((rows // steps, cols), lambda i: (i, 0))

    def body(x_ref, o_ref):
        o_ref[...] = x_ref[...].astype(BF16)

    return _Task("cast", body, steps, [w], [slab], [jax.ShapeDtypeStruct((rows, cols), BF16)], [slab])


def _swa_sample_task(proj, caches, layer, rel_bias, n_new, per):
    rows_total = proj.shape[0]
    nb = rows_total // n_new
    assert nb % per == 0
    n_groups = len(SWA_GROUPS)
    hw = SWA_HPG * HEAD_DIM
    cache_in, cache_specs, bias_in, bias_specs = [], [], [], []
    for g, (_, dil) in enumerate(SWA_GROUPS):
        depth, _, cache_len, heads, hd = caches[2 * g].shape
        if dil >= 2 * n_new and n_new % 8 == 0:
            n_res = n_new
            shape = (depth, nb, cache_len // dil, dil * heads, hd)
            spec = pl.BlockSpec((1, per, cache_len // dil, n_res * heads, hd), lambda b: (layer, b, 0, 0, 0))
        else:
            n_res = dil
            shape = (depth, nb, cache_len * heads, hd)
            spec = pl.BlockSpec((1, per, cache_len * heads, hd), lambda b: (layer, b, 0, 0))
        for c in caches[2 * g:2 * g + 2]:
            cache_in.append(c.reshape(shape))
            cache_specs.append(spec)
        for tab in _sample_bias_tables(rel_bias, g, n_res, n_new, cache_len):
            bias_in.append(tab)
            bias_specs.append(pl.BlockSpec(tab.shape, lambda b: (0, 0, 0)))
    return _Task(
        "swa_sample", functools.partial(_swa_sample_kernel, n_groups=n_groups), nb // per,
        [proj, *cache_in, *bias_in],
        [pl.BlockSpec((per * n_new, 3 * SWA_WIDTH), lambda b: (b, 0))] + cache_specs + bias_specs,
        [jax.ShapeDtypeStruct((rows_total, hw), F32)]
        + [jax.ShapeDtypeStruct((nb, n_new * SWA_HPG, HEAD_DIM), F32)] * (2 * n_groups),
        [pl.BlockSpec((per * n_new, hw), lambda b: (b, 0))]
        + [pl.BlockSpec((per, n_new * SWA_HPG, HEAD_DIM), lambda b: (b, 0, 0))] * (2 * n_groups))


def _silu(x):
    return x * (1.0 / (1.0 + jnp.exp(-x)))


def _softplus(x):
    return jnp.maximum(x, 0.0) + jnp.log(1.0 + jnp.exp(-jnp.abs(x)))


def _inv_dot(a, b):
    return _dot(a.astype(BF16), b.astype(BF16))


HALO = 8
DELTA_SEQS_PER_STEP = 8


def _delta_scan_kernel(*refs, chunk, nseq, carry):
    if carry:
        x_ref, z_ref, ba_ref, wc_ref, alog_ref, dt_ref, gain_ref, o_ref, s_out_ref, s_scr, halo_scr = refs
        s_in_ref = halo_ref = None
    else:
        (x_ref, z_ref, ba_ref, wc_ref, alog_ref, dt_ref, gain_ref, s_in_ref, halo_ref,
         o_ref, s_out_ref, hist_out_ref) = refs
        s_scr = halo_scr = None
        x3 = x_ref[...].reshape(nseq, chunk, x_ref.shape[1])
        for t in range(CONV_WIDTH - 1):
            hist_out_ref[t] = x3[:, chunk - (CONV_WIDTH - 1) + t, :]
    c = chunk
    rows = nseq * c
    seqs = range(nseq)
    part = lambda t, b: t[b * c:(b + 1) * c]
    if carry:
        @pl.when(pl.program_id(0) == 0)
        def _():
            s_scr[...] = jnp.zeros_like(s_scr)
            halo_scr[...] = jnp.zeros_like(halo_scr)

    x = x_ref[...]
    acc = x * wc_ref[CONV_WIDTH - 1:CONV_WIDTH, :]
    if carry:
        xp = jnp.concatenate([halo_scr[...], x], axis=0)
        for t in range(1, CONV_WIDTH):
            acc = acc + pltpu.roll(xp, t, 0)[HALO:] * wc_ref[CONV_WIDTH - 1 - t:CONV_WIDTH - t, :]
        halo_scr[...] = x[c - HALO:, :]
    else:
        width = x.shape[1]
        rep = [jnp.broadcast_to(halo_ref[k][:, None, :], (nseq, c, width)).reshape(rows, width)
               for k in range(CONV_WIDTH - 1)]
        pos = lax.broadcasted_iota(jnp.int32, (rows, 1), 0) % c
        for t in range(1, CONV_WIDTH):
            hv = rep[CONV_WIDTH - 1 - t]
            for s in range(1, t):
                hv = jnp.where(pos == s, rep[CONV_WIDTH - 1 + s - t], hv)
            acc = acc + jnp.where(pos >= t, pltpu.roll(x, t, 0), hv) * wc_ref[CONV_WIDTH - 1 - t:CONV_WIDTH - t, :]
    qkv = _silu(acc)

    ba = ba_ref[...]
    beta_all = 1.0 / (1.0 + jnp.exp(-ba))
    g_all = -jnp.exp(alog_ref[...]) * _softplus(ba + dt_ref[...])
    row = lax.broadcasted_iota(jnp.int32, (rows, rows), 0)
    col = lax.broadcasted_iota(jnp.int32, (rows, rows), 1)
    same = (row // c) == (col // c)
    tri = (row >= col) & same
    strict = (row > col) & same
    tril = tri.astype(BF16)
    g1 = g_all.astype(BF16)
    r1 = g_all - g1.astype(F32)
    g2 = r1.astype(BF16)
    g3 = (r1 - g2.astype(F32)).astype(BF16)
    gcum_all = _dot(tril, g1) + (_dot(tril, g2) + _dot(tril, g3))
    gcum_t = gcum_all.T
    eye = (row == col).astype(F32)

    heads = range(DN_HEADS)
    hs = lambda sec, h: slice(sec * DN_WIDTH + h * HEAD_DIM, sec * DN_WIDTH + (h + 1) * HEAD_DIM)
    lane = lambda h: slice(DN_HEADS + h, DN_HEADS + h + 1)
    l2 = lambda t: t * lax.rsqrt(jnp.sum(t * t, axis=-1, keepdims=True) + EPS)
    q = [l2(qkv[:, hs(0, h)]) * (HEAD_DIM ** -0.5) for h in heads]
    k = [l2(qkv[:, hs(1, h)]) for h in heads]
    v = [qkv[:, hs(2, h)] for h in heads]
    bcol = [beta_all[:, h:h + 1] for h in heads]
    gc = [gcum_all[:, lane(h)] for h in heads]
    glast = [[gcum_all[(b + 1) * c - 1:(b + 1) * c, lane(h)] for b in seqs] for h in heads]
    decay = [jnp.exp(jnp.where(tri, gc[h] - gcum_t[lane(h), :], NEG)) for h in heads]
    kb = [k[h] * bcol[h] for h in heads]
    k16 = [k[h].astype(BF16) for h in heads]
    nmat = [jnp.where(strict, _dot_nt(kb[h].astype(BF16), k16[h]) * decay[h], 0.0) for h in heads]
    attn = [jnp.where(tri, _dot_nt(q[h].astype(BF16), k16[h]) * decay[h], 0.0) for h in heads]
    inv = [eye] * DN_HEADS
    blk = 1
    while blk < c:
        pair = (row // (2 * blk) == col // (2 * blk)) & (row % (2 * blk) >= blk) & (col % (2 * blk) < blk)
        low = [jnp.where(pair, nmat[h], 0.0) for h in heads]
        if blk == 1:
            inv = [eye - low[h] for h in heads]
        else:
            t = [_inv_dot(low[h], inv[h]) for h in heads]
            inv = [inv[h] - _inv_dot(inv[h], t[h]) for h in heads]
        blk *= 2
    eg = [jnp.exp(gc[h]) for h in heads]
    rhs = [jnp.concatenate([v[h] * bcol[h], kb[h] * eg[h]], axis=1) for h in heads]
    sol = [_inv_dot(inv[h], rhs[h]) for h in heads]
    qdec = [q[h] * eg[h] for h in heads]
    hb = [(h, b) for h in heads for b in seqs]
    s_prev = {(h, b): (s_scr[h] if carry else s_in_ref[b, h]) for h, b in hb}
    wq = {(h, b): jnp.concatenate([part(sol[h][:, HEAD_DIM:], b), part(qdec[h], b)], axis=0).astype(BF16)
          for h, b in hb}
    ws = {i: _dot(wq[i], s_prev[i].astype(BF16)) for i in hb}
    vn = {(h, b): part(sol[h][:, :HEAD_DIM], b) - ws[h, b][:c] for h, b in hb}
    v_new = [jnp.concatenate([vn[h, b] for b in seqs], axis=0).astype(BF16) for h in heads]
    o = [jnp.concatenate([ws[h, b][c:] for b in seqs], axis=0) + _dot(attn[h].astype(BF16), v_new[h])
         for h in heads]
    kdec = {(h, b): (part(k[h], b) * jnp.exp(glast[h][b] - part(gc[h], b))).astype(BF16) for h, b in hb}
    s_new = {(h, b): s_prev[h, b] * jnp.exp(glast[h][b]) + _dot_tn(kdec[h, b], vn[h, b].astype(BF16))
             for h, b in hb}
    for h in heads:
        if carry:
            s_scr[h] = s_new[h, 0]
            s_out_ref[h] = s_new[h, 0]
        else:
            for b in seqs:
                s_out_ref[b, h] = s_new[h, b]
        y = o[h] * lax.rsqrt(jnp.mean(o[h] * o[h], axis=-1, keepdims=True) + EPS) * gain_ref[...]
        o_ref[:, hs(0, h)] = (y * _silu(z_ref[:, hs(0, h)])).astype(o_ref.dtype)


def _delta_scan(proj, w_conv, a_log, dt_bias, norm_delta, s0, conv_buf, chunk, riders=()):
    m = proj.shape[0]
    carry = s0 is None
    nseq = 1 if carry else math.gcd(m // chunk, DELTA_SEQS_PER_STEP)
    rows = nseq * chunk
    n = m // rows
    width = 3 * DN_WIDTH
    assert chunk >= HALO and B_OFF % width == 0
    pad = lambda v_, off: jnp.zeros((1, LANES), F32).at[0, off:off + DN_HEADS].set(v_.astype(F32))
    consts = [w_conv, pad(a_log, DN_HEADS), pad(dt_bias, DN_HEADS), norm_delta.reshape(1, HEAD_DIM).astype(F32)]
    const_specs = [pl.BlockSpec((CONV_WIDTH, width), lambda i: (0, 0))] + [pl.BlockSpec((1, LANES), lambda i: (0, 0))] * 3
    in_specs = [pl.BlockSpec((rows, width), lambda i: (i, B_OFF // width)),
                pl.BlockSpec((rows, DN_WIDTH), lambda i: (i, Z_OFF // DN_WIDTH)),
                pl.BlockSpec((rows, LANES), lambda i: (i, P_BA // LANES))] + const_specs
    args = [proj, proj, proj] + consts
    state = (DN_HEADS, HEAD_DIM, HEAD_DIM)
    if carry:
        out_dtype = BF16
        s_shape, s_spec = state, pl.BlockSpec(state, lambda i: (0, 0, 0))
        scratch = [pltpu.VMEM(state, F32), pltpu.VMEM((HALO, width), F32)]
    else:
        out_dtype = F32
        in_specs += [pl.BlockSpec((nseq,) + state, lambda i: (i, 0, 0, 0)),
                     pl.BlockSpec((CONV_WIDTH - 1, nseq, width), lambda i: (0, i, 0))]
        args += [s0, jnp.transpose(conv_buf, (1, 0, 2))]
        s_shape, s_spec = (m // chunk,) + state, pl.BlockSpec((nseq,) + state, lambda i: (i, 0, 0, 0))
        scratch = []
    out_specs = [pl.BlockSpec((rows, DN_WIDTH), lambda i: (i, 0)), s_spec]
    out_shape = [jax.ShapeDtypeStruct((m, DN_WIDTH), out_dtype), jax.ShapeDtypeStruct(s_shape, F32)]
    if not carry:
        out_specs.append(pl.BlockSpec((CONV_WIDTH - 1, nseq, width), lambda i: (0, i, 0)))
        out_shape.append(jax.ShapeDtypeStruct((CONV_WIDTH - 1, m // chunk, width), F32))
    host = functools.partial(_delta_scan_kernel, chunk=chunk, nseq=nseq, carry=carry)
    return _hosted_call("delta_scan", host, (n,), args, in_specs, out_shape, out_specs, scratch, riders)


def _hosted_kernel(*refs, host, n_host_in, n_host_out, riders):
    n_in = n_host_in + sum(len(t.in_specs) for t in riders)
    n_out = n_host_out + sum(len(t.out_specs) for t in riders)
    ins, outs, scratch = refs[:n_in], refs[n_in:n_in + n_out], refs[n_in + n_out:]
    host(*ins[:n_host_in], *outs[:n_host_out], *scratch)
    pi, po = n_host_in, n_host_out
    for t in riders:
        ni, no = len(t.in_specs), len(t.out_specs)
        t.body(*ins[pi:pi + ni], *outs[po:po + no])
        pi, po = pi + ni, po + no


def _hosted_call(name, host, grid, args, in_specs, out_shape, out_specs, scratch, riders):
    body = functools.partial(_hosted_kernel, host=host, n_host_in=len(in_specs), n_host_out=len(out_specs),
                             riders=tuple(riders))
    args, in_specs, out_shape, out_specs = list(args), list(in_specs), list(out_shape), list(out_specs)
    for t in riders:
        r_in, r_out = _riding_specs(t, grid)
        args, in_specs = args + list(t.args), in_specs + r_in
        out_shape, out_specs = out_shape + list(t.out_shapes), out_specs + r_out
    return pl.pallas_call(
        body, grid=grid, in_specs=in_specs, out_specs=out_specs, out_shape=out_shape, scratch_shapes=scratch,
        compiler_params=_params(len(grid)), name=name)(*args)


def _mem_attn_kernel(q_ref, k_ref, v_ref, o_ref):
    scale = MEM_HEAD_DIM ** -0.5
    parts = MEM_HEAD_DIM // LANES
    period = MEM_HEADS * parts
    tokens = k_ref.shape[2] // period
    pick = lambda b, c, h: (0, b, pl.ds(c * MEM_HEADS + h, tokens, stride=period), slice(None))
    nb = k_ref.shape[1]
    rows = q_ref.shape[0] // nb
    items = [(b, h) for b in range(nb) for h in range(MEM_HEADS)]
    idx = range(len(items))
    q = [[q_ref[b * rows:(b + 1) * rows, h * MEM_HEAD_DIM + c * LANES:h * MEM_HEAD_DIM + (c + 1) * LANES]
          .astype(BF16) for c in range(parts)] for b, h in items]
    s = [functools.reduce(lambda x, y: x + y, [_dot_nt(q[i][c], k_ref[pick(b, c, h)].astype(BF16))
                                               for c in range(parts)]) * scale for i, (b, h) in enumerate(items)]
    p = [jnp.exp(s[i] - jnp.max(s[i], axis=-1, keepdims=True)) for i in idx]
    den = [jnp.sum(p[i], axis=-1, keepdims=True) for i in idx]
    for i, (b, h) in enumerate(items):
        for c in range(parts):
            col = h * MEM_HEAD_DIM + c * LANES
            o_ref[b * rows:(b + 1) * rows, col:col + LANES] = (
                _dot(p[i].astype(BF16), v_ref[pick(b, c, h)].astype(BF16)) / den[i]).astype(o_ref.dtype)


MEM_PER_STEP = 4
RIDER_SEQS_MAX = 2


def _mem_attn_task(proj, mem_k, mem_v, layer, tm, out_dtype):
    m = proj.shape[0]
    assert P_CQ % MEM_WIDTH == 0
    depth, nb, tokens, heads, hd = mem_k.shape
    per = m // nb
    parts = hd // LANES
    mem_k, mem_v = (t.reshape(depth, nb, tokens, heads, parts, LANES).transpose(0, 1, 2, 4, 3, 5)
                    .reshape(depth, nb, tokens * heads * parts, LANES) for t in (mem_k, mem_v))
    mems = max(tm // per, 1)
    kv_spec = pl.BlockSpec((1, mems, tokens * heads * parts, LANES), lambda i: (layer, i * tm // (per * mems), 0, 0))
    return _Task(
        "mem_attn", _mem_attn_kernel, m // tm, [proj, mem_k, mem_v],
        [pl.BlockSpec((tm, MEM_WIDTH), lambda i: (i, P_CQ // MEM_WIDTH)), kv_spec, kv_spec],
        [jax.ShapeDtypeStruct((m, MEM_WIDTH), out_dtype)], [pl.BlockSpec((tm, MEM_WIDTH), lambda i: (i, 0))])


def _merge_kernel(a_ref, b_ref, c_ref, ga_ref, gb_ref, gc_ref, wa_ref, wb_ref, wc_ref, o_ref):
    def branch(x_ref, g_ref, w_ref):
        y = _dot(x_ref[...].astype(BF16), w_ref[...])
        return y * (1.0 / (1.0 + jnp.exp(-g_ref[...])))

    o_ref[...] = (branch(a_ref, ga_ref, wa_ref) + branch(b_ref, gb_ref, wb_ref)
                  + branch(c_ref, gc_ref, wc_ref)).astype(o_ref.dtype)


def _merge(a, b, c, tail, w_a, w_b, w_c, tm, tn):
    m = a.shape[0]
    d = w_a.shape[1]
    gate0 = P_GATE // tn
    assert P_GATE % tn == 0 and d % tn == 0
    row = lambda x: pl.BlockSpec((tm, x.shape[1]), lambda i, j: (i, 0))
    gate = lambda br: pl.BlockSpec((tm, tn), lambda i, j, br=br: (i, gate0 + br * (d // tn) + j))
    wsp = lambda w: pl.BlockSpec((w.shape[0], tn), lambda i, j: (0, j))
    return pl.pallas_call(
        _merge_kernel, grid=(m // tm, d // tn),
        in_specs=[row(a), row(b), row(c), gate(0), gate(1), gate(2), wsp(w_a), wsp(w_b), wsp(w_c)],
        out_specs=pl.BlockSpec((tm, tn), lambda i, j: (i, j)),
        out_shape=jax.ShapeDtypeStruct((m, d), BF16),
        compiler_params=_params(2), name="merge")(a, b, c, tail, tail, tail, w_a, w_b, w_c)


def _rms(y, g):
    return y * lax.rsqrt(jnp.mean(y * y, axis=-1, keepdims=True) + EPS) * g


def _proj_norm_residual_kernel(y_ref, w_ref, x_ref, g_post_ref, g_next_ref, o_ref, h_ref):
    x1 = x_ref[...] + _rms(_dot(y_ref[...], w_ref[...]), g_post_ref[...])
    o_ref[...] = x1
    h_ref[...] = _rms(x1, g_next_ref[...]).astype(h_ref.dtype)


def _proj_norm_residual(y, w, x, g_post, g_next, tm):
    m, d = x.shape
    k = y.shape[1]
    row = lambda width: pl.BlockSpec((tm, width), lambda i: (i, 0))
    gsp = pl.BlockSpec((1, d), lambda i: (0, 0))
    return pl.pallas_call(
        _proj_norm_residual_kernel, grid=(m // tm,),
        in_specs=[row(k), pl.BlockSpec((k, d), lambda i: (0, 0)), row(d), gsp, gsp],
        out_specs=[row(d), row(d)],
        out_shape=[jax.ShapeDtypeStruct((m, d), F32), jax.ShapeDtypeStruct((m, d), BF16)],
        compiler_params=_params(1), name="proj_norm_residual")(y, w, x, g_post.reshape(1, d), g_next.reshape(1, d))


FFN_SPLIT = 4


def _ffn_kernel(h_ref, w1_ref, w2_ref, x_ref, g_ref, o_ref, a_ref, x_scr):
    j = pl.program_id(1)
    last = pl.num_programs(1) - 1
    xw = x_ref.shape[1]
    x_scr[:, pl.ds(pl.multiple_of(j * xw, xw), xw)] = x_ref[...]
    piece = w1_ref.shape[1] // FFN_SPLIT
    for s in range(FFN_SPLIT):
        cols = slice(s * piece, (s + 1) * piece)
        a_ref[:, cols] = jnp.square(jnp.maximum(_dot(h_ref[...], w1_ref[:, cols]), 0.0)).astype(BF16)
    y = _dot(a_ref[...], w2_ref[...])

    @pl.when(j == 0)
    def _():
        o_ref[...] = y

    @pl.when(jnp.logical_and(j > 0, j < last))
    def _():
        o_ref[...] += y

    @pl.when(j == last)
    def _():
        o_ref[...] = x_scr[...] + _rms(o_ref[...] + y, g_ref[...])


def _ffn(h, w1, w2, x, g, tm, tf):
    m, d = h.shape
    f = w1.shape[1]
    nj = f // tf
    assert nj >= 2 and d % (nj * LANES) == 0
    return pl.pallas_call(
        _ffn_kernel, grid=(m // tm, nj),
        in_specs=[pl.BlockSpec((tm, d), lambda i, j: (i, 0)), pl.BlockSpec((d, tf), lambda i, j: (0, j)),
                  pl.BlockSpec((tf, d), lambda i, j: (j, 0)), pl.BlockSpec((tm, d // nj), lambda i, j: (i, j)),
                  pl.BlockSpec((1, d), lambda i, j: (0, 0))],
        out_specs=pl.BlockSpec((tm, d), lambda i, j: (i, 0), pipeline_mode=pl.Buffered(1)),
        out_shape=jax.ShapeDtypeStruct((m, d), F32),
        scratch_shapes=[pltpu.VMEM((tm, tf), BF16), pltpu.VMEM((tm, d), F32)],
        compiler_params=_params(2), name="ffn")(h, w1, w2, x, g.reshape(1, d))


def _row_tile(m, cap):
    t = min(m, cap)
    assert m % t == 0
    return t


def _layer(xp, xs, lw, layer, rel_bias, bias_prompt, mem_p, swa_caches, mem_s, conv_buf, s0):
    (w_in_t, w_conv, a_log, dt_bias, norm_delta, w_o_swa, w_o_delta, w_o_mem, w_out,
     norm_pre_mix, norm_post_mix, norm_pre_ffn, norm_post_ffn, w_ff1, w_ff2) = lw
    mp, ms, n_seq = xp.shape[0], xs.shape[0], s0.shape[0]
    seq = ms // n_seq
    p_width = P_CQ + w_in_t.shape[0] - CQ_OFF
    tm_s = _row_tile(ms, 1024)
    proj_s, h_p = _in_proj(xs, norm_pre_mix, w_in_t, tm_s, 2 * PROJ_TILE,
                           [_rmsnorm_task(xp, norm_pre_mix, (ms // tm_s) * (p_width // (2 * PROJ_TILE)))])
    tm_p = _row_tile(mp, 2048)
    host_steps = (mp // tm_p) * (p_width // (2 * PROJ_TILE))
    casts = [_cast_task(w, host_steps) for w in (w_o_swa, w_o_delta, w_o_mem, w_out, w_ff1, w_ff2)]
    proj_p, w_o_swa, w_o_delta, w_o_mem, w_out, w_ff1, w_ff2 = _in_proj(
        h_p, None, w_in_t, tm_p, 2 * PROJ_TILE, casts)
    chunk = DN_CHUNK if mp % DN_CHUNK == 0 else mp
    scan_steps = mp // chunk
    per = -(-n_seq // scan_steps)
    riding = n_seq % per == 0 and per <= RIDER_SEQS_MAX
    swa_s = _swa_sample_task(proj_s, swa_caches, layer, rel_bias, seq, per if riding else 1)
    mem_s_task = _mem_attn_task(proj_s, *mem_s, layer, seq * (per if riding else math.gcd(n_seq, MEM_PER_STEP)), F32)
    riders = [swa_s, mem_s_task] if riding else []
    b_p, state_p, *rode = _delta_scan(proj_p, w_conv, a_log, dt_bias, norm_delta, None, None, chunk, riders)
    a_s, *kv_new_s, c_s = rode if riding else (*_run_task(swa_s), *_run_task(mem_s_task))
    a_p, *kv_state_p = _swa_prompt(proj_p, bias_prompt)
    c_p = _run_task(_mem_attn_task(proj_p, *mem_p, 0, _row_tile(mp, 512), BF16))[0]
    b_s, state_s, hist_s = _delta_scan(proj_s, w_conv, a_log, dt_bias, norm_delta, s0, conv_buf, seq)
    outs = []
    for x, proj, a, b, c in ((xp, proj_p, a_p, b_p, c_p), (xs, proj_s, a_s, b_s, c_s)):
        m = x.shape[0]
        tm = _row_tile(m, 1024)
        merged = _merge(a, b, c, proj, w_o_swa, w_o_delta, w_o_mem, tm, 512)
        x1, h2 = _proj_norm_residual(merged, w_out, x, norm_post_mix, norm_pre_ffn, _row_tile(m, 512))
        outs.append(_ffn(h2, w_ff1, w_ff2, x1, norm_post_ffn, tm, 1024))
    new_s = [t.reshape(n_seq, seq, SWA_HPG, HEAD_DIM) for t in kv_new_s] + [state_s, jnp.transpose(hist_s, (1, 0, 2))]
    kv_state_p = [t.reshape(1, -1, SWA_HPG, HEAD_DIM) for t in kv_state_p]
    return outs[0], outs[1], proj_p, kv_state_p, state_p, new_s


def kernel(x_prompt, x_sample, cache_swa0_k, cache_swa0_v, cache_swa1_k, cache_swa1_v, cache_swa2_k, cache_swa2_v, state_delta, state_conv, cache_mem_k, cache_mem_v, mem_prompt, rel_bias, w_in, w_conv, A_log, dt_bias, norm_delta, norm_mem, w_mem_kv, w_o_swa, w_o_delta, w_o_mem, w_out, norm_pre_mix, norm_post_mix, norm_pre_ffn, norm_post_ffn, w_ff1, w_ff2):
    depth = w_in.shape[0]
    bp, tp, d = x_prompt.shape
    bs, ts, _ = x_sample.shape
    assert bp == 1 and ts > CONV_WIDTH - 1 and ts % 8 == 0
    sample_swa = (cache_swa0_k, cache_swa0_v, cache_swa1_k, cache_swa1_v, cache_swa2_k, cache_swa2_v)
    bias_prompt = _prompt_bias_tables(rel_bias)
    xp = x_prompt.reshape(bp * tp, d)
    xs = x_sample.reshape(bs * ts, d)
    new_p = [[] for _ in range(10)]
    new_s = [[] for _ in range(8)]
    hw = SWA_HPG * HEAD_DIM
    for l in range(depth):
        lw = (jnp.swapaxes(w_in, 1, 2)[l], w_conv[l], A_log[l], dt_bias[l], norm_delta[l], w_o_swa[l],
              w_o_delta[l], w_o_mem[l], w_out[l],
              norm_pre_mix[l], norm_post_mix[l], norm_pre_ffn[l], norm_post_ffn[l],
              w_ff1[l], w_ff2[l])
        mem = mem_prompt.reshape(-1, d)
        mkv = _matmul(_rmsnorm_cast(mem, norm_mem[l], _row_tile(mem.shape[0], 256)), w_mem_kv[l],
                      2 * MEM_WIDTH, _row_tile(mem.shape[0], 256), 512, "mem_kv")
        mk = mkv[:, :MEM_WIDTH].reshape(1, bp, -1, MEM_HEADS, MEM_HEAD_DIM)
        mv = mkv[:, MEM_WIDTH:].reshape(1, bp, -1, MEM_HEADS, MEM_HEAD_DIM)
        xp, xs, main_p, kv_p, s_p, vals_s = _layer(
            xp, xs, lw, l, rel_bias, bias_prompt, (mk, mv), sample_swa, (cache_mem_k, cache_mem_v),
            state_conv[l], state_delta[l])
        vals_p = list(kv_p)
        vals_p.append(s_p.reshape(bp, DN_HEADS, HEAD_DIM, HEAD_DIM))
        vals_p.append(main_p[tp - (CONV_WIDTH - 1):, B_OFF:B_OFF + 3 * DN_WIDTH].reshape(bp, CONV_WIDTH - 1, -1))
        vals_p += [mk[0], mv[0]]
        for lst, val in zip(new_p, vals_p):
            lst.append(val)
        for lst, val in zip(new_s, vals_s):
            lst.append(val)
    outs_p = [jnp.stack(t) for t in new_p]
    outs_s = [jnp.stack(t) for t in new_s]
    return (xp.reshape(bp, tp, d), xs.reshape(bs, ts, d), *outs_p, *outs_s)
```
